```python
import jax, jax.numpy as jnp
from jax import lax
import numpy as np

D_MODEL = 1024
BATCH = 2
SEQ = 16384
DEPTH = 2
DEC_BATCH = 32
DEC_SEQ = 64
PAST_LEN = 2048

CHUNK = 64
N_PAST_CHUNKS = 8
BAND_ROWS = N_PAST_CHUNKS * CHUNK
REL_CLIP = 2 * CHUNK
HEAD_DIM = 64
A_HEADS = 16
B_HEADS = 16
B_KV_HEADS = 4
B_GROUP = B_HEADS // B_KV_HEADS
IDX_HEADS = 8
IDX_DIM = 64
TOPK_MAX = 256
Q_BLOCK = 128
D_FF = 2816
N_EXPERTS = 8
TOP_K_EXPERTS = 2
D_FF_EXPERT = 3584
ROPE_THETA = 10000.0
LN_EPS = 1e-5
N_EVEN = (DEPTH + 1) // 2
N_ODD = DEPTH // 2
ALPHA = (2 * DEPTH) ** 0.25
BETA = (8 * DEPTH) ** -0.25
A_PROJ = 3 * A_HEADS * HEAD_DIM
B_Q = B_HEADS * HEAD_DIM
B_KV = B_KV_HEADS * HEAD_DIM
B_QI = IDX_HEADS * IDX_DIM
B_PROJ = B_Q + 2 * B_KV + B_QI + IDX_DIM + IDX_HEADS
B_SPLITS = (B_Q, B_Q + B_KV, B_Q + 2 * B_KV, B_Q + 2 * B_KV + B_QI, B_Q + 2 * B_KV + B_QI + IDX_DIM)

kernel_name = "chunk_band_dsa_hybrid_encoder_step"


def layer_norm(x, g, b):
    xf = x.astype(jnp.float32)
    mu = jnp.mean(xf, axis=-1, keepdims=True)
    var = jnp.mean(jnp.square(xf - mu), axis=-1, keepdims=True)
    y = (xf - mu) * lax.rsqrt(var + LN_EPS) * g.astype(jnp.float32) + b.astype(jnp.float32)
    return y.astype(x.dtype)


def rope(x, pos):
    half = x.shape[-1] // 2
    inv = ROPE_THETA ** (-jnp.arange(half, dtype=jnp.float32) / half)
    ang = pos.astype(jnp.float32)[:, None] * inv[None, :]
    cos = jnp.cos(ang)[:, None, :]
    sin = jnp.sin(ang)[:, None, :]
    x1 = x[..., :half].astype(jnp.float32)
    x2 = x[..., half:].astype(jnp.float32)
    return jnp.concatenate([x1 * cos - x2 * sin, x1 * sin + x2 * cos], axis=-1).astype(x.dtype)


def rel_bias(table, q_pos, k_pos):
    rel = jnp.clip(k_pos[None, :] - q_pos[:, None], -REL_CLIP, REL_CLIP) + REL_CLIP
    return jnp.transpose(table[rel], (2, 0, 1)).astype(jnp.float32)


def band_attend(q, k, v, bias, mask):
    logits = jnp.einsum('bqhd,bkhd->bhqk', q, k).astype(jnp.float32) * (HEAD_DIM ** -0.5) + bias
    if mask is not None:
        logits = jnp.where(mask, logits, -jnp.inf)
    p = jax.nn.softmax(logits, axis=-1).astype(v.dtype)
    return jnp.einsum('bhqk,bkhd->bqhd', p, v)


def a_project(h, w_in):
    b, t, _ = h.shape
    qkv = (h @ w_in).reshape(b, t, 3, A_HEADS, HEAD_DIM)
    return qkv[:, :, 0], qkv[:, :, 1], qkv[:, :, 2]


def mixer_a_prompt(h, w_in, w_o, table):
    b, t, _ = h.shape
    q, k, v = a_project(h, w_in)
    n_chunks = t // CHUNK
    pad = ((0, 0), (BAND_ROWS, 0), (0, 0), (0, 0))
    kp = jnp.pad(k, pad)
    vp = jnp.pad(v, pad)
    band = BAND_ROWS + CHUNK
    jpos = jnp.arange(band)
    bias = rel_bias(table, jnp.arange(CHUNK), jpos - BAND_ROWS)
    q_chunks = jnp.moveaxis(q.reshape(b, n_chunks, CHUNK, A_HEADS, HEAD_DIM), 1, 0)

    def one_chunk(args):
        n, qc = args
        start = n * CHUNK
        kb = lax.dynamic_slice_in_dim(kp, start, band, axis=1)
        vb = lax.dynamic_slice_in_dim(vp, start, band, axis=1)
        mask = (start - BAND_ROWS + jpos >= 0)[None, :]
        return band_attend(qc, kb, vb, bias, mask)

    o = lax.map(one_chunk, (jnp.arange(n_chunks), q_chunks))
    o = jnp.moveaxis(o, 0, 1).reshape(b, t, A_HEADS * HEAD_DIM)
    keep = min(BAND_ROWS, t)
    return o @ w_o, k[:, t - keep:], v[:, t - keep:]


def mixer_a_sample(h, cache_k, cache_v, w_in, w_o, table):
    b, s, _ = h.shape
    w = cache_k.shape[1]
    q, k, v = a_project(h, w_in)
    kk = jnp.concatenate([cache_k.astype(k.dtype), k], axis=1)
    vv = jnp.concatenate([cache_v.astype(v.dtype), v], axis=1)
    q_pos = PAST_LEN + jnp.arange(s)
    k_pos = PAST_LEN - w + jnp.arange(w + s)
    bias = rel_bias(table, q_pos, k_pos)
    o = band_attend(q, kk, vv, bias, None).reshape(b, s, A_HEADS * HEAD_DIM)
    return o @ w_o, k, v


def b_project(h, w_in, kn_g, kn_b, pos):
    b, t, _ = h.shape
    q, k, v, qi, ki, wi = jnp.split(h @ w_in, list(B_SPLITS), axis=-1)
    q = rope(q.reshape(b, t, B_HEADS, HEAD_DIM), pos)
    k = rope(k.reshape(b, t, B_KV_HEADS, HEAD_DIM), pos)
    v = v.reshape(b, t, B_KV_HEADS, HEAD_DIM)
    qi = rope(qi.reshape(b, t, IDX_HEADS, IDX_DIM), pos)
    ki = rope(layer_norm(ki, kn_g, kn_b)[:, :, None, :], pos)[:, :, 0]
    wi = wi * (IDX_HEADS ** -0.5)
    return q, k, v, qi, ki, wi


_gather_rows = jax.vmap(lambda src, ix: src[ix])


def sparse_attend(q, qi, wi, q_pos, k, v, ki, k_pos, k_sel):
    b, nq = q.shape[0], q.shape[1]
    s = jnp.einsum('bqhd,bsd->bqhs', qi, ki).astype(jnp.float32) * (IDX_DIM ** -0.5)
    score = jnp.einsum('bqhs,bqh->bqs', jax.nn.relu(s), wi.astype(jnp.float32))
    adm = (k_pos[None, :] // CHUNK) <= (q_pos[:, None] // CHUNK)
    score = jnp.where(adm[None], score, -jnp.inf)
    top_val, top_idx = lax.top_k(score, k_sel)
    valid = jnp.isfinite(top_val)
    kg = _gather_rows(k, top_idx)
    vg = _gather_rows(v, top_idx)
    qg = q.reshape(b, nq, B_KV_HEADS, B_GROUP, HEAD_DIM)
    logits = jnp.einsum('bqhgd,bqkhd->bqhgk', qg, kg).astype(jnp.float32) * (HEAD_DIM ** -0.5)
    logits = jnp.where(valid[:, :, None, None, :], logits, -jnp.inf)
    p = jax.nn.softmax(logits, axis=-1).astype(vg.dtype)
    o = jnp.einsum('bqhgk,bqkhd->bqhgd', p, vg)
    return o.reshape(b, nq, B_HEADS * HEAD_DIM)


def mixer_b_prompt(h, w_in, w_o, kn_g, kn_b):
    b, t, _ = h.shape
    pos = jnp.arange(t)
    q, k, v, qi, ki, wi = b_project(h, w_in, kn_g, kn_b, pos)
    k_sel = min(TOPK_MAX, t // 4)
    nb = t // Q_BLOCK

    def blocks(a):
        return jnp.moveaxis(a.reshape((b, nb, Q_BLOCK) + a.shape[2:]), 1, 0)

    def one_block(args):
        qb, qib, wib, pb = args
        return sparse_attend(qb, qib, wib, pb, k, v, ki, pos, k_sel)

    o = lax.map(one_block, (blocks(q), blocks(qi), blocks(wi), pos.reshape(nb, Q_BLOCK)))
    o = jnp.moveaxis(o, 0, 1).reshape(b, t, B_HEADS * HEAD_DIM)
    return o @ w_o, k, v, ki


def mixer_b_sample(h, cache_k, cache_v, cache_ki, w_in, w_o, kn_g, kn_b):
    b, s, _ = h.shape
    p_len = cache_k.shape[1]
    q_pos = p_len + jnp.arange(s)
    q, k, v, qi, ki, wi = b_project(h, w_in, kn_g, kn_b, q_pos)
    kk = jnp.concatenate([cache_k.astype(k.dtype), k], axis=1)
    vv = jnp.concatenate([cache_v.astype(v.dtype), v], axis=1)
    kki = jnp.concatenate([cache_ki.astype(ki.dtype), ki], axis=1)
    k_pos = jnp.arange(p_len + s)
    k_sel = min(TOPK_MAX, (p_len + s) // 4)
    o = sparse_attend(q, qi, wi, q_pos, kk, vv, kki, k_pos, k_sel)
    return o @ w_o, k, v, ki


def swiglu(h, w_gu, w_down):
    g, u = jnp.split(h @ w_gu, 2, axis=-1)
    return (jax.nn.silu(g) * u) @ w_down


def moe_swiglu(h, w_r, b_r, w_gu, w_down):
    logits = (h @ w_r).astype(jnp.float32) + b_r.astype(jnp.float32)
    top_val, top_idx = lax.top_k(logits, TOP_K_EXPERTS)
    gates = jax.nn.softmax(top_val, axis=-1)
    combine = jnp.sum(jax.nn.one_hot(top_idx, N_EXPERTS, dtype=jnp.float32) * gates[..., None], axis=-2)
    combine = combine.astype(h.dtype)
    y = jnp.zeros_like(h)
    for e in range(N_EXPERTS):
        y = y + combine[..., e:e + 1] * swiglu(h, w_gu[e], w_down[e])
    return y


def modulation(c, w_cond, b_cond):
    m = jax.nn.silu(c) @ w_cond + b_cond
    return [t[:, None, :] for t in jnp.split(m, 6, axis=-1)]


def post_norm(x, sub, gate, g, b):
    return layer_norm(ALPHA * x + (1 + gate) * sub, g, b)


def setup_inputs(seed: int = 0) -> dict:
    key = jax.random.key(seed)
    ks = iter(jax.random.split(key, 32))
    D = D_MODEL
    a_win = min(BAND_ROWS, PAST_LEN)

    def nrm(shape, scale):
        return jax.random.normal(next(ks), shape, jnp.float32) * scale

    return {
        "x_prompt": nrm((BATCH, SEQ, D), 1.0),
        "x_sample": nrm((DEC_BATCH, DEC_SEQ, D), 1.0),
        "cache_k_a": nrm((N_EVEN, DEC_BATCH, a_win, A_HEADS, HEAD_DIM), 1.0),
        "cache_v_a": nrm((N_EVEN, DEC_BATCH, a_win, A_HEADS, HEAD_DIM), 1.0),
        "cache_k_b": nrm((N_ODD, DEC_BATCH, PAST_LEN, B_KV_HEADS, HEAD_DIM), 1.0),
        "cache_v_b": nrm((N_ODD, DEC_BATCH, PAST_LEN, B_KV_HEADS, HEAD_DIM), 1.0),
        "cache_kidx_b": nrm((N_ODD, DEC_BATCH, PAST_LEN, IDX_DIM), 1.0),
        "c_prompt": nrm((BATCH, D), 1.0),
        "c_sample": nrm((DEC_BATCH, D), 1.0),
        "w_cond": nrm((DEPTH, D, 6 * D), 0.2 * D ** -0.5),
        "b_cond": nrm((DEPTH, 6 * D), 0.02),
        "ln_g": 1.0 + nrm((DEPTH, 2, D), 0.02),
        "ln_b": nrm((DEPTH, 2, D), 0.02),
        "a_w_in": nrm((N_EVEN, D, A_PROJ), D ** -0.5),
        "a_w_o": nrm((N_EVEN, A_HEADS * HEAD_DIM, D), BETA * (A_HEADS * HEAD_DIM) ** -0.5),
        "a_rel_bias": nrm((N_EVEN, 2 * REL_CLIP + 1, A_HEADS), 0.5),
        "b_w_in": nrm((N_ODD, D, B_PROJ), D ** -0.5),
        "b_w_o": nrm((N_ODD, B_HEADS * HEAD_DIM, D), BETA * (B_HEADS * HEAD_DIM) ** -0.5),
        "b_kidx_ln_g": 1.0 + nrm((N_ODD, IDX_DIM), 0.02),
        "b_kidx_ln_b": nrm((N_ODD, IDX_DIM), 0.02),
        "ffn_w_gu": nrm((N_EVEN, D, 2 * D_FF), D ** -0.5),
        "ffn_w_down": nrm((N_EVEN, D_FF, D), BETA * D_FF ** -0.5),
        "moe_w_router": nrm((N_ODD, D, N_EXPERTS), D ** -0.5),
        "moe_b_router": nrm((N_ODD, N_EXPERTS), 0.01),
        "moe_w_gu": nrm((N_ODD, N_EXPERTS, D, 2 * D_FF_EXPERT), D ** -0.5),
        "moe_w_down": nrm((N_ODD, N_EXPERTS, D_FF_EXPERT, D), BETA * D_FF_EXPERT ** -0.5),
    }


def reference(x_prompt, x_sample, cache_k_a, cache_v_a, cache_k_b, cache_v_b, cache_kidx_b,
              c_prompt, c_sample, w_cond, b_cond, ln_g, ln_b, a_w_in, a_w_o, a_rel_bias,
              b_w_in, b_w_o, b_kidx_ln_g, b_kidx_ln_b, ffn_w_gu, ffn_w_down,
              moe_w_router, moe_b_router, moe_w_gu, moe_w_down):
    xp, xs = x_prompt, x_sample
    ka_p, va_p, kb_p, vb_p, ib_p = [], [], [], [], []
    ka_s, va_s, kb_s, vb_s, ib_s = [], [], [], [], []
    for i in range(DEPTH):
        j = i // 2
        sh1p, sc1p, g1p, sh2p, sc2p, g2p = modulation(c_prompt, w_cond[i], b_cond[i])
        sh1s, sc1s, g1s, sh2s, sc2s, g2s = modulation(c_sample, w_cond[i], b_cond[i])
        hp = xp * (1 + sc1p) + sh1p
        hs = xs * (1 + sc1s) + sh1s
        if i % 2 == 0:
            mp, k_, v_ = mixer_a_prompt(hp, a_w_in[j], a_w_o[j], a_rel_bias[j])
            ka_p.append(k_)
            va_p.append(v_)
            ms, k_, v_ = mixer_a_sample(hs, cache_k_a[j], cache_v_a[j], a_w_in[j], a_w_o[j], a_rel_bias[j])
            ka_s.append(k_)
            va_s.append(v_)
        else:
            mp, k_, v_, ki_ = mixer_b_prompt(hp, b_w_in[j], b_w_o[j], b_kidx_ln_g[j], b_kidx_ln_b[j])
            kb_p.append(k_)
            vb_p.append(v_)
            ib_p.append(ki_)
            ms, k_, v_, ki_ = mixer_b_sample(hs, cache_k_b[j], cache_v_b[j], cache_kidx_b[j],
                                             b_w_in[j], b_w_o[j], b_kidx_ln_g[j], b_kidx_ln_b[j])
            kb_s.append(k_)
            vb_s.append(v_)
            ib_s.append(ki_)
        xp = post_norm(xp, mp, g1p, ln_g[i, 0], ln_b[i, 0])
        xs = post_norm(xs, ms, g1s, ln_g[i, 0], ln_b[i, 0])
        hp = xp * (1 + sc2p) + sh2p
        hs = xs * (1 + sc2s) + sh2s
        if i % 2 == 0:
            fp = swiglu(hp, ffn_w_gu[j], ffn_w_down[j])
            fs = swiglu(hs, ffn_w_gu[j], ffn_w_down[j])
        else:
            fp = moe_swiglu(hp, moe_w_router[j], moe_b_router[j], moe_w_gu[j], moe_w_down[j])
            fs = moe_swiglu(hs, moe_w_router[j], moe_b_router[j], moe_w_gu[j], moe_w_down[j])
        xp = post_norm(xp, fp, g2p, ln_g[i, 1], ln_b[i, 1])
        xs = post_norm(xs, fs, g2s, ln_g[i, 1], ln_b[i, 1])
    return (xp, xs,
            jnp.stack(ka_p), jnp.stack(va_p), jnp.stack(kb_p), jnp.stack(vb_p), jnp.stack(ib_p),
            jnp.stack(ka_s), jnp.stack(va_s), jnp.stack(kb_s), jnp.stack(vb_s), jnp.stack(ib_s))
```

```python
import functools

import jax
import jax.numpy as jnp
from jax import lax
from jax.experimental import pallas as pl
from jax.experimental.pallas import tpu as pltpu

F32 = jnp.float32
BF16 = jnp.bfloat16

D_MODEL = 1024
CHUNK = 64
N_PAST_CHUNKS = 8
BAND_ROWS = N_PAST_CHUNKS * CHUNK
REL_CLIP = 2 * CHUNK
HEAD_DIM = 64
A_HEADS = 16
B_HEADS = 16
B_KV_HEADS = 4
B_GROUP = B_HEADS // B_KV_HEADS
IDX_HEADS = 8
IDX_DIM = 64
TOPK_MAX = 256
D_FF = 2816
N_EXPERTS = 8
D_FF_EXPERT = 3584
ROPE_THETA = 10000.0
LN_EPS = 1e-5
B_Q = B_HEADS * HEAD_DIM
B_KV = B_KV_HEADS * HEAD_DIM
B_QI = IDX_HEADS * IDX_DIM
B_PROJ = B_Q + 2 * B_KV + B_QI + IDX_DIM + IDX_HEADS

LANES = 128
VMEM_LIMIT_BYTES = 58 * 1024 * 1024

A_Q_TILE = 4 * CHUNK
DSA_Q_TILE = 2 * CHUNK
DSA_KEY_TILE = 512
ROW_TILE = 512
FFN_ROW_TILE = 1024
B_PROJ_PAD = B_Q + 2 * B_KV + B_QI + LANES
NEG_BIG = -1e30


def _params(n_grid):
    return pltpu.CompilerParams(
        dimension_semantics=("arbitrary",) * n_grid,
        vmem_limit_bytes=VMEM_LIMIT_BYTES,
    )


def _row_blocks(batch, seq, target):
    if seq >= target:
        assert seq % target == 0
        return 1, target
    bb = max(1, min(batch, target // seq))
    while batch % bb:
        bb -= 1
    return bb, seq


def _mod_spec(bb, chunk, n_grid):
    if n_grid == 2:
        return pl.BlockSpec((bb, 1, D_MODEL), lambda b, t: (b, 0, chunk))
    return pl.BlockSpec((bb, 1, D_MODEL), lambda b, t, e, c: (b, 0, chunk))


def _silu(x):
    return x / (1.0 + jnp.exp(-x))


def _layer_norm_rows(z, g, b):
    mu = jnp.mean(z, axis=-1, keepdims=True)
    zc = z - mu
    var = jnp.mean(zc * zc, axis=-1, keepdims=True)
    return zc * lax.rsqrt(var + LN_EPS) * g + b


def _modulation_kernel(c_ref, w_ref, b_ref, o_ref):
    a = _silu(c_ref[...]).astype(BF16)
    w = w_ref[0].astype(BF16)
    o_ref[0] = jnp.dot(a, w, preferred_element_type=F32) + b_ref[0]


def _modulation(c_all, w_cond, b_cond):
    depth, d, n = w_cond.shape
    rows = c_all.shape[0]
    tn = 1536
    return pl.pallas_call(
        _modulation_kernel,
        grid=(depth, n // tn),
        in_specs=[
            pl.BlockSpec((rows, d), lambda i, j: (0, 0)),
            pl.BlockSpec((1, d, tn), lambda i, j: (i, 0, j)),
            pl.BlockSpec((1, 1, tn), lambda i, j: (i, 0, j)),
        ],
        out_specs=pl.BlockSpec((1, rows, tn), lambda i, j: (i, 0, j)),
        out_shape=jax.ShapeDtypeStruct((depth, rows, n), F32),
        compiler_params=_params(2),
        name="modulation",
    )(c_all, w_cond, b_cond.reshape(depth, 1, n))


def _modmm_kernel(x_ref, sc_ref, sh_ref, w_ref, o_ref):
    bb, tt, d = x_ref.shape
    h = x_ref[...] * (1.0 + sc_ref[...]) + sh_ref[...]
    h = h.reshape(bb * tt, d).astype(BF16)
    acc = jnp.dot(h, w_ref[...], preferred_element_type=F32)
    o_ref[...] = acc.reshape(bb, tt, acc.shape[-1]).astype(o_ref.dtype)


def _modmm(x, mod, w, out_dtype, t_start=0):
    batch, seq, d = x.shape
    n = w.shape[1]
    seq_out = seq - t_start
    bb, tt = _row_blocks(batch, seq_out, ROW_TILE)
    assert t_start % tt == 0
    off = t_start // tt
    return pl.pallas_call(
        _modmm_kernel,
        grid=(batch // bb, seq_out // tt),
        in_specs=[
            pl.BlockSpec((bb, tt, d), lambda b, t: (b, t + off, 0)),
            _mod_spec(bb, 1, 2),
            _mod_spec(bb, 0, 2),
            pl.BlockSpec((d, n), lambda b, t: (0, 0)),
        ],
        out_specs=pl.BlockSpec((bb, tt, n), lambda b, t: (b, t, 0)),
        out_shape=jax.ShapeDtypeStruct((batch, seq_out, n), out_dtype),
        compiler_params=_params(2),
        name="a_project",
    )(x, mod, mod, w)


def _band_attn_kernel(*refs, n_kb, n_maybe_invalid):
    q_ref = refs[0]
    k_refs = refs[1:1 + n_kb]
    v_refs = refs[1 + n_kb:1 + 2 * n_kb]
    f_ref = refs[1 + 2 * n_kb]
    o_ref = refs[2 + 2 * n_kb]
    bias_ref = refs[3 + 2 * n_kb]
    q_tile = q_ref.shape[1]
    kb_sizes = [r.shape[1] for r in k_refs]
    k_tot = sum(kb_sizes)
    width = f_ref.shape[-1]
    i = pl.program_id(1)

    @pl.when((pl.program_id(0) == 0) & (i == 0))
    def _():
        rq = lax.broadcasted_iota(jnp.int32, (q_tile, k_tot), 0) // CHUNK
        ck = lax.broadcasted_iota(jnp.int32, (q_tile, k_tot), 1) // CHUNK
        in_band = (ck >= rq) & (ck <= rq + N_PAST_CHUNKS)
        for h in range(A_HEADS):
            rows = jnp.broadcast_to(f_ref[h], (q_tile, width))
            toep = pltpu.roll(rows, k_tot + 1, 1, stride=1, stride_axis=0)
            bias_ref[h] = jnp.where(in_band, toep[:, :k_tot], -jnp.inf)

    q = q_ref[0] * (HEAD_DIM ** -0.5)
    for h in range(A_HEADS):
        cols = slice(h * HEAD_DIM, (h + 1) * HEAD_DIM)
        qh = q[:, cols]
        parts = []
        for kb in range(n_kb):
            lg = lax.dot_general(qh, k_refs[kb][0, :, cols], (((1,), (1,)), ((), ())),
                                 preferred_element_type=F32)
            if kb < n_maybe_invalid:
                lg = jnp.where(i - n_maybe_invalid + kb >= 0, lg, -jnp.inf)
            parts.append(lg)
        logits = jnp.concatenate(parts, axis=1) + bias_ref[h]
        m = jnp.max(logits, axis=1, keepdims=True)
        p = jnp.exp(logits - m)
        l = jnp.sum(p, axis=1, keepdims=True)
        pb = p.astype(BF16)
        acc = jnp.zeros((q_tile, HEAD_DIM), F32)
        start = 0
        for kb in range(n_kb):
            acc = acc + jnp.dot(pb[:, start:start + kb_sizes[kb]], v_refs[kb][0, :, cols],
                                preferred_element_type=F32)
            start += kb_sizes[kb]
        o_ref[0, :, cols] = (acc / l).astype(o_ref.dtype)


def _bias_vector(table, q_tile, k_tot):
    width = k_tot + q_tile
    lo = -(q_tile - 1) - BAND_ROWS + REL_CLIP
    left = max(0, -lo)
    start = max(0, lo)
    n_mid = min(2 * REL_CLIP + 1 - start, width - left)
    right = width - left - n_mid
    mid = table.T[:, start:start + n_mid]
    f = jnp.pad(mid, ((0, 0), (left, right)), mode="edge")
    return f.reshape(A_HEADS, 1, width)


def _band_attn_prompt(qkv, table):
    batch, seq, _ = qkv.shape
    hd = A_HEADS * HEAD_DIM
    qt = A_Q_TILE
    n_prev = BAND_ROWS // qt
    n_kb = n_prev + 1
    k_tot = n_kb * qt
    f = _bias_vector(table, qt, k_tot)

    def kv_spec(kb, col):
        return pl.BlockSpec((1, qt, hd), lambda b, i: (b, jnp.maximum(i - n_prev + kb, 0), col))

    return pl.pallas_call(
        functools.partial(_band_attn_kernel, n_kb=n_kb, n_maybe_invalid=n_prev),
        grid=(batch, seq // qt),
        in_specs=[pl.BlockSpec((1, qt, hd), lambda b, i: (b, i, 0))]
        + [kv_spec(kb, 1) for kb in range(n_kb)]
        + [kv_spec(kb, 2) for kb in range(n_kb)]
        + [pl.BlockSpec(f.shape, lambda b, i: (0, 0, 0))],
        out_specs=pl.BlockSpec((1, qt, hd), lambda b, i: (b, i, 0)),
        out_shape=jax.ShapeDtypeStruct((batch, seq, hd), BF16),
        scratch_shapes=[pltpu.VMEM((A_HEADS, qt, k_tot), F32)],
        compiler_params=_params(2),
        name="band_attn_prompt",
    )(*([qkv] * (1 + 2 * n_kb)), f)


def _band_attn_sample(qkv, cache_k, cache_v, table):
    batch, seq, _ = qkv.shape
    hd = A_HEADS * HEAD_DIM
    win = cache_k.shape[1]
    assert seq == CHUNK and win == BAND_ROWS
    k_tot = win + seq
    f = _bias_vector(table, seq, k_tot)
    new = lambda col: pl.BlockSpec((1, seq, hd), lambda b, i: (b, 0, col))
    old = pl.BlockSpec((1, win, hd), lambda b, i: (b, 0, 0))
    return pl.pallas_call(
        functools.partial(_band_attn_kernel, n_kb=2, n_maybe_invalid=0),
        grid=(batch, 1),
        in_specs=[new(0), old, new(1), old, new(2), pl.BlockSpec(f.shape, lambda b, i: (0, 0, 0))],
        out_specs=pl.BlockSpec((1, seq, hd), lambda b, i: (b, 0, 0)),
        out_shape=jax.ShapeDtypeStruct((batch, seq, hd), BF16),
        scratch_shapes=[pltpu.VMEM((A_HEADS, seq, k_tot), F32)],
        compiler_params=_params(2),
        name="band_attn_sample",
    )(qkv, cache_k, qkv, cache_v, qkv, f)


def _mm_postnorm_kernel(o_ref, w_ref, x_ref, gate_ref, g_ref, b_ref, y_ref, *, alpha):
    bb, tt, d = x_ref.shape
    o = o_ref[...].reshape(bb * tt, o_ref.shape[-1])
    sub = jnp.dot(o, w_ref[...], preferred_element_type=F32).reshape(bb, tt, d)
    z = alpha * x_ref[...] + (1.0 + gate_ref[...]) * sub
    y_ref[...] = _layer_norm_rows(z, g_ref[...], b_ref[...])


def _mm_postnorm(o, w, x, mod, gate_chunk, g, b, alpha):
    batch, seq, d = x.shape
    bb, tt = _row_blocks(batch, seq, ROW_TILE)
    k = o.shape[-1]
    return pl.pallas_call(
        functools.partial(_mm_postnorm_kernel, alpha=alpha),
        grid=(batch // bb, seq // tt),
        in_specs=[
            pl.BlockSpec((bb, tt, k), lambda b_, t: (b_, t, 0)),
            pl.BlockSpec((k, d), lambda b_, t: (0, 0)),
            pl.BlockSpec((bb, tt, d), lambda b_, t: (b_, t, 0)),
            _mod_spec(bb, gate_chunk, 2),
            pl.BlockSpec((1, 1, d), lambda b_, t: (0, 0, 0)),
            pl.BlockSpec((1, 1, d), lambda b_, t: (0, 0, 0)),
        ],
        out_specs=pl.BlockSpec((bb, tt, d), lambda b_, t: (b_, t, 0)),
        out_shape=jax.ShapeDtypeStruct((batch, seq, d), F32),
        compiler_params=_params(2),
        name="out_proj_postnorm",
    )(o, w, x, mod, g.reshape(1, 1, d), b.reshape(1, 1, d))


def _ffn_kernel(x_ref, sc_ref, sh_ref, gate_ref, wr_ref, br_ref, wg_ref, wu_ref, wd_ref,
                g_ref, b_ref, y_ref, h_ref, comb_ref, acc_ref, *, routed, alpha):
    bb, tt, d = x_ref.shape
    rows = bb * tt
    e = pl.program_id(2)
    c = pl.program_id(3)
    lane = lax.broadcasted_iota(jnp.int32, (rows, LANES), 1)

    @pl.when((e == 0) & (c == 0))
    def _():
        h = (x_ref[...] * (1.0 + sc_ref[...]) + sh_ref[...]).reshape(rows, d)
        h_ref[...] = h.astype(BF16)
        acc_ref[...] = jnp.zeros_like(acc_ref)
        if routed:
            logits = jnp.dot(h, wr_ref[...], preferred_element_type=F32,
                             precision=lax.Precision.HIGHEST) + br_ref[...]
            logits = jnp.where(lane < N_EXPERTS, logits, -jnp.inf)
            m1 = jnp.max(logits, axis=1, keepdims=True)
            i1 = jnp.min(jnp.where(logits == m1, lane, LANES), axis=1, keepdims=True)
            rest = jnp.where(lane == i1, -jnp.inf, logits)
            m2 = jnp.max(rest, axis=1, keepdims=True)
            i2 = jnp.min(jnp.where(rest == m2, lane, LANES), axis=1, keepdims=True)
            e2 = jnp.exp(m2 - m1)
            den = 1.0 + e2
            comb_ref[...] = jnp.where(lane == i1, 1.0 / den, 0.0) + jnp.where(lane == i2, e2 / den, 0.0)

    h = h_ref[...]
    gt = jnp.dot(h, wg_ref[0], preferred_element_type=F32)
    up = jnp.dot(h, wu_ref[0], preferred_element_type=F32)
    act = (_silu(gt) * up).astype(BF16)
    part = jnp.dot(act, wd_ref[0], preferred_element_type=F32)
    if routed:
        ce = jnp.sum(jnp.where(lane == e, comb_ref[...], 0.0), axis=1, keepdims=True)
        part = ce * part
    acc_ref[...] += part

    @pl.when((e == pl.num_programs(2) - 1) & (c == pl.num_programs(3) - 1))
    def _():
        z = alpha * x_ref[...] + (1.0 + gate_ref[...]) * acc_ref[...].reshape(bb, tt, d)
        y_ref[...] = _layer_norm_rows(z, g_ref[...], b_ref[...])


def _ffn(x, mod, w_r, b_r, w_gu, w_down, g, b, *, routed, ff_chunk, alpha):
    batch, seq, d = x.shape
    n_e, ff, _ = w_down.shape
    assert ff % ff_chunk == 0
    n_c = ff // ff_chunk
    bb, tt = _row_blocks(batch, seq, FFN_ROW_TILE)
    rows = bb * tt
    vec = pl.BlockSpec((1, 1, d), lambda b_, t, e, c: (0, 0, 0))
    return pl.pallas_call(
        functools.partial(_ffn_kernel, routed=routed, alpha=alpha),
        grid=(batch // bb, seq // tt, n_e, n_c),
        in_specs=[
            pl.BlockSpec((bb, tt, d), lambda b_, t, e, c: (b_, t, 0)),
            _mod_spec(bb, 4, 4),
            _mod_spec(bb, 3, 4),
            _mod_spec(bb, 5, 4),
            pl.BlockSpec((d, LANES), lambda b_, t, e, c: (0, 0)),
            pl.BlockSpec((1, LANES), lambda b_, t, e, c: (0, 0)),
            pl.BlockSpec((1, d, ff_chunk), lambda b_, t, e, c: (e, 0, c)),
            pl.BlockSpec((1, d, ff_chunk), lambda b_, t, e, c: (e, 0, n_c + c)),
            pl.BlockSpec((1, ff_chunk, d), lambda b_, t, e, c: (e, c, 0)),
            vec,
            vec,
        ],
        out_specs=pl.BlockSpec((bb, tt, d), lambda b_, t, e, c: (b_, t, 0)),
        out_shape=jax.ShapeDtypeStruct((batch, seq, d), F32),
        scratch_shapes=[
            pltpu.VMEM((rows, d), BF16),
            pltpu.VMEM((rows, LANES), F32),
            pltpu.VMEM((rows, d), F32),
        ],
        compiler_params=_params(4),
        name="moe_ffn" if routed else "dense_ffn",
    )(x, mod, mod, mod, w_r, b_r, w_gu, w_gu, w_down, g.reshape(1, 1, d), b.reshape(1, 1, d))


def _rope_lanes(seg, cos, sin_signed, first_half):
    width = seg.shape[-1]
    reps = width // LANES
    if reps > 1:
        cos = jnp.concatenate([cos] * reps, axis=1)
        sin_signed = jnp.concatenate([sin_signed] * reps, axis=1)
        first_half = jnp.concatenate([first_half] * reps, axis=1)
    half = HEAD_DIM // 2
    swapped = jnp.where(first_half, pltpu.roll(seg, width - half, 1), pltpu.roll(seg, half, 1))
    return seg * cos + swapped * sin_signed


def _b_project_kernel(x_ref, sc_ref, sh_ref, w_ref, cos_ref, sin_ref, kg_ref, kb_ref,
                      q_ref, qi_ref, k_ref, v_ref, kw_ref):
    bb, tt, d = x_ref.shape
    rows = bb * tt
    h = x_ref[...] * (1.0 + sc_ref[...]) + sh_ref[...]
    h = h.reshape(rows, d).astype(BF16)
    acc = jnp.dot(h, w_ref[...], preferred_element_type=F32)
    cos = cos_ref[...]
    sin_signed = sin_ref[...]
    lane = lax.broadcasted_iota(jnp.int32, (1, LANES), 1)
    first_half = (lane % HEAD_DIM) < (HEAD_DIM // 2)

    def out(ref, val):
        ref[...] = val.reshape(bb, tt, val.shape[-1]).astype(ref.dtype)

    o_k, o_v, o_qi, o_ki = B_Q, B_Q + B_KV, B_Q + 2 * B_KV, B_Q + 2 * B_KV + B_QI
    out(q_ref, _rope_lanes(acc[:, :o_k], cos, sin_signed, first_half))
    out(k_ref, _rope_lanes(acc[:, o_k:o_v], cos, sin_signed, first_half))
    out(v_ref, acc[:, o_v:o_qi])
    out(qi_ref, _rope_lanes(acc[:, o_qi:o_ki], cos, sin_signed, first_half))
    seg = acc[:, o_ki:]
    is_ki = lane < IDX_DIM
    mu = jnp.sum(jnp.where(is_ki, seg, 0.0), axis=1, keepdims=True) / IDX_DIM
    cen = jnp.where(is_ki, seg - mu, 0.0)
    var = jnp.sum(cen * cen, axis=1, keepdims=True) / IDX_DIM
    ki = cen * lax.rsqrt(var + LN_EPS) * kg_ref[...] + kb_ref[...]
    ki = _rope_lanes(ki, cos, sin_signed, first_half)
    out(kw_ref, jnp.where(is_ki, ki, seg * (IDX_HEADS ** -0.5)))


def _b_project(x, mod, w, cos, sin_signed, kn_g, kn_b, table_per_tile):
    batch, seq, d = x.shape
    bb, tt = _row_blocks(batch, seq, ROW_TILE)
    rows = bb * tt
    n = w.shape[1]
    tab = pl.BlockSpec((rows, LANES), (lambda b, t: (0, 0)) if table_per_tile else (lambda b, t: (t, 0)))
    vec = pl.BlockSpec((1, LANES), lambda b, t: (0, 0))

    def o_spec(width):
        return pl.BlockSpec((bb, tt, width), lambda b, t: (b, t, 0))

    def o_shape(width, dtype):
        return jax.ShapeDtypeStruct((batch, seq, width), dtype)

    return pl.pallas_call(
        _b_project_kernel,
        grid=(batch // bb, seq // tt),
        in_specs=[
            pl.BlockSpec((bb, tt, d), lambda b, t: (b, t, 0)),
            _mod_spec(bb, 1, 2),
            _mod_spec(bb, 0, 2),
            pl.BlockSpec((d, n), lambda b, t: (0, 0)),
            tab,
            tab,
            vec,
            vec,
        ],
        out_specs=[o_spec(B_Q), o_spec(B_QI), o_spec(B_KV), o_spec(B_KV), o_spec(LANES)],
        out_shape=[o_shape(B_Q, BF16), o_shape(B_QI, BF16), o_shape(B_KV, F32), o_shape(B_KV, F32),
                   o_shape(LANES, F32)],
        compiler_params=_params(2),
        name="b_project",
    )(x, mod, mod, w, cos, sin_signed, kn_g, kn_b)


def _rope_tables(pos):
    half = HEAD_DIM // 2
    inv = ROPE_THETA ** (-jnp.arange(half, dtype=F32) / half)
    ang = pos.astype(F32)[:, None] * inv[None, :]
    cos, sin = jnp.cos(ang), jnp.sin(ang)
    return (jnp.concatenate([cos, cos, cos, cos], axis=1),
            jnp.concatenate([-sin, sin, -sin, sin], axis=1))


def _dsa_kernel(q_ref, qi_ref, kw_ref, kit_ref, kt_ref, v_ref, o_ref, s_ref, *,
                causal, n_keys, k_sel):
    qn = q_ref.shape[1]
    lt = DSA_KEY_TILE
    i = pl.program_id(1)
    if causal:
        n_tiles = ((i + 1) * qn + lt - 1) // lt
        row = lax.broadcasted_iota(jnp.int32, (qn, 1), 0)
        limit = ((i * qn + row) // CHUNK + 1) * CHUNK
    else:
        n_tiles = kit_ref.shape[2] // lt
        limit = jnp.full((qn, 1), n_keys, jnp.int32)

    def key_slice(t):
        return pl.ds(pl.multiple_of(t * lt, lt), lt)

    qi = qi_ref[0]
    kw = kw_ref[0]
    qi_s = jnp.concatenate([qi[:, h * IDX_DIM:(h + 1) * IDX_DIM] for h in range(IDX_HEADS)], axis=0)
    w_s = jnp.concatenate([kw[:, IDX_DIM + h:IDX_DIM + h + 1] for h in range(IDX_HEADS)], axis=0)
    w_s = w_s * (IDX_DIM ** -0.5)

    def score_body(t, carry):
        rmax, rmin = carry
        s = jnp.dot(qi_s, kit_ref[0, :, key_slice(t)], preferred_element_type=F32)
        s = jnp.maximum(s, 0.0) * w_s
        sc = s[0:qn]
        for h in range(1, IDX_HEADS):
            sc = sc + s[h * qn:(h + 1) * qn]
        kpos = t * lt + lax.broadcasted_iota(jnp.int32, (1, lt), 1)
        adm = kpos < limit
        s_ref[:, key_slice(t)] = jnp.where(adm, sc, -jnp.inf)
        rmax = jnp.maximum(rmax, jnp.max(jnp.where(adm, sc, -jnp.inf), axis=1, keepdims=True))
        rmin = jnp.minimum(rmin, jnp.min(jnp.where(adm, sc, jnp.inf), axis=1, keepdims=True))
        return rmax, rmin

    rmax, rmin = lax.fori_loop(0, n_tiles, score_body,
                               (jnp.full((qn, 1), -jnp.inf, F32), jnp.full((qn, 1), jnp.inf, F32)))

    def count_ge(thr):
        thr_b = jnp.broadcast_to(thr, (qn, LANES))

        def body(t, acc):
            st = s_ref[:, key_slice(t)]
            for j in range(lt // LANES):
                acc = acc + jnp.where(st[:, j * LANES:(j + 1) * LANES] >= thr_b, 1.0, 0.0)
            return acc

        acc = lax.fori_loop(0, n_tiles, body, jnp.zeros((qn, LANES), F32))
        return jnp.sum(acc, axis=1, keepdims=True)

    n_adm = limit.astype(F32)
    target = jnp.minimum(n_adm, float(k_sel))
    hi0 = jnp.where(rmax > 0, 2.0 * rmax, jnp.where(rmax < 0, 0.5 * rmax, 1.0))

    def bis_cond(st):
        return jnp.sum(st[3]) < qn

    def bis_body(st):
        lo, hi, c_lo, done = st
        mid = 0.5 * lo + 0.5 * hi
        stuck = (mid <= lo) | (mid >= hi)
        c = count_ge(mid)
        live = (done < 0.5) & jnp.logical_not(stuck)
        up = live & (c >= target)
        dn = live & (c < target)
        lo = jnp.where(up, mid, lo)
        c_lo = jnp.where(up, c, c_lo)
        hi = jnp.where(dn, mid, hi)
        done = jnp.where(stuck | (c_lo == target), 1.0, done)
        return lo, hi, c_lo, done

    done0 = jnp.where(n_adm == target, 1.0, 0.0)
    thr, _, _, _ = lax.while_loop(bis_cond, bis_body, (rmin, hi0, n_adm, done0))

    q = q_ref[0] * (HEAD_DIM ** -0.5)
    gq = B_GROUP * qn
    for g in range(B_KV_HEADS):
        q_g = jnp.concatenate(
            [q[:, (g * B_GROUP + j) * HEAD_DIM:(g * B_GROUP + j + 1) * HEAD_DIM] for j in range(B_GROUP)],
            axis=0)
        rows_g = slice(g * HEAD_DIM, (g + 1) * HEAD_DIM)

        def attn_body(t, carry):
            m, l, acc = carry
            lg = jnp.dot(q_g, kt_ref[0, rows_g, key_slice(t)], preferred_element_type=F32)
            mb = jnp.where(s_ref[:, key_slice(t)] >= thr, 0.0, -jnp.inf)
            lg = (lg.reshape(B_GROUP, qn, lt) + mb[None]).reshape(gq, lt)
            m_new = jnp.maximum(m, jnp.max(lg, axis=1, keepdims=True))
            alpha = jnp.exp(m - m_new)
            p = jnp.exp(lg - m_new)
            l = alpha * l + jnp.sum(p, axis=1, keepdims=True)
            pv = jnp.dot(p.astype(BF16), v_ref[0, key_slice(t), rows_g], preferred_element_type=F32)
            return m_new, l, alpha * acc + pv

        m0 = jnp.full((gq, 1), NEG_BIG, F32)
        _, l, acc = lax.fori_loop(0, n_tiles, attn_body,
                                  (m0, jnp.zeros((gq, 1), F32), jnp.zeros((gq, HEAD_DIM), F32)))
        o_g = acc / l
        for j in range(B_GROUP):
            hh = g * B_GROUP + j
            o_ref[0, :, hh * HEAD_DIM:(hh + 1) * HEAD_DIM] = o_g[j * qn:(j + 1) * qn].astype(o_ref.dtype)


def _dsa(q, qi, kw, kit, kt, v, *, q_tile, causal, n_keys, k_sel):
    batch, seq, _ = q.shape
    lp = kit.shape[2]
    assert lp % DSA_KEY_TILE == 0 and seq % q_tile == 0

    def qspec(width):
        return pl.BlockSpec((1, q_tile, width), lambda b, i: (b, i, 0))

    def resident(shape):
        return pl.BlockSpec((1,) + shape, lambda b, i: (b, 0, 0), pipeline_mode=pl.Buffered(1))

    return pl.pallas_call(
        functools.partial(_dsa_kernel, causal=causal, n_keys=n_keys, k_sel=k_sel),
        grid=(batch, seq // q_tile),
        in_specs=[qspec(B_Q), qspec(B_QI), qspec(LANES),
                  resident((IDX_DIM, lp)), resident((B_KV, lp)), resident((lp, B_KV))],
        out_specs=qspec(B_Q),
        out_shape=jax.ShapeDtypeStruct((batch, seq, B_Q), BF16),
        scratch_shapes=[pltpu.VMEM((q_tile, lp), F32)],
        compiler_params=_params(2),
        name="dsa_prompt" if causal else "dsa_sample",
    )(q, qi, kw, kit, kt, v)


def _key_layouts(k, v, ki, pad_to):
    pad = pad_to - k.shape[1]
    cfg = ((0, 0), (0, pad), (0, 0))
    k, v, ki = (jnp.pad(a.astype(BF16), cfg) for a in (k, v, ki))
    return jnp.swapaxes(ki, 1, 2), jnp.swapaxes(k, 1, 2), v


def kernel(x_prompt, x_sample, cache_k_a, cache_v_a, cache_k_b, cache_v_b, cache_kidx_b,
           c_prompt, c_sample, w_cond, b_cond, ln_g, ln_b, a_w_in, a_w_o, a_rel_bias,
           b_w_in, b_w_o, b_kidx_ln_g, b_kidx_ln_b, ffn_w_gu, ffn_w_down,
           moe_w_router, moe_b_router, moe_w_gu, moe_w_down):
    depth = w_cond.shape[0]
    alpha = (2 * depth) ** 0.25
    n_p, seq, d = x_prompt.shape
    n_s, dec_seq, _ = x_sample.shape
    past = cache_k_b.shape[2]
    a_hd = A_HEADS * HEAD_DIM

    rows = n_p + n_s
    rows_pad = -(-rows // 8) * 8
    c_all = jnp.pad(jnp.concatenate([c_prompt, c_sample], axis=0), ((0, rows_pad - rows), (0, 0)))
    mod_all = _modulation(c_all, w_cond, b_cond)

    xp, xs = x_prompt, x_sample
    outs = {k: [] for k in ("ka_p", "va_p", "kb_p", "vb_p", "ib_p", "ka_s", "va_s", "kb_s", "vb_s", "ib_s")}
    dummy_wr = jnp.zeros((d, LANES), F32)
    dummy_br = jnp.zeros((1, LANES), F32)
    for i in range(depth):
        j = i // 2
        mod_p = mod_all[i, :n_p].reshape(n_p, 1, 6 * d)
        mod_s = mod_all[i, n_p:rows].reshape(n_s, 1, 6 * d)
        g1, b1, g2, b2 = ln_g[i, 0], ln_b[i, 0], ln_g[i, 1], ln_b[i, 1]
        if i % 2 == 0:
            w_in = a_w_in[j].astype(BF16)
            keep = min(BAND_ROWS, seq)
            qkv_p = _modmm(xp, mod_p, w_in, BF16)
            kv_p = _modmm(xp, mod_p, w_in[:, a_hd:], F32, t_start=seq - keep)
            qkv_s = _modmm(xs, mod_s, w_in, BF16)
            kv_s = _modmm(xs, mod_s, w_in[:, a_hd:], F32)
            outs["ka_p"].append(kv_p[..., :a_hd].reshape(n_p, keep, A_HEADS, HEAD_DIM))
            outs["va_p"].append(kv_p[..., a_hd:].reshape(n_p, keep, A_HEADS, HEAD_DIM))
            outs["ka_s"].append(kv_s[..., :a_hd].reshape(n_s, dec_seq, A_HEADS, HEAD_DIM))
            outs["va_s"].append(kv_s[..., a_hd:].reshape(n_s, dec_seq, A_HEADS, HEAD_DIM))
            o_p = _band_attn_prompt(qkv_p, a_rel_bias[j])
            ck = cache_k_a[j].reshape(n_s, -1, a_hd).astype(BF16)
            cv = cache_v_a[j].reshape(n_s, -1, a_hd).astype(BF16)
            o_s = _band_attn_sample(qkv_s, ck, cv, a_rel_bias[j])
            w_o = a_w_o[j].astype(BF16)
        else:
            w_in = jnp.pad(b_w_in[j], ((0, 0), (0, B_PROJ_PAD - B_PROJ))).astype(BF16)
            kn_g = jnp.pad(b_kidx_ln_g[j], (0, LANES - IDX_DIM)).reshape(1, LANES)
            kn_b = jnp.pad(b_kidx_ln_b[j], (0, LANES - IDX_DIM)).reshape(1, LANES)
            cos_p, sin_p = _rope_tables(jnp.arange(seq))
            cos_s, sin_s = _rope_tables(past + jnp.arange(dec_seq))
            bb_s, _ = _row_blocks(n_s, dec_seq, ROW_TILE)
            cos_s, sin_s = jnp.tile(cos_s, (bb_s, 1)), jnp.tile(sin_s, (bb_s, 1))
            q_p, qi_p, k_p, v_p, kw_p = _b_project(xp, mod_p, w_in, cos_p, sin_p, kn_g, kn_b, False)
            q_s, qi_s, k_s, v_s, kw_s = _b_project(xs, mod_s, w_in, cos_s, sin_s, kn_g, kn_b, True)
            outs["kb_p"].append(k_p.reshape(n_p, seq, B_KV_HEADS, HEAD_DIM))
            outs["vb_p"].append(v_p.reshape(n_p, seq, B_KV_HEADS, HEAD_DIM))
            outs["ib_p"].append(kw_p[..., :IDX_DIM])
            outs["kb_s"].append(k_s.reshape(n_s, dec_seq, B_KV_HEADS, HEAD_DIM))
            outs["vb_s"].append(v_s.reshape(n_s, dec_seq, B_KV_HEADS, HEAD_DIM))
            outs["ib_s"].append(kw_s[..., :IDX_DIM])
            lt = DSA_KEY_TILE
            kit, kt, vv = _key_layouts(k_p, v_p, kw_p[..., :IDX_DIM], -(-seq // lt) * lt)
            o_p = _dsa(q_p, qi_p, kw_p, kit, kt, vv, q_tile=min(DSA_Q_TILE, seq), causal=True,
                       n_keys=seq, k_sel=min(TOPK_MAX, seq // 4))
            n_keys = past + dec_seq
            kk = jnp.concatenate([cache_k_b[j].reshape(n_s, past, B_KV), k_s], axis=1)
            vc = jnp.concatenate([cache_v_b[j].reshape(n_s, past, B_KV), v_s], axis=1)
            kki = jnp.concatenate([cache_kidx_b[j], kw_s[..., :IDX_DIM]], axis=1)
            kit, kt, vv = _key_layouts(kk, vc, kki, -(-n_keys // lt) * lt)
            o_s = _dsa(q_s, qi_s, kw_s, kit, kt, vv, q_tile=dec_seq, causal=False,
                       n_keys=n_keys, k_sel=min(TOPK_MAX, n_keys // 4))
            w_o = b_w_o[j].astype(BF16)
        xp = _mm_postnorm(o_p, w_o, xp, mod_p, 2, g1, b1, alpha)
        xs = _mm_postnorm(o_s, w_o, xs, mod_s, 2, g1, b1, alpha)
        if i % 2 == 0:
            w_gu = ffn_w_gu[j].astype(BF16)[None]
            w_dn = ffn_w_down[j].astype(BF16)[None]
            dense = dict(routed=False, ff_chunk=D_FF // 2, alpha=alpha)
            xp = _ffn(xp, mod_p, dummy_wr, dummy_br, w_gu, w_dn, g2, b2, **dense)
            xs = _ffn(xs, mod_s, dummy_wr, dummy_br, w_gu, w_dn, g2, b2, **dense)
        else:
            w_gu = moe_w_gu[j].astype(BF16)
            w_dn = moe_w_down[j].astype(BF16)
            w_r = jnp.pad(moe_w_router[j], ((0, 0), (0, LANES - N_EXPERTS)))
            b_r = jnp.pad(moe_b_router[j], (0, LANES - N_EXPERTS)).reshape(1, LANES)
            moe = dict(routed=True, ff_chunk=D_FF_EXPERT // 4, alpha=alpha)
            xp = _ffn(xp, mod_p, w_r, b_r, w_gu, w_dn, g2, b2, **moe)
            xs = _ffn(xs, mod_s, w_r, b_r, w_gu, w_dn, g2, b2, **moe)

    st = lambda name: jnp.stack(outs[name])
    return (xp, xs, st("ka_p"), st("va_p"), st("kb_p"), st("vb_p"), st("ib_p"),
            st("ka_s"), st("va_s"), st("kb_s"), st("vb_s"), st("ib_s"))
```

```python
import functools

import jax
import jax.numpy as jnp
from jax import lax
from jax.experimental import pallas as pl
from jax.experimental.pallas import tpu as pltpu

F32 = jnp.float32
BF16 = jnp.bfloat16

D_MODEL = 1024
CHUNK = 64
N_PAST_CHUNKS = 8
BAND_ROWS = N_PAST_CHUNKS * CHUNK
REL_CLIP = 2 * CHUNK
HEAD_DIM = 64
A_HEADS = 16
B_HEADS = 16
B_KV_HEADS = 4
B_GROUP = B_HEADS // B_KV_HEADS
IDX_HEADS = 8
IDX_DIM = 64
TOPK_MAX = 256
D_FF = 2816
N_EXPERTS = 8
D_FF_EXPERT = 3584
ROPE_THETA = 10000.0
LN_EPS = 1e-5
B_Q = B_HEADS * HEAD_DIM
B_KV = B_KV_HEADS * HEAD_DIM
B_QI = IDX_HEADS * IDX_DIM
B_PROJ = B_Q + 2 * B_KV + B_QI + IDX_DIM + IDX_HEADS

LANES = 128
VMEM_LIMIT_BYTES = 58 * 1024 * 1024

A_Q_TILE = 4 * CHUNK
DSA_Q_TILE = 2 * CHUNK
DSA_KEY_TILE = 512
ROW_TILE = 512
FFN_ROW_TILE = 1024
B_PROJ_PAD = B_Q + 2 * B_KV + B_QI + LANES
NEG_BIG = -1e30
QK_SCALE_LOG2 = HEAD_DIM ** -0.5 * 1.4426950408889634


def _params(n_grid):
    return pltpu.CompilerParams(
        dimension_semantics=("arbitrary",) * n_grid,
        vmem_limit_bytes=VMEM_LIMIT_BYTES,
    )


def _row_blocks(batch, seq, target):
    if seq >= target:
        assert seq % target == 0
        return 1, target
    bb = max(1, min(batch, target // seq))
    while batch % bb:
        bb -= 1
    return bb, seq


def _mod_spec(bb, chunk, n_grid):
    if n_grid == 2:
        return pl.BlockSpec((bb, 1, D_MODEL), lambda b, t: (b, 0, chunk))
    return pl.BlockSpec((bb, 1, D_MODEL), lambda b, t, e, c: (b, 0, chunk))


def _silu(x):
    return x / (1.0 + jnp.exp(-x))


def _layer_norm_rows(z, g, b):
    mu = jnp.mean(z, axis=-1, keepdims=True)
    zc = z - mu
    var = jnp.mean(zc * zc, axis=-1, keepdims=True)
    return zc * lax.rsqrt(var + LN_EPS) * g + b


def _modulation_kernel(c_ref, w_ref, b_ref, o_ref):
    a = _silu(c_ref[...]).astype(BF16)
    w = w_ref[0].astype(BF16)
    o_ref[0] = jnp.dot(a, w, preferred_element_type=F32) + b_ref[0]


def _modulation(c_all, w_cond, b_cond):
    depth, d, n = w_cond.shape
    rows = c_all.shape[0]
    tn = 1536
    return pl.pallas_call(
        _modulation_kernel,
        grid=(depth, n // tn),
        in_specs=[
            pl.BlockSpec((rows, d), lambda i, j: (0, 0)),
            pl.BlockSpec((1, d, tn), lambda i, j: (i, 0, j)),
            pl.BlockSpec((1, 1, tn), lambda i, j: (i, 0, j)),
        ],
        out_specs=pl.BlockSpec((1, rows, tn), lambda i, j: (i, 0, j)),
        out_shape=jax.ShapeDtypeStruct((depth, rows, n), F32),
        compiler_params=_params(2),
        name="modulation",
    )(c_all, w_cond, b_cond.reshape(depth, 1, n))


def _modmm_kernel(x_ref, sc_ref, sh_ref, w_ref, o_ref):
    bb, tt, d = x_ref.shape
    h = x_ref[...] * (1.0 + sc_ref[...]) + sh_ref[...]
    h = h.reshape(bb * tt, d).astype(BF16)
    acc = jnp.dot(h, w_ref[...], preferred_element_type=F32)
    o_ref[...] = acc.reshape(bb, tt, acc.shape[-1]).astype(o_ref.dtype)


def _modmm(x, mod, w, out_dtype, t_start=0):
    batch, seq, d = x.shape
    n = w.shape[1]
    seq_out = seq - t_start
    bb, tt = _row_blocks(batch, seq_out, ROW_TILE)
    assert t_start % tt == 0
    off = t_start // tt
    return pl.pallas_call(
        _modmm_kernel,
        grid=(batch // bb, seq_out // tt),
        in_specs=[
            pl.BlockSpec((bb, tt, d), lambda b, t: (b, t + off, 0)),
            _mod_spec(bb, 1, 2),
            _mod_spec(bb, 0, 2),
            pl.BlockSpec((d, n), lambda b, t: (0, 0)),
        ],
        out_specs=pl.BlockSpec((bb, tt, n), lambda b, t: (b, t, 0)),
        out_shape=jax.ShapeDtypeStruct((batch, seq_out, n), out_dtype),
        compiler_params=_params(2),
        name="a_project",
    )(x, mod, mod, w)


def _band_attn_kernel(*refs, n_kb, n_maybe_invalid):
    q_ref = refs[0]
    k_refs = refs[1:1 + n_kb]
    v_refs = refs[1 + n_kb:1 + 2 * n_kb]
    f_ref = refs[1 + 2 * n_kb]
    o_ref = refs[2 + 2 * n_kb]
    bias_ref = refs[3 + 2 * n_kb]
    q_tile = q_ref.shape[1]
    kb_sizes = [r.shape[1] for r in k_refs]
    k_tot = sum(kb_sizes)
    width = f_ref.shape[-1]
    i = pl.program_id(1)

    @pl.when((pl.program_id(0) == 0) & (i == 0))
    def _():
        rq = lax.broadcasted_iota(jnp.int32, (q_tile, k_tot), 0) // CHUNK
        ck = lax.broadcasted_iota(jnp.int32, (q_tile, k_tot), 1) // CHUNK
        in_band = (ck >= rq) & (ck <= rq + N_PAST_CHUNKS)
        for h in range(A_HEADS):
            rows = jnp.broadcast_to(f_ref[h], (q_tile, width))
            toep = pltpu.roll(rows, k_tot + 1, 1, stride=1, stride_axis=0)
            bias_ref[h] = jnp.where(in_band, toep[:, :k_tot], -jnp.inf)

    q = q_ref[0] * (HEAD_DIM ** -0.5)
    for h in range(A_HEADS):
        cols = slice(h * HEAD_DIM, (h + 1) * HEAD_DIM)
        qh = q[:, cols]
        parts = []
        for kb in range(n_kb):
            lg = lax.dot_general(qh, k_refs[kb][0, :, cols], (((1,), (1,)), ((), ())),
                                 preferred_element_type=F32)
            if kb < n_maybe_invalid:
                lg = jnp.where(i - n_maybe_invalid + kb >= 0, lg, -jnp.inf)
            parts.append(lg)
        logits = jnp.concatenate(parts, axis=1) + bias_ref[h]
        m = jnp.max(logits, axis=1, keepdims=True)
        p = jnp.exp(logits - m)
        l = jnp.sum(p, axis=1, keepdims=True)
        pb = p.astype(BF16)
        acc = jnp.zeros((q_tile, HEAD_DIM), F32)
        start = 0
        for kb in range(n_kb):
            acc = acc + jnp.dot(pb[:, start:start + kb_sizes[kb]], v_refs[kb][0, :, cols],
                                preferred_element_type=F32)
            start += kb_sizes[kb]
        o_ref[0, :, cols] = (acc / l).astype(o_ref.dtype)


def _bias_vector(table, q_tile, k_tot):
    width = k_tot + q_tile
    lo = -(q_tile - 1) - BAND_ROWS + REL_CLIP
    left = max(0, -lo)
    start = max(0, lo)
    n_mid = min(2 * REL_CLIP + 1 - start, width - left)
    right = width - left - n_mid
    mid = table.T[:, start:start + n_mid]
    f = jnp.pad(mid, ((0, 0), (left, right)), mode="edge")
    return f.reshape(A_HEADS, 1, width)


def _band_attn_prompt(qkv, table):
    batch, seq, _ = qkv.shape
    hd = A_HEADS * HEAD_DIM
    qt = A_Q_TILE
    n_prev = BAND_ROWS // qt
    n_kb = n_prev + 1
    k_tot = n_kb * qt
    f = _bias_vector(table, qt, k_tot)

    def kv_spec(kb, col):
        return pl.BlockSpec((1, qt, hd), lambda b, i: (b, jnp.maximum(i - n_prev + kb, 0), col))

    return pl.pallas_call(
        functools.partial(_band_attn_kernel, n_kb=n_kb, n_maybe_invalid=n_prev),
        grid=(batch, seq // qt),
        in_specs=[pl.BlockSpec((1, qt, hd), lambda b, i: (b, i, 0))]
        + [kv_spec(kb, 1) for kb in range(n_kb)]
        + [kv_spec(kb, 2) for kb in range(n_kb)]
        + [pl.BlockSpec(f.shape, lambda b, i: (0, 0, 0))],
        out_specs=pl.BlockSpec((1, qt, hd), lambda b, i: (b, i, 0)),
        out_shape=jax.ShapeDtypeStruct((batch, seq, hd), BF16),
        scratch_shapes=[pltpu.VMEM((A_HEADS, qt, k_tot), F32)],
        compiler_params=_params(2),
        name="band_attn_prompt",
    )(*([qkv] * (1 + 2 * n_kb)), f)


def _band_attn_sample(qkv, cache_k, cache_v, table):
    batch, seq, _ = qkv.shape
    hd = A_HEADS * HEAD_DIM
    win = cache_k.shape[1]
    assert seq == CHUNK and win == BAND_ROWS
    k_tot = win + seq
    f = _bias_vector(table, seq, k_tot)
    new = lambda col: pl.BlockSpec((1, seq, hd), lambda b, i: (b, 0, col))
    old = pl.BlockSpec((1, win, hd), lambda b, i: (b, 0, 0))
    return pl.pallas_call(
        functools.partial(_band_attn_kernel, n_kb=2, n_maybe_invalid=0),
        grid=(batch, 1),
        in_specs=[new(0), old, new(1), old, new(2), pl.BlockSpec(f.shape, lambda b, i: (0, 0, 0))],
        out_specs=pl.BlockSpec((1, seq, hd), lambda b, i: (b, 0, 0)),
        out_shape=jax.ShapeDtypeStruct((batch, seq, hd), BF16),
        scratch_shapes=[pltpu.VMEM((A_HEADS, seq, k_tot), F32)],
        compiler_params=_params(2),
        name="band_attn_sample",
    )(qkv, cache_k, qkv, cache_v, qkv, f)


def _mm_postnorm_kernel(o_ref, w_ref, x_ref, gate_ref, g_ref, b_ref, y_ref, *, alpha):
    bb, tt, d = x_ref.shape
    o = o_ref[...].reshape(bb * tt, o_ref.shape[-1])
    sub = jnp.dot(o, w_ref[...], preferred_element_type=F32).reshape(bb, tt, d)
    z = alpha * x_ref[...] + (1.0 + gate_ref[...]) * sub
    y_ref[...] = _layer_norm_rows(z, g_ref[...], b_ref[...])


def _mm_postnorm(o, w, x, mod, gate_chunk, g, b, alpha):
    batch, seq, d = x.shape
    bb, tt = _row_blocks(batch, seq, ROW_TILE)
    k = o.shape[-1]
    return pl.pallas_call(
        functools.partial(_mm_postnorm_kernel, alpha=alpha),
        grid=(batch // bb, seq // tt),
        in_specs=[
            pl.BlockSpec((bb, tt, k), lambda b_, t: (b_, t, 0)),
            pl.BlockSpec((k, d), lambda b_, t: (0, 0)),
            pl.BlockSpec((bb, tt, d), lambda b_, t: (b_, t, 0)),
            _mod_spec(bb, gate_chunk, 2),
            pl.BlockSpec((1, 1, d), lambda b_, t: (0, 0, 0)),
            pl.BlockSpec((1, 1, d), lambda b_, t: (0, 0, 0)),
        ],
        out_specs=pl.BlockSpec((bb, tt, d), lambda b_, t: (b_, t, 0)),
        out_shape=jax.ShapeDtypeStruct((batch, seq, d), F32),
        compiler_params=_params(2),
        name="out_proj_postnorm",
    )(o, w, x, mod, g.reshape(1, 1, d), b.reshape(1, 1, d))


def _ffn_kernel(x_ref, sc_ref, sh_ref, gate_ref, wr_ref, br_ref, wg_ref, wu_ref, wd_ref,
                g_ref, b_ref, y_ref, h_ref, comb_ref, acc_ref, *, routed, alpha):
    bb, tt, d = x_ref.shape
    rows = bb * tt
    e = pl.program_id(2)
    c = pl.program_id(3)
    lane = lax.broadcasted_iota(jnp.int32, (rows, LANES), 1)

    @pl.when((e == 0) & (c == 0))
    def _():
        h = (x_ref[...] * (1.0 + sc_ref[...]) + sh_ref[...]).reshape(rows, d)
        h_ref[...] = h.astype(BF16)
        acc_ref[...] = jnp.zeros_like(acc_ref)
        if routed:
            logits = jnp.dot(h, wr_ref[...], preferred_element_type=F32,
                             precision=lax.Precision.HIGHEST) + br_ref[...]
            logits = jnp.where(lane < N_EXPERTS, logits, -jnp.inf)
            m1 = jnp.max(logits, axis=1, keepdims=True)
            i1 = jnp.min(jnp.where(logits == m1, lane, LANES), axis=1, keepdims=True)
            rest = jnp.where(lane == i1, -jnp.inf, logits)
            m2 = jnp.max(rest, axis=1, keepdims=True)
            i2 = jnp.min(jnp.where(rest == m2, lane, LANES), axis=1, keepdims=True)
            e2 = jnp.exp(m2 - m1)
            den = 1.0 + e2
            comb_ref[...] = jnp.where(lane == i1, 1.0 / den, 0.0) + jnp.where(lane == i2, e2 / den, 0.0)

    h = h_ref[...]
    gt = jnp.dot(h, wg_ref[0], preferred_element_type=F32)
    up = jnp.dot(h, wu_ref[0], preferred_element_type=F32)
    act = (_silu(gt) * up).astype(BF16)
    part = jnp.dot(act, wd_ref[0], preferred_element_type=F32)
    if routed:
        ce = jnp.sum(jnp.where(lane == e, comb_ref[...], 0.0), axis=1, keepdims=True)
        part = ce * part
    acc_ref[...] += part

    @pl.when((e == pl.num_programs(2) - 1) & (c == pl.num_programs(3) - 1))
    def _():
        z = alpha * x_ref[...] + (1.0 + gate_ref[...]) * acc_ref[...].reshape(bb, tt, d)
        y_ref[...] = _layer_norm_rows(z, g_ref[...], b_ref[...])


def _ffn(x, mod, w_r, b_r, w_gu, w_down, g, b, *, routed, ff_chunk, alpha):
    batch, seq, d = x.shape
    n_e, ff, _ = w_down.shape
    assert ff % ff_chunk == 0
    n_c = ff // ff_chunk
    bb, tt = _row_blocks(batch, seq, FFN_ROW_TILE)
    rows = bb * tt
    vec = pl.BlockSpec((1, 1, d), lambda b_, t, e, c: (0, 0, 0))
    return pl.pallas_call(
        functools.partial(_ffn_kernel, routed=routed, alpha=alpha),
        grid=(batch // bb, seq // tt, n_e, n_c),
        in_specs=[
            pl.BlockSpec((bb, tt, d), lambda b_, t, e, c: (b_, t, 0)),
            _mod_spec(bb, 4, 4),
            _mod_spec(bb, 3, 4),
            _mod_spec(bb, 5, 4),
            pl.BlockSpec((d, LANES), lambda b_, t, e, c: (0, 0)),
            pl.BlockSpec((1, LANES), lambda b_, t, e, c: (0, 0)),
            pl.BlockSpec((1, d, ff_chunk), lambda b_, t, e, c: (e, 0, c)),
            pl.BlockSpec((1, d, ff_chunk), lambda b_, t, e, c: (e, 0, n_c + c)),
            pl.BlockSpec((1, ff_chunk, d), lambda b_, t, e, c: (e, c, 0)),
            vec,
            vec,
        ],
        out_specs=pl.BlockSpec((bb, tt, d), lambda b_, t, e, c: (b_, t, 0)),
        out_shape=jax.ShapeDtypeStruct((batch, seq, d), F32),
        scratch_shapes=[
            pltpu.VMEM((rows, d), BF16),
            pltpu.VMEM((rows, LANES), F32),
            pltpu.VMEM((rows, d), F32),
        ],
        compiler_params=_params(4),
        name="moe_ffn" if routed else "dense_ffn",
    )(x, mod, mod, mod, w_r, b_r, w_gu, w_gu, w_down, g.reshape(1, 1, d), b.reshape(1, 1, d))


def _rope_lanes(seg, cos, sin_signed, first_half):
    width = seg.shape[-1]
    reps = width // LANES
    if reps > 1:
        cos = jnp.concatenate([cos] * reps, axis=1)
        sin_signed = jnp.concatenate([sin_signed] * reps, axis=1)
        first_half = jnp.concatenate([first_half] * reps, axis=1)
    half = HEAD_DIM // 2
    swapped = jnp.where(first_half, pltpu.roll(seg, width - half, 1), pltpu.roll(seg, half, 1))
    return seg * cos + swapped * sin_signed


def _b_project_kernel(x_ref, sc_ref, sh_ref, w_ref, cos_ref, sin_ref, kg_ref, kb_ref,
                      q_ref, qi_ref, k_ref, v_ref, kw_ref):
    bb, tt, d = x_ref.shape
    rows = bb * tt
    h = x_ref[...] * (1.0 + sc_ref[...]) + sh_ref[...]
    h = h.reshape(rows, d).astype(BF16)
    acc = jnp.dot(h, w_ref[...], preferred_element_type=F32)
    cos = cos_ref[...]
    sin_signed = sin_ref[...]
    lane = lax.broadcasted_iota(jnp.int32, (1, LANES), 1)
    first_half = (lane % HEAD_DIM) < (HEAD_DIM // 2)

    def out(ref, val):
        ref[...] = val.reshape(bb, tt, val.shape[-1]).astype(ref.dtype)

    o_k, o_v, o_qi, o_ki = B_Q, B_Q + B_KV, B_Q + 2 * B_KV, B_Q + 2 * B_KV + B_QI
    out(q_ref, _rope_lanes(acc[:, :o_k], cos, sin_signed, first_half) * QK_SCALE_LOG2)
    out(k_ref, _rope_lanes(acc[:, o_k:o_v], cos, sin_signed, first_half))
    out(v_ref, acc[:, o_v:o_qi])
    out(qi_ref, _rope_lanes(acc[:, o_qi:o_ki], cos, sin_signed, first_half))
    seg = acc[:, o_ki:]
    is_ki = lane < IDX_DIM
    mu = jnp.sum(jnp.where(is_ki, seg, 0.0), axis=1, keepdims=True) / IDX_DIM
    cen = jnp.where(is_ki, seg - mu, 0.0)
    var = jnp.sum(cen * cen, axis=1, keepdims=True) / IDX_DIM
    ki = cen * lax.rsqrt(var + LN_EPS) * kg_ref[...] + kb_ref[...]
    ki = _rope_lanes(ki, cos, sin_signed, first_half)
    out(kw_ref, jnp.where(is_ki, ki, seg * (IDX_HEADS ** -0.5)))


def _b_project(x, mod, w, cos, sin_signed, kn_g, kn_b, table_per_tile):
    batch, seq, d = x.shape
    bb, tt = _row_blocks(batch, seq, ROW_TILE)
    rows = bb * tt
    n = w.shape[1]
    tab = pl.BlockSpec((rows, LANES), (lambda b, t: (0, 0)) if table_per_tile else (lambda b, t: (t, 0)))
    vec = pl.BlockSpec((1, LANES), lambda b, t: (0, 0))

    def o_spec(width):
        return pl.BlockSpec((bb, tt, width), lambda b, t: (b, t, 0))

    def o_shape(width, dtype):
        return jax.ShapeDtypeStruct((batch, seq, width), dtype)

    return pl.pallas_call(
        _b_project_kernel,
        grid=(batch // bb, seq // tt),
        in_specs=[
            pl.BlockSpec((bb, tt, d), lambda b, t: (b, t, 0)),
            _mod_spec(bb, 1, 2),
            _mod_spec(bb, 0, 2),
            pl.BlockSpec((d, n), lambda b, t: (0, 0)),
            tab,
            tab,
            vec,
            vec,
        ],
        out_specs=[o_spec(B_Q), o_spec(B_QI), o_spec(B_KV), o_spec(B_KV), o_spec(LANES)],
        out_shape=[o_shape(B_Q, BF16), o_shape(B_QI, BF16), o_shape(B_KV, F32), o_shape(B_KV, F32),
                   o_shape(LANES, F32)],
        compiler_params=_params(2),
        name="b_project",
    )(x, mod, mod, w, cos, sin_signed, kn_g, kn_b)


def _rope_tables(pos):
    half = HEAD_DIM // 2
    inv = ROPE_THETA ** (-jnp.arange(half, dtype=F32) / half)
    ang = pos.astype(F32)[:, None] * inv[None, :]
    cos, sin = jnp.cos(ang), jnp.sin(ang)
    return (jnp.concatenate([cos, cos, cos, cos], axis=1),
            jnp.concatenate([-sin, sin, -sin, sin], axis=1))


def _dsa_kernel(qt_ref, qit_ref, kwt_ref, ki_ref, k_ref, vta_ref, o_ref, s_ref, qpad_ref, m_ref, acc_ref,
                *, causal, n_keys, k_sel):
    qn = qt_ref.shape[2]
    lt = DSA_KEY_TILE
    i = pl.program_id(1)
    if causal:
        n_tiles = ((i + 1) * qn + lt - 1) // lt
        qpos = i * qn + lax.broadcasted_iota(jnp.int32, (1, qn), 1)
        limit = (qpos // CHUNK + 1) * CHUNK
    else:
        n_tiles = ki_ref.shape[1] // lt
        limit = jnp.full((1, qn), n_keys, jnp.int32)

    def key_slice(t):
        return pl.ds(pl.multiple_of(t * lt, lt), lt)

    def key_pos(t):
        return t * lt + lax.broadcasted_iota(jnp.int32, (lt, 1), 0)

    def fold_rows(x, op):
        return op(op(x.reshape(lt // 32, 4, 8, x.shape[-1]), axis=0), axis=0)

    qit = qit_ref[0]
    kwt = kwt_ref[0]
    qi_w = jnp.concatenate([qit[h * IDX_DIM:(h + 1) * IDX_DIM, :] for h in range(IDX_HEADS)], axis=1)
    w_row = jnp.concatenate([kwt[IDX_DIM + h:IDX_DIM + h + 1, :] for h in range(IDX_HEADS)], axis=1)
    w_row = w_row * (IDX_DIM ** -0.5)

    def score_body(t, carry):
        rmax, rmin = carry
        s = jnp.dot(ki_ref[0, key_slice(t), :], qi_w, preferred_element_type=F32)
        s = jnp.maximum(s, 0.0) * w_row
        sc = s[:, 0:qn]
        for h in range(1, IDX_HEADS):
            sc = sc + s[:, h * qn:(h + 1) * qn]
        adm = key_pos(t) < limit
        s_ref[key_slice(t), :] = jnp.where(adm, sc, -jnp.inf)
        rmax = jnp.maximum(rmax, jnp.max(jnp.where(adm, sc, -jnp.inf), axis=0, keepdims=True))
        rmin = jnp.minimum(rmin, jnp.min(jnp.where(adm, sc, jnp.inf), axis=0, keepdims=True))
        return rmax, rmin

    rmax, rmin = lax.fori_loop(0, n_tiles, score_body,
                               (jnp.full((1, qn), -jnp.inf, F32), jnp.full((1, qn), jnp.inf, F32)))

    def count(pred):
        def body(t, acc):
            return acc + fold_rows(jnp.where(pred(s_ref[key_slice(t), :], key_pos(t)), 1.0, 0.0), jnp.sum)
        acc = lax.fori_loop(0, n_tiles, body, jnp.zeros((8, qn), F32))
        return jnp.sum(acc, axis=0, keepdims=True)

    n_adm = limit.astype(F32)
    target = jnp.minimum(n_adm, float(k_sel))
    hi0 = jnp.where(rmax > 0, 2.0 * rmax, jnp.where(rmax < 0, 0.5 * rmax, 1.0))

    def bis_cond(st):
        return jnp.sum(st[3]) < qn

    def bis_body(st):
        lo, hi, c_lo, done = st
        mid = 0.5 * lo + 0.5 * hi
        stuck = (mid <= lo) | (mid >= hi)
        c = count(lambda s, _: s >= mid)
        live = (done < 0.5) & jnp.logical_not(stuck)
        up = live & (c >= target)
        dn = live & (c < target)
        lo = jnp.where(up, mid, lo)
        c_lo = jnp.where(up, c, c_lo)
        hi = jnp.where(dn, mid, hi)
        done = jnp.where(stuck | (c_lo == target), 1.0, done)
        return lo, hi, c_lo, done

    done0 = jnp.where(n_adm == target, 1.0, 0.0)
    thr, _, c_thr, _ = lax.while_loop(bis_cond, bis_body, (rmin, hi0, n_adm, done0))

    tied = c_thr > target

    @pl.when(jnp.sum(jnp.where(tied, 1.0, 0.0)) > 0)
    def _():
        need = target - count(lambda s, _: s > thr)

        def idx_body(_, st):
            lo_j, hi_j = st
            mid = (lo_j + hi_j) // 2
            c = count(lambda s, kp: (s == thr) & (kp <= mid))
            return jnp.where(c < need, mid, lo_j), jnp.where(c >= need, mid, hi_j)

        last = jnp.full((1, qn), n_tiles * lt - 1, jnp.int32)
        n_steps = (ki_ref.shape[1] - 1).bit_length() + 1
        _, last = lax.fori_loop(0, n_steps, idx_body, (jnp.full((1, qn), -1, jnp.int32), last))

        def drop_body(t, carry):
            s = s_ref[key_slice(t), :]
            s_ref[key_slice(t), :] = jnp.where(tied & (s == thr) & (key_pos(t) > last), -jnp.inf, s)
            return carry

        lax.fori_loop(0, n_tiles, drop_body, 0)

    qt = qt_ref[0]
    zeros = jnp.zeros((HEAD_DIM, B_GROUP * qn), BF16)
    for g in range(B_KV_HEADS):
        q_g = jnp.concatenate(
            [qt[(g * B_GROUP + j) * HEAD_DIM:(g * B_GROUP + j + 1) * HEAD_DIM, :] for j in range(B_GROUP)],
            axis=1)
        qpad_ref[g] = jnp.concatenate([zeros] * g + [q_g] + [zeros] * (B_KV_HEADS - 1 - g), axis=0)
    m_ref[...] = jnp.full(m_ref.shape, NEG_BIG, F32)
    acc_ref[...] = jnp.zeros(acc_ref.shape, F32)

    def attn_body(t, carry):
        keys = k_ref[0, key_slice(t), :]
        mb = jnp.where(s_ref[key_slice(t), :] >= thr, 0.0, -jnp.inf)
        mb = jnp.concatenate([mb] * B_GROUP, axis=1)
        for g in range(B_KV_HEADS):
            lg = jnp.dot(keys, qpad_ref[g], preferred_element_type=F32) + mb
            m_old = m_ref[g]
            m_new = jnp.maximum(m_old, jnp.max(fold_rows(lg, jnp.max), axis=0, keepdims=True))
            p = jnp.exp2(lg - m_new).astype(BF16)
            pv = jnp.dot(vta_ref[0, g, :, key_slice(t)], p, preferred_element_type=F32)
            acc_ref[g] = jnp.exp2(m_old - m_new) * acc_ref[g] + pv
            m_ref[g] = m_new
        return carry

    lax.fori_loop(0, n_tiles, attn_body, 0)
    for g in range(B_KV_HEADS):
        acc = acc_ref[g]
        o_g = acc[:HEAD_DIM] / acc[HEAD_DIM:HEAD_DIM + 1]
        for j in range(B_GROUP):
            hh = g * B_GROUP + j
            o_ref[0, hh * HEAD_DIM:(hh + 1) * HEAD_DIM, :] = o_g[:, j * qn:(j + 1) * qn].astype(o_ref.dtype)


V_AUG_ROWS = HEAD_DIM + 16


def _dsa(q, qi, kw, k, v, ki, *, q_tile, causal, k_sel):
    batch, seq, _ = q.shape
    n_keys = k.shape[1]
    lp = -(-n_keys // DSA_KEY_TILE) * DSA_KEY_TILE
    assert seq % q_tile == 0 and q_tile % LANES == 0
    pad = ((0, 0), (0, lp - n_keys), (0, 0))
    k, v, ki = (jnp.pad(a.astype(BF16), pad) for a in (k, v, ki))
    vt = jnp.swapaxes(v, 1, 2).reshape(batch, B_KV_HEADS, HEAD_DIM, lp)
    vta = jnp.concatenate([vt, jnp.ones((batch, B_KV_HEADS, 1, lp), BF16),
                           jnp.zeros((batch, B_KV_HEADS, V_AUG_ROWS - HEAD_DIM - 1, lp), BF16)], axis=2)
    qt, qit, kwt = (jnp.swapaxes(a, 1, 2) for a in (q, qi, kw))

    def qspec(rows):
        return pl.BlockSpec((1, rows, q_tile), lambda b, i: (b, 0, i))

    def resident(shape):
        zero = (0,) * len(shape)
        return pl.BlockSpec((1,) + shape, lambda b, i: (b,) + zero, pipeline_mode=pl.Buffered(1))

    ot = pl.pallas_call(
        functools.partial(_dsa_kernel, causal=causal, n_keys=n_keys, k_sel=k_sel),
        grid=(batch, seq // q_tile),
        in_specs=[qspec(B_Q), qspec(B_QI), qspec(LANES),
                  resident((lp, IDX_DIM)), resident((lp, B_KV)),
                  resident((B_KV_HEADS, V_AUG_ROWS, lp))],
        out_specs=qspec(B_Q),
        out_shape=jax.ShapeDtypeStruct((batch, B_Q, seq), BF16),
        scratch_shapes=[pltpu.VMEM((lp, q_tile), F32),
                        pltpu.VMEM((B_KV_HEADS, B_KV, B_GROUP * q_tile), BF16),
                        pltpu.VMEM((B_KV_HEADS, 1, B_GROUP * q_tile), F32),
                        pltpu.VMEM((B_KV_HEADS, V_AUG_ROWS, B_GROUP * q_tile), F32)],
        compiler_params=_params(2),
        name="dsa_prompt" if causal else "dsa_sample",
    )(qt, qit, kwt, ki, k, vta)
    return jnp.swapaxes(ot, 1, 2)


def kernel(x_prompt, x_sample, cache_k_a, cache_v_a, cache_k_b, cache_v_b, cache_kidx_b,
           c_prompt, c_sample, w_cond, b_cond, ln_g, ln_b, a_w_in, a_w_o, a_rel_bias,
           b_w_in, b_w_o, b_kidx_ln_g, b_kidx_ln_b, ffn_w_gu, ffn_w_down,
           moe_w_router, moe_b_router, moe_w_gu, moe_w_down):
    depth = w_cond.shape[0]
    alpha = (2 * depth) ** 0.25
    n_p, seq, d = x_prompt.shape
    n_s, dec_seq, _ = x_sample.shape
    past = cache_k_b.shape[2]
    a_hd = A_HEADS * HEAD_DIM

    rows = n_p + n_s
    rows_pad = -(-rows // 8) * 8
    c_all = jnp.pad(jnp.concatenate([c_prompt, c_sample], axis=0), ((0, rows_pad - rows), (0, 0)))
    mod_all = _modulation(c_all, w_cond, b_cond)

    xp, xs = x_prompt, x_sample
    outs = {k: [] for k in ("ka_p", "va_p", "kb_p", "vb_p", "ib_p", "ka_s", "va_s", "kb_s", "vb_s", "ib_s")}
    dummy_wr = jnp.zeros((d, LANES), F32)
    dummy_br = jnp.zeros((1, LANES), F32)
    for i in range(depth):
        j = i // 2
        mod_p = mod_all[i, :n_p].reshape(n_p, 1, 6 * d)
        mod_s = mod_all[i, n_p:rows].reshape(n_s, 1, 6 * d)
        g1, b1, g2, b2 = ln_g[i, 0], ln_b[i, 0], ln_g[i, 1], ln_b[i, 1]
        if i % 2 == 0:
            w_in = a_w_in[j].astype(BF16)
            keep = min(BAND_ROWS, seq)
            qkv_p = _modmm(xp, mod_p, w_in, BF16)
            kv_p = _modmm(xp, mod_p, w_in[:, a_hd:], F32, t_start=seq - keep)
            qkv_s = _modmm(xs, mod_s, w_in, BF16)
            kv_s = _modmm(xs, mod_s, w_in[:, a_hd:], F32)
            outs["ka_p"].append(kv_p[..., :a_hd].reshape(n_p, keep, A_HEADS, HEAD_DIM))
            outs["va_p"].append(kv_p[..., a_hd:].reshape(n_p, keep, A_HEADS, HEAD_DIM))
            outs["ka_s"].append(kv_s[..., :a_hd].reshape(n_s, dec_seq, A_HEADS, HEAD_DIM))
            outs["va_s"].append(kv_s[..., a_hd:].reshape(n_s, dec_seq, A_HEADS, HEAD_DIM))
            o_p = _band_attn_prompt(qkv_p, a_rel_bias[j])
            ck = cache_k_a[j].reshape(n_s, -1, a_hd).astype(BF16)
            cv = cache_v_a[j].reshape(n_s, -1, a_hd).astype(BF16)
            o_s = _band_attn_sample(qkv_s, ck, cv, a_rel_bias[j])
            w_o = a_w_o[j].astype(BF16)
        else:
            w_in = jnp.pad(b_w_in[j], ((0, 0), (0, B_PROJ_PAD - B_PROJ))).astype(BF16)
            kn_g = jnp.pad(b_kidx_ln_g[j], (0, LANES - IDX_DIM)).reshape(1, LANES)
            kn_b = jnp.pad(b_kidx_ln_b[j], (0, LANES - IDX_DIM)).reshape(1, LANES)
            cos_p, sin_p = _rope_tables(jnp.arange(seq))
            cos_s, sin_s = _rope_tables(past + jnp.arange(dec_seq))
            bb_s, _ = _row_blocks(n_s, dec_seq, ROW_TILE)
            cos_s, sin_s = jnp.tile(cos_s, (bb_s, 1)), jnp.tile(sin_s, (bb_s, 1))
            q_p, qi_p, k_p, v_p, kw_p = _b_project(xp, mod_p, w_in, cos_p, sin_p, kn_g, kn_b, False)
            q_s, qi_s, k_s, v_s, kw_s = _b_project(xs, mod_s, w_in, cos_s, sin_s, kn_g, kn_b, True)
            outs["kb_p"].append(k_p.reshape(n_p, seq, B_KV_HEADS, HEAD_DIM))
            outs["vb_p"].append(v_p.reshape(n_p, seq, B_KV_HEADS, HEAD_DIM))
            outs["ib_p"].append(kw_p[..., :IDX_DIM])
            outs["kb_s"].append(k_s.reshape(n_s, dec_seq, B_KV_HEADS, HEAD_DIM))
            outs["vb_s"].append(v_s.reshape(n_s, dec_seq, B_KV_HEADS, HEAD_DIM))
            outs["ib_s"].append(kw_s[..., :IDX_DIM])
            o_p = _dsa(q_p, qi_p, kw_p, k_p, v_p, kw_p[..., :IDX_DIM], q_tile=DSA_Q_TILE, causal=True,
                       k_sel=min(TOPK_MAX, seq // 4))
            n_keys = past + dec_seq
            kk = jnp.concatenate([cache_k_b[j].reshape(n_s, past, B_KV), k_s], axis=1)
            vc = jnp.concatenate([cache_v_b[j].reshape(n_s, past, B_KV), v_s], axis=1)
            kki = jnp.concatenate([cache_kidx_b[j], kw_s[..., :IDX_DIM]], axis=1)
            rep = LANES // dec_seq
            q_r, qi_r, kw_r = (jnp.concatenate([a] * rep, axis=1) for a in (q_s, qi_s, kw_s))
            o_s = _dsa(q_r, qi_r, kw_r, kk, vc, kki, q_tile=LANES, causal=False,
                       k_sel=min(TOPK_MAX, n_keys // 4))[:, :dec_seq]
            w_o = b_w_o[j].astype(BF16)
        xp = _mm_postnorm(o_p, w_o, xp, mod_p, 2, g1, b1, alpha)
        xs = _mm_postnorm(o_s, w_o, xs, mod_s, 2, g1, b1, alpha)
        if i % 2 == 0:
            w_gu = ffn_w_gu[j].astype(BF16)[None]
            w_dn = ffn_w_down[j].astype(BF16)[None]
            dense = dict(routed=False, ff_chunk=D_FF // 2, alpha=alpha)
            xp = _ffn(xp, mod_p, dummy_wr, dummy_br, w_gu, w_dn, g2, b2, **dense)
            xs = _ffn(xs, mod_s, dummy_wr, dummy_br, w_gu, w_dn, g2, b2, **dense)
        else:
            w_gu = moe_w_gu[j].astype(BF16)
            w_dn = moe_w_down[j].astype(BF16)
            w_r = jnp.pad(moe_w_router[j], ((0, 0), (0, LANES - N_EXPERTS)))
            b_r = jnp.pad(moe_b_router[j], (0, LANES - N_EXPERTS)).reshape(1, LANES)
            moe = dict(routed=True, ff_chunk=D_FF_EXPERT // 4, alpha=alpha)
            xp = _ffn(xp, mod_p, w_r, b_r, w_gu, w_dn, g2, b2, **moe)
            xs = _ffn(xs, mod_s, w_r, b_r, w_gu, w_dn, g2, b2, **moe)

    st = lambda name: jnp.stack(outs[name])
    return (xp, xs, st("ka_p"), st("va_p"), st("kb_p"), st("vb_p"), st("ib_p"),
            st("ka_s"), st("va_s"), st("kb_s"), st("vb_s"), st("ib_s"))
```

```python
import functools

import jax
import jax.numpy as jnp
from jax import lax
from jax.experimental import pallas as pl
from jax.experimental.pallas import tpu as pltpu

F32 = jnp.float32
BF16 = jnp.bfloat16

D_MODEL = 1024
CHUNK = 64
N_PAST_CHUNKS = 8
BAND_ROWS = N_PAST_CHUNKS * CHUNK
REL_CLIP = 2 * CHUNK
HEAD_DIM = 64
A_HEADS = 16
B_HEADS = 16
B_KV_HEADS = 4
B_GROUP = B_HEADS // B_KV_HEADS
IDX_HEADS = 8
IDX_DIM = 64
TOPK_MAX = 256
D_FF = 2816
N_EXPERTS = 8
D_FF_EXPERT = 3584
ROPE_THETA = 10000.0
LN_EPS = 1e-5
B_Q = B_HEADS * HEAD_DIM
B_KV = B_KV_HEADS * HEAD_DIM
B_QI = IDX_HEADS * IDX_DIM
B_PROJ = B_Q + 2 * B_KV + B_QI + IDX_DIM + IDX_HEADS

LANES = 128
VMEM_LIMIT_BYTES = 58 * 1024 * 1024

A_Q_TILE = 4 * CHUNK
DSA_Q_TILE = 4 * CHUNK
DSA_KEY_TILE = 512
ROW_TILE = 512
FFN_ROW_TILE = 1024
B_PROJ_PAD = B_Q + 2 * B_KV + B_QI + LANES
NEG_BIG = -1e30
QK_SCALE_LOG2 = HEAD_DIM ** -0.5 * 1.4426950408889634
BOUND_SLACK = 1.02
BOUND_SLACK_ABS = 0.01
MIN_TRUSTED_DENOMINATOR = 2.0 ** -100


def _params(n_grid):
    return pltpu.CompilerParams(
        dimension_semantics=("arbitrary",) * n_grid,
        vmem_limit_bytes=VMEM_LIMIT_BYTES,
    )


def _row_blocks(batch, seq, target):
    if seq >= target:
        assert seq % target == 0
        return 1, target
    bb = max(1, min(batch, target // seq))
    while batch % bb:
        bb -= 1
    return bb, seq


def _mod_spec(bb, chunk, n_grid):
    if n_grid == 2:
        return pl.BlockSpec((bb, 1, D_MODEL), lambda b, t: (b, 0, chunk))
    return pl.BlockSpec((bb, 1, D_MODEL), lambda b, t, e, c: (b, 0, chunk))


def _silu(x):
    return x / (1.0 + jnp.exp(-x))


def _layer_norm_rows(z, g, b):
    mu = jnp.mean(z, axis=-1, keepdims=True)
    zc = z - mu
    var = jnp.mean(zc * zc, axis=-1, keepdims=True)
    return zc * lax.rsqrt(var + LN_EPS) * g + b


def _modulation_kernel(c_ref, w_ref, b_ref, o_ref):
    a = _silu(c_ref[...]).astype(BF16)
    w = w_ref[0].astype(BF16)
    o_ref[0] = jnp.dot(a, w, preferred_element_type=F32) + b_ref[0]


def _modulation(c_all, w_cond, b_cond):
    depth, d, n = w_cond.shape
    rows = c_all.shape[0]
    tn = 1536
    return pl.pallas_call(
        _modulation_kernel,
        grid=(depth, n // tn),
        in_specs=[
            pl.BlockSpec((rows, d), lambda i, j: (0, 0)),
            pl.BlockSpec((1, d, tn), lambda i, j: (i, 0, j)),
            pl.BlockSpec((1, 1, tn), lambda i, j: (i, 0, j)),
        ],
        out_specs=pl.BlockSpec((1, rows, tn), lambda i, j: (i, 0, j)),
        out_shape=jax.ShapeDtypeStruct((depth, rows, n), F32),
        compiler_params=_params(2),
        name="modulation",
    )(c_all, w_cond, b_cond.reshape(depth, 1, n))


def _modmm_kernel(x_ref, sc_ref, sh_ref, w_ref, o_ref):
    bb, tt, d = x_ref.shape
    h = x_ref[...] * (1.0 + sc_ref[...]) + sh_ref[...]
    h = h.reshape(bb * tt, d).astype(BF16)
    acc = jnp.dot(h, w_ref[...], preferred_element_type=F32)
    o_ref[...] = acc.reshape(bb, tt, acc.shape[-1]).astype(o_ref.dtype)


def _modmm(x, mod, w, out_dtype, t_start=0):
    batch, seq, d = x.shape
    n = w.shape[1]
    seq_out = seq - t_start
    bb, tt = _row_blocks(batch, seq_out, ROW_TILE)
    assert t_start % tt == 0
    off = t_start // tt
    return pl.pallas_call(
        _modmm_kernel,
        grid=(batch // bb, seq_out // tt),
        in_specs=[
            pl.BlockSpec((bb, tt, d), lambda b, t: (b, t + off, 0)),
            _mod_spec(bb, 1, 2),
            _mod_spec(bb, 0, 2),
            pl.BlockSpec((d, n), lambda b, t: (0, 0)),
        ],
        out_specs=pl.BlockSpec((bb, tt, n), lambda b, t: (b, t, 0)),
        out_shape=jax.ShapeDtypeStruct((batch, seq_out, n), out_dtype),
        compiler_params=_params(2),
        name="a_project",
    )(x, mod, mod, w)


def _band_attn_kernel(*refs, n_kb, n_maybe_invalid):
    q_ref = refs[0]
    k_refs = refs[1:1 + n_kb]
    v_refs = refs[1 + n_kb:1 + 2 * n_kb]
    f_ref = refs[1 + 2 * n_kb]
    o_ref = refs[2 + 2 * n_kb]
    bias_ref = refs[3 + 2 * n_kb]
    q_tile = q_ref.shape[1]
    kb_sizes = [r.shape[1] for r in k_refs]
    k_tot = sum(kb_sizes)
    width = f_ref.shape[-1]
    i = pl.program_id(1)

    @pl.when((pl.program_id(0) == 0) & (i == 0))
    def _():
        rq = lax.broadcasted_iota(jnp.int32, (q_tile, k_tot), 0) // CHUNK
        ck = lax.broadcasted_iota(jnp.int32, (q_tile, k_tot), 1) // CHUNK
        in_band = (ck >= rq) & (ck <= rq + N_PAST_CHUNKS)
        for h in range(A_HEADS):
            rows = jnp.broadcast_to(f_ref[h], (q_tile, width))
            toep = pltpu.roll(rows, k_tot + 1, 1, stride=1, stride_axis=0)
            bias_ref[h] = jnp.where(in_band, toep[:, :k_tot], -jnp.inf)

    q = q_ref[0] * (HEAD_DIM ** -0.5)
    for h in range(A_HEADS):
        cols = slice(h * HEAD_DIM, (h + 1) * HEAD_DIM)
        qh = q[:, cols]
        parts = []
        for kb in range(n_kb):
            lg = lax.dot_general(qh, k_refs[kb][0, :, cols], (((1,), (1,)), ((), ())),
                                 preferred_element_type=F32)
            if kb < n_maybe_invalid:
                lg = jnp.where(i - n_maybe_invalid + kb >= 0, lg, -jnp.inf)
            parts.append(lg)
        logits = jnp.concatenate(parts, axis=1) + bias_ref[h]
        m = jnp.max(logits, axis=1, keepdims=True)
        p = jnp.exp(logits - m)
        l = jnp.sum(p, axis=1, keepdims=True)
        pb = p.astype(BF16)
        acc = jnp.zeros((q_tile, HEAD_DIM), F32)
        start = 0
        for kb in range(n_kb):
            acc = acc + jnp.dot(pb[:, start:start + kb_sizes[kb]], v_refs[kb][0, :, cols],
                                preferred_element_type=F32)
            start += kb_sizes[kb]
        o_ref[0, :, cols] = (acc / l).astype(o_ref.dtype)


def _bias_vector(table, q_tile, k_tot):
    width = k_tot + q_tile
    lo = -(q_tile - 1) - BAND_ROWS + REL_CLIP
    left = max(0, -lo)
    start = max(0, lo)
    n_mid = min(2 * REL_CLIP + 1 - start, width - left)
    right = width - left - n_mid
    mid = table.T[:, start:start + n_mid]
    f = jnp.pad(mid, ((0, 0), (left, right)), mode="edge")
    return f.reshape(A_HEADS, 1, width)


def _band_attn_prompt(qkv, table):
    batch, seq, _ = qkv.shape
    hd = A_HEADS * HEAD_DIM
    qt = A_Q_TILE
    n_prev = BAND_ROWS // qt
    n_kb = n_prev + 1
    k_tot = n_kb * qt
    f = _bias_vector(table, qt, k_tot)

    def kv_spec(kb, col):
        return pl.BlockSpec((1, qt, hd), lambda b, i: (b, jnp.maximum(i - n_prev + kb, 0), col))

    return pl.pallas_call(
        functools.partial(_band_attn_kernel, n_kb=n_kb, n_maybe_invalid=n_prev),
        grid=(batch, seq // qt),
        in_specs=[pl.BlockSpec((1, qt, hd), lambda b, i: (b, i, 0))]
        + [kv_spec(kb, 1) for kb in range(n_kb)]
        + [kv_spec(kb, 2) for kb in range(n_kb)]
        + [pl.BlockSpec(f.shape, lambda b, i: (0, 0, 0))],
        out_specs=pl.BlockSpec((1, qt, hd), lambda b, i: (b, i, 0)),
        out_shape=jax.ShapeDtypeStruct((batch, seq, hd), BF16),
        scratch_shapes=[pltpu.VMEM((A_HEADS, qt, k_tot), F32)],
        compiler_params=_params(2),
        name="band_attn_prompt",
    )(*([qkv] * (1 + 2 * n_kb)), f)


def _band_attn_sample(qkv, cache_k, cache_v, table):
    batch, seq, _ = qkv.shape
    hd = A_HEADS * HEAD_DIM
    win = cache_k.shape[1]
    assert seq == CHUNK and win == BAND_ROWS
    k_tot = win + seq
    f = _bias_vector(table, seq, k_tot)
    new = lambda col: pl.BlockSpec((1, seq, hd), lambda b, i: (b, 0, col))
    old = pl.BlockSpec((1, win, hd), lambda b, i: (b, 0, 0))
    return pl.pallas_call(
        functools.partial(_band_attn_kernel, n_kb=2, n_maybe_invalid=0),
        grid=(batch, 1),
        in_specs=[new(0), old, new(1), old, new(2), pl.BlockSpec(f.shape, lambda b, i: (0, 0, 0))],
        out_specs=pl.BlockSpec((1, seq, hd), lambda b, i: (b, 0, 0)),
        out_shape=jax.ShapeDtypeStruct((batch, seq, hd), BF16),
        scratch_shapes=[pltpu.VMEM((A_HEADS, seq, k_tot), F32)],
        compiler_params=_params(2),
        name="band_attn_sample",
    )(qkv, cache_k, qkv, cache_v, qkv, f)


def _mm_postnorm_kernel(o_ref, w_ref, x_ref, gate_ref, g_ref, b_ref, y_ref, *, alpha):
    bb, tt, d = x_ref.shape
    o = o_ref[...].reshape(bb * tt, o_ref.shape[-1])
    sub = jnp.dot(o, w_ref[...], preferred_element_type=F32).reshape(bb, tt, d)
    z = alpha * x_ref[...] + (1.0 + gate_ref[...]) * sub
    y_ref[...] = _layer_norm_rows(z, g_ref[...], b_ref[...])


def _mm_postnorm(o, w, x, mod, gate_chunk, g, b, alpha):
    batch, seq, d = x.shape
    bb, tt = _row_blocks(batch, seq, ROW_TILE)
    k = o.shape[-1]
    return pl.pallas_call(
        functools.partial(_mm_postnorm_kernel, alpha=alpha),
        grid=(batch // bb, seq // tt),
        in_specs=[
            pl.BlockSpec((bb, tt, k), lambda b_, t: (b_, t, 0)),
            pl.BlockSpec((k, d), lambda b_, t: (0, 0)),
            pl.BlockSpec((bb, tt, d), lambda b_, t: (b_, t, 0)),
            _mod_spec(bb, gate_chunk, 2),
            pl.BlockSpec((1, 1, d), lambda b_, t: (0, 0, 0)),
            pl.BlockSpec((1, 1, d), lambda b_, t: (0, 0, 0)),
        ],
        out_specs=pl.BlockSpec((bb, tt, d), lambda b_, t: (b_, t, 0)),
        out_shape=jax.ShapeDtypeStruct((batch, seq, d), F32),
        compiler_params=_params(2),
        name="out_proj_postnorm",
    )(o, w, x, mod, g.reshape(1, 1, d), b.reshape(1, 1, d))


def _ffn_kernel(x_ref, sc_ref, sh_ref, gate_ref, wr_ref, br_ref, wg_ref, wu_ref, wd_ref,
                g_ref, b_ref, y_ref, h_ref, comb_ref, acc_ref, *, routed, alpha):
    bb, tt, d = x_ref.shape
    rows = bb * tt
    e = pl.program_id(2)
    c = pl.program_id(3)
    lane = lax.broadcasted_iota(jnp.int32, (rows, LANES), 1)

    @pl.when((e == 0) & (c == 0))
    def _():
        h = (x_ref[...] * (1.0 + sc_ref[...]) + sh_ref[...]).reshape(rows, d)
        h_ref[...] = h.astype(BF16)
        acc_ref[...] = jnp.zeros_like(acc_ref)
        if routed:
            logits = jnp.dot(h, wr_ref[...], preferred_element_type=F32,
                             precision=lax.Precision.HIGHEST) + br_ref[...]
            logits = jnp.where(lane < N_EXPERTS, logits, -jnp.inf)
            m1 = jnp.max(logits, axis=1, keepdims=True)
            i1 = jnp.min(jnp.where(logits == m1, lane, LANES), axis=1, keepdims=True)
            rest = jnp.where(lane == i1, -jnp.inf, logits)
            m2 = jnp.max(rest, axis=1, keepdims=True)
            i2 = jnp.min(jnp.where(rest == m2, lane, LANES), axis=1, keepdims=True)
            e2 = jnp.exp(m2 - m1)
            den = 1.0 + e2
            comb_ref[...] = jnp.where(lane == i1, 1.0 / den, 0.0) + jnp.where(lane == i2, e2 / den, 0.0)

    h = h_ref[...]
    gt = jnp.dot(h, wg_ref[0], preferred_element_type=F32)
    up = jnp.dot(h, wu_ref[0], preferred_element_type=F32)
    act = (_silu(gt) * up).astype(BF16)
    part = jnp.dot(act, wd_ref[0], preferred_element_type=F32)
    if routed:
        ce = jnp.sum(jnp.where(lane == e, comb_ref[...], 0.0), axis=1, keepdims=True)
        part = ce * part
    acc_ref[...] += part

    @pl.when((e == pl.num_programs(2) - 1) & (c == pl.num_programs(3) - 1))
    def _():
        z = alpha * x_ref[...] + (1.0 + gate_ref[...]) * acc_ref[...].reshape(bb, tt, d)
        y_ref[...] = _layer_norm_rows(z, g_ref[...], b_ref[...])


def _ffn(x, mod, w_r, b_r, w_gu, w_down, g, b, *, routed, ff_chunk, alpha):
    batch, seq, d = x.shape
    n_e, ff, _ = w_down.shape
    assert ff % ff_chunk == 0
    n_c = ff // ff_chunk
    bb, tt = _row_blocks(batch, seq, FFN_ROW_TILE)
    rows = bb * tt
    vec = pl.BlockSpec((1, 1, d), lambda b_, t, e, c: (0, 0, 0))
    return pl.pallas_call(
        functools.partial(_ffn_kernel, routed=routed, alpha=alpha),
        grid=(batch // bb, seq // tt, n_e, n_c),
        in_specs=[
            pl.BlockSpec((bb, tt, d), lambda b_, t, e, c: (b_, t, 0)),
            _mod_spec(bb, 4, 4),
            _mod_spec(bb, 3, 4),
            _mod_spec(bb, 5, 4),
            pl.BlockSpec((d, LANES), lambda b_, t, e, c: (0, 0)),
            pl.BlockSpec((1, LANES), lambda b_, t, e, c: (0, 0)),
            pl.BlockSpec((1, d, ff_chunk), lambda b_, t, e, c: (e, 0, c)),
            pl.BlockSpec((1, d, ff_chunk), lambda b_, t, e, c: (e, 0, n_c + c)),
            pl.BlockSpec((1, ff_chunk, d), lambda b_, t, e, c: (e, c, 0)),
            vec,
            vec,
        ],
        out_specs=pl.BlockSpec((bb, tt, d), lambda b_, t, e, c: (b_, t, 0)),
        out_shape=jax.ShapeDtypeStruct((batch, seq, d), F32),
        scratch_shapes=[
            pltpu.VMEM((rows, d), BF16),
            pltpu.VMEM((rows, LANES), F32),
            pltpu.VMEM((rows, d), F32),
        ],
        compiler_params=_params(4),
        name="moe_ffn" if routed else "dense_ffn",
    )(x, mod, mod, mod, w_r, b_r, w_gu, w_gu, w_down, g.reshape(1, 1, d), b.reshape(1, 1, d))


def _rope_lanes(seg, cos, sin_signed, first_half):
    width = seg.shape[-1]
    reps = width // LANES
    if reps > 1:
        cos = jnp.concatenate([cos] * reps, axis=1)
        sin_signed = jnp.concatenate([sin_signed] * reps, axis=1)
        first_half = jnp.concatenate([first_half] * reps, axis=1)
    half = HEAD_DIM // 2
    swapped = jnp.where(first_half, pltpu.roll(seg, width - half, 1), pltpu.roll(seg, half, 1))
    return seg * cos + swapped * sin_signed


def _b_project_kernel(x_ref, sc_ref, sh_ref, w_ref, cos_ref, sin_ref, kg_ref, kb_ref,
                      q_ref, qi_ref, k_ref, v_ref, kw_ref):
    bb, tt, d = x_ref.shape
    rows = bb * tt
    h = x_ref[...] * (1.0 + sc_ref[...]) + sh_ref[...]
    h = h.reshape(rows, d).astype(BF16)
    acc = jnp.dot(h, w_ref[...], preferred_element_type=F32)
    cos = cos_ref[...]
    sin_signed = sin_ref[...]
    lane = lax.broadcasted_iota(jnp.int32, (1, LANES), 1)
    first_half = (lane % HEAD_DIM) < (HEAD_DIM // 2)

    def out(ref, val):
        ref[...] = val.reshape(bb, tt, val.shape[-1]).astype(ref.dtype)

    o_k, o_v, o_qi, o_ki = B_Q, B_Q + B_KV, B_Q + 2 * B_KV, B_Q + 2 * B_KV + B_QI
    out(q_ref, _rope_lanes(acc[:, :o_k], cos, sin_signed, first_half) * QK_SCALE_LOG2)
    out(k_ref, _rope_lanes(acc[:, o_k:o_v], cos, sin_signed, first_half))
    out(v_ref, acc[:, o_v:o_qi])
    out(qi_ref, _rope_lanes(acc[:, o_qi:o_ki], cos, sin_signed, first_half))
    seg = acc[:, o_ki:]
    is_ki = lane < IDX_DIM
    mu = jnp.sum(jnp.where(is_ki, seg, 0.0), axis=1, keepdims=True) / IDX_DIM
    cen = jnp.where(is_ki, seg - mu, 0.0)
    var = jnp.sum(cen * cen, axis=1, keepdims=True) / IDX_DIM
    ki = cen * lax.rsqrt(var + LN_EPS) * kg_ref[...] + kb_ref[...]
    ki = _rope_lanes(ki, cos, sin_signed, first_half)
    out(kw_ref, jnp.where(is_ki, ki, seg * (IDX_HEADS ** -0.5)))


def _b_project(x, mod, w, cos, sin_signed, kn_g, kn_b, table_per_tile):
    batch, seq, d = x.shape
    bb, tt = _row_blocks(batch, seq, ROW_TILE)
    rows = bb * tt
    n = w.shape[1]
    tab = pl.BlockSpec((rows, LANES), (lambda b, t: (0, 0)) if table_per_tile else (lambda b, t: (t, 0)))
    vec = pl.BlockSpec((1, LANES), lambda b, t: (0, 0))

    def o_spec(width):
        return pl.BlockSpec((bb, tt, width), lambda b, t: (b, t, 0))

    def o_shape(width, dtype):
        return jax.ShapeDtypeStruct((batch, seq, width), dtype)

    return pl.pallas_call(
        _b_project_kernel,
        grid=(batch // bb, seq // tt),
        in_specs=[
            pl.BlockSpec((bb, tt, d), lambda b, t: (b, t, 0)),
            _mod_spec(bb, 1, 2),
            _mod_spec(bb, 0, 2),
            pl.BlockSpec((d, n), lambda b, t: (0, 0)),
            tab,
            tab,
            vec,
            vec,
        ],
        out_specs=[o_spec(B_Q), o_spec(B_QI), o_spec(B_KV), o_spec(B_KV), o_spec(LANES)],
        out_shape=[o_shape(B_Q, BF16), o_shape(B_QI, BF16), o_shape(B_KV, F32), o_shape(B_KV, F32),
                   o_shape(LANES, F32)],
        compiler_params=_params(2),
        name="b_project",
    )(x, mod, mod, w, cos, sin_signed, kn_g, kn_b)


def _rope_tables(pos):
    half = HEAD_DIM // 2
    inv = ROPE_THETA ** (-jnp.arange(half, dtype=F32) / half)
    ang = pos.astype(F32)[:, None] * inv[None, :]
    cos, sin = jnp.cos(ang), jnp.sin(ang)
    return (jnp.concatenate([cos, cos, cos, cos], axis=1),
            jnp.concatenate([-sin, sin, -sin, sin], axis=1))


def _key_norm_kernel(k_ref, grp_ref, o_ref):
    k = k_ref[0].astype(F32)
    n2 = jnp.dot(k * k, grp_ref[...], preferred_element_type=F32, precision=lax.Precision.HIGHEST)
    mx = jnp.broadcast_to(jnp.max(n2, axis=0, keepdims=True), o_ref.shape[1:])

    @pl.when(pl.program_id(1) == 0)
    def _():
        o_ref[0] = mx

    @pl.when(pl.program_id(1) > 0)
    def _():
        o_ref[0] = jnp.maximum(o_ref[0], mx)


def _key_norm_max(k):
    batch, lp, width = k.shape
    tile = DSA_KEY_TILE
    grp = (jnp.arange(width)[:, None] // HEAD_DIM == jnp.arange(LANES)[None, :]).astype(F32)
    return pl.pallas_call(
        _key_norm_kernel,
        grid=(batch, lp // tile),
        in_specs=[pl.BlockSpec((1, tile, width), lambda b, t: (b, t, 0)),
                  pl.BlockSpec((width, LANES), lambda b, t: (0, 0))],
        out_specs=pl.BlockSpec((1, 8, LANES), lambda b, t: (b, 0, 0)),
        out_shape=jax.ShapeDtypeStruct((batch, 8, LANES), F32),
        compiler_params=_params(2),
        name="key_norm_max",
    )(k, grp)


def _dsa_kernel(qt_ref, qit_ref, kwt_ref, ki_ref, k_ref, vta_ref, kmax_ref, o_ref,
                s_ref, qpad_ref, bound_ref, m_ref, acc_ref, *, causal, n_keys, k_sel):
    qn = qt_ref.shape[2]
    lt = DSA_KEY_TILE
    i = pl.program_id(1)
    if causal:
        n_tiles = ((i + 1) * qn + lt - 1) // lt
        qpos = i * qn + lax.broadcasted_iota(jnp.int32, (1, qn), 1)
        limit = (qpos // CHUNK + 1) * CHUNK
    else:
        n_tiles = ki_ref.shape[1] // lt
        limit = jnp.full((1, qn), n_keys, jnp.int32)

    def key_slice(t):
        return pl.ds(pl.multiple_of(t * lt, lt), lt)

    def key_pos(t):
        return t * lt + lax.broadcasted_iota(jnp.int32, (lt, 1), 0)

    def fold_rows(x, op):
        return op(op(x.reshape(lt // 32, 4, 8, x.shape[-1]), axis=0), axis=0)

    qit = qit_ref[0]
    kwt = kwt_ref[0]
    qi_w = jnp.concatenate([qit[h * IDX_DIM:(h + 1) * IDX_DIM, :] for h in range(IDX_HEADS)], axis=1)
    w_row = jnp.concatenate([kwt[IDX_DIM + h:IDX_DIM + h + 1, :] for h in range(IDX_HEADS)], axis=1)
    w_row = w_row * (IDX_DIM ** -0.5)

    def score_body(t, carry):
        rmax, rmin = carry
        s = jnp.dot(ki_ref[0, key_slice(t), :], qi_w, preferred_element_type=F32)
        s = jnp.maximum(s, 0.0) * w_row
        sc = s[:, 0:qn]
        for h in range(1, IDX_HEADS):
            sc = sc + s[:, h * qn:(h + 1) * qn]
        adm = key_pos(t) < limit
        s_ref[key_slice(t), :] = jnp.where(adm, sc, -jnp.inf)
        rmax = jnp.maximum(rmax, jnp.max(jnp.where(adm, sc, -jnp.inf), axis=0, keepdims=True))
        rmin = jnp.minimum(rmin, jnp.min(jnp.where(adm, sc, jnp.inf), axis=0, keepdims=True))
        return rmax, rmin

    rmax, rmin = lax.fori_loop(0, n_tiles, score_body,
                               (jnp.full((1, qn), -jnp.inf, F32), jnp.full((1, qn), jnp.inf, F32)))

    def count(pred):
        def body(t, acc):
            return acc + fold_rows(jnp.where(pred(s_ref[key_slice(t), :], key_pos(t)), 1.0, 0.0), jnp.sum)
        acc = lax.fori_loop(0, n_tiles, body, jnp.zeros((8, qn), F32))
        return jnp.sum(acc, axis=0, keepdims=True)

    def to_key(x):
        bits = lax.bitcast_convert_type(x, jnp.int32)
        return bits ^ ((bits >> 31) & 0x7FFFFFFF)

    def from_key(key):
        return lax.bitcast_convert_type(key ^ ((key >> 31) & 0x7FFFFFFF), F32)

    n_adm = limit.astype(F32)
    target = jnp.minimum(n_adm, float(k_sel))

    def bis_cond(st):
        return jnp.sum(st[3]) < qn

    def bis_body(st):
        lo, hi, c_lo, done = st
        mid = (lo >> 1) + (hi >> 1) + (lo & hi & 1)
        stuck = mid == lo
        c = count(lambda s, _: s >= from_key(mid))
        live = (done < 0.5) & jnp.logical_not(stuck)
        up = live & (c >= target)
        dn = live & (c < target)
        lo = jnp.where(up, mid, lo)
        c_lo = jnp.where(up, c, c_lo)
        hi = jnp.where(dn, mid, hi)
        done = jnp.where(stuck | (c_lo == target), 1.0, done)
        return lo, hi, c_lo, done

    done0 = jnp.where(n_adm == target, 1.0, 0.0)
    k_thr, _, c_thr, _ = lax.while_loop(bis_cond, bis_body, (to_key(rmin), to_key(rmax) + 1, n_adm, done0))
    thr = from_key(k_thr)

    tied = c_thr > target

    @pl.when(jnp.sum(jnp.where(tied, 1.0, 0.0)) > 0)
    def _():
        need = target - count(lambda s, _: s > thr)

        def idx_body(_, st):
            lo_j, hi_j = st
            mid = (lo_j + hi_j) // 2
            c = count(lambda s, kp: (s == thr) & (kp <= mid))
            return jnp.where(c < need, mid, lo_j), jnp.where(c >= need, mid, hi_j)

        last = jnp.full((1, qn), n_tiles * lt - 1, jnp.int32)
        n_steps = (ki_ref.shape[1] - 1).bit_length() + 1
        _, last = lax.fori_loop(0, n_steps, idx_body, (jnp.full((1, qn), -1, jnp.int32), last))

        def drop_body(t, carry):
            s = s_ref[key_slice(t), :]
            s_ref[key_slice(t), :] = jnp.where(tied & (s == thr) & (key_pos(t) > last), -jnp.inf, s)
            return carry

        lax.fori_loop(0, n_tiles, drop_body, 0)

    qt = qt_ref[0]
    zeros = jnp.zeros((HEAD_DIM, B_GROUP * qn), BF16)
    for g in range(B_KV_HEADS):
        q_g = jnp.concatenate(
            [qt[(g * B_GROUP + j) * HEAD_DIM:(g * B_GROUP + j + 1) * HEAD_DIM, :] for j in range(B_GROUP)],
            axis=1)
        qpad_ref[g] = jnp.concatenate([zeros] * g + [q_g] + [zeros] * (B_KV_HEADS - 1 - g), axis=0)

    qf = qt.astype(F32)
    q_norm2 = jnp.sum((qf * qf).reshape(B_HEADS, HEAD_DIM, qn), axis=1)
    for g in range(B_KV_HEADS):
        b = jnp.sqrt(q_norm2[g * B_GROUP:(g + 1) * B_GROUP] * kmax_ref[0, 0:1, g:g + 1])
        b = b * BOUND_SLACK + BOUND_SLACK_ABS
        bound_ref[g] = jnp.concatenate([b[j:j + 1] for j in range(B_GROUP)], axis=1)
    acc_ref[...] = jnp.zeros(acc_ref.shape, F32)

    def fast_body(t, carry):
        keys = k_ref[0, key_slice(t), :]
        sel = jnp.where(s_ref[key_slice(t), :] >= thr, 1.0, 0.0).astype(BF16)
        sel = jnp.concatenate([sel] * B_GROUP, axis=1)
        def logits(g):
            return jnp.dot(keys, qpad_ref[g], preferred_element_type=F32)

        lg = logits(0)
        for g in range(B_KV_HEADS):
            lg_next = logits(g + 1) if g + 1 < B_KV_HEADS else None
            p = jnp.exp2(lg - bound_ref[g]).astype(BF16) * sel
            acc_ref[g] += jnp.dot(vta_ref[0, g, :, key_slice(t)], p, preferred_element_type=F32)
            lg = lg_next
        return carry

    lax.fori_loop(0, n_tiles, fast_body, 0)
    den_min = jnp.min(acc_ref[:, HEAD_DIM:HEAD_DIM + 1, :])

    @pl.when(jnp.logical_not(den_min >= MIN_TRUSTED_DENOMINATOR))
    def _():
        m_ref[...] = jnp.full(m_ref.shape, NEG_BIG, F32)
        acc_ref[...] = jnp.zeros(acc_ref.shape, F32)

        def exact_body(t, carry):
            keys = k_ref[0, key_slice(t), :]
            mb = jnp.where(s_ref[key_slice(t), :] >= thr, 0.0, -jnp.inf)
            mb = jnp.concatenate([mb] * B_GROUP, axis=1)
            for g in range(B_KV_HEADS):
                lg = jnp.dot(keys, qpad_ref[g], preferred_element_type=F32) + mb
                m_old = m_ref[g]
                m_new = jnp.maximum(m_old, jnp.max(fold_rows(lg, jnp.max), axis=0, keepdims=True))
                p = jnp.exp2(lg - m_new).astype(BF16)
                pv = jnp.dot(vta_ref[0, g, :, key_slice(t)], p, preferred_element_type=F32)
                acc_ref[g] = jnp.exp2(m_old - m_new) * acc_ref[g] + pv
                m_ref[g] = m_new
            return carry

        lax.fori_loop(0, n_tiles, exact_body, 0)

    for g in range(B_KV_HEADS):
        acc = acc_ref[g]
        o_g = acc[:HEAD_DIM] / acc[HEAD_DIM:HEAD_DIM + 1]
        for j in range(B_GROUP):
            hh = g * B_GROUP + j
            o_ref[0, hh * HEAD_DIM:(hh + 1) * HEAD_DIM, :] = o_g[:, j * qn:(j + 1) * qn].astype(o_ref.dtype)


V_AUG_ROWS = HEAD_DIM + 16


def _dsa(q, qi, kw, k, v, ki, *, q_tile, causal, k_sel):
    batch, seq, _ = q.shape
    n_keys = k.shape[1]
    lp = -(-n_keys // DSA_KEY_TILE) * DSA_KEY_TILE
    assert seq % q_tile == 0 and q_tile % LANES == 0
    pad = ((0, 0), (0, lp - n_keys), (0, 0))
    k, v, ki = (jnp.pad(a.astype(BF16), pad) for a in (k, v, ki))
    vt = jnp.swapaxes(v, 1, 2).reshape(batch, B_KV_HEADS, HEAD_DIM, lp)
    vta = jnp.concatenate([vt, jnp.ones((batch, B_KV_HEADS, 1, lp), BF16),
                           jnp.zeros((batch, B_KV_HEADS, V_AUG_ROWS - HEAD_DIM - 1, lp), BF16)], axis=2)
    qt, qit, kwt = (jnp.swapaxes(a, 1, 2) for a in (q, qi, kw))

    def qspec(rows):
        return pl.BlockSpec((1, rows, q_tile), lambda b, i: (b, 0, i))

    def resident(shape):
        zero = (0,) * len(shape)
        return pl.BlockSpec((1,) + shape, lambda b, i: (b,) + zero, pipeline_mode=pl.Buffered(1))

    ot = pl.pallas_call(
        functools.partial(_dsa_kernel, causal=causal, n_keys=n_keys, k_sel=k_sel),
        grid=(batch, seq // q_tile),
        in_specs=[qspec(B_Q), qspec(B_QI), qspec(LANES),
                  resident((lp, IDX_DIM)), resident((lp, B_KV)),
                  resident((B_KV_HEADS, V_AUG_ROWS, lp)),
                  pl.BlockSpec((1, 8, LANES), lambda b, i: (b, 0, 0))],
        out_specs=qspec(B_Q),
        out_shape=jax.ShapeDtypeStruct((batch, B_Q, seq), BF16),
        scratch_shapes=[pltpu.VMEM((lp, q_tile), F32),
                        pltpu.VMEM((B_KV_HEADS, B_KV, B_GROUP * q_tile), BF16),
                        pltpu.VMEM((B_KV_HEADS, 1, B_GROUP * q_tile), F32),
                        pltpu.VMEM((B_KV_HEADS, 1, B_GROUP * q_tile), F32),
                        pltpu.VMEM((B_KV_HEADS, V_AUG_ROWS, B_GROUP * q_tile), F32)],
        compiler_params=_params(2),
        name="dsa_prompt" if causal else "dsa_sample",
    )(qt, qit, kwt, ki, k, vta, _key_norm_max(k))
    return jnp.swapaxes(ot, 1, 2)


def kernel(x_prompt, x_sample, cache_k_a, cache_v_a, cache_k_b, cache_v_b, cache_kidx_b,
           c_prompt, c_sample, w_cond, b_cond, ln_g, ln_b, a_w_in, a_w_o, a_rel_bias,
           b_w_in, b_w_o, b_kidx_ln_g, b_kidx_ln_b, ffn_w_gu, ffn_w_down,
           moe_w_router, moe_b_router, moe_w_gu, moe_w_down):
    depth = w_cond.shape[0]
    alpha = (2 * depth) ** 0.25
    n_p, seq, d = x_prompt.shape
    n_s, dec_seq, _ = x_sample.shape
    past = cache_k_b.shape[2]
    a_hd = A_HEADS * HEAD_DIM

    rows = n_p + n_s
    rows_pad = -(-rows // 8) * 8
    c_all = jnp.pad(jnp.concatenate([c_prompt, c_sample], axis=0), ((0, rows_pad - rows), (0, 0)))
    mod_all = _modulation(c_all, w_cond, b_cond)

    xp, xs = x_prompt, x_sample
    outs = {k: [] for k in ("ka_p", "va_p", "kb_p", "vb_p", "ib_p", "ka_s", "va_s", "kb_s", "vb_s", "ib_s")}
    dummy_wr = jnp.zeros((d, LANES), F32)
    dummy_br = jnp.zeros((1, LANES), F32)
    for i in range(depth):
        j = i // 2
        mod_p = mod_all[i, :n_p].reshape(n_p, 1, 6 * d)
        mod_s = mod_all[i, n_p:rows].reshape(n_s, 1, 6 * d)
        g1, b1, g2, b2 = ln_g[i, 0], ln_b[i, 0], ln_g[i, 1], ln_b[i, 1]
        if i % 2 == 0:
            w_in = a_w_in[j].astype(BF16)
            keep = min(BAND_ROWS, seq)
            qkv_p = _modmm(xp, mod_p, w_in, BF16)
            kv_p = _modmm(xp, mod_p, w_in[:, a_hd:], F32, t_start=seq - keep)
            qkv_s = _modmm(xs, mod_s, w_in, BF16)
            kv_s = _modmm(xs, mod_s, w_in[:, a_hd:], F32)
            outs["ka_p"].append(kv_p[..., :a_hd].reshape(n_p, keep, A_HEADS, HEAD_DIM))
            outs["va_p"].append(kv_p[..., a_hd:].reshape(n_p, keep, A_HEADS, HEAD_DIM))
            outs["ka_s"].append(kv_s[..., :a_hd].reshape(n_s, dec_seq, A_HEADS, HEAD_DIM))
            outs["va_s"].append(kv_s[..., a_hd:].reshape(n_s, dec_seq, A_HEADS, HEAD_DIM))
            o_p = _band_attn_prompt(qkv_p, a_rel_bias[j])
            ck = cache_k_a[j].reshape(n_s, -1, a_hd).astype(BF16)
            cv = cache_v_a[j].reshape(n_s, -1, a_hd).astype(BF16)
            o_s = _band_attn_sample(qkv_s, ck, cv, a_rel_bias[j])
            w_o = a_w_o[j].astype(BF16)
        else:
            w_in = jnp.pad(b_w_in[j], ((0, 0), (0, B_PROJ_PAD - B_PROJ))).astype(BF16)
            kn_g = jnp.pad(b_kidx_ln_g[j], (0, LANES - IDX_DIM)).reshape(1, LANES)
            kn_b = jnp.pad(b_kidx_ln_b[j], (0, LANES - IDX_DIM)).reshape(1, LANES)
            cos_p, sin_p = _rope_tables(jnp.arange(seq))
            cos_s, sin_s = _rope_tables(past + jnp.arange(dec_seq))
            bb_s, _ = _row_blocks(n_s, dec_seq, ROW_TILE)
            cos_s, sin_s = jnp.tile(cos_s, (bb_s, 1)), jnp.tile(sin_s, (bb_s, 1))
            q_p, qi_p, k_p, v_p, kw_p = _b_project(xp, mod_p, w_in, cos_p, sin_p, kn_g, kn_b, False)
            q_s, qi_s, k_s, v_s, kw_s = _b_project(xs, mod_s, w_in, cos_s, sin_s, kn_g, kn_b, True)
            outs["kb_p"].append(k_p.reshape(n_p, seq, B_KV_HEADS, HEAD_DIM))
            outs["vb_p"].append(v_p.reshape(n_p, seq, B_KV_HEADS, HEAD_DIM))
            outs["ib_p"].append(kw_p[..., :IDX_DIM])
            outs["kb_s"].append(k_s.reshape(n_s, dec_seq, B_KV_HEADS, HEAD_DIM))
            outs["vb_s"].append(v_s.reshape(n_s, dec_seq, B_KV_HEADS, HEAD_DIM))
            outs["ib_s"].append(kw_s[..., :IDX_DIM])
            o_p = _dsa(q_p, qi_p, kw_p, k_p, v_p, kw_p[..., :IDX_DIM], q_tile=DSA_Q_TILE, causal=True,
                       k_sel=min(TOPK_MAX, seq // 4))
            n_keys = past + dec_seq
            kk = jnp.concatenate([cache_k_b[j].reshape(n_s, past, B_KV), k_s], axis=1)
            vc = jnp.concatenate([cache_v_b[j].reshape(n_s, past, B_KV), v_s], axis=1)
            kki = jnp.concatenate([cache_kidx_b[j], kw_s[..., :IDX_DIM]], axis=1)
            rep = LANES // dec_seq
            q_r, qi_r, kw_r = (jnp.concatenate([a] * rep, axis=1) for a in (q_s, qi_s, kw_s))
            o_s = _dsa(q_r, qi_r, kw_r, kk, vc, kki, q_tile=LANES, causal=False,
                       k_sel=min(TOPK_MAX, n_keys // 4))[:, :dec_seq]
            w_o = b_w_o[j].astype(BF16)
        xp = _mm_postnorm(o_p, w_o, xp, mod_p, 2, g1, b1, alpha)
        xs = _mm_postnorm(o_s, w_o, xs, mod_s, 2, g1, b1, alpha)
        if i % 2 == 0:
            w_gu = ffn_w_gu[j].astype(BF16)[None]
            w_dn = ffn_w_down[j].astype(BF16)[None]
            dense = dict(routed=False, ff_chunk=D_FF // 2, alpha=alpha)
            xp = _ffn(xp, mod_p, dummy_wr, dummy_br, w_gu, w_dn, g2, b2, **dense)
            xs = _ffn(xs, mod_s, dummy_wr, dummy_br, w_gu, w_dn, g2, b2, **dense)
        else:
            w_gu = moe_w_gu[j].astype(BF16)
            w_dn = moe_w_down[j].astype(BF16)
            w_r = jnp.pad(moe_w_router[j], ((0, 0), (0, LANES - N_EXPERTS)))
            b_r = jnp.pad(moe_b_router[j], (0, LANES - N_EXPERTS)).reshape(1, LANES)
            moe = dict(routed=True, ff_chunk=D_FF_EXPERT // 4, alpha=alpha)
            xp = _ffn(xp, mod_p, w_r, b_r, w_gu, w_dn, g2, b2, **moe)
            xs = _ffn(xs, mod_s, w_r, b_r, w_gu, w_dn, g2, b2, **moe)

    st = lambda name: jnp.stack(outs[name])
    return (xp, xs, st("ka_p"), st("va_p"), st("kb_p"), st("vb_p"), st("ib_p"),
            st("ka_s"), st("va_s"), st("kb_s"), st("vb_s"), st("ib_s"))
```

```python
import functools

import jax
import jax.numpy as jnp
from jax import lax
from jax.experimental import pallas as pl
from jax.experimental.pallas import tpu as pltpu

F32 = jnp.float32
BF16 = jnp.bfloat16

D_MODEL = 1024
CHUNK = 64
N_PAST_CHUNKS = 8
BAND_ROWS = N_PAST_CHUNKS * CHUNK
REL_CLIP = 2 * CHUNK
HEAD_DIM = 64
A_HEADS = 16
B_HEADS = 16
B_KV_HEADS = 4
B_GROUP = B_HEADS // B_KV_HEADS
IDX_HEADS = 8
IDX_DIM = 64
TOPK_MAX = 256
D_FF = 2816
N_EXPERTS = 8
D_FF_EXPERT = 3584
ROPE_THETA = 10000.0
LN_EPS = 1e-5
B_Q = B_HEADS * HEAD_DIM
B_KV = B_KV_HEADS * HEAD_DIM
B_QI = IDX_HEADS * IDX_DIM
B_PROJ = B_Q + 2 * B_KV + B_QI + IDX_DIM + IDX_HEADS

LANES = 128
VMEM_LIMIT_BYTES = 58 * 1024 * 1024

A_Q_TILE = 4 * CHUNK
DSA_Q_TILE = 4 * CHUNK
DSA_KEY_TILE = 512
ROW_TILE = 512
FFN_ROW_TILE = 1024
B_PROJ_PAD = B_Q + 2 * B_KV + B_QI + LANES
NEG_BIG = -1e30
QK_SCALE_LOG2 = HEAD_DIM ** -0.5 * 1.4426950408889634
BOUND_SLACK = 1.02
BOUND_SLACK_ABS = 0.01
MIN_TRUSTED_DENOMINATOR = 2.0 ** -100
SECANT_STEPS = 12


def _params(n_grid):
    return pltpu.CompilerParams(
        dimension_semantics=("arbitrary",) * n_grid,
        vmem_limit_bytes=VMEM_LIMIT_BYTES,
    )


def _row_blocks(batch, seq, target):
    if seq >= target:
        assert seq % target == 0
        return 1, target
    bb = max(1, min(batch, target // seq))
    while batch % bb:
        bb -= 1
    return bb, seq


def _mod_spec(bb, chunk, n_grid):
    if n_grid == 2:
        return pl.BlockSpec((bb, 1, D_MODEL), lambda b, t: (b, 0, chunk))
    return pl.BlockSpec((bb, 1, D_MODEL), lambda b, t, e, c: (b, 0, chunk))


def _silu(x):
    return x / (1.0 + jnp.exp(-x))


def _layer_norm_rows(z, g, b):
    mu = jnp.mean(z, axis=-1, keepdims=True)
    zc = z - mu
    var = jnp.mean(zc * zc, axis=-1, keepdims=True)
    return zc * lax.rsqrt(var + LN_EPS) * g + b


def _modulation_kernel(c_ref, w_ref, b_ref, o_ref):
    a = _silu(c_ref[...]).astype(BF16)
    w = w_ref[0].astype(BF16)
    o_ref[0] = jnp.dot(a, w, preferred_element_type=F32) + b_ref[0]


def _modulation(c_all, w_cond, b_cond):
    depth, d, n = w_cond.shape
    rows = c_all.shape[0]
    tn = 1536
    return pl.pallas_call(
        _modulation_kernel,
        grid=(depth, n // tn),
        in_specs=[
            pl.BlockSpec((rows, d), lambda i, j: (0, 0)),
            pl.BlockSpec((1, d, tn), lambda i, j: (i, 0, j)),
            pl.BlockSpec((1, 1, tn), lambda i, j: (i, 0, j)),
        ],
        out_specs=pl.BlockSpec((1, rows, tn), lambda i, j: (i, 0, j)),
        out_shape=jax.ShapeDtypeStruct((depth, rows, n), F32),
        compiler_params=_params(2),
        name="modulation",
    )(c_all, w_cond, b_cond.reshape(depth, 1, n))


def _modmm_kernel(x_ref, sc_ref, sh_ref, w_ref, o_ref):
    bb, tt, d = x_ref.shape
    h = x_ref[...] * (1.0 + sc_ref[...]) + sh_ref[...]
    h = h.reshape(bb * tt, d).astype(BF16)
    acc = jnp.dot(h, w_ref[...], preferred_element_type=F32)
    o_ref[...] = acc.reshape(bb, tt, acc.shape[-1]).astype(o_ref.dtype)


def _modmm(x, mod, w, out_dtype, t_start=0):
    batch, seq, d = x.shape
    n = w.shape[1]
    seq_out = seq - t_start
    bb, tt = _row_blocks(batch, seq_out, ROW_TILE)
    assert t_start % tt == 0
    off = t_start // tt
    return pl.pallas_call(
        _modmm_kernel,
        grid=(batch // bb, seq_out // tt),
        in_specs=[
            pl.BlockSpec((bb, tt, d), lambda b, t: (b, t + off, 0)),
            _mod_spec(bb, 1, 2),
            _mod_spec(bb, 0, 2),
            pl.BlockSpec((d, n), lambda b, t: (0, 0)),
        ],
        out_specs=pl.BlockSpec((bb, tt, n), lambda b, t: (b, t, 0)),
        out_shape=jax.ShapeDtypeStruct((batch, seq_out, n), out_dtype),
        compiler_params=_params(2),
        name="a_project",
    )(x, mod, mod, w)


def _band_attn_kernel(*refs, n_kb, n_maybe_invalid):
    q_ref = refs[0]
    k_refs = refs[1:1 + n_kb]
    v_refs = refs[1 + n_kb:1 + 2 * n_kb]
    f_ref = refs[1 + 2 * n_kb]
    o_ref = refs[2 + 2 * n_kb]
    bias_ref = refs[3 + 2 * n_kb]
    q_tile = q_ref.shape[1]
    kb_sizes = [r.shape[1] for r in k_refs]
    k_tot = sum(kb_sizes)
    width = f_ref.shape[-1]
    i = pl.program_id(1)

    @pl.when((pl.program_id(0) == 0) & (i == 0))
    def _():
        rq = lax.broadcasted_iota(jnp.int32, (q_tile, k_tot), 0) // CHUNK
        ck = lax.broadcasted_iota(jnp.int32, (q_tile, k_tot), 1) // CHUNK
        in_band = (ck >= rq) & (ck <= rq + N_PAST_CHUNKS)
        for h in range(A_HEADS):
            rows = jnp.broadcast_to(f_ref[h], (q_tile, width))
            toep = pltpu.roll(rows, k_tot + 1, 1, stride=1, stride_axis=0)
            bias_ref[h] = jnp.where(in_band, toep[:, :k_tot], -jnp.inf)

    q = q_ref[0] * (HEAD_DIM ** -0.5)
    for h in range(A_HEADS):
        cols = slice(h * HEAD_DIM, (h + 1) * HEAD_DIM)
        qh = q[:, cols]
        parts = []
        for kb in range(n_kb):
            lg = lax.dot_general(qh, k_refs[kb][0, :, cols], (((1,), (1,)), ((), ())),
                                 preferred_element_type=F32)
            if kb < n_maybe_invalid:
                lg = jnp.where(i - n_maybe_invalid + kb >= 0, lg, -jnp.inf)
            parts.append(lg)
        logits = jnp.concatenate(parts, axis=1) + bias_ref[h]
        m = jnp.max(logits, axis=1, keepdims=True)
        p = jnp.exp(logits - m)
        l = jnp.sum(p, axis=1, keepdims=True)
        pb = p.astype(BF16)
        acc = jnp.zeros((q_tile, HEAD_DIM), F32)
        start = 0
        for kb in range(n_kb):
            acc = acc + jnp.dot(pb[:, start:start + kb_sizes[kb]], v_refs[kb][0, :, cols],
                                preferred_element_type=F32)
            start += kb_sizes[kb]
        o_ref[0, :, cols] = (acc / l).astype(o_ref.dtype)


def _bias_vector(table, q_tile, k_tot):
    width = k_tot + q_tile
    lo = -(q_tile - 1) - BAND_ROWS + REL_CLIP
    left = max(0, -lo)
    start = max(0, lo)
    n_mid = min(2 * REL_CLIP + 1 - start, width - left)
    right = width - left - n_mid
    mid = table.T[:, start:start + n_mid]
    f = jnp.pad(mid, ((0, 0), (left, right)), mode="edge")
    return f.reshape(A_HEADS, 1, width)


def _band_attn_prompt(qkv, table):
    batch, seq, _ = qkv.shape
    hd = A_HEADS * HEAD_DIM
    qt = A_Q_TILE
    n_prev = BAND_ROWS // qt
    n_kb = n_prev + 1
    k_tot = n_kb * qt
    f = _bias_vector(table, qt, k_tot)

    def kv_spec(kb, col):
        return pl.BlockSpec((1, qt, hd), lambda b, i: (b, jnp.maximum(i - n_prev + kb, 0), col))

    return pl.pallas_call(
        functools.partial(_band_attn_kernel, n_kb=n_kb, n_maybe_invalid=n_prev),
        grid=(batch, seq // qt),
        in_specs=[pl.BlockSpec((1, qt, hd), lambda b, i: (b, i, 0))]
        + [kv_spec(kb, 1) for kb in range(n_kb)]
        + [kv_spec(kb, 2) for kb in range(n_kb)]
        + [pl.BlockSpec(f.shape, lambda b, i: (0, 0, 0))],
        out_specs=pl.BlockSpec((1, qt, hd), lambda b, i: (b, i, 0)),
        out_shape=jax.ShapeDtypeStruct((batch, seq, hd), BF16),
        scratch_shapes=[pltpu.VMEM((A_HEADS, qt, k_tot), F32)],
        compiler_params=_params(2),
        name="band_attn_prompt",
    )(*([qkv] * (1 + 2 * n_kb)), f)


def _band_attn_sample(qkv, cache_k, cache_v, table):
    batch, seq, _ = qkv.shape
    hd = A_HEADS * HEAD_DIM
    win = cache_k.shape[1]
    assert seq == CHUNK and win == BAND_ROWS
    k_tot = win + seq
    f = _bias_vector(table, seq, k_tot)
    new = lambda col: pl.BlockSpec((1, seq, hd), lambda b, i: (b, 0, col))
    old = pl.BlockSpec((1, win, hd), lambda b, i: (b, 0, 0))
    return pl.pallas_call(
        functools.partial(_band_attn_kernel, n_kb=2, n_maybe_invalid=0),
        grid=(batch, 1),
        in_specs=[new(0), old, new(1), old, new(2), pl.BlockSpec(f.shape, lambda b, i: (0, 0, 0))],
        out_specs=pl.BlockSpec((1, seq, hd), lambda b, i: (b, 0, 0)),
        out_shape=jax.ShapeDtypeStruct((batch, seq, hd), BF16),
        scratch_shapes=[pltpu.VMEM((A_HEADS, seq, k_tot), F32)],
        compiler_params=_params(2),
        name="band_attn_sample",
    )(qkv, cache_k, qkv, cache_v, qkv, f)


def _mm_postnorm_kernel(o_ref, w_ref, x_ref, gate_ref, g_ref, b_ref, y_ref, *, alpha):
    bb, tt, d = x_ref.shape
    o = o_ref[...].reshape(bb * tt, o_ref.shape[-1])
    sub = jnp.dot(o, w_ref[...], preferred_element_type=F32).reshape(bb, tt, d)
    z = alpha * x_ref[...] + (1.0 + gate_ref[...]) * sub
    y_ref[...] = _layer_norm_rows(z, g_ref[...], b_ref[...])


def _mm_postnorm(o, w, x, mod, gate_chunk, g, b, alpha):
    batch, seq, d = x.shape
    bb, tt = _row_blocks(batch, seq, ROW_TILE)
    k = o.shape[-1]
    return pl.pallas_call(
        functools.partial(_mm_postnorm_kernel, alpha=alpha),
        grid=(batch // bb, seq // tt),
        in_specs=[
            pl.BlockSpec((bb, tt, k), lambda b_, t: (b_, t, 0)),
            pl.BlockSpec((k, d), lambda b_, t: (0, 0)),
            pl.BlockSpec((bb, tt, d), lambda b_, t: (b_, t, 0)),
            _mod_spec(bb, gate_chunk, 2),
            pl.BlockSpec((1, 1, d), lambda b_, t: (0, 0, 0)),
            pl.BlockSpec((1, 1, d), lambda b_, t: (0, 0, 0)),
        ],
        out_specs=pl.BlockSpec((bb, tt, d), lambda b_, t: (b_, t, 0)),
        out_shape=jax.ShapeDtypeStruct((batch, seq, d), F32),
        compiler_params=_params(2),
        name="out_proj_postnorm",
    )(o, w, x, mod, g.reshape(1, 1, d), b.reshape(1, 1, d))


def _ffn_kernel(x_ref, sc_ref, sh_ref, gate_ref, wr_ref, br_ref, wg_ref, wu_ref, wd_ref,
                g_ref, b_ref, y_ref, h_ref, comb_ref, acc_ref, *, routed, alpha):
    bb, tt, d = x_ref.shape
    rows = bb * tt
    e = pl.program_id(2)
    c = pl.program_id(3)
    lane = lax.broadcasted_iota(jnp.int32, (rows, LANES), 1)

    @pl.when((e == 0) & (c == 0))
    def _():
        h = (x_ref[...] * (1.0 + sc_ref[...]) + sh_ref[...]).reshape(rows, d)
        h_ref[...] = h.astype(BF16)
        acc_ref[...] = jnp.zeros_like(acc_ref)
        if routed:
            logits = jnp.dot(h, wr_ref[...], preferred_element_type=F32,
                             precision=lax.Precision.HIGHEST) + br_ref[...]
            logits = jnp.where(lane < N_EXPERTS, logits, -jnp.inf)
            m1 = jnp.max(logits, axis=1, keepdims=True)
            i1 = jnp.min(jnp.where(logits == m1, lane, LANES), axis=1, keepdims=True)
            rest = jnp.where(lane == i1, -jnp.inf, logits)
            m2 = jnp.max(rest, axis=1, keepdims=True)
            i2 = jnp.min(jnp.where(rest == m2, lane, LANES), axis=1, keepdims=True)
            e2 = jnp.exp(m2 - m1)
            den = 1.0 + e2
            comb_ref[...] = jnp.where(lane == i1, 1.0 / den, 0.0) + jnp.where(lane == i2, e2 / den, 0.0)

    h = h_ref[...]
    gt = jnp.dot(h, wg_ref[0], preferred_element_type=F32)
    up = jnp.dot(h, wu_ref[0], preferred_element_type=F32)
    act = (_silu(gt) * up).astype(BF16)
    part = jnp.dot(act, wd_ref[0], preferred_element_type=F32)
    if routed:
        ce = jnp.sum(jnp.where(lane == e, comb_ref[...], 0.0), axis=1, keepdims=True)
        part = ce * part
    acc_ref[...] += part

    @pl.when((e == pl.num_programs(2) - 1) & (c == pl.num_programs(3) - 1))
    def _():
        z = alpha * x_ref[...] + (1.0 + gate_ref[...]) * acc_ref[...].reshape(bb, tt, d)
        y_ref[...] = _layer_norm_rows(z, g_ref[...], b_ref[...])


def _ffn(x, mod, w_r, b_r, w_gu, w_down, g, b, *, routed, ff_chunk, alpha):
    batch, seq, d = x.shape
    n_e, ff, _ = w_down.shape
    assert ff % ff_chunk == 0
    n_c = ff // ff_chunk
    bb, tt = _row_blocks(batch, seq, FFN_ROW_TILE)
    rows = bb * tt
    vec = pl.BlockSpec((1, 1, d), lambda b_, t, e, c: (0, 0, 0))
    return pl.pallas_call(
        functools.partial(_ffn_kernel, routed=routed, alpha=alpha),
        grid=(batch // bb, seq // tt, n_e, n_c),
        in_specs=[
            pl.BlockSpec((bb, tt, d), lambda b_, t, e, c: (b_, t, 0)),
            _mod_spec(bb, 4, 4),
            _mod_spec(bb, 3, 4),
            _mod_spec(bb, 5, 4),
            pl.BlockSpec((d, LANES), lambda b_, t, e, c: (0, 0)),
            pl.BlockSpec((1, LANES), lambda b_, t, e, c: (0, 0)),
            pl.BlockSpec((1, d, ff_chunk), lambda b_, t, e, c: (e, 0, c)),
            pl.BlockSpec((1, d, ff_chunk), lambda b_, t, e, c: (e, 0, n_c + c)),
            pl.BlockSpec((1, ff_chunk, d), lambda b_, t, e, c: (e, c, 0)),
            vec,
            vec,
        ],
        out_specs=pl.BlockSpec((bb, tt, d), lambda b_, t, e, c: (b_, t, 0)),
        out_shape=jax.ShapeDtypeStruct((batch, seq, d), F32),
        scratch_shapes=[
            pltpu.VMEM((rows, d), BF16),
            pltpu.VMEM((rows, LANES), F32),
            pltpu.VMEM((rows, d), F32),
        ],
        compiler_params=_params(4),
        name="moe_ffn" if routed else "dense_ffn",
    )(x, mod, mod, mod, w_r, b_r, w_gu, w_gu, w_down, g.reshape(1, 1, d), b.reshape(1, 1, d))


def _rope_lanes(seg, cos, sin_signed, first_half):
    width = seg.shape[-1]
    reps = width // LANES
    if reps > 1:
        cos = jnp.concatenate([cos] * reps, axis=1)
        sin_signed = jnp.concatenate([sin_signed] * reps, axis=1)
        first_half = jnp.concatenate([first_half] * reps, axis=1)
    half = HEAD_DIM // 2
    swapped = jnp.where(first_half, pltpu.roll(seg, width - half, 1), pltpu.roll(seg, half, 1))
    return seg * cos + swapped * sin_signed


def _b_project_kernel(x_ref, sc_ref, sh_ref, w_ref, cos_ref, sin_ref, kg_ref, kb_ref,
                      q_ref, qi_ref, k_ref, v_ref, kw_ref):
    bb, tt, d = x_ref.shape
    rows = bb * tt
    h = x_ref[...] * (1.0 + sc_ref[...]) + sh_ref[...]
    h = h.reshape(rows, d).astype(BF16)
    acc = jnp.dot(h, w_ref[...], preferred_element_type=F32)
    cos = cos_ref[...]
    sin_signed = sin_ref[...]
    lane = lax.broadcasted_iota(jnp.int32, (1, LANES), 1)
    first_half = (lane % HEAD_DIM) < (HEAD_DIM // 2)

    def out(ref, val):
        ref[...] = val.reshape(bb, tt, val.shape[-1]).astype(ref.dtype)

    o_k, o_v, o_qi, o_ki = B_Q, B_Q + B_KV, B_Q + 2 * B_KV, B_Q + 2 * B_KV + B_QI
    out(q_ref, _rope_lanes(acc[:, :o_k], cos, sin_signed, first_half) * QK_SCALE_LOG2)
    out(k_ref, _rope_lanes(acc[:, o_k:o_v], cos, sin_signed, first_half))
    out(v_ref, acc[:, o_v:o_qi])
    out(qi_ref, _rope_lanes(acc[:, o_qi:o_ki], cos, sin_signed, first_half))
    seg = acc[:, o_ki:]
    is_ki = lane < IDX_DIM
    mu = jnp.sum(jnp.where(is_ki, seg, 0.0), axis=1, keepdims=True) / IDX_DIM
    cen = jnp.where(is_ki, seg - mu, 0.0)
    var = jnp.sum(cen * cen, axis=1, keepdims=True) / IDX_DIM
    ki = cen * lax.rsqrt(var + LN_EPS) * kg_ref[...] + kb_ref[...]
    ki = _rope_lanes(ki, cos, sin_signed, first_half)
    out(kw_ref, jnp.where(is_ki, ki, seg * (IDX_HEADS ** -0.5)))


def _b_project(x, mod, w, cos, sin_signed, kn_g, kn_b, table_per_tile):
    batch, seq, d = x.shape
    bb, tt = _row_blocks(batch, seq, ROW_TILE)
    rows = bb * tt
    n = w.shape[1]
    tab = pl.BlockSpec((rows, LANES), (lambda b, t: (0, 0)) if table_per_tile else (lambda b, t: (t, 0)))
    vec = pl.BlockSpec((1, LANES), lambda b, t: (0, 0))

    def o_spec(width):
        return pl.BlockSpec((bb, tt, width), lambda b, t: (b, t, 0))

    def o_shape(width, dtype):
        return jax.ShapeDtypeStruct((batch, seq, width), dtype)

    return pl.pallas_call(
        _b_project_kernel,
        grid=(batch // bb, seq // tt),
        in_specs=[
            pl.BlockSpec((bb, tt, d), lambda b, t: (b, t, 0)),
            _mod_spec(bb, 1, 2),
            _mod_spec(bb, 0, 2),
            pl.BlockSpec((d, n), lambda b, t: (0, 0)),
            tab,
            tab,
            vec,
            vec,
        ],
        out_specs=[o_spec(B_Q), o_spec(B_QI), o_spec(B_KV), o_spec(B_KV), o_spec(LANES)],
        out_shape=[o_shape(B_Q, BF16), o_shape(B_QI, BF16), o_shape(B_KV, F32), o_shape(B_KV, F32),
                   o_shape(LANES, F32)],
        compiler_params=_params(2),
        name="b_project",
    )(x, mod, mod, w, cos, sin_signed, kn_g, kn_b)


def _rope_tables(pos):
    half = HEAD_DIM // 2
    inv = ROPE_THETA ** (-jnp.arange(half, dtype=F32) / half)
    ang = pos.astype(F32)[:, None] * inv[None, :]
    cos, sin = jnp.cos(ang), jnp.sin(ang)
    return (jnp.concatenate([cos, cos, cos, cos], axis=1),
            jnp.concatenate([-sin, sin, -sin, sin], axis=1))


def _key_norm_kernel(k_ref, grp_ref, o_ref):
    k = k_ref[0].astype(F32)
    n2 = jnp.dot(k * k, grp_ref[...], preferred_element_type=F32, precision=lax.Precision.HIGHEST)
    mx = jnp.broadcast_to(jnp.max(n2, axis=0, keepdims=True), o_ref.shape[1:])

    @pl.when(pl.program_id(1) == 0)
    def _():
        o_ref[0] = mx

    @pl.when(pl.program_id(1) > 0)
    def _():
        o_ref[0] = jnp.maximum(o_ref[0], mx)


def _key_norm_max(k):
    batch, lp, width = k.shape
    tile = DSA_KEY_TILE
    grp = (jnp.arange(width)[:, None] // HEAD_DIM == jnp.arange(LANES)[None, :]).astype(F32)
    return pl.pallas_call(
        _key_norm_kernel,
        grid=(batch, lp // tile),
        in_specs=[pl.BlockSpec((1, tile, width), lambda b, t: (b, t, 0)),
                  pl.BlockSpec((width, LANES), lambda b, t: (0, 0))],
        out_specs=pl.BlockSpec((1, 8, LANES), lambda b, t: (b, 0, 0)),
        out_shape=jax.ShapeDtypeStruct((batch, 8, LANES), F32),
        compiler_params=_params(2),
        name="key_norm_max",
    )(k, grp)


def _dsa_kernel(qt_ref, qit_ref, kwt_ref, ki_ref, k_ref, vta_ref, kmax_ref, o_ref,
                s_ref, qpad_ref, bound_ref, m_ref, acc_ref, *, causal, n_keys, k_sel):
    qn = qt_ref.shape[2]
    lt = DSA_KEY_TILE
    i = pl.program_id(1)
    if causal:
        n_tiles = ((i + 1) * qn + lt - 1) // lt
        qpos = i * qn + lax.broadcasted_iota(jnp.int32, (1, qn), 1)
        limit = (qpos // CHUNK + 1) * CHUNK
    else:
        n_tiles = ki_ref.shape[1] // lt
        limit = jnp.full((1, qn), n_keys, jnp.int32)

    def key_slice(t):
        return pl.ds(pl.multiple_of(t * lt, lt), lt)

    def key_pos(t):
        return t * lt + lax.broadcasted_iota(jnp.int32, (lt, 1), 0)

    def fold_rows(x, op):
        return op(op(x.reshape(lt // 32, 4, 8, x.shape[-1]), axis=0), axis=0)

    qit = qit_ref[0]
    kwt = kwt_ref[0]
    qi_w = jnp.concatenate([qit[h * IDX_DIM:(h + 1) * IDX_DIM, :] for h in range(IDX_HEADS)], axis=1)
    w_row = jnp.concatenate([kwt[IDX_DIM + h:IDX_DIM + h + 1, :] for h in range(IDX_HEADS)], axis=1)
    w_row = w_row * (IDX_DIM ** -0.5)

    def score_body(t, carry):
        rmax, rmin = carry
        s = jnp.dot(ki_ref[0, key_slice(t), :], qi_w, preferred_element_type=F32)
        s = jnp.maximum(s, 0.0) * w_row
        sc = s[:, 0:qn]
        for h in range(1, IDX_HEADS):
            sc = sc + s[:, h * qn:(h + 1) * qn]
        adm = key_pos(t) < limit
        s_ref[key_slice(t), :] = jnp.where(adm, sc, -jnp.inf)
        rmax = jnp.maximum(rmax, jnp.max(jnp.where(adm, sc, -jnp.inf), axis=0, keepdims=True))
        rmin = jnp.minimum(rmin, jnp.min(jnp.where(adm, sc, jnp.inf), axis=0, keepdims=True))
        return rmax, rmin

    rmax, rmin = lax.fori_loop(0, n_tiles, score_body,
                               (jnp.full((1, qn), -jnp.inf, F32), jnp.full((1, qn), jnp.inf, F32)))

    def count(pred):
        def body(t, acc):
            return acc + fold_rows(jnp.where(pred(s_ref[key_slice(t), :], key_pos(t)), 1.0, 0.0), jnp.sum)
        acc = lax.fori_loop(0, n_tiles, body, jnp.zeros((8, qn), F32))
        return jnp.sum(acc, axis=0, keepdims=True)

    def to_key(x):
        bits = lax.bitcast_convert_type(x, jnp.int32)
        return bits ^ ((bits >> 31) & 0x7FFFFFFF)

    def from_key(key):
        return lax.bitcast_convert_type(key ^ ((key >> 31) & 0x7FFFFFFF), F32)

    n_adm = limit.astype(F32)
    target = jnp.minimum(n_adm, float(k_sel))
    log_target = jnp.log(target)

    def any_left(done):
        return jnp.sum(done) < qn

    c_ge0 = count(lambda s, _: s >= 0.0)
    c_gt0 = count(lambda s, _: s > 0.0)
    is_pos = c_gt0 >= target
    is_neg = c_ge0 < target
    zero_key = jnp.zeros((1, qn), jnp.int32)
    lo0 = jnp.where(is_neg, to_key(rmin), zero_key)
    hi0 = jnp.where(is_pos, to_key(rmax) + 1, zero_key)
    c_lo0 = jnp.where(is_neg, n_adm, c_ge0)
    c_hi0 = jnp.where(is_pos, 0.0, c_ge0)
    done0 = jnp.where((n_adm == target) | jnp.logical_not(is_pos | is_neg) | (c_lo0 == target), 1.0, 0.0)
    lo0 = jnp.where(n_adm == target, to_key(rmin), lo0)
    c_lo0 = jnp.where(n_adm == target, n_adm, c_lo0)

    def sel_cond(st):
        return any_left(st[7])

    def sel_body(st):
        it, lo, hi, c_lo, c_hi, w_lo, w_hi, done, side = st
        t_lo, t_hi = from_key(lo), from_key(hi)
        f_lo = (jnp.log(c_lo) - log_target) * w_lo
        f_hi = (log_target - jnp.log(jnp.maximum(c_hi, 0.5))) * w_hi
        guess = to_key(t_lo + f_lo / (f_lo + f_hi) * (t_hi - t_lo))
        halve = (lo >> 1) + (hi >> 1) + (lo & hi & 1)
        mid = jnp.where(it >= SECANT_STEPS, halve, jnp.clip(guess, lo + 1, hi - 1))
        stuck = halve == lo
        c = count(lambda s, _: s >= from_key(mid))
        live = (done < 0.5) & jnp.logical_not(stuck)
        up = live & (c >= target)
        dn = live & (c < target)
        w_hi = jnp.where(up, jnp.where(side > 0, 0.5 * w_hi, 1.0), jnp.where(dn, 1.0, w_hi))
        w_lo = jnp.where(dn, jnp.where(side < 0, 0.5 * w_lo, 1.0), jnp.where(up, 1.0, w_lo))
        side = jnp.where(up, 1.0, jnp.where(dn, -1.0, side))
        lo = jnp.where(up, mid, lo)
        c_lo = jnp.where(up, c, c_lo)
        hi = jnp.where(dn, mid, hi)
        c_hi = jnp.where(dn, c, c_hi)
        done = jnp.where(stuck | (c_lo == target), 1.0, done)
        return it + 1, lo, hi, c_lo, c_hi, w_lo, w_hi, done, side

    ones = jnp.ones((1, qn), F32)
    sel = lax.while_loop(sel_cond, sel_body,
                         (jnp.int32(0), lo0, hi0, c_lo0, c_hi0, ones, ones, done0, 0.0 * ones))
    thr, c_thr = from_key(sel[1]), sel[3]

    tied = c_thr > target

    @pl.when(jnp.sum(jnp.where(tied, 1.0, 0.0)) > 0)
    def _():
        need = target - count(lambda s, _: s > thr)

        def idx_cond(st):
            return any_left(st[5])

        def idx_body(st):
            it, lo_j, hi_j, c_lo_j, c_hi_j, done = st
            span = (hi_j - lo_j).astype(F32)
            step = jnp.ceil((need - c_lo_j) / (c_hi_j - c_lo_j) * span).astype(jnp.int32)
            step = jnp.where(it % 2 == 0, step, (hi_j - lo_j) // 2)
            mid = lo_j + jnp.clip(step, 1, jnp.maximum(hi_j - lo_j - 1, 1))
            c = count(lambda s, kp: (s == thr) & (kp <= mid))
            live = (done < 0.5) & (hi_j - lo_j > 1)
            below = live & (c < need)
            above = live & (c >= need)
            lo_j = jnp.where(below, mid, lo_j)
            c_lo_j = jnp.where(below, c, c_lo_j)
            hi_j = jnp.where(above, mid, hi_j)
            c_hi_j = jnp.where(above, c, c_hi_j)
            done = jnp.where((c_hi_j == need) | (hi_j - lo_j <= 1), 1.0, done)
            return it + 1, lo_j, hi_j, c_lo_j, c_hi_j, done

        first = jnp.full((1, qn), -1, jnp.int32)
        final = jnp.full((1, qn), n_tiles * lt - 1, jnp.int32)
        n_ties = c_thr - (target - need)
        idx_done0 = jnp.where(tied & (n_ties > need), 0.0, 1.0)
        idx = lax.while_loop(idx_cond, idx_body,
                             (jnp.int32(0), first, final, 0.0 * ones, n_ties, idx_done0))
        last = idx[2]

        def drop_body(t, carry):
            s = s_ref[key_slice(t), :]
            s_ref[key_slice(t), :] = jnp.where(tied & (s == thr) & (key_pos(t) > last), -jnp.inf, s)
            return carry

        lax.fori_loop(0, n_tiles, drop_body, 0)

    qt = qt_ref[0]
    zeros = jnp.zeros((HEAD_DIM, B_GROUP * qn), BF16)
    for g in range(B_KV_HEADS):
        q_g = jnp.concatenate(
            [qt[(g * B_GROUP + j) * HEAD_DIM:(g * B_GROUP + j + 1) * HEAD_DIM, :] for j in range(B_GROUP)],
            axis=1)
        qpad_ref[g] = jnp.concatenate([zeros] * g + [q_g] + [zeros] * (B_KV_HEADS - 1 - g), axis=0)

    qf = qt.astype(F32)
    q_norm2 = jnp.sum((qf * qf).reshape(B_HEADS, HEAD_DIM, qn), axis=1)
    for g in range(B_KV_HEADS):
        b = jnp.sqrt(q_norm2[g * B_GROUP:(g + 1) * B_GROUP] * kmax_ref[0, 0:1, g:g + 1])
        b = b * BOUND_SLACK + BOUND_SLACK_ABS
        bound_ref[g] = jnp.concatenate([b[j:j + 1] for j in range(B_GROUP)], axis=1)
    acc_ref[...] = jnp.zeros(acc_ref.shape, F32)

    def fast_body(t, carry):
        keys = k_ref[0, key_slice(t), :]
        sel = jnp.where(s_ref[key_slice(t), :] >= thr, 1.0, 0.0).astype(BF16)
        sel = jnp.concatenate([sel] * B_GROUP, axis=1)
        def logits(g):
            return jnp.dot(keys, qpad_ref[g], preferred_element_type=F32)

        lg = logits(0)
        for g in range(B_KV_HEADS):
            lg_next = logits(g + 1) if g + 1 < B_KV_HEADS else None
            p = jnp.exp2(lg - bound_ref[g]).astype(BF16) * sel
            acc_ref[g] += jnp.dot(vta_ref[0, g, :, key_slice(t)], p, preferred_element_type=F32)
            lg = lg_next
        return carry

    lax.fori_loop(0, n_tiles, fast_body, 0)
    den_min = jnp.min(acc_ref[:, HEAD_DIM:HEAD_DIM + 1, :])

    @pl.when(jnp.logical_not(den_min >= MIN_TRUSTED_DENOMINATOR))
    def _():
        m_ref[...] = jnp.full(m_ref.shape, NEG_BIG, F32)
        acc_ref[...] = jnp.zeros(acc_ref.shape, F32)

        def exact_body(t, carry):
            keys = k_ref[0, key_slice(t), :]
            mb = jnp.where(s_ref[key_slice(t), :] >= thr, 0.0, -jnp.inf)
            mb = jnp.concatenate([mb] * B_GROUP, axis=1)
            for g in range(B_KV_HEADS):
                lg = jnp.dot(keys, qpad_ref[g], preferred_element_type=F32) + mb
                m_old = m_ref[g]
                m_new = jnp.maximum(m_old, jnp.max(fold_rows(lg, jnp.max), axis=0, keepdims=True))
                p = jnp.exp2(lg - m_new).astype(BF16)
                pv = jnp.dot(vta_ref[0, g, :, key_slice(t)], p, preferred_element_type=F32)
                acc_ref[g] = jnp.exp2(m_old - m_new) * acc_ref[g] + pv
                m_ref[g] = m_new
            return carry

        lax.fori_loop(0, n_tiles, exact_body, 0)

    for g in range(B_KV_HEADS):
        acc = acc_ref[g]
        o_g = acc[:HEAD_DIM] / acc[HEAD_DIM:HEAD_DIM + 1]
        for j in range(B_GROUP):
            hh = g * B_GROUP + j
            o_ref[0, hh * HEAD_DIM:(hh + 1) * HEAD_DIM, :] = o_g[:, j * qn:(j + 1) * qn].astype(o_ref.dtype)


V_AUG_ROWS = HEAD_DIM + 16


def _dsa(q, qi, kw, k, v, ki, *, q_tile, causal, k_sel):
    batch, seq, _ = q.shape
    n_keys = k.shape[1]
    lp = -(-n_keys // DSA_KEY_TILE) * DSA_KEY_TILE
    assert seq % q_tile == 0 and q_tile % LANES == 0
    pad = ((0, 0), (0, lp - n_keys), (0, 0))
    k, v, ki = (jnp.pad(a.astype(BF16), pad) for a in (k, v, ki))
    vt = jnp.swapaxes(v, 1, 2).reshape(batch, B_KV_HEADS, HEAD_DIM, lp)
    vta = jnp.concatenate([vt, jnp.ones((batch, B_KV_HEADS, 1, lp), BF16),
                           jnp.zeros((batch, B_KV_HEADS, V_AUG_ROWS - HEAD_DIM - 1, lp), BF16)], axis=2)
    qt, qit, kwt = (jnp.swapaxes(a, 1, 2) for a in (q, qi, kw))

    def qspec(rows):
        return pl.BlockSpec((1, rows, q_tile), lambda b, i: (b, 0, i))

    def resident(shape):
        zero = (0,) * len(shape)
        return pl.BlockSpec((1,) + shape, lambda b, i: (b,) + zero, pipeline_mode=pl.Buffered(1))

    ot = pl.pallas_call(
        functools.partial(_dsa_kernel, causal=causal, n_keys=n_keys, k_sel=k_sel),
        grid=(batch, seq // q_tile),
        in_specs=[qspec(B_Q), qspec(B_QI), qspec(LANES),
                  resident((lp, IDX_DIM)), resident((lp, B_KV)),
                  resident((B_KV_HEADS, V_AUG_ROWS, lp)),
                  pl.BlockSpec((1, 8, LANES), lambda b, i: (b, 0, 0))],
        out_specs=qspec(B_Q),
        out_shape=jax.ShapeDtypeStruct((batch, B_Q, seq), BF16),
        scratch_shapes=[pltpu.VMEM((lp, q_tile), F32),
                        pltpu.VMEM((B_KV_HEADS, B_KV, B_GROUP * q_tile), BF16),
                        pltpu.VMEM((B_KV_HEADS, 1, B_GROUP * q_tile), F32),
                        pltpu.VMEM((B_KV_HEADS, 1, B_GROUP * q_tile), F32),
                        pltpu.VMEM((B_KV_HEADS, V_AUG_ROWS, B_GROUP * q_tile), F32)],
        compiler_params=_params(2),
        name="dsa_prompt" if causal else "dsa_sample",
    )(qt, qit, kwt, ki, k, vta, _key_norm_max(k))
    return jnp.swapaxes(ot, 1, 2)


def kernel(x_prompt, x_sample, cache_k_a, cache_v_a, cache_k_b, cache_v_b, cache_kidx_b,
           c_prompt, c_sample, w_cond, b_cond, ln_g, ln_b, a_w_in, a_w_o, a_rel_bias,
           b_w_in, b_w_o, b_kidx_ln_g, b_kidx_ln_b, ffn_w_gu, ffn_w_down,
           moe_w_router, moe_b_router, moe_w_gu, moe_w_down):
    depth = w_cond.shape[0]
    alpha = (2 * depth) ** 0.25
    n_p, seq, d = x_prompt.shape
    n_s, dec_seq, _ = x_sample.shape
    past = cache_k_b.shape[2]
    a_hd = A_HEADS * HEAD_DIM

    rows = n_p + n_s
    rows_pad = -(-rows // 8) * 8
    c_all = jnp.pad(jnp.concatenate([c_prompt, c_sample], axis=0), ((0, rows_pad - rows), (0, 0)))
    mod_all = _modulation(c_all, w_cond, b_cond)

    xp, xs = x_prompt, x_sample
    outs = {k: [] for k in ("ka_p", "va_p", "kb_p", "vb_p", "ib_p", "ka_s", "va_s", "kb_s", "vb_s", "ib_s")}
    dummy_wr = jnp.zeros((d, LANES), F32)
    dummy_br = jnp.zeros((1, LANES), F32)
    for i in range(depth):
        j = i // 2
        mod_p = mod_all[i, :n_p].reshape(n_p, 1, 6 * d)
        mod_s = mod_all[i, n_p:rows].reshape(n_s, 1, 6 * d)
        g1, b1, g2, b2 = ln_g[i, 0], ln_b[i, 0], ln_g[i, 1], ln_b[i, 1]
        if i % 2 == 0:
            w_in = a_w_in[j].astype(BF16)
            keep = min(BAND_ROWS, seq)
            qkv_p = _modmm(xp, mod_p, w_in, BF16)
            kv_p = _modmm(xp, mod_p, w_in[:, a_hd:], F32, t_start=seq - keep)
            qkv_s = _modmm(xs, mod_s, w_in, BF16)
            kv_s = _modmm(xs, mod_s, w_in[:, a_hd:], F32)
            outs["ka_p"].append(kv_p[..., :a_hd].reshape(n_p, keep, A_HEADS, HEAD_DIM))
            outs["va_p"].append(kv_p[..., a_hd:].reshape(n_p, keep, A_HEADS, HEAD_DIM))
            outs["ka_s"].append(kv_s[..., :a_hd].reshape(n_s, dec_seq, A_HEADS, HEAD_DIM))
            outs["va_s"].append(kv_s[..., a_hd:].reshape(n_s, dec_seq, A_HEADS, HEAD_DIM))
            o_p = _band_attn_prompt(qkv_p, a_rel_bias[j])
            ck = cache_k_a[j].reshape(n_s, -1, a_hd).astype(BF16)
            cv = cache_v_a[j].reshape(n_s, -1, a_hd).astype(BF16)
            o_s = _band_attn_sample(qkv_s, ck, cv, a_rel_bias[j])
            w_o = a_w_o[j].astype(BF16)
        else:
            w_in = jnp.pad(b_w_in[j], ((0, 0), (0, B_PROJ_PAD - B_PROJ))).astype(BF16)
            kn_g = jnp.pad(b_kidx_ln_g[j], (0, LANES - IDX_DIM)).reshape(1, LANES)
            kn_b = jnp.pad(b_kidx_ln_b[j], (0, LANES - IDX_DIM)).reshape(1, LANES)
            cos_p, sin_p = _rope_tables(jnp.arange(seq))
            cos_s, sin_s = _rope_tables(past + jnp.arange(dec_seq))
            bb_s, _ = _row_blocks(n_s, dec_seq, ROW_TILE)
            cos_s, sin_s = jnp.tile(cos_s, (bb_s, 1)), jnp.tile(sin_s, (bb_s, 1))
            q_p, qi_p, k_p, v_p, kw_p = _b_project(xp, mod_p, w_in, cos_p, sin_p, kn_g, kn_b, False)
            q_s, qi_s, k_s, v_s, kw_s = _b_project(xs, mod_s, w_in, cos_s, sin_s, kn_g, kn_b, True)
            outs["kb_p"].append(k_p.reshape(n_p, seq, B_KV_HEADS, HEAD_DIM))
            outs["vb_p"].append(v_p.reshape(n_p, seq, B_KV_HEADS, HEAD_DIM))
            outs["ib_p"].append(kw_p[..., :IDX_DIM])
            outs["kb_s"].append(k_s.reshape(n_s, dec_seq, B_KV_HEADS, HEAD_DIM))
            outs["vb_s"].append(v_s.reshape(n_s, dec_seq, B_KV_HEADS, HEAD_DIM))
            outs["ib_s"].append(kw_s[..., :IDX_DIM])
            o_p = _dsa(q_p, qi_p, kw_p, k_p, v_p, kw_p[..., :IDX_DIM], q_tile=DSA_Q_TILE, causal=True,
                       k_sel=min(TOPK_MAX, seq // 4))
            n_keys = past + dec_seq
            kk = jnp.concatenate([cache_k_b[j].reshape(n_s, past, B_KV), k_s], axis=1)
            vc = jnp.concatenate([cache_v_b[j].reshape(n_s, past, B_KV), v_s], axis=1)
            kki = jnp.concatenate([cache_kidx_b[j], kw_s[..., :IDX_DIM]], axis=1)
            rep = LANES // dec_seq
            q_r, qi_r, kw_r = (jnp.concatenate([a] * rep, axis=1) for a in (q_s, qi_s, kw_s))
            o_s = _dsa(q_r, qi_r, kw_r, kk, vc, kki, q_tile=LANES, causal=False,
                       k_sel=min(TOPK_MAX, n_keys // 4))[:, :dec_seq]
            w_o = b_w_o[j].astype(BF16)
        xp = _mm_postnorm(o_p, w_o, xp, mod_p, 2, g1, b1, alpha)
        xs = _mm_postnorm(o_s, w_o, xs, mod_s, 2, g1, b1, alpha)
        if i % 2 == 0:
            w_gu = ffn_w_gu[j].astype(BF16)[None]
            w_dn = ffn_w_down[j].astype(BF16)[None]
            dense = dict(routed=False, ff_chunk=D_FF // 2, alpha=alpha)
            xp = _ffn(xp, mod_p, dummy_wr, dummy_br, w_gu, w_dn, g2, b2, **dense)
            xs = _ffn(xs, mod_s, dummy_wr, dummy_br, w_gu, w_dn, g2, b2, **dense)
        else:
            w_gu = moe_w_gu[j].astype(BF16)
            w_dn = moe_w_down[j].astype(BF16)
            w_r = jnp.pad(moe_w_router[j], ((0, 0), (0, LANES - N_EXPERTS)))
            b_r = jnp.pad(moe_b_router[j], (0, LANES - N_EXPERTS)).reshape(1, LANES)
            moe = dict(routed=True, ff_chunk=D_FF_EXPERT // 4, alpha=alpha)
            xp = _ffn(xp, mod_p, w_r, b_r, w_gu, w_dn, g2, b2, **moe)
            xs = _ffn(xs, mod_s, w_r, b_r, w_gu, w_dn, g2, b2, **moe)

    st = lambda name: jnp.stack(outs[name])
    return (xp, xs, st("ka_p"), st("va_p"), st("kb_p"), st("vb_p"), st("ib_p"),
            st("ka_s"), st("va_s"), st("kb_s"), st("vb_s"), st("ib_s"))
```

```python
import functools

import jax
import jax.numpy as jnp
from jax import lax
from jax.experimental import pallas as pl
from jax.experimental.pallas import tpu as pltpu

F32 = jnp.float32
BF16 = jnp.bfloat16

D_MODEL = 1024
CHUNK = 64
N_PAST_CHUNKS = 8
BAND_ROWS = N_PAST_CHUNKS * CHUNK
REL_CLIP = 2 * CHUNK
HEAD_DIM = 64
A_HEADS = 16
B_HEADS = 16
B_KV_HEADS = 4
B_GROUP = B_HEADS // B_KV_HEADS
IDX_HEADS = 8
IDX_DIM = 64
TOPK_MAX = 256
D_FF = 2816
N_EXPERTS = 8
D_FF_EXPERT = 3584
ROPE_THETA = 10000.0
LN_EPS = 1e-5
B_Q = B_HEADS * HEAD_DIM
B_KV = B_KV_HEADS * HEAD_DIM
B_QI = IDX_HEADS * IDX_DIM
B_PROJ = B_Q + 2 * B_KV + B_QI + IDX_DIM + IDX_HEADS

LANES = 128
VMEM_LIMIT_BYTES = 58 * 1024 * 1024

A_Q_TILE = 4 * CHUNK
DSA_Q_TILE = 4 * CHUNK
DSA_KEY_TILE = 512
ROW_TILE = 512
FFN_ROW_TILE = 1024
MOE_CAPACITY = 384
B_PROJ_PAD = B_Q + 2 * B_KV + B_QI + LANES
NEG_BIG = -1e30
QK_SCALE_LOG2 = HEAD_DIM ** -0.5 * 1.4426950408889634
BOUND_SLACK = 1.02
BOUND_SLACK_ABS = 0.01
MIN_TRUSTED_DENOMINATOR = 2.0 ** -100
SECANT_STEPS = 12


def _params(n_grid):
    return pltpu.CompilerParams(
        dimension_semantics=("arbitrary",) * n_grid,
        vmem_limit_bytes=VMEM_LIMIT_BYTES,
    )


def _row_blocks(batch, seq, target):
    if seq >= target:
        assert seq % target == 0
        return 1, target
    bb = max(1, min(batch, target // seq))
    while batch % bb:
        bb -= 1
    return bb, seq


def _mod_spec(bb, chunk, n_grid):
    if n_grid == 2:
        return pl.BlockSpec((bb, 1, D_MODEL), lambda b, t: (b, 0, chunk))
    return pl.BlockSpec((bb, 1, D_MODEL), lambda b, t, e, c: (b, 0, chunk))


def _silu(x):
    return x / (1.0 + jnp.exp(-x))


def _layer_norm_rows(z, g, b):
    mu = jnp.mean(z, axis=-1, keepdims=True)
    zc = z - mu
    var = jnp.mean(zc * zc, axis=-1, keepdims=True)
    return zc * lax.rsqrt(var + LN_EPS) * g + b


def _modulation_kernel(c_ref, w_ref, b_ref, o_ref):
    a = _silu(c_ref[...]).astype(BF16)
    w = w_ref[0].astype(BF16)
    o_ref[0] = jnp.dot(a, w, preferred_element_type=F32) + b_ref[0]


def _modulation(c_all, w_cond, b_cond):
    depth, d, n = w_cond.shape
    rows = c_all.shape[0]
    tn = 1536
    return pl.pallas_call(
        _modulation_kernel,
        grid=(depth, n // tn),
        in_specs=[
            pl.BlockSpec((rows, d), lambda i, j: (0, 0)),
            pl.BlockSpec((1, d, tn), lambda i, j: (i, 0, j)),
            pl.BlockSpec((1, 1, tn), lambda i, j: (i, 0, j)),
        ],
        out_specs=pl.BlockSpec((1, rows, tn), lambda i, j: (i, 0, j)),
        out_shape=jax.ShapeDtypeStruct((depth, rows, n), F32),
        compiler_params=_params(2),
        name="modulation",
    )(c_all, w_cond, b_cond.reshape(depth, 1, n))


def _modmm_kernel(x_ref, sc_ref, sh_ref, w_ref, o_ref):
    bb, tt, d = x_ref.shape
    h = x_ref[...] * (1.0 + sc_ref[...]) + sh_ref[...]
    h = h.reshape(bb * tt, d).astype(BF16)
    acc = jnp.dot(h, w_ref[...], preferred_element_type=F32)
    o_ref[...] = acc.reshape(bb, tt, acc.shape[-1]).astype(o_ref.dtype)


def _modmm(x, mod, w, out_dtype, t_start=0):
    batch, seq, d = x.shape
    n = w.shape[1]
    seq_out = seq - t_start
    bb, tt = _row_blocks(batch, seq_out, ROW_TILE)
    assert t_start % tt == 0
    off = t_start // tt
    return pl.pallas_call(
        _modmm_kernel,
        grid=(batch // bb, seq_out // tt),
        in_specs=[
            pl.BlockSpec((bb, tt, d), lambda b, t: (b, t + off, 0)),
            _mod_spec(bb, 1, 2),
            _mod_spec(bb, 0, 2),
            pl.BlockSpec((d, n), lambda b, t: (0, 0)),
        ],
        out_specs=pl.BlockSpec((bb, tt, n), lambda b, t: (b, t, 0)),
        out_shape=jax.ShapeDtypeStruct((batch, seq_out, n), out_dtype),
        compiler_params=_params(2),
        name="a_project",
    )(x, mod, mod, w)


def _band_attn_kernel(*refs, n_kb, n_maybe_invalid):
    q_ref = refs[0]
    k_refs = refs[1:1 + n_kb]
    v_refs = refs[1 + n_kb:1 + 2 * n_kb]
    f_ref = refs[1 + 2 * n_kb]
    o_ref = refs[2 + 2 * n_kb]
    bias_ref = refs[3 + 2 * n_kb]
    q_tile = q_ref.shape[1]
    kb_sizes = [r.shape[1] for r in k_refs]
    k_tot = sum(kb_sizes)
    width = f_ref.shape[-1]
    i = pl.program_id(1)

    @pl.when((pl.program_id(0) == 0) & (i == 0))
    def _():
        rq = lax.broadcasted_iota(jnp.int32, (q_tile, k_tot), 0) // CHUNK
        ck = lax.broadcasted_iota(jnp.int32, (q_tile, k_tot), 1) // CHUNK
        in_band = (ck >= rq) & (ck <= rq + N_PAST_CHUNKS)
        for h in range(A_HEADS):
            rows = jnp.broadcast_to(f_ref[h], (q_tile, width))
            toep = pltpu.roll(rows, k_tot + 1, 1, stride=1, stride_axis=0)
            bias_ref[h] = jnp.where(in_band, toep[:, :k_tot], -jnp.inf)

    q = q_ref[0] * (HEAD_DIM ** -0.5)
    for h in range(A_HEADS):
        cols = slice(h * HEAD_DIM, (h + 1) * HEAD_DIM)
        qh = q[:, cols]
        parts = []
        for kb in range(n_kb):
            lg = lax.dot_general(qh, k_refs[kb][0, :, cols], (((1,), (1,)), ((), ())),
                                 preferred_element_type=F32)
            if kb < n_maybe_invalid:
                lg = jnp.where(i - n_maybe_invalid + kb >= 0, lg, -jnp.inf)
            parts.append(lg)
        logits = jnp.concatenate(parts, axis=1) + bias_ref[h]
        m = jnp.max(logits, axis=1, keepdims=True)
        p = jnp.exp(logits - m)
        l = jnp.sum(p, axis=1, keepdims=True)
        pb = p.astype(BF16)
        acc = jnp.zeros((q_tile, HEAD_DIM), F32)
        start = 0
        for kb in range(n_kb):
            acc = acc + jnp.dot(pb[:, start:start + kb_sizes[kb]], v_refs[kb][0, :, cols],
                                preferred_element_type=F32)
            start += kb_sizes[kb]
        o_ref[0, :, cols] = (acc / l).astype(o_ref.dtype)


def _bias_vector(table, q_tile, k_tot):
    width = k_tot + q_tile
    lo = -(q_tile - 1) - BAND_ROWS + REL_CLIP
    left = max(0, -lo)
    start = max(0, lo)
    n_mid = min(2 * REL_CLIP + 1 - start, width - left)
    right = width - left - n_mid
    mid = table.T[:, start:start + n_mid]
    f = jnp.pad(mid, ((0, 0), (left, right)), mode="edge")
    return f.reshape(A_HEADS, 1, width)


def _band_attn_prompt(qkv, table):
    batch, seq, _ = qkv.shape
    hd = A_HEADS * HEAD_DIM
    qt = A_Q_TILE
    n_prev = BAND_ROWS // qt
    n_kb = n_prev + 1
    k_tot = n_kb * qt
    f = _bias_vector(table, qt, k_tot)

    def kv_spec(kb, col):
        return pl.BlockSpec((1, qt, hd), lambda b, i: (b, jnp.maximum(i - n_prev + kb, 0), col))

    return pl.pallas_call(
        functools.partial(_band_attn_kernel, n_kb=n_kb, n_maybe_invalid=n_prev),
        grid=(batch, seq // qt),
        in_specs=[pl.BlockSpec((1, qt, hd), lambda b, i: (b, i, 0))]
        + [kv_spec(kb, 1) for kb in range(n_kb)]
        + [kv_spec(kb, 2) for kb in range(n_kb)]
        + [pl.BlockSpec(f.shape, lambda b, i: (0, 0, 0))],
        out_specs=pl.BlockSpec((1, qt, hd), lambda b, i: (b, i, 0)),
        out_shape=jax.ShapeDtypeStruct((batch, seq, hd), BF16),
        scratch_shapes=[pltpu.VMEM((A_HEADS, qt, k_tot), F32)],
        compiler_params=_params(2),
        name="band_attn_prompt",
    )(*([qkv] * (1 + 2 * n_kb)), f)


def _band_attn_sample(qkv, cache_k, cache_v, table):
    batch, seq, _ = qkv.shape
    hd = A_HEADS * HEAD_DIM
    win = cache_k.shape[1]
    assert seq == CHUNK and win == BAND_ROWS
    k_tot = win + seq
    f = _bias_vector(table, seq, k_tot)
    new = lambda col: pl.BlockSpec((1, seq, hd), lambda b, i: (b, 0, col))
    old = pl.BlockSpec((1, win, hd), lambda b, i: (b, 0, 0))
    return pl.pallas_call(
        functools.partial(_band_attn_kernel, n_kb=2, n_maybe_invalid=0),
        grid=(batch, 1),
        in_specs=[new(0), old, new(1), old, new(2), pl.BlockSpec(f.shape, lambda b, i: (0, 0, 0))],
        out_specs=pl.BlockSpec((1, seq, hd), lambda b, i: (b, 0, 0)),
        out_shape=jax.ShapeDtypeStruct((batch, seq, hd), BF16),
        scratch_shapes=[pltpu.VMEM((A_HEADS, seq, k_tot), F32)],
        compiler_params=_params(2),
        name="band_attn_sample",
    )(qkv, cache_k, qkv, cache_v, qkv, f)


def _mm_postnorm_kernel(o_ref, w_ref, x_ref, gate_ref, g_ref, b_ref, y_ref, *, alpha):
    bb, tt, d = x_ref.shape
    o = o_ref[...].reshape(bb * tt, o_ref.shape[-1])
    sub = jnp.dot(o, w_ref[...], preferred_element_type=F32).reshape(bb, tt, d)
    z = alpha * x_ref[...] + (1.0 + gate_ref[...]) * sub
    y_ref[...] = _layer_norm_rows(z, g_ref[...], b_ref[...])


def _mm_postnorm(o, w, x, mod, gate_chunk, g, b, alpha):
    batch, seq, d = x.shape
    bb, tt = _row_blocks(batch, seq, ROW_TILE)
    k = o.shape[-1]
    return pl.pallas_call(
        functools.partial(_mm_postnorm_kernel, alpha=alpha),
        grid=(batch // bb, seq // tt),
        in_specs=[
            pl.BlockSpec((bb, tt, k), lambda b_, t: (b_, t, 0)),
            pl.BlockSpec((k, d), lambda b_, t: (0, 0)),
            pl.BlockSpec((bb, tt, d), lambda b_, t: (b_, t, 0)),
            _mod_spec(bb, gate_chunk, 2),
            pl.BlockSpec((1, 1, d), lambda b_, t: (0, 0, 0)),
            pl.BlockSpec((1, 1, d), lambda b_, t: (0, 0, 0)),
        ],
        out_specs=pl.BlockSpec((bb, tt, d), lambda b_, t: (b_, t, 0)),
        out_shape=jax.ShapeDtypeStruct((batch, seq, d), F32),
        compiler_params=_params(2),
        name="out_proj_postnorm",
    )(o, w, x, mod, g.reshape(1, 1, d), b.reshape(1, 1, d))


def _ffn_kernel(x_ref, sc_ref, sh_ref, gate_ref, wr_ref, br_ref, wg_ref, wu_ref, wd_ref,
                g_ref, b_ref, y_ref, h_ref, comb_ref, acc_ref, *, routed, alpha):
    bb, tt, d = x_ref.shape
    rows = bb * tt
    e = pl.program_id(2)
    c = pl.program_id(3)
    lane = lax.broadcasted_iota(jnp.int32, (rows, LANES), 1)

    @pl.when((e == 0) & (c == 0))
    def _():
        h = (x_ref[...] * (1.0 + sc_ref[...]) + sh_ref[...]).reshape(rows, d)
        h_ref[...] = h.astype(BF16)
        acc_ref[...] = jnp.zeros_like(acc_ref)
        if routed:
            logits = jnp.dot(h, wr_ref[...], preferred_element_type=F32,
                             precision=lax.Precision.HIGHEST) + br_ref[...]
            logits = jnp.where(lane < N_EXPERTS, logits, -jnp.inf)
            m1 = jnp.max(logits, axis=1, keepdims=True)
            i1 = jnp.min(jnp.where(logits == m1, lane, LANES), axis=1, keepdims=True)
            rest = jnp.where(lane == i1, -jnp.inf, logits)
            m2 = jnp.max(rest, axis=1, keepdims=True)
            i2 = jnp.min(jnp.where(rest == m2, lane, LANES), axis=1, keepdims=True)
            e2 = jnp.exp(m2 - m1)
            den = 1.0 + e2
            comb_ref[...] = jnp.where(lane == i1, 1.0 / den, 0.0) + jnp.where(lane == i2, e2 / den, 0.0)

    h = h_ref[...]
    gt = jnp.dot(h, wg_ref[0], preferred_element_type=F32)
    up = jnp.dot(h, wu_ref[0], preferred_element_type=F32)
    act = (_silu(gt) * up).astype(BF16)
    part = jnp.dot(act, wd_ref[0], preferred_element_type=F32)
    if routed:
        ce = jnp.sum(jnp.where(lane == e, comb_ref[...], 0.0), axis=1, keepdims=True)
        part = ce * part
    acc_ref[...] += part

    @pl.when((e == pl.num_programs(2) - 1) & (c == pl.num_programs(3) - 1))
    def _():
        z = alpha * x_ref[...] + (1.0 + gate_ref[...]) * acc_ref[...].reshape(bb, tt, d)
        y_ref[...] = _layer_norm_rows(z, g_ref[...], b_ref[...])


def _ffn(x, mod, w_r, b_r, w_gu, w_down, g, b, *, routed, ff_chunk, alpha):
    batch, seq, d = x.shape
    n_e, ff, _ = w_down.shape
    assert ff % ff_chunk == 0
    n_c = ff // ff_chunk
    bb, tt = _row_blocks(batch, seq, FFN_ROW_TILE)
    rows = bb * tt
    vec = pl.BlockSpec((1, 1, d), lambda b_, t, e, c: (0, 0, 0))
    return pl.pallas_call(
        functools.partial(_ffn_kernel, routed=routed, alpha=alpha),
        grid=(batch // bb, seq // tt, n_e, n_c),
        in_specs=[
            pl.BlockSpec((bb, tt, d), lambda b_, t, e, c: (b_, t, 0)),
            _mod_spec(bb, 4, 4),
            _mod_spec(bb, 3, 4),
            _mod_spec(bb, 5, 4),
            pl.BlockSpec((d, LANES), lambda b_, t, e, c: (0, 0)),
            pl.BlockSpec((1, LANES), lambda b_, t, e, c: (0, 0)),
            pl.BlockSpec((1, d, ff_chunk), lambda b_, t, e, c: (e, 0, c)),
            pl.BlockSpec((1, d, ff_chunk), lambda b_, t, e, c: (e, 0, n_c + c)),
            pl.BlockSpec((1, ff_chunk, d), lambda b_, t, e, c: (e, c, 0)),
            vec,
            vec,
        ],
        out_specs=pl.BlockSpec((bb, tt, d), lambda b_, t, e, c: (b_, t, 0)),
        out_shape=jax.ShapeDtypeStruct((batch, seq, d), F32),
        scratch_shapes=[
            pltpu.VMEM((rows, d), BF16),
            pltpu.VMEM((rows, LANES), F32),
            pltpu.VMEM((rows, d), F32),
        ],
        compiler_params=_params(4),
        name="moe_ffn" if routed else "dense_ffn",
    )(x, mod, mod, mod, w_r, b_r, w_gu, w_gu, w_down, g.reshape(1, 1, d), b.reshape(1, 1, d))


def _moe_kernel(x_ref, sc_ref, sh_ref, gate_ref, wr_ref, br_ref, wg_ref, wu_ref, wd_ref, g_ref, b_ref,
                y_ref, h_ref, comb_ref, slot_ref, cnt_ref, tri_ref, hc_ref, acc_ref, out_ref, *, alpha):
    bb, tt, d = x_ref.shape
    rows = bb * tt
    cap = hc_ref.shape[1]
    e = pl.program_id(2)
    c = pl.program_id(3)
    last_c = pl.num_programs(3) - 1
    lane = lax.broadcasted_iota(jnp.int32, (rows, LANES), 1)

    @pl.when((pl.program_id(0) == 0) & (pl.program_id(1) == 0) & (e == 0) & (c == 0))
    def _():
        r = lax.broadcasted_iota(jnp.int32, (rows, rows), 0)
        col = lax.broadcasted_iota(jnp.int32, (rows, rows), 1)
        tri_ref[...] = jnp.where(r <= col, 1.0, 0.0).astype(BF16)

    @pl.when((e == 0) & (c == 0))
    def _():
        h = (x_ref[...] * (1.0 + sc_ref[...]) + sh_ref[...]).reshape(rows, d)
        h_ref[...] = h.astype(BF16)
        out_ref[...] = jnp.zeros_like(out_ref)
        logits = jnp.dot(h, wr_ref[...], preferred_element_type=F32,
                         precision=lax.Precision.HIGHEST) + br_ref[...]
        logits = jnp.where(lane < N_EXPERTS, logits, -jnp.inf)
        m1 = jnp.max(logits, axis=1, keepdims=True)
        i1 = jnp.min(jnp.where(logits == m1, lane, LANES), axis=1, keepdims=True)
        rest = jnp.where(lane == i1, -jnp.inf, logits)
        m2 = jnp.max(rest, axis=1, keepdims=True)
        i2 = jnp.min(jnp.where(rest == m2, lane, LANES), axis=1, keepdims=True)
        e2 = jnp.exp(m2 - m1)
        den = 1.0 + e2
        comb_ref[...] = jnp.where(lane == i1, 1.0 / den, 0.0) + jnp.where(lane == i2, e2 / den, 0.0)
        member = jnp.where((lane == i1) | (lane == i2), 1.0, 0.0)
        member_t = member.T[:2 * N_EXPERTS]
        upto = jnp.dot(member_t.astype(BF16), tri_ref[...], preferred_element_type=F32)
        slot_ref[...] = jnp.where(member_t > 0, upto - 1.0, -1.0)
        cnt_ref[...] = jnp.broadcast_to(upto[:, rows - 1:rows], cnt_ref.shape)

    n_sub = (jnp.max(cnt_ref[pl.ds(e, 1), :]).astype(jnp.int32) + cap - 1) // cap

    def selection(s):
        slot = slot_ref[pl.ds(e, 1), :] - (s * cap).astype(F32)
        r = lax.broadcasted_iota(jnp.int32, (cap, rows), 0).astype(F32)
        return jnp.where(slot == r, 1.0, 0.0).astype(BF16)

    @pl.when(c == 0)
    def _():
        def pack(s, carry):
            hc_ref[s] = jnp.dot(selection(s), h_ref[...], preferred_element_type=F32).astype(BF16)
            acc_ref[s] = jnp.zeros(acc_ref.shape[1:], F32)
            return carry
        lax.fori_loop(0, n_sub, pack, 0)

    def expert(s, carry):
        hc = hc_ref[s]
        gt = jnp.dot(hc, wg_ref[0], preferred_element_type=F32)
        up = jnp.dot(hc, wu_ref[0], preferred_element_type=F32)
        act = (_silu(gt) * up).astype(BF16)
        acc_ref[s] += jnp.dot(act, wd_ref[0], preferred_element_type=F32)
        return carry

    lax.fori_loop(0, n_sub, expert, 0)

    @pl.when(c == last_c)
    def _():
        ce = jnp.sum(jnp.where(lane == e, comb_ref[...], 0.0), axis=1, keepdims=True)

        def unpack(s, carry):
            back = lax.dot_general(selection(s), acc_ref[s].astype(BF16), (((0,), (0,)), ((), ())),
                                   preferred_element_type=F32)
            out_ref[...] += ce * back
            return carry
        lax.fori_loop(0, n_sub, unpack, 0)

    @pl.when((e == pl.num_programs(2) - 1) & (c == last_c))
    def _():
        z = alpha * x_ref[...] + (1.0 + gate_ref[...]) * out_ref[...].reshape(bb, tt, d)
        y_ref[...] = _layer_norm_rows(z, g_ref[...], b_ref[...])


def _moe(x, mod, w_r, b_r, w_gu, w_down, g, b, *, ff_chunk, alpha):
    batch, seq, d = x.shape
    n_e, ff, _ = w_down.shape
    assert ff % ff_chunk == 0
    n_c = ff // ff_chunk
    bb, tt = _row_blocks(batch, seq, FFN_ROW_TILE)
    rows = bb * tt
    cap = MOE_CAPACITY
    max_sub = -(-rows // cap)
    vec = pl.BlockSpec((1, 1, d), lambda b_, t, e, c: (0, 0, 0))
    return pl.pallas_call(
        functools.partial(_moe_kernel, alpha=alpha),
        grid=(batch // bb, seq // tt, n_e, n_c),
        in_specs=[
            pl.BlockSpec((bb, tt, d), lambda b_, t, e, c: (b_, t, 0), pipeline_mode=pl.Buffered(1)),
            _mod_spec(bb, 4, 4),
            _mod_spec(bb, 3, 4),
            _mod_spec(bb, 5, 4),
            pl.BlockSpec((d, LANES), lambda b_, t, e, c: (0, 0)),
            pl.BlockSpec((1, LANES), lambda b_, t, e, c: (0, 0)),
            pl.BlockSpec((1, d, ff_chunk), lambda b_, t, e, c: (e, 0, c)),
            pl.BlockSpec((1, d, ff_chunk), lambda b_, t, e, c: (e, 0, n_c + c)),
            pl.BlockSpec((1, ff_chunk, d), lambda b_, t, e, c: (e, c, 0)),
            vec,
            vec,
        ],
        out_specs=pl.BlockSpec((bb, tt, d), lambda b_, t, e, c: (b_, t, 0)),
        out_shape=jax.ShapeDtypeStruct((batch, seq, d), F32),
        scratch_shapes=[
            pltpu.VMEM((rows, d), BF16),
            pltpu.VMEM((rows, LANES), F32),
            pltpu.VMEM((2 * N_EXPERTS, rows), F32),
            pltpu.VMEM((2 * N_EXPERTS, LANES), F32),
            pltpu.VMEM((rows, rows), BF16),
            pltpu.VMEM((max_sub, cap, d), BF16),
            pltpu.VMEM((max_sub, cap, d), F32),
            pltpu.VMEM((rows, d), F32),
        ],
        compiler_params=_params(4),
        name="moe_ffn",
    )(x, mod, mod, mod, w_r, b_r, w_gu, w_gu, w_down, g.reshape(1, 1, d), b.reshape(1, 1, d))


def _rope_lanes(seg, cos, sin_signed, first_half):
    width = seg.shape[-1]
    reps = width // LANES
    if reps > 1:
        cos = jnp.concatenate([cos] * reps, axis=1)
        sin_signed = jnp.concatenate([sin_signed] * reps, axis=1)
        first_half = jnp.concatenate([first_half] * reps, axis=1)
    half = HEAD_DIM // 2
    swapped = jnp.where(first_half, pltpu.roll(seg, width - half, 1), pltpu.roll(seg, half, 1))
    return seg * cos + swapped * sin_signed


def _b_project_kernel(x_ref, sc_ref, sh_ref, w_ref, cos_ref, sin_ref, kg_ref, kb_ref,
                      q_ref, qi_ref, k_ref, v_ref, kw_ref):
    bb, tt, d = x_ref.shape
    rows = bb * tt
    h = x_ref[...] * (1.0 + sc_ref[...]) + sh_ref[...]
    h = h.reshape(rows, d).astype(BF16)
    acc = jnp.dot(h, w_ref[...], preferred_element_type=F32)
    cos = cos_ref[...]
    sin_signed = sin_ref[...]
    lane = lax.broadcasted_iota(jnp.int32, (1, LANES), 1)
    first_half = (lane % HEAD_DIM) < (HEAD_DIM // 2)

    def out(ref, val):
        ref[...] = val.reshape(bb, tt, val.shape[-1]).astype(ref.dtype)

    o_k, o_v, o_qi, o_ki = B_Q, B_Q + B_KV, B_Q + 2 * B_KV, B_Q + 2 * B_KV + B_QI
    out(q_ref, _rope_lanes(acc[:, :o_k], cos, sin_signed, first_half) * QK_SCALE_LOG2)
    out(k_ref, _rope_lanes(acc[:, o_k:o_v], cos, sin_signed, first_half))
    out(v_ref, acc[:, o_v:o_qi])
    out(qi_ref, _rope_lanes(acc[:, o_qi:o_ki], cos, sin_signed, first_half))
    seg = acc[:, o_ki:]
    is_ki = lane < IDX_DIM
    mu = jnp.sum(jnp.where(is_ki, seg, 0.0), axis=1, keepdims=True) / IDX_DIM
    cen = jnp.where(is_ki, seg - mu, 0.0)
    var = jnp.sum(cen * cen, axis=1, keepdims=True) / IDX_DIM
    ki = cen * lax.rsqrt(var + LN_EPS) * kg_ref[...] + kb_ref[...]
    ki = _rope_lanes(ki, cos, sin_signed, first_half)
    out(kw_ref, jnp.where(is_ki, ki, seg * (IDX_HEADS ** -0.5)))


def _b_project(x, mod, w, cos, sin_signed, kn_g, kn_b, table_per_tile):
    batch, seq, d = x.shape
    bb, tt = _row_blocks(batch, seq, ROW_TILE)
    rows = bb * tt
    n = w.shape[1]
    tab = pl.BlockSpec((rows, LANES), (lambda b, t: (0, 0)) if table_per_tile else (lambda b, t: (t, 0)))
    vec = pl.BlockSpec((1, LANES), lambda b, t: (0, 0))

    def o_spec(width):
        return pl.BlockSpec((bb, tt, width), lambda b, t: (b, t, 0))

    def o_shape(width, dtype):
        return jax.ShapeDtypeStruct((batch, seq, width), dtype)

    return pl.pallas_call(
        _b_project_kernel,
        grid=(batch // bb, seq // tt),
        in_specs=[
            pl.BlockSpec((bb, tt, d), lambda b, t: (b, t, 0)),
            _mod_spec(bb, 1, 2),
            _mod_spec(bb, 0, 2),
            pl.BlockSpec((d, n), lambda b, t: (0, 0)),
            tab,
            tab,
            vec,
            vec,
        ],
        out_specs=[o_spec(B_Q), o_spec(B_QI), o_spec(B_KV), o_spec(B_KV), o_spec(LANES)],
        out_shape=[o_shape(B_Q, BF16), o_shape(B_QI, BF16), o_shape(B_KV, F32), o_shape(B_KV, F32),
                   o_shape(LANES, F32)],
        compiler_params=_params(2),
        name="b_project",
    )(x, mod, mod, w, cos, sin_signed, kn_g, kn_b)


def _rope_tables(pos):
    half = HEAD_DIM // 2
    inv = ROPE_THETA ** (-jnp.arange(half, dtype=F32) / half)
    ang = pos.astype(F32)[:, None] * inv[None, :]
    cos, sin = jnp.cos(ang), jnp.sin(ang)
    return (jnp.concatenate([cos, cos, cos, cos], axis=1),
            jnp.concatenate([-sin, sin, -sin, sin], axis=1))


def _key_norm_kernel(k_ref, grp_ref, o_ref):
    k = k_ref[0].astype(F32)
    n2 = jnp.dot(k * k, grp_ref[...], preferred_element_type=F32, precision=lax.Precision.HIGHEST)
    mx = jnp.broadcast_to(jnp.max(n2, axis=0, keepdims=True), o_ref.shape[1:])

    @pl.when(pl.program_id(1) == 0)
    def _():
        o_ref[0] = mx

    @pl.when(pl.program_id(1) > 0)
    def _():
        o_ref[0] = jnp.maximum(o_ref[0], mx)


def _key_norm_max(k):
    batch, lp, width = k.shape
    tile = DSA_KEY_TILE
    grp = (jnp.arange(width)[:, None] // HEAD_DIM == jnp.arange(LANES)[None, :]).astype(F32)
    return pl.pallas_call(
        _key_norm_kernel,
        grid=(batch, lp // tile),
        in_specs=[pl.BlockSpec((1, tile, width), lambda b, t: (b, t, 0)),
                  pl.BlockSpec((width, LANES), lambda b, t: (0, 0))],
        out_specs=pl.BlockSpec((1, 8, LANES), lambda b, t: (b, 0, 0)),
        out_shape=jax.ShapeDtypeStruct((batch, 8, LANES), F32),
        compiler_params=_params(2),
        name="key_norm_max",
    )(k, grp)


def _dsa_kernel(qt_ref, qit_ref, kwt_ref, ki_ref, k_ref, vta_ref, kmax_ref, o_ref,
                s_ref, qpad_ref, bound_ref, m_ref, acc_ref, *, causal, n_keys, k_sel):
    qn = qt_ref.shape[2]
    lt = DSA_KEY_TILE
    i = pl.program_id(1)
    if causal:
        n_tiles = ((i + 1) * qn + lt - 1) // lt
        qpos = i * qn + lax.broadcasted_iota(jnp.int32, (1, qn), 1)
        limit = (qpos // CHUNK + 1) * CHUNK
    else:
        n_tiles = ki_ref.shape[1] // lt
        limit = jnp.full((1, qn), n_keys, jnp.int32)

    def key_slice(t):
        return pl.ds(pl.multiple_of(t * lt, lt), lt)

    def key_pos(t):
        return t * lt + lax.broadcasted_iota(jnp.int32, (lt, 1), 0)

    def fold_rows(x, op):
        return op(op(x.reshape(lt // 32, 4, 8, x.shape[-1]), axis=0), axis=0)

    qit = qit_ref[0]
    kwt = kwt_ref[0]
    qi_w = jnp.concatenate([qit[h * IDX_DIM:(h + 1) * IDX_DIM, :] for h in range(IDX_HEADS)], axis=1)
    w_row = jnp.concatenate([kwt[IDX_DIM + h:IDX_DIM + h + 1, :] for h in range(IDX_HEADS)], axis=1)
    w_row = w_row * (IDX_DIM ** -0.5)

    def score_body(t, carry):
        rmax, rmin = carry
        s = jnp.dot(ki_ref[0, key_slice(t), :], qi_w, preferred_element_type=F32)
        s = jnp.maximum(s, 0.0) * w_row
        sc = s[:, 0:qn]
        for h in range(1, IDX_HEADS):
            sc = sc + s[:, h * qn:(h + 1) * qn]
        adm = key_pos(t) < limit
        s_ref[key_slice(t), :] = jnp.where(adm, sc, -jnp.inf)
        rmax = jnp.maximum(rmax, jnp.max(jnp.where(adm, sc, -jnp.inf), axis=0, keepdims=True))
        rmin = jnp.minimum(rmin, jnp.min(jnp.where(adm, sc, jnp.inf), axis=0, keepdims=True))
        return rmax, rmin

    rmax, rmin = lax.fori_loop(0, n_tiles, score_body,
                               (jnp.full((1, qn), -jnp.inf, F32), jnp.full((1, qn), jnp.inf, F32)))

    def count(pred):
        def body(t, acc):
            return acc + fold_rows(jnp.where(pred(s_ref[key_slice(t), :], key_pos(t)), 1.0, 0.0), jnp.sum)
        acc = lax.fori_loop(0, n_tiles, body, jnp.zeros((8, qn), F32))
        return jnp.sum(acc, axis=0, keepdims=True)

    def to_key(x):
        bits = lax.bitcast_convert_type(x, jnp.int32)
        return bits ^ ((bits >> 31) & 0x7FFFFFFF)

    def from_key(key):
        return lax.bitcast_convert_type(key ^ ((key >> 31) & 0x7FFFFFFF), F32)

    n_adm = limit.astype(F32)
    target = jnp.minimum(n_adm, float(k_sel))
    log_target = jnp.log(target)

    def any_left(done):
        return jnp.sum(done) < qn

    c_ge0 = count(lambda s, _: s >= 0.0)
    c_gt0 = count(lambda s, _: s > 0.0)
    is_pos = c_gt0 >= target
    is_neg = c_ge0 < target
    zero_key = jnp.zeros((1, qn), jnp.int32)
    lo0 = jnp.where(is_neg, to_key(rmin), zero_key)
    hi0 = jnp.where(is_pos, to_key(rmax) + 1, zero_key)
    c_lo0 = jnp.where(is_neg, n_adm, c_ge0)
    c_hi0 = jnp.where(is_pos, 0.0, c_ge0)
    done0 = jnp.where((n_adm == target) | jnp.logical_not(is_pos | is_neg) | (c_lo0 == target), 1.0, 0.0)
    lo0 = jnp.where(n_adm == target, to_key(rmin), lo0)
    c_lo0 = jnp.where(n_adm == target, n_adm, c_lo0)

    def sel_cond(st):
        return any_left(st[7])

    def sel_body(st):
        it, lo, hi, c_lo, c_hi, w_lo, w_hi, done, side = st
        t_lo, t_hi = from_key(lo), from_key(hi)
        f_lo = (jnp.log(c_lo) - log_target) * w_lo
        f_hi = (log_target - jnp.log(jnp.maximum(c_hi, 0.5))) * w_hi
        guess = to_key(t_lo + f_lo / (f_lo + f_hi) * (t_hi - t_lo))
        halve = (lo >> 1) + (hi >> 1) + (lo & hi & 1)
        mid = jnp.where(it >= SECANT_STEPS, halve, jnp.clip(guess, lo + 1, hi - 1))
        stuck = halve == lo
        c = count(lambda s, _: s >= from_key(mid))
        live = (done < 0.5) & jnp.logical_not(stuck)
        up = live & (c >= target)
        dn = live & (c < target)
        w_hi = jnp.where(up, jnp.where(side > 0, 0.5 * w_hi, 1.0), jnp.where(dn, 1.0, w_hi))
        w_lo = jnp.where(dn, jnp.where(side < 0, 0.5 * w_lo, 1.0), jnp.where(up, 1.0, w_lo))
        side = jnp.where(up, 1.0, jnp.where(dn, -1.0, side))
        lo = jnp.where(up, mid, lo)
        c_lo = jnp.where(up, c, c_lo)
        hi = jnp.where(dn, mid, hi)
        c_hi = jnp.where(dn, c, c_hi)
        done = jnp.where(stuck | (c_lo == target), 1.0, done)
        return it + 1, lo, hi, c_lo, c_hi, w_lo, w_hi, done, side

    ones = jnp.ones((1, qn), F32)
    sel = lax.while_loop(sel_cond, sel_body,
                         (jnp.int32(0), lo0, hi0, c_lo0, c_hi0, ones, ones, done0, 0.0 * ones))
    thr, c_thr = from_key(sel[1]), sel[3]

    tied = c_thr > target

    @pl.when(jnp.sum(jnp.where(tied, 1.0, 0.0)) > 0)
    def _():
        need = target - count(lambda s, _: s > thr)

        def idx_cond(st):
            return any_left(st[5])

        def idx_body(st):
            it, lo_j, hi_j, c_lo_j, c_hi_j, done = st
            span = (hi_j - lo_j).astype(F32)
            step = jnp.ceil((need - c_lo_j) / (c_hi_j - c_lo_j) * span).astype(jnp.int32)
            step = jnp.where(it % 2 == 0, step, (hi_j - lo_j) // 2)
            mid = lo_j + jnp.clip(step, 1, jnp.maximum(hi_j - lo_j - 1, 1))
            c = count(lambda s, kp: (s == thr) & (kp <= mid))
            live = (done < 0.5) & (hi_j - lo_j > 1)
            below = live & (c < need)
            above = live & (c >= need)
            lo_j = jnp.where(below, mid, lo_j)
            c_lo_j = jnp.where(below, c, c_lo_j)
            hi_j = jnp.where(above, mid, hi_j)
            c_hi_j = jnp.where(above, c, c_hi_j)
            done = jnp.where((c_hi_j == need) | (hi_j - lo_j <= 1), 1.0, done)
            return it + 1, lo_j, hi_j, c_lo_j, c_hi_j, done

        first = jnp.full((1, qn), -1, jnp.int32)
        final = jnp.full((1, qn), n_tiles * lt - 1, jnp.int32)
        n_ties = c_thr - (target - need)
        idx_done0 = jnp.where(tied & (n_ties > need), 0.0, 1.0)
        idx = lax.while_loop(idx_cond, idx_body,
                             (jnp.int32(0), first, final, 0.0 * ones, n_ties, idx_done0))
        last = idx[2]

        def drop_body(t, carry):
            s = s_ref[key_slice(t), :]
            s_ref[key_slice(t), :] = jnp.where(tied & (s == thr) & (key_pos(t) > last), -jnp.inf, s)
            return carry

        lax.fori_loop(0, n_tiles, drop_body, 0)

    qt = qt_ref[0]
    zeros = jnp.zeros((HEAD_DIM, B_GROUP * qn), BF16)
    for g in range(B_KV_HEADS):
        q_g = jnp.concatenate(
            [qt[(g * B_GROUP + j) * HEAD_DIM:(g * B_GROUP + j + 1) * HEAD_DIM, :] for j in range(B_GROUP)],
            axis=1)
        qpad_ref[g] = jnp.concatenate([zeros] * g + [q_g] + [zeros] * (B_KV_HEADS - 1 - g), axis=0)

    qf = qt.astype(F32)
    q_norm2 = jnp.sum((qf * qf).reshape(B_HEADS, HEAD_DIM, qn), axis=1)
    for g in range(B_KV_HEADS):
        b = jnp.sqrt(q_norm2[g * B_GROUP:(g + 1) * B_GROUP] * kmax_ref[0, 0:1, g:g + 1])
        b = b * BOUND_SLACK + BOUND_SLACK_ABS
        bound_ref[g] = jnp.concatenate([b[j:j + 1] for j in range(B_GROUP)], axis=1)
    acc_ref[...] = jnp.zeros(acc_ref.shape, F32)

    def fast_body(t, carry):
        keys = k_ref[0, key_slice(t), :]
        sel = jnp.where(s_ref[key_slice(t), :] >= thr, 1.0, 0.0).astype(BF16)
        sel = jnp.concatenate([sel] * B_GROUP, axis=1)
        def logits(g):
            return jnp.dot(keys, qpad_ref[g], preferred_element_type=F32)

        lg = logits(0)
        for g in range(B_KV_HEADS):
            lg_next = logits(g + 1) if g + 1 < B_KV_HEADS else None
            p = jnp.exp2(lg - bound_ref[g]).astype(BF16) * sel
            acc_ref[g] += jnp.dot(vta_ref[0, g, :, key_slice(t)], p, preferred_element_type=F32)
            lg = lg_next
        return carry

    lax.fori_loop(0, n_tiles, fast_body, 0)
    den_min = jnp.min(acc_ref[:, HEAD_DIM:HEAD_DIM + 1, :])

    @pl.when(jnp.logical_not(den_min >= MIN_TRUSTED_DENOMINATOR))
    def _():
        m_ref[...] = jnp.full(m_ref.shape, NEG_BIG, F32)
        acc_ref[...] = jnp.zeros(acc_ref.shape, F32)

        def exact_body(t, carry):
            keys = k_ref[0, key_slice(t), :]
            mb = jnp.where(s_ref[key_slice(t), :] >= thr, 0.0, -jnp.inf)
            mb = jnp.concatenate([mb] * B_GROUP, axis=1)
            for g in range(B_KV_HEADS):
                lg = jnp.dot(keys, qpad_ref[g], preferred_element_type=F32) + mb
                m_old = m_ref[g]
                m_new = jnp.maximum(m_old, jnp.max(fold_rows(lg, jnp.max), axis=0, keepdims=True))
                p = jnp.exp2(lg - m_new).astype(BF16)
                pv = jnp.dot(vta_ref[0, g, :, key_slice(t)], p, preferred_element_type=F32)
                acc_ref[g] = jnp.exp2(m_old - m_new) * acc_ref[g] + pv
                m_ref[g] = m_new
            return carry

        lax.fori_loop(0, n_tiles, exact_body, 0)

    for g in range(B_KV_HEADS):
        acc = acc_ref[g]
        o_g = acc[:HEAD_DIM] / acc[HEAD_DIM:HEAD_DIM + 1]
        for j in range(B_GROUP):
            hh = g * B_GROUP + j
            o_ref[0, hh * HEAD_DIM:(hh + 1) * HEAD_DIM, :] = o_g[:, j * qn:(j + 1) * qn].astype(o_ref.dtype)


V_AUG_ROWS = HEAD_DIM + 16


def _dsa(q, qi, kw, k, v, ki, *, q_tile, causal, k_sel):
    batch, seq, _ = q.shape
    n_keys = k.shape[1]
    lp = -(-n_keys // DSA_KEY_TILE) * DSA_KEY_TILE
    assert seq % q_tile == 0 and q_tile % LANES == 0
    pad = ((0, 0), (0, lp - n_keys), (0, 0))
    k, v, ki = (jnp.pad(a.astype(BF16), pad) for a in (k, v, ki))
    vt = jnp.swapaxes(v, 1, 2).reshape(batch, B_KV_HEADS, HEAD_DIM, lp)
    vta = jnp.concatenate([vt, jnp.ones((batch, B_KV_HEADS, 1, lp), BF16),
                           jnp.zeros((batch, B_KV_HEADS, V_AUG_ROWS - HEAD_DIM - 1, lp), BF16)], axis=2)
    qt, qit, kwt = (jnp.swapaxes(a, 1, 2) for a in (q, qi, kw))

    def qspec(rows):
        return pl.BlockSpec((1, rows, q_tile), lambda b, i: (b, 0, i))

    def resident(shape):
        zero = (0,) * len(shape)
        return pl.BlockSpec((1,) + shape, lambda b, i: (b,) + zero, pipeline_mode=pl.Buffered(1))

    ot = pl.pallas_call(
        functools.partial(_dsa_kernel, causal=causal, n_keys=n_keys, k_sel=k_sel),
        grid=(batch, seq // q_tile),
        in_specs=[qspec(B_Q), qspec(B_QI), qspec(LANES),
                  resident((lp, IDX_DIM)), resident((lp, B_KV)),
                  resident((B_KV_HEADS, V_AUG_ROWS, lp)),
                  pl.BlockSpec((1, 8, LANES), lambda b, i: (b, 0, 0))],
        out_specs=qspec(B_Q),
        out_shape=jax.ShapeDtypeStruct((batch, B_Q, seq), BF16),
        scratch_shapes=[pltpu.VMEM((lp, q_tile), F32),
                        pltpu.VMEM((B_KV_HEADS, B_KV, B_GROUP * q_tile), BF16),
                        pltpu.VMEM((B_KV_HEADS, 1, B_GROUP * q_tile), F32),
                        pltpu.VMEM((B_KV_HEADS, 1, B_GROUP * q_tile), F32),
                        pltpu.VMEM((B_KV_HEADS, V_AUG_ROWS, B_GROUP * q_tile), F32)],
        compiler_params=_params(2),
        name="dsa_prompt" if causal else "dsa_sample",
    )(qt, qit, kwt, ki, k, vta, _key_norm_max(k))
    return jnp.swapaxes(ot, 1, 2)


def kernel(x_prompt, x_sample, cache_k_a, cache_v_a, cache_k_b, cache_v_b, cache_kidx_b,
           c_prompt, c_sample, w_cond, b_cond, ln_g, ln_b, a_w_in, a_w_o, a_rel_bias,
           b_w_in, b_w_o, b_kidx_ln_g, b_kidx_ln_b, ffn_w_gu, ffn_w_down,
           moe_w_router, moe_b_router, moe_w_gu, moe_w_down):
    depth = w_cond.shape[0]
    alpha = (2 * depth) ** 0.25
    n_p, seq, d = x_prompt.shape
    n_s, dec_seq, _ = x_sample.shape
    past = cache_k_b.shape[2]
    a_hd = A_HEADS * HEAD_DIM

    rows = n_p + n_s
    rows_pad = -(-rows // 8) * 8
    c_all = jnp.pad(jnp.concatenate([c_prompt, c_sample], axis=0), ((0, rows_pad - rows), (0, 0)))
    mod_all = _modulation(c_all, w_cond, b_cond)

    xp, xs = x_prompt, x_sample
    outs = {k: [] for k in ("ka_p", "va_p", "kb_p", "vb_p", "ib_p", "ka_s", "va_s", "kb_s", "vb_s", "ib_s")}
    dummy_wr = jnp.zeros((d, LANES), F32)
    dummy_br = jnp.zeros((1, LANES), F32)
    for i in range(depth):
        j = i // 2
        mod_p = mod_all[i, :n_p].reshape(n_p, 1, 6 * d)
        mod_s = mod_all[i, n_p:rows].reshape(n_s, 1, 6 * d)
        g1, b1, g2, b2 = ln_g[i, 0], ln_b[i, 0], ln_g[i, 1], ln_b[i, 1]
        if i % 2 == 0:
            w_in = a_w_in[j].astype(BF16)
            keep = min(BAND_ROWS, seq)
            qkv_p = _modmm(xp, mod_p, w_in, BF16)
            kv_p = _modmm(xp, mod_p, w_in[:, a_hd:], F32, t_start=seq - keep)
            qkv_s = _modmm(xs, mod_s, w_in, BF16)
            kv_s = _modmm(xs, mod_s, w_in[:, a_hd:], F32)
            outs["ka_p"].append(kv_p[..., :a_hd].reshape(n_p, keep, A_HEADS, HEAD_DIM))
            outs["va_p"].append(kv_p[..., a_hd:].reshape(n_p, keep, A_HEADS, HEAD_DIM))
            outs["ka_s"].append(kv_s[..., :a_hd].reshape(n_s, dec_seq, A_HEADS, HEAD_DIM))
            outs["va_s"].append(kv_s[..., a_hd:].reshape(n_s, dec_seq, A_HEADS, HEAD_DIM))
            o_p = _band_attn_prompt(qkv_p, a_rel_bias[j])
            ck = cache_k_a[j].reshape(n_s, -1, a_hd).astype(BF16)
            cv = cache_v_a[j].reshape(n_s, -1, a_hd).astype(BF16)
            o_s = _band_attn_sample(qkv_s, ck, cv, a_rel_bias[j])
            w_o = a_w_o[j].astype(BF16)
        else:
            w_in = jnp.pad(b_w_in[j], ((0, 0), (0, B_PROJ_PAD - B_PROJ))).astype(BF16)
            kn_g = jnp.pad(b_kidx_ln_g[j], (0, LANES - IDX_DIM)).reshape(1, LANES)
            kn_b = jnp.pad(b_kidx_ln_b[j], (0, LANES - IDX_DIM)).reshape(1, LANES)
            cos_p, sin_p = _rope_tables(jnp.arange(seq))
            cos_s, sin_s = _rope_tables(past + jnp.arange(dec_seq))
            bb_s, _ = _row_blocks(n_s, dec_seq, ROW_TILE)
            cos_s, sin_s = jnp.tile(cos_s, (bb_s, 1)), jnp.tile(sin_s, (bb_s, 1))
            q_p, qi_p, k_p, v_p, kw_p = _b_project(xp, mod_p, w_in, cos_p, sin_p, kn_g, kn_b, False)
            q_s, qi_s, k_s, v_s, kw_s = _b_project(xs, mod_s, w_in, cos_s, sin_s, kn_g, kn_b, True)
            outs["kb_p"].append(k_p.reshape(n_p, seq, B_KV_HEADS, HEAD_DIM))
            outs["vb_p"].append(v_p.reshape(n_p, seq, B_KV_HEADS, HEAD_DIM))
            outs["ib_p"].append(kw_p[..., :IDX_DIM])
            outs["kb_s"].append(k_s.reshape(n_s, dec_seq, B_KV_HEADS, HEAD_DIM))
            outs["vb_s"].append(v_s.reshape(n_s, dec_seq, B_KV_HEADS, HEAD_DIM))
            outs["ib_s"].append(kw_s[..., :IDX_DIM])
            o_p = _dsa(q_p, qi_p, kw_p, k_p, v_p, kw_p[..., :IDX_DIM], q_tile=DSA_Q_TILE, causal=True,
                       k_sel=min(TOPK_MAX, seq // 4))
            n_keys = past + dec_seq
            kk = jnp.concatenate([cache_k_b[j].reshape(n_s, past, B_KV), k_s], axis=1)
            vc = jnp.concatenate([cache_v_b[j].reshape(n_s, past, B_KV), v_s], axis=1)
            kki = jnp.concatenate([cache_kidx_b[j], kw_s[..., :IDX_DIM]], axis=1)
            rep = LANES // dec_seq
            q_r, qi_r, kw_r = (jnp.concatenate([a] * rep, axis=1) for a in (q_s, qi_s, kw_s))
            o_s = _dsa(q_r, qi_r, kw_r, kk, vc, kki, q_tile=LANES, causal=False,
                       k_sel=min(TOPK_MAX, n_keys // 4))[:, :dec_seq]
            w_o = b_w_o[j].astype(BF16)
        xp = _mm_postnorm(o_p, w_o, xp, mod_p, 2, g1, b1, alpha)
        xs = _mm_postnorm(o_s, w_o, xs, mod_s, 2, g1, b1, alpha)
        if i % 2 == 0:
            w_gu = ffn_w_gu[j].astype(BF16)[None]
            w_dn = ffn_w_down[j].astype(BF16)[None]
            dense = dict(routed=False, ff_chunk=D_FF // 2, alpha=alpha)
            xp = _ffn(xp, mod_p, dummy_wr, dummy_br, w_gu, w_dn, g2, b2, **dense)
            xs = _ffn(xs, mod_s, dummy_wr, dummy_br, w_gu, w_dn, g2, b2, **dense)
        else:
            w_gu = moe_w_gu[j].astype(BF16)
            w_dn = moe_w_down[j].astype(BF16)
            w_r = jnp.pad(moe_w_router[j], ((0, 0), (0, LANES - N_EXPERTS)))
            b_r = jnp.pad(moe_b_router[j], (0, LANES - N_EXPERTS)).reshape(1, LANES)
            moe = dict(ff_chunk=D_FF_EXPERT // 4, alpha=alpha)
            xp = _moe(xp, mod_p, w_r, b_r, w_gu, w_dn, g2, b2, **moe)
            xs = _moe(xs, mod_s, w_r, b_r, w_gu, w_dn, g2, b2, **moe)

    st = lambda name: jnp.stack(outs[name])
    return (xp, xs, st("ka_p"), st("va_p"), st("kb_p"), st("vb_p"), st("ib_p"),
            st("ka_s"), st("va_s"), st("kb_s"), st("vb_s"), st("ib_s"))
```

```python
import functools

import jax
import jax.numpy as jnp
from jax import lax
from jax.experimental import pallas as pl
from jax.experimental.pallas import tpu as pltpu

F32 = jnp.float32
BF16 = jnp.bfloat16

D_MODEL = 1024
CHUNK = 64
N_PAST_CHUNKS = 8
BAND_ROWS = N_PAST_CHUNKS * CHUNK
REL_CLIP = 2 * CHUNK
HEAD_DIM = 64
A_HEADS = 16
B_HEADS = 16
B_KV_HEADS = 4
B_GROUP = B_HEADS // B_KV_HEADS
IDX_HEADS = 8
IDX_DIM = 64
TOPK_MAX = 256
D_FF = 2816
N_EXPERTS = 8
D_FF_EXPERT = 3584
ROPE_THETA = 10000.0
LN_EPS = 1e-5
B_Q = B_HEADS * HEAD_DIM
B_KV = B_KV_HEADS * HEAD_DIM
B_QI = IDX_HEADS * IDX_DIM
B_PROJ = B_Q + 2 * B_KV + B_QI + IDX_DIM + IDX_HEADS

LANES = 128
VMEM_LIMIT_BYTES = 58 * 1024 * 1024

A_Q_TILE = 4 * CHUNK
DSA_Q_TILE = 4 * CHUNK
DSA_KEY_TILE = 512
ROW_TILE = 512
FFN_ROW_TILE = 1024
MOE_CAPACITY = 288
B_PROJ_PAD = B_Q + 2 * B_KV + B_QI + LANES
NEG_BIG = -1e30
QK_SCALE_LOG2 = HEAD_DIM ** -0.5 * 1.4426950408889634
BOUND_SLACK = 1.02
BOUND_SLACK_ABS = 0.01
MIN_TRUSTED_DENOMINATOR = 2.0 ** -100
SECANT_STEPS = 12


def _params(n_grid):
    return pltpu.CompilerParams(
        dimension_semantics=("arbitrary",) * n_grid,
        vmem_limit_bytes=VMEM_LIMIT_BYTES,
    )


def _row_blocks(batch, seq, target):
    if seq >= target:
        assert seq % target == 0
        return 1, target
    bb = max(1, min(batch, target // seq))
    while batch % bb:
        bb -= 1
    return bb, seq


def _mod_spec(bb, chunk, n_grid):
    if n_grid == 2:
        return pl.BlockSpec((bb, 1, D_MODEL), lambda b, t: (b, 0, chunk))
    return pl.BlockSpec((bb, 1, D_MODEL), lambda b, t, e, c: (b, 0, chunk))


def _silu(x):
    return x / (1.0 + jnp.exp(-x))


def _layer_norm_rows(z, g, b):
    mu = jnp.mean(z, axis=-1, keepdims=True)
    zc = z - mu
    var = jnp.mean(zc * zc, axis=-1, keepdims=True)
    return zc * lax.rsqrt(var + LN_EPS) * g + b


def _modulation_kernel(c_ref, w_ref, b_ref, o_ref):
    a = _silu(c_ref[...]).astype(BF16)
    w = w_ref[0].astype(BF16)
    o_ref[0] = jnp.dot(a, w, preferred_element_type=F32) + b_ref[0]


def _modulation(c_all, w_cond, b_cond):
    depth, d, n = w_cond.shape
    rows = c_all.shape[0]
    tn = 1536
    return pl.pallas_call(
        _modulation_kernel,
        grid=(depth, n // tn),
        in_specs=[
            pl.BlockSpec((rows, d), lambda i, j: (0, 0)),
            pl.BlockSpec((1, d, tn), lambda i, j: (i, 0, j)),
            pl.BlockSpec((1, 1, tn), lambda i, j: (i, 0, j)),
        ],
        out_specs=pl.BlockSpec((1, rows, tn), lambda i, j: (i, 0, j)),
        out_shape=jax.ShapeDtypeStruct((depth, rows, n), F32),
        compiler_params=_params(2),
        name="modulation",
    )(c_all, w_cond, b_cond.reshape(depth, 1, n))


def _modmm_kernel(x_ref, sc_ref, sh_ref, w_ref, o_ref):
    bb, tt, d = x_ref.shape
    h = x_ref[...] * (1.0 + sc_ref[...]) + sh_ref[...]
    h = h.reshape(bb * tt, d).astype(BF16)
    acc = jnp.dot(h, w_ref[...], preferred_element_type=F32)
    o_ref[...] = acc.reshape(bb, tt, acc.shape[-1]).astype(o_ref.dtype)


def _modmm(x, mod, w, out_dtype, t_start=0):
    batch, seq, d = x.shape
    n = w.shape[1]
    seq_out = seq - t_start
    bb, tt = _row_blocks(batch, seq_out, ROW_TILE)
    assert t_start % tt == 0
    off = t_start // tt
    return pl.pallas_call(
        _modmm_kernel,
        grid=(batch // bb, seq_out // tt),
        in_specs=[
            pl.BlockSpec((bb, tt, d), lambda b, t: (b, t + off, 0)),
            _mod_spec(bb, 1, 2),
            _mod_spec(bb, 0, 2),
            pl.BlockSpec((d, n), lambda b, t: (0, 0)),
        ],
        out_specs=pl.BlockSpec((bb, tt, n), lambda b, t: (b, t, 0)),
        out_shape=jax.ShapeDtypeStruct((batch, seq_out, n), out_dtype),
        compiler_params=_params(2),
        name="a_project",
    )(x, mod, mod, w)


def _band_attn_kernel(*refs, n_kb, n_maybe_invalid):
    q_ref = refs[0]
    k_refs = refs[1:1 + n_kb]
    v_refs = refs[1 + n_kb:1 + 2 * n_kb]
    f_ref = refs[1 + 2 * n_kb]
    o_ref = refs[2 + 2 * n_kb]
    bias_ref = refs[3 + 2 * n_kb]
    q_tile = q_ref.shape[1]
    kb_sizes = [r.shape[1] for r in k_refs]
    k_tot = sum(kb_sizes)
    width = f_ref.shape[-1]
    i = pl.program_id(1)

    @pl.when((pl.program_id(0) == 0) & (i == 0))
    def _():
        rq = lax.broadcasted_iota(jnp.int32, (q_tile, k_tot), 0) // CHUNK
        ck = lax.broadcasted_iota(jnp.int32, (q_tile, k_tot), 1) // CHUNK
        in_band = (ck >= rq) & (ck <= rq + N_PAST_CHUNKS)
        for h in range(A_HEADS):
            rows = jnp.broadcast_to(f_ref[h], (q_tile, width))
            toep = pltpu.roll(rows, k_tot + 1, 1, stride=1, stride_axis=0)
            bias_ref[h] = jnp.where(in_band, toep[:, :k_tot], -jnp.inf)

    q = q_ref[0] * (HEAD_DIM ** -0.5)
    for h in range(A_HEADS):
        cols = slice(h * HEAD_DIM, (h + 1) * HEAD_DIM)
        qh = q[:, cols]
        parts = []
        for kb in range(n_kb):
            lg = lax.dot_general(qh, k_refs[kb][0, :, cols], (((1,), (1,)), ((), ())),
                                 preferred_element_type=F32)
            if kb < n_maybe_invalid:
                lg = jnp.where(i - n_maybe_invalid + kb >= 0, lg, -jnp.inf)
            parts.append(lg)
        logits = jnp.concatenate(parts, axis=1) + bias_ref[h]
        m = jnp.max(logits, axis=1, keepdims=True)
        p = jnp.exp(logits - m)
        l = jnp.sum(p, axis=1, keepdims=True)
        pb = p.astype(BF16)
        acc = jnp.zeros((q_tile, HEAD_DIM), F32)
        start = 0
        for kb in range(n_kb):
            acc = acc + jnp.dot(pb[:, start:start + kb_sizes[kb]], v_refs[kb][0, :, cols],
                                preferred_element_type=F32)
            start += kb_sizes[kb]
        o_ref[0, :, cols] = (acc / l).astype(o_ref.dtype)


def _bias_vector(table, q_tile, k_tot):
    width = k_tot + q_tile
    lo = -(q_tile - 1) - BAND_ROWS + REL_CLIP
    left = max(0, -lo)
    start = max(0, lo)
    n_mid = min(2 * REL_CLIP + 1 - start, width - left)
    right = width - left - n_mid
    mid = table.T[:, start:start + n_mid]
    f = jnp.pad(mid, ((0, 0), (left, right)), mode="edge")
    return f.reshape(A_HEADS, 1, width)


def _band_attn_prompt(qkv, table):
    batch, seq, _ = qkv.shape
    hd = A_HEADS * HEAD_DIM
    qt = A_Q_TILE
    n_prev = BAND_ROWS // qt
    n_kb = n_prev + 1
    k_tot = n_kb * qt
    f = _bias_vector(table, qt, k_tot)

    def kv_spec(kb, col):
        return pl.BlockSpec((1, qt, hd), lambda b, i: (b, jnp.maximum(i - n_prev + kb, 0), col))

    return pl.pallas_call(
        functools.partial(_band_attn_kernel, n_kb=n_kb, n_maybe_invalid=n_prev),
        grid=(batch, seq // qt),
        in_specs=[pl.BlockSpec((1, qt, hd), lambda b, i: (b, i, 0))]
        + [kv_spec(kb, 1) for kb in range(n_kb)]
        + [kv_spec(kb, 2) for kb in range(n_kb)]
        + [pl.BlockSpec(f.shape, lambda b, i: (0, 0, 0))],
        out_specs=pl.BlockSpec((1, qt, hd), lambda b, i: (b, i, 0)),
        out_shape=jax.ShapeDtypeStruct((batch, seq, hd), BF16),
        scratch_shapes=[pltpu.VMEM((A_HEADS, qt, k_tot), F32)],
        compiler_params=_params(2),
        name="band_attn_prompt",
    )(*([qkv] * (1 + 2 * n_kb)), f)


def _band_attn_sample(qkv, cache_k, cache_v, table):
    batch, seq, _ = qkv.shape
    hd = A_HEADS * HEAD_DIM
    win = cache_k.shape[1]
    assert seq == CHUNK and win == BAND_ROWS
    k_tot = win + seq
    f = _bias_vector(table, seq, k_tot)
    new = lambda col: pl.BlockSpec((1, seq, hd), lambda b, i: (b, 0, col))
    old = pl.BlockSpec((1, win, hd), lambda b, i: (b, 0, 0))
    return pl.pallas_call(
        functools.partial(_band_attn_kernel, n_kb=2, n_maybe_invalid=0),
        grid=(batch, 1),
        in_specs=[new(0), old, new(1), old, new(2), pl.BlockSpec(f.shape, lambda b, i: (0, 0, 0))],
        out_specs=pl.BlockSpec((1, seq, hd), lambda b, i: (b, 0, 0)),
        out_shape=jax.ShapeDtypeStruct((batch, seq, hd), BF16),
        scratch_shapes=[pltpu.VMEM((A_HEADS, seq, k_tot), F32)],
        compiler_params=_params(2),
        name="band_attn_sample",
    )(qkv, cache_k, qkv, cache_v, qkv, f)


def _mm_postnorm_kernel(o_ref, w_ref, x_ref, gate_ref, g_ref, b_ref, y_ref, *, alpha):
    bb, tt, d = x_ref.shape
    o = o_ref[...].reshape(bb * tt, o_ref.shape[-1])
    sub = jnp.dot(o, w_ref[...], preferred_element_type=F32).reshape(bb, tt, d)
    z = alpha * x_ref[...] + (1.0 + gate_ref[...]) * sub
    y_ref[...] = _layer_norm_rows(z, g_ref[...], b_ref[...])


def _mm_postnorm(o, w, x, mod, gate_chunk, g, b, alpha):
    batch, seq, d = x.shape
    bb, tt = _row_blocks(batch, seq, ROW_TILE)
    k = o.shape[-1]
    return pl.pallas_call(
        functools.partial(_mm_postnorm_kernel, alpha=alpha),
        grid=(batch // bb, seq // tt),
        in_specs=[
            pl.BlockSpec((bb, tt, k), lambda b_, t: (b_, t, 0)),
            pl.BlockSpec((k, d), lambda b_, t: (0, 0)),
            pl.BlockSpec((bb, tt, d), lambda b_, t: (b_, t, 0)),
            _mod_spec(bb, gate_chunk, 2),
            pl.BlockSpec((1, 1, d), lambda b_, t: (0, 0, 0)),
            pl.BlockSpec((1, 1, d), lambda b_, t: (0, 0, 0)),
        ],
        out_specs=pl.BlockSpec((bb, tt, d), lambda b_, t: (b_, t, 0)),
        out_shape=jax.ShapeDtypeStruct((batch, seq, d), F32),
        compiler_params=_params(2),
        name="out_proj_postnorm",
    )(o, w, x, mod, g.reshape(1, 1, d), b.reshape(1, 1, d))


def _ffn_kernel(x_ref, sc_ref, sh_ref, gate_ref, wr_ref, br_ref, wg_ref, wu_ref, wd_ref,
                g_ref, b_ref, y_ref, h_ref, comb_ref, acc_ref, *, routed, alpha):
    bb, tt, d = x_ref.shape
    rows = bb * tt
    e = pl.program_id(2)
    c = pl.program_id(3)
    lane = lax.broadcasted_iota(jnp.int32, (rows, LANES), 1)

    @pl.when((e == 0) & (c == 0))
    def _():
        h = (x_ref[...] * (1.0 + sc_ref[...]) + sh_ref[...]).reshape(rows, d)
        h_ref[...] = h.astype(BF16)
        acc_ref[...] = jnp.zeros_like(acc_ref)
        if routed:
            logits = jnp.dot(h, wr_ref[...], preferred_element_type=F32,
                             precision=lax.Precision.HIGHEST) + br_ref[...]
            logits = jnp.where(lane < N_EXPERTS, logits, -jnp.inf)
            m1 = jnp.max(logits, axis=1, keepdims=True)
            i1 = jnp.min(jnp.where(logits == m1, lane, LANES), axis=1, keepdims=True)
            rest = jnp.where(lane == i1, -jnp.inf, logits)
            m2 = jnp.max(rest, axis=1, keepdims=True)
            i2 = jnp.min(jnp.where(rest == m2, lane, LANES), axis=1, keepdims=True)
            e2 = jnp.exp(m2 - m1)
            den = 1.0 + e2
            comb_ref[...] = jnp.where(lane == i1, 1.0 / den, 0.0) + jnp.where(lane == i2, e2 / den, 0.0)

    h = h_ref[...]
    gt = jnp.dot(h, wg_ref[0], preferred_element_type=F32)
    up = jnp.dot(h, wu_ref[0], preferred_element_type=F32)
    act = (_silu(gt) * up).astype(BF16)
    part = jnp.dot(act, wd_ref[0], preferred_element_type=F32)
    if routed:
        ce = jnp.sum(jnp.where(lane == e, comb_ref[...], 0.0), axis=1, keepdims=True)
        part = ce * part
    acc_ref[...] += part

    @pl.when((e == pl.num_programs(2) - 1) & (c == pl.num_programs(3) - 1))
    def _():
        z = alpha * x_ref[...] + (1.0 + gate_ref[...]) * acc_ref[...].reshape(bb, tt, d)
        y_ref[...] = _layer_norm_rows(z, g_ref[...], b_ref[...])


def _ffn(x, mod, w_r, b_r, w_gu, w_down, g, b, *, routed, ff_chunk, alpha):
    batch, seq, d = x.shape
    n_e, ff, _ = w_down.shape
    assert ff % ff_chunk == 0
    n_c = ff // ff_chunk
    bb, tt = _row_blocks(batch, seq, FFN_ROW_TILE)
    rows = bb * tt
    vec = pl.BlockSpec((1, 1, d), lambda b_, t, e, c: (0, 0, 0))
    return pl.pallas_call(
        functools.partial(_ffn_kernel, routed=routed, alpha=alpha),
        grid=(batch // bb, seq // tt, n_e, n_c),
        in_specs=[
            pl.BlockSpec((bb, tt, d), lambda b_, t, e, c: (b_, t, 0)),
            _mod_spec(bb, 4, 4),
            _mod_spec(bb, 3, 4),
            _mod_spec(bb, 5, 4),
            pl.BlockSpec((d, LANES), lambda b_, t, e, c: (0, 0)),
            pl.BlockSpec((1, LANES), lambda b_, t, e, c: (0, 0)),
            pl.BlockSpec((1, d, ff_chunk), lambda b_, t, e, c: (e, 0, c)),
            pl.BlockSpec((1, d, ff_chunk), lambda b_, t, e, c: (e, 0, n_c + c)),
            pl.BlockSpec((1, ff_chunk, d), lambda b_, t, e, c: (e, c, 0)),
            vec,
            vec,
        ],
        out_specs=pl.BlockSpec((bb, tt, d), lambda b_, t, e, c: (b_, t, 0)),
        out_shape=jax.ShapeDtypeStruct((batch, seq, d), F32),
        scratch_shapes=[
            pltpu.VMEM((rows, d), BF16),
            pltpu.VMEM((rows, LANES), F32),
            pltpu.VMEM((rows, d), F32),
        ],
        compiler_params=_params(4),
        name="moe_ffn" if routed else "dense_ffn",
    )(x, mod, mod, mod, w_r, b_r, w_gu, w_gu, w_down, g.reshape(1, 1, d), b.reshape(1, 1, d))


def _moe_kernel(x_ref, sc_ref, sh_ref, gate_ref, wr_ref, br_ref, wg_ref, wu_ref, wd_ref, g_ref, b_ref,
                y_ref, h_ref, comb_ref, slot_ref, cnt_ref, tri_ref, hc_ref, acc_ref, out_ref, *, alpha):
    bb, tt, d = x_ref.shape
    rows = bb * tt
    cap = hc_ref.shape[1]
    e = pl.program_id(2)
    c = pl.program_id(3)
    last_c = pl.num_programs(3) - 1
    lane = lax.broadcasted_iota(jnp.int32, (rows, LANES), 1)

    @pl.when((pl.program_id(0) == 0) & (pl.program_id(1) == 0) & (e == 0) & (c == 0))
    def _():
        r = lax.broadcasted_iota(jnp.int32, (rows, rows), 0)
        col = lax.broadcasted_iota(jnp.int32, (rows, rows), 1)
        tri_ref[...] = jnp.where(r <= col, 1.0, 0.0).astype(BF16)

    @pl.when((e == 0) & (c == 0))
    def _():
        h = (x_ref[...] * (1.0 + sc_ref[...]) + sh_ref[...]).reshape(rows, d)
        h_ref[...] = h.astype(BF16)
        out_ref[...] = jnp.zeros_like(out_ref)
        logits = jnp.dot(h, wr_ref[...], preferred_element_type=F32,
                         precision=lax.Precision.HIGHEST) + br_ref[...]
        logits = jnp.where(lane < N_EXPERTS, logits, -jnp.inf)
        m1 = jnp.max(logits, axis=1, keepdims=True)
        i1 = jnp.min(jnp.where(logits == m1, lane, LANES), axis=1, keepdims=True)
        rest = jnp.where(lane == i1, -jnp.inf, logits)
        m2 = jnp.max(rest, axis=1, keepdims=True)
        i2 = jnp.min(jnp.where(rest == m2, lane, LANES), axis=1, keepdims=True)
        e2 = jnp.exp(m2 - m1)
        den = 1.0 + e2
        comb_ref[...] = jnp.where(lane == i1, 1.0 / den, 0.0) + jnp.where(lane == i2, e2 / den, 0.0)
        member = jnp.where((lane == i1) | (lane == i2), 1.0, 0.0)
        member_t = member.T[:2 * N_EXPERTS]
        upto = jnp.dot(member_t.astype(BF16), tri_ref[...], preferred_element_type=F32)
        slot_ref[...] = jnp.where(member_t > 0, upto - 1.0, -1.0)
        cnt_ref[...] = jnp.broadcast_to(upto[:, rows - 1:rows], cnt_ref.shape)

    n_sub = (jnp.max(cnt_ref[pl.ds(e, 1), :]).astype(jnp.int32) + cap - 1) // cap

    def selection(s):
        slot = slot_ref[pl.ds(e, 1), :] - (s * cap).astype(F32)
        r = lax.broadcasted_iota(jnp.int32, (cap, rows), 0).astype(F32)
        return jnp.where(slot == r, 1.0, 0.0).astype(BF16)

    @pl.when(c == 0)
    def _():
        def pack(s, carry):
            hc_ref[s] = jnp.dot(selection(s), h_ref[...], preferred_element_type=F32).astype(BF16)
            acc_ref[s] = jnp.zeros(acc_ref.shape[1:], F32)
            return carry
        lax.fori_loop(0, n_sub, pack, 0)

    def expert(s, carry):
        hc = hc_ref[s]
        gt = jnp.dot(hc, wg_ref[0], preferred_element_type=F32)
        up = jnp.dot(hc, wu_ref[0], preferred_element_type=F32)
        act = (_silu(gt) * up).astype(BF16)
        acc_ref[s] += jnp.dot(act, wd_ref[0], preferred_element_type=F32)
        return carry

    lax.fori_loop(0, n_sub, expert, 0)

    @pl.when(c == last_c)
    def _():
        ce = jnp.sum(jnp.where(lane == e, comb_ref[...], 0.0), axis=1, keepdims=True)

        def unpack(s, carry):
            back = lax.dot_general(selection(s), acc_ref[s].astype(BF16), (((0,), (0,)), ((), ())),
                                   preferred_element_type=F32)
            out_ref[...] += ce * back
            return carry
        lax.fori_loop(0, n_sub, unpack, 0)

    @pl.when((e == pl.num_programs(2) - 1) & (c == last_c))
    def _():
        z = alpha * x_ref[...] + (1.0 + gate_ref[...]) * out_ref[...].reshape(bb, tt, d)
        y_ref[...] = _layer_norm_rows(z, g_ref[...], b_ref[...])


def _moe(x, mod, w_r, b_r, w_gu, w_down, g, b, *, ff_chunk, alpha):
    batch, seq, d = x.shape
    n_e, ff, _ = w_down.shape
    assert ff % ff_chunk == 0
    n_c = ff // ff_chunk
    bb, tt = _row_blocks(batch, seq, FFN_ROW_TILE)
    rows = bb * tt
    cap = MOE_CAPACITY
    max_sub = -(-rows // cap)
    vec = pl.BlockSpec((1, 1, d), lambda b_, t, e, c: (0, 0, 0))
    return pl.pallas_call(
        functools.partial(_moe_kernel, alpha=alpha),
        grid=(batch // bb, seq // tt, n_e, n_c),
        in_specs=[
            pl.BlockSpec((bb, tt, d), lambda b_, t, e, c: (b_, t, 0), pipeline_mode=pl.Buffered(1)),
            _mod_spec(bb, 4, 4),
            _mod_spec(bb, 3, 4),
            _mod_spec(bb, 5, 4),
            pl.BlockSpec((d, LANES), lambda b_, t, e, c: (0, 0)),
            pl.BlockSpec((1, LANES), lambda b_, t, e, c: (0, 0)),
            pl.BlockSpec((1, d, ff_chunk), lambda b_, t, e, c: (e, 0, c)),
            pl.BlockSpec((1, d, ff_chunk), lambda b_, t, e, c: (e, 0, n_c + c)),
            pl.BlockSpec((1, ff_chunk, d), lambda b_, t, e, c: (e, c, 0)),
            vec,
            vec,
        ],
        out_specs=pl.BlockSpec((bb, tt, d), lambda b_, t, e, c: (b_, t, 0)),
        out_shape=jax.ShapeDtypeStruct((batch, seq, d), F32),
        scratch_shapes=[
            pltpu.VMEM((rows, d), BF16),
            pltpu.VMEM((rows, LANES), F32),
            pltpu.VMEM((2 * N_EXPERTS, rows), F32),
            pltpu.VMEM((2 * N_EXPERTS, LANES), F32),
            pltpu.VMEM((rows, rows), BF16),
            pltpu.VMEM((max_sub, cap, d), BF16),
            pltpu.VMEM((max_sub, cap, d), F32),
            pltpu.VMEM((rows, d), F32),
        ],
        compiler_params=_params(4),
        name="moe_ffn",
    )(x, mod, mod, mod, w_r, b_r, w_gu, w_gu, w_down, g.reshape(1, 1, d), b.reshape(1, 1, d))


def _rope_lanes(seg, cos, sin_signed, first_half):
    width = seg.shape[-1]
    reps = width // LANES
    if reps > 1:
        cos = jnp.concatenate([cos] * reps, axis=1)
        sin_signed = jnp.concatenate([sin_signed] * reps, axis=1)
        first_half = jnp.concatenate([first_half] * reps, axis=1)
    half = HEAD_DIM // 2
    swapped = jnp.where(first_half, pltpu.roll(seg, width - half, 1), pltpu.roll(seg, half, 1))
    return seg * cos + swapped * sin_signed


def _b_project_kernel(x_ref, sc_ref, sh_ref, w_ref, cos_ref, sin_ref, kg_ref, kb_ref,
                      q_ref, qi_ref, k_ref, v_ref, kw_ref):
    bb, tt, d = x_ref.shape
    rows = bb * tt
    h = x_ref[...] * (1.0 + sc_ref[...]) + sh_ref[...]
    h = h.reshape(rows, d).astype(BF16)
    acc = jnp.dot(h, w_ref[...], preferred_element_type=F32)
    cos = cos_ref[...]
    sin_signed = sin_ref[...]
    lane = lax.broadcasted_iota(jnp.int32, (1, LANES), 1)
    first_half = (lane % HEAD_DIM) < (HEAD_DIM // 2)

    def out(ref, val):
        ref[...] = val.reshape(bb, tt, val.shape[-1]).astype(ref.dtype)

    o_k, o_v, o_qi, o_ki = B_Q, B_Q + B_KV, B_Q + 2 * B_KV, B_Q + 2 * B_KV + B_QI
    out(q_ref, _rope_lanes(acc[:, :o_k], cos, sin_signed, first_half) * QK_SCALE_LOG2)
    out(k_ref, _rope_lanes(acc[:, o_k:o_v], cos, sin_signed, first_half))
    out(v_ref, acc[:, o_v:o_qi])
    out(qi_ref, _rope_lanes(acc[:, o_qi:o_ki], cos, sin_signed, first_half))
    seg = acc[:, o_ki:]
    is_ki = lane < IDX_DIM
    mu = jnp.sum(jnp.where(is_ki, seg, 0.0), axis=1, keepdims=True) / IDX_DIM
    cen = jnp.where(is_ki, seg - mu, 0.0)
    var = jnp.sum(cen * cen, axis=1, keepdims=True) / IDX_DIM
    ki = cen * lax.rsqrt(var + LN_EPS) * kg_ref[...] + kb_ref[...]
    ki = _rope_lanes(ki, cos, sin_signed, first_half)
    out(kw_ref, jnp.where(is_ki, ki, seg * (IDX_HEADS ** -0.5)))


def _b_project(x, mod, w, cos, sin_signed, kn_g, kn_b, table_per_tile):
    batch, seq, d = x.shape
    bb, tt = _row_blocks(batch, seq, ROW_TILE)
    rows = bb * tt
    n = w.shape[1]
    tab = pl.BlockSpec((rows, LANES), (lambda b, t: (0, 0)) if table_per_tile else (lambda b, t: (t, 0)))
    vec = pl.BlockSpec((1, LANES), lambda b, t: (0, 0))

    def o_spec(width):
        return pl.BlockSpec((bb, tt, width), lambda b, t: (b, t, 0))

    def o_shape(width, dtype):
        return jax.ShapeDtypeStruct((batch, seq, width), dtype)

    return pl.pallas_call(
        _b_project_kernel,
        grid=(batch // bb, seq // tt),
        in_specs=[
            pl.BlockSpec((bb, tt, d), lambda b, t: (b, t, 0)),
            _mod_spec(bb, 1, 2),
            _mod_spec(bb, 0, 2),
            pl.BlockSpec((d, n), lambda b, t: (0, 0)),
            tab,
            tab,
            vec,
            vec,
        ],
        out_specs=[o_spec(B_Q), o_spec(B_QI), o_spec(B_KV), o_spec(B_KV), o_spec(LANES)],
        out_shape=[o_shape(B_Q, BF16), o_shape(B_QI, BF16), o_shape(B_KV, F32), o_shape(B_KV, F32),
                   o_shape(LANES, F32)],
        compiler_params=_params(2),
        name="b_project",
    )(x, mod, mod, w, cos, sin_signed, kn_g, kn_b)


def _rope_tables(pos):
    half = HEAD_DIM // 2
    inv = ROPE_THETA ** (-jnp.arange(half, dtype=F32) / half)
    ang = pos.astype(F32)[:, None] * inv[None, :]
    cos, sin = jnp.cos(ang), jnp.sin(ang)
    return (jnp.concatenate([cos, cos, cos, cos], axis=1),
            jnp.concatenate([-sin, sin, -sin, sin], axis=1))


def _key_norm_kernel(k_ref, grp_ref, o_ref):
    k = k_ref[0].astype(F32)
    n2 = jnp.dot((k * k).astype(BF16), grp_ref[...], preferred_element_type=F32)
    mx = jnp.broadcast_to(jnp.max(n2, axis=0, keepdims=True), o_ref.shape[1:])

    @pl.when(pl.program_id(1) == 0)
    def _():
        o_ref[0] = mx

    @pl.when(pl.program_id(1) > 0)
    def _():
        o_ref[0] = jnp.maximum(o_ref[0], mx)


def _key_norm_max(k):
    batch, lp, width = k.shape
    tile = DSA_KEY_TILE
    grp = (jnp.arange(width)[:, None] // HEAD_DIM == jnp.arange(LANES)[None, :]).astype(BF16)
    return pl.pallas_call(
        _key_norm_kernel,
        grid=(batch, lp // tile),
        in_specs=[pl.BlockSpec((1, tile, width), lambda b, t: (b, t, 0)),
                  pl.BlockSpec((width, LANES), lambda b, t: (0, 0))],
        out_specs=pl.BlockSpec((1, 8, LANES), lambda b, t: (b, 0, 0)),
        out_shape=jax.ShapeDtypeStruct((batch, 8, LANES), F32),
        compiler_params=_params(2),
        name="key_norm_max",
    )(k, grp)


def _dsa_kernel(qt_ref, qit_ref, kwt_ref, ki_ref, k_ref, vta_ref, kmax_ref, o_ref,
                s_ref, qpad_ref, bound_ref, m_ref, acc_ref, *, causal, n_keys, k_sel):
    qn = qt_ref.shape[2]
    lt = DSA_KEY_TILE
    i = pl.program_id(1)
    if causal:
        n_tiles = ((i + 1) * qn + lt - 1) // lt
        qpos = i * qn + lax.broadcasted_iota(jnp.int32, (1, qn), 1)
        limit = (qpos // CHUNK + 1) * CHUNK
    else:
        n_tiles = ki_ref.shape[1] // lt
        limit = jnp.full((1, qn), n_keys, jnp.int32)

    def key_slice(t):
        return pl.ds(pl.multiple_of(t * lt, lt), lt)

    def key_pos(t):
        return t * lt + lax.broadcasted_iota(jnp.int32, (lt, 1), 0)

    def fold_rows(x, op):
        return op(op(x.reshape(lt // 32, 4, 8, x.shape[-1]), axis=0), axis=0)

    qit = qit_ref[0]
    kwt = kwt_ref[0]
    qi_w = jnp.concatenate([qit[h * IDX_DIM:(h + 1) * IDX_DIM, :] for h in range(IDX_HEADS)], axis=1)
    w_row = jnp.concatenate([kwt[IDX_DIM + h:IDX_DIM + h + 1, :] for h in range(IDX_HEADS)], axis=1)
    w_row = w_row * (IDX_DIM ** -0.5)

    def score_body(t, carry):
        rmax, rmin = carry
        s = jnp.dot(ki_ref[0, key_slice(t), :], qi_w, preferred_element_type=F32)
        s = jnp.maximum(s, 0.0) * w_row
        sc = s[:, 0:qn]
        for h in range(1, IDX_HEADS):
            sc = sc + s[:, h * qn:(h + 1) * qn]
        adm = key_pos(t) < limit
        s_ref[key_slice(t), :] = jnp.where(adm, sc, -jnp.inf)
        rmax = jnp.maximum(rmax, jnp.max(jnp.where(adm, sc, -jnp.inf), axis=0, keepdims=True))
        rmin = jnp.minimum(rmin, jnp.min(jnp.where(adm, sc, jnp.inf), axis=0, keepdims=True))
        return rmax, rmin

    rmax, rmin = lax.fori_loop(0, n_tiles, score_body,
                               (jnp.full((1, qn), -jnp.inf, F32), jnp.full((1, qn), jnp.inf, F32)))

    def count(pred):
        def body(t, acc):
            return acc + fold_rows(jnp.where(pred(s_ref[key_slice(t), :], key_pos(t)), 1.0, 0.0), jnp.sum)
        acc = lax.fori_loop(0, n_tiles, body, jnp.zeros((8, qn), F32))
        return jnp.sum(acc, axis=0, keepdims=True)

    def to_key(x):
        bits = lax.bitcast_convert_type(x, jnp.int32)
        return bits ^ ((bits >> 31) & 0x7FFFFFFF)

    def from_key(key):
        return lax.bitcast_convert_type(key ^ ((key >> 31) & 0x7FFFFFFF), F32)

    n_adm = limit.astype(F32)
    target = jnp.minimum(n_adm, float(k_sel))
    log_target = jnp.log(target)

    def any_left(done):
        return jnp.sum(done) < qn

    c_ge0 = count(lambda s, _: s >= 0.0)
    c_gt0 = count(lambda s, _: s > 0.0)
    is_pos = c_gt0 >= target
    is_neg = c_ge0 < target
    zero_key = jnp.zeros((1, qn), jnp.int32)
    lo0 = jnp.where(is_neg, to_key(rmin), zero_key)
    hi0 = jnp.where(is_pos, to_key(rmax) + 1, zero_key)
    c_lo0 = jnp.where(is_neg, n_adm, c_ge0)
    c_hi0 = jnp.where(is_pos, 0.0, c_ge0)
    done0 = jnp.where((n_adm == target) | jnp.logical_not(is_pos | is_neg) | (c_lo0 == target), 1.0, 0.0)
    lo0 = jnp.where(n_adm == target, to_key(rmin), lo0)
    c_lo0 = jnp.where(n_adm == target, n_adm, c_lo0)

    def sel_cond(st):
        return any_left(st[7])

    def sel_body(st):
        it, lo, hi, c_lo, c_hi, w_lo, w_hi, done, side = st
        t_lo, t_hi = from_key(lo), from_key(hi)
        f_lo = (jnp.log(c_lo) - log_target) * w_lo
        f_hi = (log_target - jnp.log(jnp.maximum(c_hi, 0.5))) * w_hi
        guess = to_key(t_lo + f_lo / (f_lo + f_hi) * (t_hi - t_lo))
        halve = (lo >> 1) + (hi >> 1) + (lo & hi & 1)
        mid = jnp.where(it >= SECANT_STEPS, halve, jnp.clip(guess, lo + 1, hi - 1))
        stuck = halve == lo
        c = count(lambda s, _: s >= from_key(mid))
        live = (done < 0.5) & jnp.logical_not(stuck)
        up = live & (c >= target)
        dn = live & (c < target)
        w_hi = jnp.where(up, jnp.where(side > 0, 0.5 * w_hi, 1.0), jnp.where(dn, 1.0, w_hi))
        w_lo = jnp.where(dn, jnp.where(side < 0, 0.5 * w_lo, 1.0), jnp.where(up, 1.0, w_lo))
        side = jnp.where(up, 1.0, jnp.where(dn, -1.0, side))
        lo = jnp.where(up, mid, lo)
        c_lo = jnp.where(up, c, c_lo)
        hi = jnp.where(dn, mid, hi)
        c_hi = jnp.where(dn, c, c_hi)
        done = jnp.where(stuck | (c_lo == target), 1.0, done)
        return it + 1, lo, hi, c_lo, c_hi, w_lo, w_hi, done, side

    ones = jnp.ones((1, qn), F32)
    sel = lax.while_loop(sel_cond, sel_body,
                         (jnp.int32(0), lo0, hi0, c_lo0, c_hi0, ones, ones, done0, 0.0 * ones))
    thr, c_thr = from_key(sel[1]), sel[3]

    tied = c_thr > target

    @pl.when(jnp.sum(jnp.where(tied, 1.0, 0.0)) > 0)
    def _():
        need = target - count(lambda s, _: s > thr)

        def idx_cond(st):
            return any_left(st[5])

        def idx_body(st):
            it, lo_j, hi_j, c_lo_j, c_hi_j, done = st
            span = (hi_j - lo_j).astype(F32)
            step = jnp.ceil((need - c_lo_j) / (c_hi_j - c_lo_j) * span).astype(jnp.int32)
            step = jnp.where(it % 2 == 0, step, (hi_j - lo_j) // 2)
            mid = lo_j + jnp.clip(step, 1, jnp.maximum(hi_j - lo_j - 1, 1))
            c = count(lambda s, kp: (s == thr) & (kp <= mid))
            live = (done < 0.5) & (hi_j - lo_j > 1)
            below = live & (c < need)
            above = live & (c >= need)
            lo_j = jnp.where(below, mid, lo_j)
            c_lo_j = jnp.where(below, c, c_lo_j)
            hi_j = jnp.where(above, mid, hi_j)
            c_hi_j = jnp.where(above, c, c_hi_j)
            done = jnp.where((c_hi_j == need) | (hi_j - lo_j <= 1), 1.0, done)
            return it + 1, lo_j, hi_j, c_lo_j, c_hi_j, done

        first = jnp.full((1, qn), -1, jnp.int32)
        final = jnp.full((1, qn), n_tiles * lt - 1, jnp.int32)
        n_ties = c_thr - (target - need)
        idx_done0 = jnp.where(tied & (n_ties > need), 0.0, 1.0)
        idx = lax.while_loop(idx_cond, idx_body,
                             (jnp.int32(0), first, final, 0.0 * ones, n_ties, idx_done0))
        last = idx[2]

        def drop_body(t, carry):
            s = s_ref[key_slice(t), :]
            s_ref[key_slice(t), :] = jnp.where(tied & (s == thr) & (key_pos(t) > last), -jnp.inf, s)
            return carry

        lax.fori_loop(0, n_tiles, drop_body, 0)

    qt = qt_ref[0]
    zeros = jnp.zeros((HEAD_DIM, B_GROUP * qn), BF16)
    for g in range(B_KV_HEADS):
        q_g = jnp.concatenate(
            [qt[(g * B_GROUP + j) * HEAD_DIM:(g * B_GROUP + j + 1) * HEAD_DIM, :] for j in range(B_GROUP)],
            axis=1)
        qpad_ref[g] = jnp.concatenate([zeros] * g + [q_g] + [zeros] * (B_KV_HEADS - 1 - g), axis=0)

    qf = qt.astype(F32)
    q_norm2 = jnp.sum((qf * qf).reshape(B_HEADS, HEAD_DIM, qn), axis=1)
    for g in range(B_KV_HEADS):
        b = jnp.sqrt(q_norm2[g * B_GROUP:(g + 1) * B_GROUP] * kmax_ref[0, 0:1, g:g + 1])
        b = b * BOUND_SLACK + BOUND_SLACK_ABS
        bound_ref[g] = jnp.concatenate([b[j:j + 1] for j in range(B_GROUP)], axis=1)
    acc_ref[...] = jnp.zeros(acc_ref.shape, F32)

    def fast_body(t, carry):
        keys = k_ref[0, key_slice(t), :]
        sel = jnp.where(s_ref[key_slice(t), :] >= thr, 1.0, 0.0).astype(BF16)
        sel = jnp.concatenate([sel] * B_GROUP, axis=1)
        def logits(g):
            return jnp.dot(keys, qpad_ref[g], preferred_element_type=F32)

        lg = logits(0)
        for g in range(B_KV_HEADS):
            lg_next = logits(g + 1) if g + 1 < B_KV_HEADS else None
            p = jnp.exp2(lg - bound_ref[g]).astype(BF16) * sel
            acc_ref[g] += jnp.dot(vta_ref[0, g, :, key_slice(t)], p, preferred_element_type=F32)
            lg = lg_next
        return carry

    lax.fori_loop(0, n_tiles, fast_body, 0)
    den_min = jnp.min(acc_ref[:, HEAD_DIM:HEAD_DIM + 1, :])

    @pl.when(jnp.logical_not(den_min >= MIN_TRUSTED_DENOMINATOR))
    def _():
        m_ref[...] = jnp.full(m_ref.shape, NEG_BIG, F32)
        acc_ref[...] = jnp.zeros(acc_ref.shape, F32)

        def exact_body(t, carry):
            keys = k_ref[0, key_slice(t), :]
            mb = jnp.where(s_ref[key_slice(t), :] >= thr, 0.0, -jnp.inf)
            mb = jnp.concatenate([mb] * B_GROUP, axis=1)
            for g in range(B_KV_HEADS):
                lg = jnp.dot(keys, qpad_ref[g], preferred_element_type=F32) + mb
                m_old = m_ref[g]
                m_new = jnp.maximum(m_old, jnp.max(fold_rows(lg, jnp.max), axis=0, keepdims=True))
                p = jnp.exp2(lg - m_new).astype(BF16)
                pv = jnp.dot(vta_ref[0, g, :, key_slice(t)], p, preferred_element_type=F32)
                acc_ref[g] = jnp.exp2(m_old - m_new) * acc_ref[g] + pv
                m_ref[g] = m_new
            return carry

        lax.fori_loop(0, n_tiles, exact_body, 0)

    for g in range(B_KV_HEADS):
        acc = acc_ref[g]
        o_g = acc[:HEAD_DIM] / acc[HEAD_DIM:HEAD_DIM + 1]
        for j in range(B_GROUP):
            hh = g * B_GROUP + j
            o_ref[0, hh * HEAD_DIM:(hh + 1) * HEAD_DIM, :] = o_g[:, j * qn:(j + 1) * qn].astype(o_ref.dtype)


V_AUG_ROWS = HEAD_DIM + 16


def _dsa(q, qi, kw, k, v, ki, *, q_tile, causal, k_sel):
    batch, seq, _ = q.shape
    n_keys = k.shape[1]
    lp = -(-n_keys // DSA_KEY_TILE) * DSA_KEY_TILE
    assert seq % q_tile == 0 and q_tile % LANES == 0
    pad = ((0, 0), (0, lp - n_keys), (0, 0))
    k, v, ki = (jnp.pad(a.astype(BF16), pad) for a in (k, v, ki))
    vt = jnp.swapaxes(v, 1, 2).reshape(batch, B_KV_HEADS, HEAD_DIM, lp)
    vta = jnp.concatenate([vt, jnp.ones((batch, B_KV_HEADS, 1, lp), BF16),
                           jnp.zeros((batch, B_KV_HEADS, V_AUG_ROWS - HEAD_DIM - 1, lp), BF16)], axis=2)
    qt, qit, kwt = (jnp.swapaxes(a, 1, 2) for a in (q, qi, kw))

    def qspec(rows):
        return pl.BlockSpec((1, rows, q_tile), lambda b, i: (b, 0, i))

    def resident(shape):
        zero = (0,) * len(shape)
        return pl.BlockSpec((1,) + shape, lambda b, i: (b,) + zero, pipeline_mode=pl.Buffered(1))

    ot = pl.pallas_call(
        functools.partial(_dsa_kernel, causal=causal, n_keys=n_keys, k_sel=k_sel),
        grid=(batch, seq // q_tile),
        in_specs=[qspec(B_Q), qspec(B_QI), qspec(LANES),
                  resident((lp, IDX_DIM)), resident((lp, B_KV)),
                  resident((B_KV_HEADS, V_AUG_ROWS, lp)),
                  pl.BlockSpec((1, 8, LANES), lambda b, i: (b, 0, 0))],
        out_specs=qspec(B_Q),
        out_shape=jax.ShapeDtypeStruct((batch, B_Q, seq), BF16),
        scratch_shapes=[pltpu.VMEM((lp, q_tile), F32),
                        pltpu.VMEM((B_KV_HEADS, B_KV, B_GROUP * q_tile), BF16),
                        pltpu.VMEM((B_KV_HEADS, 1, B_GROUP * q_tile), F32),
                        pltpu.VMEM((B_KV_HEADS, 1, B_GROUP * q_tile), F32),
                        pltpu.VMEM((B_KV_HEADS, V_AUG_ROWS, B_GROUP * q_tile), F32)],
        compiler_params=_params(2),
        name="dsa_prompt" if causal else "dsa_sample",
    )(qt, qit, kwt, ki, k, vta, _key_norm_max(k))
    return jnp.swapaxes(ot, 1, 2)


def kernel(x_prompt, x_sample, cache_k_a, cache_v_a, cache_k_b, cache_v_b, cache_kidx_b,
           c_prompt, c_sample, w_cond, b_cond, ln_g, ln_b, a_w_in, a_w_o, a_rel_bias,
           b_w_in, b_w_o, b_kidx_ln_g, b_kidx_ln_b, ffn_w_gu, ffn_w_down,
           moe_w_router, moe_b_router, moe_w_gu, moe_w_down):
    depth = w_cond.shape[0]
    alpha = (2 * depth) ** 0.25
    n_p, seq, d = x_prompt.shape
    n_s, dec_seq, _ = x_sample.shape
    past = cache_k_b.shape[2]
    a_hd = A_HEADS * HEAD_DIM

    rows = n_p + n_s
    rows_pad = -(-rows // 8) * 8
    c_all = jnp.pad(jnp.concatenate([c_prompt, c_sample], axis=0), ((0, rows_pad - rows), (0, 0)))
    mod_all = _modulation(c_all, w_cond, b_cond)

    xp, xs = x_prompt, x_sample
    outs = {k: [] for k in ("ka_p", "va_p", "kb_p", "vb_p", "ib_p", "ka_s", "va_s", "kb_s", "vb_s", "ib_s")}
    dummy_wr = jnp.zeros((d, LANES), F32)
    dummy_br = jnp.zeros((1, LANES), F32)
    for i in range(depth):
        j = i // 2
        mod_p = mod_all[i, :n_p].reshape(n_p, 1, 6 * d)
        mod_s = mod_all[i, n_p:rows].reshape(n_s, 1, 6 * d)
        g1, b1, g2, b2 = ln_g[i, 0], ln_b[i, 0], ln_g[i, 1], ln_b[i, 1]
        if i % 2 == 0:
            w_in = a_w_in[j].astype(BF16)
            keep = min(BAND_ROWS, seq)
            qkv_p = _modmm(xp, mod_p, w_in, BF16)
            kv_p = _modmm(xp, mod_p, w_in[:, a_hd:], F32, t_start=seq - keep)
            qkv_s = _modmm(xs, mod_s, w_in, BF16)
            kv_s = _modmm(xs, mod_s, w_in[:, a_hd:], F32)
            outs["ka_p"].append(kv_p[..., :a_hd].reshape(n_p, keep, A_HEADS, HEAD_DIM))
            outs["va_p"].append(kv_p[..., a_hd:].reshape(n_p, keep, A_HEADS, HEAD_DIM))
            outs["ka_s"].append(kv_s[..., :a_hd].reshape(n_s, dec_seq, A_HEADS, HEAD_DIM))
            outs["va_s"].append(kv_s[..., a_hd:].reshape(n_s, dec_seq, A_HEADS, HEAD_DIM))
            o_p = _band_attn_prompt(qkv_p, a_rel_bias[j])
            ck = cache_k_a[j].reshape(n_s, -1, a_hd).astype(BF16)
            cv = cache_v_a[j].reshape(n_s, -1, a_hd).astype(BF16)
            o_s = _band_attn_sample(qkv_s, ck, cv, a_rel_bias[j])
            w_o = a_w_o[j].astype(BF16)
        else:
            w_in = jnp.pad(b_w_in[j], ((0, 0), (0, B_PROJ_PAD - B_PROJ))).astype(BF16)
            kn_g = jnp.pad(b_kidx_ln_g[j], (0, LANES - IDX_DIM)).reshape(1, LANES)
            kn_b = jnp.pad(b_kidx_ln_b[j], (0, LANES - IDX_DIM)).reshape(1, LANES)
            cos_p, sin_p = _rope_tables(jnp.arange(seq))
            cos_s, sin_s = _rope_tables(past + jnp.arange(dec_seq))
            bb_s, _ = _row_blocks(n_s, dec_seq, ROW_TILE)
            cos_s, sin_s = jnp.tile(cos_s, (bb_s, 1)), jnp.tile(sin_s, (bb_s, 1))
            q_p, qi_p, k_p, v_p, kw_p = _b_project(xp, mod_p, w_in, cos_p, sin_p, kn_g, kn_b, False)
            q_s, qi_s, k_s, v_s, kw_s = _b_project(xs, mod_s, w_in, cos_s, sin_s, kn_g, kn_b, True)
            outs["kb_p"].append(k_p.reshape(n_p, seq, B_KV_HEADS, HEAD_DIM))
            outs["vb_p"].append(v_p.reshape(n_p, seq, B_KV_HEADS, HEAD_DIM))
            outs["ib_p"].append(kw_p[..., :IDX_DIM])
            outs["kb_s"].append(k_s.reshape(n_s, dec_seq, B_KV_HEADS, HEAD_DIM))
            outs["vb_s"].append(v_s.reshape(n_s, dec_seq, B_KV_HEADS, HEAD_DIM))
            outs["ib_s"].append(kw_s[..., :IDX_DIM])
            o_p = _dsa(q_p, qi_p, kw_p, k_p, v_p, kw_p[..., :IDX_DIM], q_tile=DSA_Q_TILE, causal=True,
                       k_sel=min(TOPK_MAX, seq // 4))
            n_keys = past + dec_seq
            kk = jnp.concatenate([cache_k_b[j].reshape(n_s, past, B_KV), k_s], axis=1)
            vc = jnp.concatenate([cache_v_b[j].reshape(n_s, past, B_KV), v_s], axis=1)
            kki = jnp.concatenate([cache_kidx_b[j], kw_s[..., :IDX_DIM]], axis=1)
            rep = LANES // dec_seq
            q_r, qi_r, kw_r = (jnp.concatenate([a] * rep, axis=1) for a in (q_s, qi_s, kw_s))
            o_s = _dsa(q_r, qi_r, kw_r, kk, vc, kki, q_tile=LANES, causal=False,
                       k_sel=min(TOPK_MAX, n_keys // 4))[:, :dec_seq]
            w_o = b_w_o[j].astype(BF16)
        xp = _mm_postnorm(o_p, w_o, xp, mod_p, 2, g1, b1, alpha)
        xs = _mm_postnorm(o_s, w_o, xs, mod_s, 2, g1, b1, alpha)
        if i % 2 == 0:
            w_gu = ffn_w_gu[j].astype(BF16)[None]
            w_dn = ffn_w_down[j].astype(BF16)[None]
            dense = dict(routed=False, ff_chunk=D_FF // 2, alpha=alpha)
            xp = _ffn(xp, mod_p, dummy_wr, dummy_br, w_gu, w_dn, g2, b2, **dense)
            xs = _ffn(xs, mod_s, dummy_wr, dummy_br, w_gu, w_dn, g2, b2, **dense)
        else:
            w_gu = moe_w_gu[j].astype(BF16)
            w_dn = moe_w_down[j].astype(BF16)
            w_r = jnp.pad(moe_w_router[j], ((0, 0), (0, LANES - N_EXPERTS)))
            b_r = jnp.pad(moe_b_router[j], (0, LANES - N_EXPERTS)).reshape(1, LANES)
            moe = dict(ff_chunk=D_FF_EXPERT // 4, alpha=alpha)
            xp = _moe(xp, mod_p, w_r, b_r, w_gu, w_dn, g2, b2, **moe)
            xs = _moe(xs, mod_s, w_r, b_r, w_gu, w_dn, g2, b2, **moe)

    st = lambda name: jnp.stack(outs[name])
    return (xp, xs, st("ka_p"), st("va_p"), st("kb_p"), st("vb_p"), st("ib_p"),
            st("ka_s"), st("va_s"), st("kb_s"), st("vb_s"), st("ib_s"))
```

```python
import functools

import jax
import jax.numpy as jnp
from jax import lax
from jax.experimental import pallas as pl
from jax.experimental.pallas import tpu as pltpu

F32 = jnp.float32
BF16 = jnp.bfloat16

D_MODEL = 1024
CHUNK = 64
N_PAST_CHUNKS = 8
BAND_ROWS = N_PAST_CHUNKS * CHUNK
REL_CLIP = 2 * CHUNK
HEAD_DIM = 64
A_HEADS = 16
B_HEADS = 16
B_KV_HEADS = 4
B_GROUP = B_HEADS // B_KV_HEADS
IDX_HEADS = 8
IDX_DIM = 64
TOPK_MAX = 256
D_FF = 2816
N_EXPERTS = 8
D_FF_EXPERT = 3584
ROPE_THETA = 10000.0
LN_EPS = 1e-5
B_Q = B_HEADS * HEAD_DIM
B_KV = B_KV_HEADS * HEAD_DIM
B_QI = IDX_HEADS * IDX_DIM
B_PROJ = B_Q + 2 * B_KV + B_QI + IDX_DIM + IDX_HEADS

LANES = 128
VMEM_LIMIT_BYTES = 58 * 1024 * 1024

A_Q_TILE = 4 * CHUNK
DSA_Q_TILE = 4 * CHUNK
DSA_KEY_TILE = 512
ROW_TILE = 512
FFN_ROW_TILE = 1024
MOE_CAPACITY = 288
MOE_FF_CHUNK = 1792
B_PROJ_PAD = B_Q + 2 * B_KV + B_QI + LANES
NEG_BIG = -1e30
QK_SCALE_LOG2 = HEAD_DIM ** -0.5 * 1.4426950408889634
BOUND_SLACK = 1.02
BOUND_SLACK_ABS = 0.01
MIN_TRUSTED_DENOMINATOR = 2.0 ** -100
SECANT_STEPS = 12


def _params(n_grid):
    return pltpu.CompilerParams(
        dimension_semantics=("arbitrary",) * n_grid,
        vmem_limit_bytes=VMEM_LIMIT_BYTES,
    )


def _row_blocks(batch, seq, target):
    if seq >= target:
        assert seq % target == 0
        return 1, target
    bb = max(1, min(batch, target // seq))
    while batch % bb:
        bb -= 1
    return bb, seq


def _mod_spec(bb, chunk, n_grid):
    if n_grid == 2:
        return pl.BlockSpec((bb, 1, D_MODEL), lambda b, t: (b, 0, chunk))
    return pl.BlockSpec((bb, 1, D_MODEL), lambda b, t, e, c: (b, 0, chunk))


def _silu(x):
    return x / (1.0 + jnp.exp(-x))


def _layer_norm_rows(z, g, b):
    mu = jnp.mean(z, axis=-1, keepdims=True)
    zc = z - mu
    var = jnp.mean(zc * zc, axis=-1, keepdims=True)
    return zc * lax.rsqrt(var + LN_EPS) * g + b


def _modulation_kernel(c_ref, w_ref, b_ref, o_ref):
    a = _silu(c_ref[...]).astype(BF16)
    w = w_ref[0].astype(BF16)
    o_ref[0] = jnp.dot(a, w, preferred_element_type=F32) + b_ref[0]


def _modulation(c_all, w_cond, b_cond):
    depth, d, n = w_cond.shape
    rows = c_all.shape[0]
    tn = 1536
    return pl.pallas_call(
        _modulation_kernel,
        grid=(depth, n // tn),
        in_specs=[
            pl.BlockSpec((rows, d), lambda i, j: (0, 0)),
            pl.BlockSpec((1, d, tn), lambda i, j: (i, 0, j)),
            pl.BlockSpec((1, 1, tn), lambda i, j: (i, 0, j)),
        ],
        out_specs=pl.BlockSpec((1, rows, tn), lambda i, j: (i, 0, j)),
        out_shape=jax.ShapeDtypeStruct((depth, rows, n), F32),
        compiler_params=_params(2),
        name="modulation",
    )(c_all, w_cond, b_cond.reshape(depth, 1, n))


def _modmm_kernel(x_ref, sc_ref, sh_ref, w_ref, o_ref):
    bb, tt, d = x_ref.shape
    h = x_ref[...] * (1.0 + sc_ref[...]) + sh_ref[...]
    h = h.reshape(bb * tt, d).astype(BF16)
    acc = jnp.dot(h, w_ref[...], preferred_element_type=F32)
    o_ref[...] = acc.reshape(bb, tt, acc.shape[-1]).astype(o_ref.dtype)


def _modmm(x, mod, w, out_dtype, t_start=0):
    batch, seq, d = x.shape
    n = w.shape[1]
    seq_out = seq - t_start
    bb, tt = _row_blocks(batch, seq_out, ROW_TILE)
    assert t_start % tt == 0
    off = t_start // tt
    return pl.pallas_call(
        _modmm_kernel,
        grid=(batch // bb, seq_out // tt),
        in_specs=[
            pl.BlockSpec((bb, tt, d), lambda b, t: (b, t + off, 0)),
            _mod_spec(bb, 1, 2),
            _mod_spec(bb, 0, 2),
            pl.BlockSpec((d, n), lambda b, t: (0, 0)),
        ],
        out_specs=pl.BlockSpec((bb, tt, n), lambda b, t: (b, t, 0)),
        out_shape=jax.ShapeDtypeStruct((batch, seq_out, n), out_dtype),
        compiler_params=_params(2),
        name="a_project",
    )(x, mod, mod, w)


def _band_attn_kernel(*refs, n_kb, n_maybe_invalid):
    q_ref = refs[0]
    k_refs = refs[1:1 + n_kb]
    v_refs = refs[1 + n_kb:1 + 2 * n_kb]
    f_ref = refs[1 + 2 * n_kb]
    o_ref = refs[2 + 2 * n_kb]
    bias_ref = refs[3 + 2 * n_kb]
    q_tile = q_ref.shape[1]
    kb_sizes = [r.shape[1] for r in k_refs]
    k_tot = sum(kb_sizes)
    width = f_ref.shape[-1]
    i = pl.program_id(1)

    @pl.when((pl.program_id(0) == 0) & (i == 0))
    def _():
        rq = lax.broadcasted_iota(jnp.int32, (q_tile, k_tot), 0) // CHUNK
        ck = lax.broadcasted_iota(jnp.int32, (q_tile, k_tot), 1) // CHUNK
        in_band = (ck >= rq) & (ck <= rq + N_PAST_CHUNKS)
        for h in range(A_HEADS):
            rows = jnp.broadcast_to(f_ref[h], (q_tile, width))
            toep = pltpu.roll(rows, k_tot + 1, 1, stride=1, stride_axis=0)
            bias_ref[h] = jnp.where(in_band, toep[:, :k_tot], -jnp.inf)

    q = q_ref[0] * (HEAD_DIM ** -0.5)
    for h in range(A_HEADS):
        cols = slice(h * HEAD_DIM, (h + 1) * HEAD_DIM)
        qh = q[:, cols]
        parts = []
        for kb in range(n_kb):
            lg = lax.dot_general(qh, k_refs[kb][0, :, cols], (((1,), (1,)), ((), ())),
                                 preferred_element_type=F32)
            if kb < n_maybe_invalid:
                lg = jnp.where(i - n_maybe_invalid + kb >= 0, lg, -jnp.inf)
            parts.append(lg)
        logits = jnp.concatenate(parts, axis=1) + bias_ref[h]
        m = jnp.max(logits, axis=1, keepdims=True)
        p = jnp.exp(logits - m)
        l = jnp.sum(p, axis=1, keepdims=True)
        pb = p.astype(BF16)
        acc = jnp.zeros((q_tile, HEAD_DIM), F32)
        start = 0
        for kb in range(n_kb):
            acc = acc + jnp.dot(pb[:, start:start + kb_sizes[kb]], v_refs[kb][0, :, cols],
                                preferred_element_type=F32)
            start += kb_sizes[kb]
        o_ref[0, :, cols] = (acc / l).astype(o_ref.dtype)


def _bias_vector(table, q_tile, k_tot):
    width = k_tot + q_tile
    lo = -(q_tile - 1) - BAND_ROWS + REL_CLIP
    left = max(0, -lo)
    start = max(0, lo)
    n_mid = min(2 * REL_CLIP + 1 - start, width - left)
    right = width - left - n_mid
    mid = table.T[:, start:start + n_mid]
    f = jnp.pad(mid, ((0, 0), (left, right)), mode="edge")
    return f.reshape(A_HEADS, 1, width)


def _band_attn_prompt(qkv, table):
    batch, seq, _ = qkv.shape
    hd = A_HEADS * HEAD_DIM
    qt = A_Q_TILE
    n_prev = BAND_ROWS // qt
    n_kb = n_prev + 1
    k_tot = n_kb * qt
    f = _bias_vector(table, qt, k_tot)

    def kv_spec(kb, col):
        return pl.BlockSpec((1, qt, hd), lambda b, i: (b, jnp.maximum(i - n_prev + kb, 0), col))

    return pl.pallas_call(
        functools.partial(_band_attn_kernel, n_kb=n_kb, n_maybe_invalid=n_prev),
        grid=(batch, seq // qt),
        in_specs=[pl.BlockSpec((1, qt, hd), lambda b, i: (b, i, 0))]
        + [kv_spec(kb, 1) for kb in range(n_kb)]
        + [kv_spec(kb, 2) for kb in range(n_kb)]
        + [pl.BlockSpec(f.shape, lambda b, i: (0, 0, 0))],
        out_specs=pl.BlockSpec((1, qt, hd), lambda b, i: (b, i, 0)),
        out_shape=jax.ShapeDtypeStruct((batch, seq, hd), BF16),
        scratch_shapes=[pltpu.VMEM((A_HEADS, qt, k_tot), F32)],
        compiler_params=_params(2),
        name="band_attn_prompt",
    )(*([qkv] * (1 + 2 * n_kb)), f)


def _band_attn_sample(qkv, cache_k, cache_v, table):
    batch, seq, _ = qkv.shape
    hd = A_HEADS * HEAD_DIM
    win = cache_k.shape[1]
    assert seq == CHUNK and win == BAND_ROWS
    k_tot = win + seq
    f = _bias_vector(table, seq, k_tot)
    new = lambda col: pl.BlockSpec((1, seq, hd), lambda b, i: (b, 0, col))
    old = pl.BlockSpec((1, win, hd), lambda b, i: (b, 0, 0))
    return pl.pallas_call(
        functools.partial(_band_attn_kernel, n_kb=2, n_maybe_invalid=0),
        grid=(batch, 1),
        in_specs=[new(0), old, new(1), old, new(2), pl.BlockSpec(f.shape, lambda b, i: (0, 0, 0))],
        out_specs=pl.BlockSpec((1, seq, hd), lambda b, i: (b, 0, 0)),
        out_shape=jax.ShapeDtypeStruct((batch, seq, hd), BF16),
        scratch_shapes=[pltpu.VMEM((A_HEADS, seq, k_tot), F32)],
        compiler_params=_params(2),
        name="band_attn_sample",
    )(qkv, cache_k, qkv, cache_v, qkv, f)


def _mm_postnorm_kernel(o_ref, w_ref, x_ref, gate_ref, g_ref, b_ref, y_ref, *, alpha):
    bb, tt, d = x_ref.shape
    o = o_ref[...].reshape(bb * tt, o_ref.shape[-1])
    sub = jnp.dot(o, w_ref[...], preferred_element_type=F32).reshape(bb, tt, d)
    z = alpha * x_ref[...] + (1.0 + gate_ref[...]) * sub
    y_ref[...] = _layer_norm_rows(z, g_ref[...], b_ref[...])


def _mm_postnorm(o, w, x, mod, gate_chunk, g, b, alpha):
    batch, seq, d = x.shape
    bb, tt = _row_blocks(batch, seq, ROW_TILE)
    k = o.shape[-1]
    return pl.pallas_call(
        functools.partial(_mm_postnorm_kernel, alpha=alpha),
        grid=(batch // bb, seq // tt),
        in_specs=[
            pl.BlockSpec((bb, tt, k), lambda b_, t: (b_, t, 0)),
            pl.BlockSpec((k, d), lambda b_, t: (0, 0)),
            pl.BlockSpec((bb, tt, d), lambda b_, t: (b_, t, 0)),
            _mod_spec(bb, gate_chunk, 2),
            pl.BlockSpec((1, 1, d), lambda b_, t: (0, 0, 0)),
            pl.BlockSpec((1, 1, d), lambda b_, t: (0, 0, 0)),
        ],
        out_specs=pl.BlockSpec((bb, tt, d), lambda b_, t: (b_, t, 0)),
        out_shape=jax.ShapeDtypeStruct((batch, seq, d), F32),
        compiler_params=_params(2),
        name="out_proj_postnorm",
    )(o, w, x, mod, g.reshape(1, 1, d), b.reshape(1, 1, d))


def _ffn_kernel(x_ref, sc_ref, sh_ref, gate_ref, wr_ref, br_ref, wg_ref, wu_ref, wd_ref,
                g_ref, b_ref, y_ref, h_ref, comb_ref, acc_ref, *, routed, alpha):
    bb, tt, d = x_ref.shape
    rows = bb * tt
    e = pl.program_id(2)
    c = pl.program_id(3)
    lane = lax.broadcasted_iota(jnp.int32, (rows, LANES), 1)

    @pl.when((e == 0) & (c == 0))
    def _():
        h = (x_ref[...] * (1.0 + sc_ref[...]) + sh_ref[...]).reshape(rows, d)
        h_ref[...] = h.astype(BF16)
        acc_ref[...] = jnp.zeros_like(acc_ref)
        if routed:
            logits = jnp.dot(h, wr_ref[...], preferred_element_type=F32,
                             precision=lax.Precision.HIGHEST) + br_ref[...]
            logits = jnp.where(lane < N_EXPERTS, logits, -jnp.inf)
            m1 = jnp.max(logits, axis=1, keepdims=True)
            i1 = jnp.min(jnp.where(logits == m1, lane, LANES), axis=1, keepdims=True)
            rest = jnp.where(lane == i1, -jnp.inf, logits)
            m2 = jnp.max(rest, axis=1, keepdims=True)
            i2 = jnp.min(jnp.where(rest == m2, lane, LANES), axis=1, keepdims=True)
            e2 = jnp.exp(m2 - m1)
            den = 1.0 + e2
            comb_ref[...] = jnp.where(lane == i1, 1.0 / den, 0.0) + jnp.where(lane == i2, e2 / den, 0.0)

    h = h_ref[...]
    gt = jnp.dot(h, wg_ref[0], preferred_element_type=F32)
    up = jnp.dot(h, wu_ref[0], preferred_element_type=F32)
    act = (_silu(gt) * up).astype(BF16)
    part = jnp.dot(act, wd_ref[0], preferred_element_type=F32)
    if routed:
        ce = jnp.sum(jnp.where(lane == e, comb_ref[...], 0.0), axis=1, keepdims=True)
        part = ce * part
    acc_ref[...] += part

    @pl.when((e == pl.num_programs(2) - 1) & (c == pl.num_programs(3) - 1))
    def _():
        z = alpha * x_ref[...] + (1.0 + gate_ref[...]) * acc_ref[...].reshape(bb, tt, d)
        y_ref[...] = _layer_norm_rows(z, g_ref[...], b_ref[...])


def _ffn(x, mod, w_r, b_r, w_gu, w_down, g, b, *, routed, ff_chunk, alpha):
    batch, seq, d = x.shape
    n_e, ff, _ = w_down.shape
    assert ff % ff_chunk == 0
    n_c = ff // ff_chunk
    bb, tt = _row_blocks(batch, seq, FFN_ROW_TILE)
    rows = bb * tt
    vec = pl.BlockSpec((1, 1, d), lambda b_, t, e, c: (0, 0, 0))
    return pl.pallas_call(
        functools.partial(_ffn_kernel, routed=routed, alpha=alpha),
        grid=(batch // bb, seq // tt, n_e, n_c),
        in_specs=[
            pl.BlockSpec((bb, tt, d), lambda b_, t, e, c: (b_, t, 0)),
            _mod_spec(bb, 4, 4),
            _mod_spec(bb, 3, 4),
            _mod_spec(bb, 5, 4),
            pl.BlockSpec((d, LANES), lambda b_, t, e, c: (0, 0)),
            pl.BlockSpec((1, LANES), lambda b_, t, e, c: (0, 0)),
            pl.BlockSpec((1, d, ff_chunk), lambda b_, t, e, c: (e, 0, c)),
            pl.BlockSpec((1, d, ff_chunk), lambda b_, t, e, c: (e, 0, n_c + c)),
            pl.BlockSpec((1, ff_chunk, d), lambda b_, t, e, c: (e, c, 0)),
            vec,
            vec,
        ],
        out_specs=pl.BlockSpec((bb, tt, d), lambda b_, t, e, c: (b_, t, 0)),
        out_shape=jax.ShapeDtypeStruct((batch, seq, d), F32),
        scratch_shapes=[
            pltpu.VMEM((rows, d), BF16),
            pltpu.VMEM((rows, LANES), F32),
            pltpu.VMEM((rows, d), F32),
        ],
        compiler_params=_params(4),
        name="moe_ffn" if routed else "dense_ffn",
    )(x, mod, mod, mod, w_r, b_r, w_gu, w_gu, w_down, g.reshape(1, 1, d), b.reshape(1, 1, d))


def _moe_kernel(x_ref, sc_ref, sh_ref, gate_ref, wr_ref, br_ref, wg_ref, wu_ref, wd_ref, g_ref, b_ref,
                y_ref, h_ref, comb_ref, slot_ref, cnt_ref, tri_ref, hc_ref, acc_ref, out_ref, *, alpha):
    bb, tt, d = x_ref.shape
    rows = bb * tt
    cap = hc_ref.shape[1]
    e = pl.program_id(2)
    c = pl.program_id(3)
    last_c = pl.num_programs(3) - 1
    lane = lax.broadcasted_iota(jnp.int32, (rows, LANES), 1)

    @pl.when((pl.program_id(0) == 0) & (pl.program_id(1) == 0) & (e == 0) & (c == 0))
    def _():
        r = lax.broadcasted_iota(jnp.int32, (rows, rows), 0)
        col = lax.broadcasted_iota(jnp.int32, (rows, rows), 1)
        tri_ref[...] = jnp.where(r <= col, 1.0, 0.0).astype(BF16)

    @pl.when((e == 0) & (c == 0))
    def _():
        h = (x_ref[...] * (1.0 + sc_ref[...]) + sh_ref[...]).reshape(rows, d)
        h_ref[...] = h.astype(BF16)
        out_ref[...] = jnp.zeros_like(out_ref)
        logits = jnp.dot(h, wr_ref[...], preferred_element_type=F32,
                         precision=lax.Precision.HIGHEST) + br_ref[...]
        logits = jnp.where(lane < N_EXPERTS, logits, -jnp.inf)
        m1 = jnp.max(logits, axis=1, keepdims=True)
        i1 = jnp.min(jnp.where(logits == m1, lane, LANES), axis=1, keepdims=True)
        rest = jnp.where(lane == i1, -jnp.inf, logits)
        m2 = jnp.max(rest, axis=1, keepdims=True)
        i2 = jnp.min(jnp.where(rest == m2, lane, LANES), axis=1, keepdims=True)
        e2 = jnp.exp(m2 - m1)
        den = 1.0 + e2
        comb_ref[...] = jnp.where(lane == i1, 1.0 / den, 0.0) + jnp.where(lane == i2, e2 / den, 0.0)
        member = jnp.where((lane == i1) | (lane == i2), 1.0, 0.0)
        member_t = member.T[:2 * N_EXPERTS]
        upto = jnp.dot(member_t.astype(BF16), tri_ref[...], preferred_element_type=F32)
        slot_ref[...] = jnp.where(member_t > 0, upto - 1.0, -1.0)
        cnt_ref[...] = jnp.broadcast_to(upto[:, rows - 1:rows], cnt_ref.shape)

    n_sub = (jnp.max(cnt_ref[pl.ds(e, 1), :]).astype(jnp.int32) + cap - 1) // cap

    def selection(s):
        slot = slot_ref[pl.ds(e, 1), :] - (s * cap).astype(F32)
        r = lax.broadcasted_iota(jnp.int32, (cap, rows), 0).astype(F32)
        return jnp.where(slot == r, 1.0, 0.0).astype(BF16)

    @pl.when(c == 0)
    def _():
        def pack(s, carry):
            hc_ref[s] = jnp.dot(selection(s), h_ref[...], preferred_element_type=F32).astype(BF16)
            acc_ref[s] = jnp.zeros(acc_ref.shape[1:], F32)
            return carry
        lax.fori_loop(0, n_sub, pack, 0)

    def expert(s, carry):
        hc = hc_ref[s]
        gt = jnp.dot(hc, wg_ref[0, 0], preferred_element_type=F32)
        up = jnp.dot(hc, wu_ref[0, 0], preferred_element_type=F32)
        act = (_silu(gt) * up).astype(BF16)
        acc_ref[s] += jnp.dot(act, wd_ref[0], preferred_element_type=F32)
        return carry

    lax.fori_loop(0, n_sub, expert, 0)

    @pl.when(c == last_c)
    def _():
        ce = jnp.sum(jnp.where(lane == e, comb_ref[...], 0.0), axis=1, keepdims=True)

        def unpack(s, carry):
            back = lax.dot_general(selection(s), acc_ref[s].astype(BF16), (((0,), (0,)), ((), ())),
                                   preferred_element_type=F32)
            out_ref[...] += ce * back
            return carry
        lax.fori_loop(0, n_sub, unpack, 0)

    @pl.when((e == pl.num_programs(2) - 1) & (c == last_c))
    def _():
        z = alpha * x_ref[...] + (1.0 + gate_ref[...]) * out_ref[...].reshape(bb, tt, d)
        y_ref[...] = _layer_norm_rows(z, g_ref[...], b_ref[...])


def _moe(x, mod, w_r, b_r, w_gu, w_down, g, b, *, ff_chunk, alpha):
    batch, seq, d = x.shape
    n_e, ff, _ = w_down.shape
    assert ff % ff_chunk == 0
    n_c = ff // ff_chunk
    bb, tt = _row_blocks(batch, seq, FFN_ROW_TILE)
    rows = bb * tt
    cap = MOE_CAPACITY
    max_sub = -(-rows // cap)
    vec = pl.BlockSpec((1, 1, d), lambda b_, t, e, c: (0, 0, 0))
    return pl.pallas_call(
        functools.partial(_moe_kernel, alpha=alpha),
        grid=(batch // bb, seq // tt, n_e, n_c),
        in_specs=[
            pl.BlockSpec((bb, tt, d), lambda b_, t, e, c: (b_, t, 0), pipeline_mode=pl.Buffered(1)),
            _mod_spec(bb, 4, 4),
            _mod_spec(bb, 3, 4),
            _mod_spec(bb, 5, 4),
            pl.BlockSpec((d, LANES), lambda b_, t, e, c: (0, 0)),
            pl.BlockSpec((1, LANES), lambda b_, t, e, c: (0, 0)),
            pl.BlockSpec((1, 1, d, ff_chunk), lambda b_, t, e, c: (e, c, 0, 0)),
            pl.BlockSpec((1, 1, d, ff_chunk), lambda b_, t, e, c: (e, n_c + c, 0, 0)),
            pl.BlockSpec((1, ff_chunk, d), lambda b_, t, e, c: (e, c, 0)),
            vec,
            vec,
        ],
        out_specs=pl.BlockSpec((bb, tt, d), lambda b_, t, e, c: (b_, t, 0)),
        out_shape=jax.ShapeDtypeStruct((batch, seq, d), F32),
        scratch_shapes=[
            pltpu.VMEM((rows, d), BF16),
            pltpu.VMEM((rows, LANES), F32),
            pltpu.VMEM((2 * N_EXPERTS, rows), F32),
            pltpu.VMEM((2 * N_EXPERTS, LANES), F32),
            pltpu.VMEM((rows, rows), BF16),
            pltpu.VMEM((max_sub, cap, d), BF16),
            pltpu.VMEM((max_sub, cap, d), F32),
            pltpu.VMEM((rows, d), F32),
        ],
        compiler_params=_params(4),
        name="moe_ffn",
    )(x, mod, mod, mod, w_r, b_r, w_gu, w_gu, w_down, g.reshape(1, 1, d), b.reshape(1, 1, d))


def _rope_lanes(seg, cos, sin_signed, first_half):
    width = seg.shape[-1]
    reps = width // LANES
    if reps > 1:
        cos = jnp.concatenate([cos] * reps, axis=1)
        sin_signed = jnp.concatenate([sin_signed] * reps, axis=1)
        first_half = jnp.concatenate([first_half] * reps, axis=1)
    half = HEAD_DIM // 2
    swapped = jnp.where(first_half, pltpu.roll(seg, width - half, 1), pltpu.roll(seg, half, 1))
    return seg * cos + swapped * sin_signed


def _b_project_kernel(x_ref, sc_ref, sh_ref, w_ref, cos_ref, sin_ref, kg_ref, kb_ref,
                      q_ref, qi_ref, k_ref, v_ref, kw_ref):
    bb, tt, d = x_ref.shape
    rows = bb * tt
    h = x_ref[...] * (1.0 + sc_ref[...]) + sh_ref[...]
    h = h.reshape(rows, d).astype(BF16)
    acc = jnp.dot(h, w_ref[...], preferred_element_type=F32)
    cos = cos_ref[...]
    sin_signed = sin_ref[...]
    lane = lax.broadcasted_iota(jnp.int32, (1, LANES), 1)
    first_half = (lane % HEAD_DIM) < (HEAD_DIM // 2)

    def out(ref, val):
        ref[...] = val.reshape(bb, tt, val.shape[-1]).astype(ref.dtype)

    o_k, o_v, o_qi, o_ki = B_Q, B_Q + B_KV, B_Q + 2 * B_KV, B_Q + 2 * B_KV + B_QI
    out(q_ref, _rope_lanes(acc[:, :o_k], cos, sin_signed, first_half) * QK_SCALE_LOG2)
    out(k_ref, _rope_lanes(acc[:, o_k:o_v], cos, sin_signed, first_half))
    out(v_ref, acc[:, o_v:o_qi])
    out(qi_ref, _rope_lanes(acc[:, o_qi:o_ki], cos, sin_signed, first_half))
    seg = acc[:, o_ki:]
    is_ki = lane < IDX_DIM
    mu = jnp.sum(jnp.where(is_ki, seg, 0.0), axis=1, keepdims=True) / IDX_DIM
    cen = jnp.where(is_ki, seg - mu, 0.0)
    var = jnp.sum(cen * cen, axis=1, keepdims=True) / IDX_DIM
    ki = cen * lax.rsqrt(var + LN_EPS) * kg_ref[...] + kb_ref[...]
    ki = _rope_lanes(ki, cos, sin_signed, first_half)
    out(kw_ref, jnp.where(is_ki, ki, seg * (IDX_HEADS ** -0.5)))


def _b_project(x, mod, w, cos, sin_signed, kn_g, kn_b, table_per_tile):
    batch, seq, d = x.shape
    bb, tt = _row_blocks(batch, seq, ROW_TILE)
    rows = bb * tt
    n = w.shape[1]
    tab = pl.BlockSpec((rows, LANES), (lambda b, t: (0, 0)) if table_per_tile else (lambda b, t: (t, 0)))
    vec = pl.BlockSpec((1, LANES), lambda b, t: (0, 0))

    def o_spec(width):
        return pl.BlockSpec((bb, tt, width), lambda b, t: (b, t, 0))

    def o_shape(width, dtype):
        return jax.ShapeDtypeStruct((batch, seq, width), dtype)

    return pl.pallas_call(
        _b_project_kernel,
        grid=(batch // bb, seq // tt),
        in_specs=[
            pl.BlockSpec((bb, tt, d), lambda b, t: (b, t, 0)),
            _mod_spec(bb, 1, 2),
            _mod_spec(bb, 0, 2),
            pl.BlockSpec((d, n), lambda b, t: (0, 0)),
            tab,
            tab,
            vec,
            vec,
        ],
        out_specs=[o_spec(B_Q), o_spec(B_QI), o_spec(B_KV), o_spec(B_KV), o_spec(LANES)],
        out_shape=[o_shape(B_Q, BF16), o_shape(B_QI, BF16), o_shape(B_KV, F32), o_shape(B_KV, F32),
                   o_shape(LANES, F32)],
        compiler_params=_params(2),
        name="b_project",
    )(x, mod, mod, w, cos, sin_signed, kn_g, kn_b)


def _rope_tables(pos):
    half = HEAD_DIM // 2
    inv = ROPE_THETA ** (-jnp.arange(half, dtype=F32) / half)
    ang = pos.astype(F32)[:, None] * inv[None, :]
    cos, sin = jnp.cos(ang), jnp.sin(ang)
    return (jnp.concatenate([cos, cos, cos, cos], axis=1),
            jnp.concatenate([-sin, sin, -sin, sin], axis=1))


def _key_norm_kernel(k_ref, grp_ref, o_ref):
    k = k_ref[0].astype(F32)
    n2 = jnp.dot((k * k).astype(BF16), grp_ref[...], preferred_element_type=F32)
    mx = jnp.broadcast_to(jnp.max(n2, axis=0, keepdims=True), o_ref.shape[1:])

    @pl.when(pl.program_id(1) == 0)
    def _():
        o_ref[0] = mx

    @pl.when(pl.program_id(1) > 0)
    def _():
        o_ref[0] = jnp.maximum(o_ref[0], mx)


def _key_norm_max(k):
    batch, lp, width = k.shape
    tile = DSA_KEY_TILE
    grp = (jnp.arange(width)[:, None] // HEAD_DIM == jnp.arange(LANES)[None, :]).astype(BF16)
    return pl.pallas_call(
        _key_norm_kernel,
        grid=(batch, lp // tile),
        in_specs=[pl.BlockSpec((1, tile, width), lambda b, t: (b, t, 0)),
                  pl.BlockSpec((width, LANES), lambda b, t: (0, 0))],
        out_specs=pl.BlockSpec((1, 8, LANES), lambda b, t: (b, 0, 0)),
        out_shape=jax.ShapeDtypeStruct((batch, 8, LANES), F32),
        compiler_params=_params(2),
        name="key_norm_max",
    )(k, grp)


def _dsa_kernel(qt_ref, qit_ref, kwt_ref, ki_ref, k_ref, vta_ref, kmax_ref, o_ref,
                s_ref, qpad_ref, bound_ref, m_ref, acc_ref, *, causal, n_keys, k_sel):
    qn = qt_ref.shape[2]
    lt = DSA_KEY_TILE
    i = pl.program_id(1)
    if causal:
        n_tiles = ((i + 1) * qn + lt - 1) // lt
        qpos = i * qn + lax.broadcasted_iota(jnp.int32, (1, qn), 1)
        limit = (qpos // CHUNK + 1) * CHUNK
    else:
        n_tiles = ki_ref.shape[1] // lt
        limit = jnp.full((1, qn), n_keys, jnp.int32)

    def key_slice(t):
        return pl.ds(pl.multiple_of(t * lt, lt), lt)

    def key_pos(t):
        return t * lt + lax.broadcasted_iota(jnp.int32, (lt, 1), 0)

    def fold_rows(x, op):
        return op(op(x.reshape(lt // 32, 4, 8, x.shape[-1]), axis=0), axis=0)

    def paired_tile_loop(body, init):
        def pair(t2, carry):
            return body(2 * t2 + 1, body(2 * t2, carry))
        carry = lax.fori_loop(0, n_tiles // 2, pair, init)
        return lax.cond(n_tiles % 2 == 1, lambda c: body(n_tiles - 1, c), lambda c: c, carry)

    qit = qit_ref[0]
    kwt = kwt_ref[0]
    qi_w = jnp.concatenate([qit[h * IDX_DIM:(h + 1) * IDX_DIM, :] for h in range(IDX_HEADS)], axis=1)
    w_row = jnp.concatenate([kwt[IDX_DIM + h:IDX_DIM + h + 1, :] for h in range(IDX_HEADS)], axis=1)
    w_row = w_row * (IDX_DIM ** -0.5)

    def score_body(t, carry):
        rmax, rmin = carry
        s = jnp.dot(ki_ref[0, key_slice(t), :], qi_w, preferred_element_type=F32)
        s = jnp.maximum(s, 0.0) * w_row
        sc = s[:, 0:qn]
        for h in range(1, IDX_HEADS):
            sc = sc + s[:, h * qn:(h + 1) * qn]
        adm = key_pos(t) < limit
        s_ref[key_slice(t), :] = jnp.where(adm, sc, -jnp.inf)
        rmax = jnp.maximum(rmax, jnp.max(jnp.where(adm, sc, -jnp.inf), axis=0, keepdims=True))
        rmin = jnp.minimum(rmin, jnp.min(jnp.where(adm, sc, jnp.inf), axis=0, keepdims=True))
        return rmax, rmin

    rmax, rmin = paired_tile_loop(score_body,
                                  (jnp.full((1, qn), -jnp.inf, F32), jnp.full((1, qn), jnp.inf, F32)))

    def count(pred):
        def body(t, acc):
            return acc + fold_rows(jnp.where(pred(s_ref[key_slice(t), :], key_pos(t)), 1.0, 0.0), jnp.sum)
        acc = lax.fori_loop(0, n_tiles, body, jnp.zeros((8, qn), F32))
        return jnp.sum(acc, axis=0, keepdims=True)

    def to_key(x):
        bits = lax.bitcast_convert_type(x, jnp.int32)
        return bits ^ ((bits >> 31) & 0x7FFFFFFF)

    def from_key(key):
        return lax.bitcast_convert_type(key ^ ((key >> 31) & 0x7FFFFFFF), F32)

    n_adm = limit.astype(F32)
    target = jnp.minimum(n_adm, float(k_sel))
    log_target = jnp.log(target)

    def any_left(done):
        return jnp.sum(done) < qn

    c_ge0 = count(lambda s, _: s >= 0.0)
    c_gt0 = count(lambda s, _: s > 0.0)
    is_pos = c_gt0 >= target
    is_neg = c_ge0 < target
    zero_key = jnp.zeros((1, qn), jnp.int32)
    lo0 = jnp.where(is_neg, to_key(rmin), zero_key)
    hi0 = jnp.where(is_pos, to_key(rmax) + 1, zero_key)
    c_lo0 = jnp.where(is_neg, n_adm, c_ge0)
    c_hi0 = jnp.where(is_pos, 0.0, c_ge0)
    done0 = jnp.where((n_adm == target) | jnp.logical_not(is_pos | is_neg) | (c_lo0 == target), 1.0, 0.0)
    lo0 = jnp.where(n_adm == target, to_key(rmin), lo0)
    c_lo0 = jnp.where(n_adm == target, n_adm, c_lo0)

    def sel_cond(st):
        return any_left(st[7])

    def sel_body(st):
        it, lo, hi, c_lo, c_hi, w_lo, w_hi, done, side = st
        t_lo, t_hi = from_key(lo), from_key(hi)
        f_lo = (jnp.log(c_lo) - log_target) * w_lo
        f_hi = (log_target - jnp.log(jnp.maximum(c_hi, 0.5))) * w_hi
        guess = to_key(t_lo + f_lo / (f_lo + f_hi) * (t_hi - t_lo))
        halve = (lo >> 1) + (hi >> 1) + (lo & hi & 1)
        mid = jnp.where(it >= SECANT_STEPS, halve, jnp.clip(guess, lo + 1, hi - 1))
        stuck = halve == lo
        c = count(lambda s, _: s >= from_key(mid))
        live = (done < 0.5) & jnp.logical_not(stuck)
        up = live & (c >= target)
        dn = live & (c < target)
        w_hi = jnp.where(up, jnp.where(side > 0, 0.5 * w_hi, 1.0), jnp.where(dn, 1.0, w_hi))
        w_lo = jnp.where(dn, jnp.where(side < 0, 0.5 * w_lo, 1.0), jnp.where(up, 1.0, w_lo))
        side = jnp.where(up, 1.0, jnp.where(dn, -1.0, side))
        lo = jnp.where(up, mid, lo)
        c_lo = jnp.where(up, c, c_lo)
        hi = jnp.where(dn, mid, hi)
        c_hi = jnp.where(dn, c, c_hi)
        done = jnp.where(stuck | (c_lo == target), 1.0, done)
        return it + 1, lo, hi, c_lo, c_hi, w_lo, w_hi, done, side

    ones = jnp.ones((1, qn), F32)
    sel = lax.while_loop(sel_cond, sel_body,
                         (jnp.int32(0), lo0, hi0, c_lo0, c_hi0, ones, ones, done0, 0.0 * ones))
    thr, c_thr = from_key(sel[1]), sel[3]

    tied = c_thr > target

    @pl.when(jnp.sum(jnp.where(tied, 1.0, 0.0)) > 0)
    def _():
        need = target - count(lambda s, _: s > thr)

        def idx_cond(st):
            return any_left(st[5])

        def idx_body(st):
            it, lo_j, hi_j, c_lo_j, c_hi_j, done = st
            span = (hi_j - lo_j).astype(F32)
            step = jnp.ceil((need - c_lo_j) / (c_hi_j - c_lo_j) * span).astype(jnp.int32)
            step = jnp.where(it % 2 == 0, step, (hi_j - lo_j) // 2)
            mid = lo_j + jnp.clip(step, 1, jnp.maximum(hi_j - lo_j - 1, 1))
            c = count(lambda s, kp: (s == thr) & (kp <= mid))
            live = (done < 0.5) & (hi_j - lo_j > 1)
            below = live & (c < need)
            above = live & (c >= need)
            lo_j = jnp.where(below, mid, lo_j)
            c_lo_j = jnp.where(below, c, c_lo_j)
            hi_j = jnp.where(above, mid, hi_j)
            c_hi_j = jnp.where(above, c, c_hi_j)
            done = jnp.where((c_hi_j == need) | (hi_j - lo_j <= 1), 1.0, done)
            return it + 1, lo_j, hi_j, c_lo_j, c_hi_j, done

        first = jnp.full((1, qn), -1, jnp.int32)
        final = jnp.full((1, qn), n_tiles * lt - 1, jnp.int32)
        n_ties = c_thr - (target - need)
        idx_done0 = jnp.where(tied & (n_ties > need), 0.0, 1.0)
        idx = lax.while_loop(idx_cond, idx_body,
                             (jnp.int32(0), first, final, 0.0 * ones, n_ties, idx_done0))
        last = idx[2]

        def drop_body(t, carry):
            s = s_ref[key_slice(t), :]
            s_ref[key_slice(t), :] = jnp.where(tied & (s == thr) & (key_pos(t) > last), -jnp.inf, s)
            return carry

        lax.fori_loop(0, n_tiles, drop_body, 0)

    qt = qt_ref[0]
    zeros = jnp.zeros((HEAD_DIM, B_GROUP * qn), BF16)
    for g in range(B_KV_HEADS):
        q_g = jnp.concatenate(
            [qt[(g * B_GROUP + j) * HEAD_DIM:(g * B_GROUP + j + 1) * HEAD_DIM, :] for j in range(B_GROUP)],
            axis=1)
        qpad_ref[g] = jnp.concatenate([zeros] * g + [q_g] + [zeros] * (B_KV_HEADS - 1 - g), axis=0)

    qf = qt.astype(F32)
    q_norm2 = jnp.sum((qf * qf).reshape(B_HEADS, HEAD_DIM, qn), axis=1)
    for g in range(B_KV_HEADS):
        b = jnp.sqrt(q_norm2[g * B_GROUP:(g + 1) * B_GROUP] * kmax_ref[0, 0:1, g:g + 1])
        b = b * BOUND_SLACK + BOUND_SLACK_ABS
        bound_ref[g] = jnp.concatenate([b[j:j + 1] for j in range(B_GROUP)], axis=1)
    acc_ref[...] = jnp.zeros(acc_ref.shape, F32)

    def fast_body(t, carry):
        keys = k_ref[0, key_slice(t), :]
        sel = jnp.where(s_ref[key_slice(t), :] >= thr, 1.0, 0.0).astype(BF16)
        sel = jnp.concatenate([sel] * B_GROUP, axis=1)
        def logits(g):
            return jnp.dot(keys, qpad_ref[g], preferred_element_type=F32)

        lg = logits(0)
        for g in range(B_KV_HEADS):
            lg_next = logits(g + 1) if g + 1 < B_KV_HEADS else None
            p = jnp.exp2(lg - bound_ref[g]).astype(BF16) * sel
            acc_ref[g] += jnp.dot(vta_ref[0, g, :, key_slice(t)], p, preferred_element_type=F32)
            lg = lg_next
        return carry

    paired_tile_loop(fast_body, 0)
    den_min = jnp.min(acc_ref[:, HEAD_DIM:HEAD_DIM + 1, :])

    @pl.when(jnp.logical_not(den_min >= MIN_TRUSTED_DENOMINATOR))
    def _():
        m_ref[...] = jnp.full(m_ref.shape, NEG_BIG, F32)
        acc_ref[...] = jnp.zeros(acc_ref.shape, F32)

        def exact_body(t, carry):
            keys = k_ref[0, key_slice(t), :]
            mb = jnp.where(s_ref[key_slice(t), :] >= thr, 0.0, -jnp.inf)
            mb = jnp.concatenate([mb] * B_GROUP, axis=1)
            for g in range(B_KV_HEADS):
                lg = jnp.dot(keys, qpad_ref[g], preferred_element_type=F32) + mb
                m_old = m_ref[g]
                m_new = jnp.maximum(m_old, jnp.max(fold_rows(lg, jnp.max), axis=0, keepdims=True))
                p = jnp.exp2(lg - m_new).astype(BF16)
                pv = jnp.dot(vta_ref[0, g, :, key_slice(t)], p, preferred_element_type=F32)
                acc_ref[g] = jnp.exp2(m_old - m_new) * acc_ref[g] + pv
                m_ref[g] = m_new
            return carry

        lax.fori_loop(0, n_tiles, exact_body, 0)

    for g in range(B_KV_HEADS):
        acc = acc_ref[g]
        o_g = acc[:HEAD_DIM] / acc[HEAD_DIM:HEAD_DIM + 1]
        for j in range(B_GROUP):
            hh = g * B_GROUP + j
            o_ref[0, hh * HEAD_DIM:(hh + 1) * HEAD_DIM, :] = o_g[:, j * qn:(j + 1) * qn].astype(o_ref.dtype)


V_AUG_ROWS = HEAD_DIM + 16


def _dsa(q, qi, kw, k, v, ki, *, q_tile, causal, k_sel):
    batch, seq, _ = q.shape
    n_keys = k.shape[1]
    lp = -(-n_keys // DSA_KEY_TILE) * DSA_KEY_TILE
    assert seq % q_tile == 0 and q_tile % LANES == 0
    pad = ((0, 0), (0, lp - n_keys), (0, 0))
    k, v, ki = (jnp.pad(a.astype(BF16), pad) for a in (k, v, ki))
    vt = jnp.swapaxes(v, 1, 2).reshape(batch, B_KV_HEADS, HEAD_DIM, lp)
    vta = jnp.concatenate([vt, jnp.ones((batch, B_KV_HEADS, 1, lp), BF16),
                           jnp.zeros((batch, B_KV_HEADS, V_AUG_ROWS - HEAD_DIM - 1, lp), BF16)], axis=2)
    qt, qit, kwt = (jnp.swapaxes(a, 1, 2) for a in (q, qi, kw))

    def qspec(rows):
        return pl.BlockSpec((1, rows, q_tile), lambda b, i: (b, 0, i))

    def resident(shape):
        zero = (0,) * len(shape)
        return pl.BlockSpec((1,) + shape, lambda b, i: (b,) + zero, pipeline_mode=pl.Buffered(1))

    ot = pl.pallas_call(
        functools.partial(_dsa_kernel, causal=causal, n_keys=n_keys, k_sel=k_sel),
        grid=(batch, seq // q_tile),
        in_specs=[qspec(B_Q), qspec(B_QI), qspec(LANES),
                  resident((lp, IDX_DIM)), resident((lp, B_KV)),
                  resident((B_KV_HEADS, V_AUG_ROWS, lp)),
                  pl.BlockSpec((1, 8, LANES), lambda b, i: (b, 0, 0))],
        out_specs=qspec(B_Q),
        out_shape=jax.ShapeDtypeStruct((batch, B_Q, seq), BF16),
        scratch_shapes=[pltpu.VMEM((lp, q_tile), F32),
                        pltpu.VMEM((B_KV_HEADS, B_KV, B_GROUP * q_tile), BF16),
                        pltpu.VMEM((B_KV_HEADS, 1, B_GROUP * q_tile), F32),
                        pltpu.VMEM((B_KV_HEADS, 1, B_GROUP * q_tile), F32),
                        pltpu.VMEM((B_KV_HEADS, V_AUG_ROWS, B_GROUP * q_tile), F32)],
        compiler_params=_params(2),
        name="dsa_prompt" if causal else "dsa_sample",
    )(qt, qit, kwt, ki, k, vta, _key_norm_max(k))
    return jnp.swapaxes(ot, 1, 2)


def kernel(x_prompt, x_sample, cache_k_a, cache_v_a, cache_k_b, cache_v_b, cache_kidx_b,
           c_prompt, c_sample, w_cond, b_cond, ln_g, ln_b, a_w_in, a_w_o, a_rel_bias,
           b_w_in, b_w_o, b_kidx_ln_g, b_kidx_ln_b, ffn_w_gu, ffn_w_down,
           moe_w_router, moe_b_router, moe_w_gu, moe_w_down):
    depth = w_cond.shape[0]
    alpha = (2 * depth) ** 0.25
    n_p, seq, d = x_prompt.shape
    n_s, dec_seq, _ = x_sample.shape
    past = cache_k_b.shape[2]
    a_hd = A_HEADS * HEAD_DIM

    rows = n_p + n_s
    rows_pad = -(-rows // 8) * 8
    c_all = jnp.pad(jnp.concatenate([c_prompt, c_sample], axis=0), ((0, rows_pad - rows), (0, 0)))
    mod_all = _modulation(c_all, w_cond, b_cond)

    xp, xs = x_prompt, x_sample
    outs = {k: [] for k in ("ka_p", "va_p", "kb_p", "vb_p", "ib_p", "ka_s", "va_s", "kb_s", "vb_s", "ib_s")}
    dummy_wr = jnp.zeros((d, LANES), F32)
    dummy_br = jnp.zeros((1, LANES), F32)
    for i in range(depth):
        j = i // 2
        mod_p = mod_all[i, :n_p].reshape(n_p, 1, 6 * d)
        mod_s = mod_all[i, n_p:rows].reshape(n_s, 1, 6 * d)
        g1, b1, g2, b2 = ln_g[i, 0], ln_b[i, 0], ln_g[i, 1], ln_b[i, 1]
        if i % 2 == 0:
            w_in = a_w_in[j].astype(BF16)
            keep = min(BAND_ROWS, seq)
            qkv_p = _modmm(xp, mod_p, w_in, BF16)
            kv_p = _modmm(xp, mod_p, w_in[:, a_hd:], F32, t_start=seq - keep)
            qkv_s = _modmm(xs, mod_s, w_in, BF16)
            kv_s = _modmm(xs, mod_s, w_in[:, a_hd:], F32)
            outs["ka_p"].append(kv_p[..., :a_hd].reshape(n_p, keep, A_HEADS, HEAD_DIM))
            outs["va_p"].append(kv_p[..., a_hd:].reshape(n_p, keep, A_HEADS, HEAD_DIM))
            outs["ka_s"].append(kv_s[..., :a_hd].reshape(n_s, dec_seq, A_HEADS, HEAD_DIM))
            outs["va_s"].append(kv_s[..., a_hd:].reshape(n_s, dec_seq, A_HEADS, HEAD_DIM))
            o_p = _band_attn_prompt(qkv_p, a_rel_bias[j])
            ck = cache_k_a[j].reshape(n_s, -1, a_hd).astype(BF16)
            cv = cache_v_a[j].reshape(n_s, -1, a_hd).astype(BF16)
            o_s = _band_attn_sample(qkv_s, ck, cv, a_rel_bias[j])
            w_o = a_w_o[j].astype(BF16)
        else:
            w_in = jnp.pad(b_w_in[j], ((0, 0), (0, B_PROJ_PAD - B_PROJ))).astype(BF16)
            kn_g = jnp.pad(b_kidx_ln_g[j], (0, LANES - IDX_DIM)).reshape(1, LANES)
            kn_b = jnp.pad(b_kidx_ln_b[j], (0, LANES - IDX_DIM)).reshape(1, LANES)
            cos_p, sin_p = _rope_tables(jnp.arange(seq))
            cos_s, sin_s = _rope_tables(past + jnp.arange(dec_seq))
            bb_s, _ = _row_blocks(n_s, dec_seq, ROW_TILE)
            cos_s, sin_s = jnp.tile(cos_s, (bb_s, 1)), jnp.tile(sin_s, (bb_s, 1))
            q_p, qi_p, k_p, v_p, kw_p = _b_project(xp, mod_p, w_in, cos_p, sin_p, kn_g, kn_b, False)
            q_s, qi_s, k_s, v_s, kw_s = _b_project(xs, mod_s, w_in, cos_s, sin_s, kn_g, kn_b, True)
            outs["kb_p"].append(k_p.reshape(n_p, seq, B_KV_HEADS, HEAD_DIM))
            outs["vb_p"].append(v_p.reshape(n_p, seq, B_KV_HEADS, HEAD_DIM))
            outs["ib_p"].append(kw_p[..., :IDX_DIM])
            outs["kb_s"].append(k_s.reshape(n_s, dec_seq, B_KV_HEADS, HEAD_DIM))
            outs["vb_s"].append(v_s.reshape(n_s, dec_seq, B_KV_HEADS, HEAD_DIM))
            outs["ib_s"].append(kw_s[..., :IDX_DIM])
            o_p = _dsa(q_p, qi_p, kw_p, k_p, v_p, kw_p[..., :IDX_DIM], q_tile=DSA_Q_TILE, causal=True,
                       k_sel=min(TOPK_MAX, seq // 4))
            n_keys = past + dec_seq
            kk = jnp.concatenate([cache_k_b[j].reshape(n_s, past, B_KV), k_s], axis=1)
            vc = jnp.concatenate([cache_v_b[j].reshape(n_s, past, B_KV), v_s], axis=1)
            kki = jnp.concatenate([cache_kidx_b[j], kw_s[..., :IDX_DIM]], axis=1)
            rep = LANES // dec_seq
            q_r, qi_r, kw_r = (jnp.concatenate([a] * rep, axis=1) for a in (q_s, qi_s, kw_s))
            o_s = _dsa(q_r, qi_r, kw_r, kk, vc, kki, q_tile=LANES, causal=False,
                       k_sel=min(TOPK_MAX, n_keys // 4))[:, :dec_seq]
            w_o = b_w_o[j].astype(BF16)
        xp = _mm_postnorm(o_p, w_o, xp, mod_p, 2, g1, b1, alpha)
        xs = _mm_postnorm(o_s, w_o, xs, mod_s, 2, g1, b1, alpha)
        if i % 2 == 0:
            w_gu = ffn_w_gu[j].astype(BF16)[None]
            w_dn = ffn_w_down[j].astype(BF16)[None]
            dense = dict(routed=False, ff_chunk=D_FF // 2, alpha=alpha)
            xp = _ffn(xp, mod_p, dummy_wr, dummy_br, w_gu, w_dn, g2, b2, **dense)
            xs = _ffn(xs, mod_s, dummy_wr, dummy_br, w_gu, w_dn, g2, b2, **dense)
        else:
            fc = MOE_FF_CHUNK
            w_gu = moe_w_gu[j].astype(BF16).reshape(N_EXPERTS, d, 2 * D_FF_EXPERT // fc, fc)
            w_gu = jnp.swapaxes(w_gu, 1, 2)
            w_dn = moe_w_down[j].astype(BF16)
            w_r = jnp.pad(moe_w_router[j], ((0, 0), (0, LANES - N_EXPERTS)))
            b_r = jnp.pad(moe_b_router[j], (0, LANES - N_EXPERTS)).reshape(1, LANES)
            moe = dict(ff_chunk=fc, alpha=alpha)
            xp = _moe(xp, mod_p, w_r, b_r, w_gu, w_dn, g2, b2, **moe)
            xs = _moe(xs, mod_s, w_r, b_r, w_gu, w_dn, g2, b2, **moe)

    st = lambda name: jnp.stack(outs[name])
    return (xp, xs, st("ka_p"), st("va_p"), st("kb_p"), st("vb_p"), st("ib_p"),
            st("ka_s"), st("va_s"), st("kb_s"), st("vb_s"), st("ib_s"))
```

```python
import functools

import jax
import jax.numpy as jnp
from jax import lax
from jax.experimental import pallas as pl
from jax.experimental.pallas import tpu as pltpu

F32 = jnp.float32
BF16 = jnp.bfloat16

D_MODEL = 1024
CHUNK = 64
N_PAST_CHUNKS = 8
BAND_ROWS = N_PAST_CHUNKS * CHUNK
REL_CLIP = 2 * CHUNK
HEAD_DIM = 64
A_HEADS = 16
B_HEADS = 16
B_KV_HEADS = 4
B_GROUP = B_HEADS // B_KV_HEADS
IDX_HEADS = 8
IDX_DIM = 64
TOPK_MAX = 256
D_FF = 2816
N_EXPERTS = 8
D_FF_EXPERT = 3584
ROPE_THETA = 10000.0
LN_EPS = 1e-5
B_Q = B_HEADS * HEAD_DIM
B_KV = B_KV_HEADS * HEAD_DIM
B_QI = IDX_HEADS * IDX_DIM
B_PROJ = B_Q + 2 * B_KV + B_QI + IDX_DIM + IDX_HEADS

LANES = 128
VMEM_LIMIT_BYTES = 58 * 1024 * 1024

A_Q_TILE = 4 * CHUNK
DSA_Q_TILE = 4 * CHUNK
DSA_KEY_TILE = 512
ROW_TILE = 512
FFN_ROW_TILE = 1024
MOE_CAPACITY = 288
MOE_FF_CHUNK = 1792
B_PROJ_PAD = B_Q + 2 * B_KV + B_QI + LANES
NEG_BIG = -1e30
QK_SCALE_LOG2 = HEAD_DIM ** -0.5 * 1.4426950408889634
BOUND_SLACK = 1.02
BOUND_SLACK_ABS = 0.01
MIN_TRUSTED_DENOMINATOR = 2.0 ** -100
SECANT_STEPS = 12


def _params(n_grid):
    return pltpu.CompilerParams(
        dimension_semantics=("arbitrary",) * n_grid,
        vmem_limit_bytes=VMEM_LIMIT_BYTES,
    )


def _row_blocks(batch, seq, target):
    if seq >= target:
        assert seq % target == 0
        return 1, target
    bb = max(1, min(batch, target // seq))
    while batch % bb:
        bb -= 1
    return bb, seq


def _mod_spec(bb, chunk, n_grid):
    if n_grid == 2:
        return pl.BlockSpec((bb, 1, D_MODEL), lambda b, t: (b, 0, chunk))
    return pl.BlockSpec((bb, 1, D_MODEL), lambda b, t, e, c: (b, 0, chunk))


def _silu(x):
    return x / (1.0 + jnp.exp(-x))


def _layer_norm_rows(z, g, b):
    mu = jnp.mean(z, axis=-1, keepdims=True)
    zc = z - mu
    var = jnp.mean(zc * zc, axis=-1, keepdims=True)
    return zc * lax.rsqrt(var + LN_EPS) * g + b


def _modulation_kernel(c_ref, w_ref, b_ref, o_ref):
    a = _silu(c_ref[...]).astype(BF16)
    w = w_ref[0].astype(BF16)
    o_ref[0] = jnp.dot(a, w, preferred_element_type=F32) + b_ref[0]


def _modulation(c_all, w_cond, b_cond):
    depth, d, n = w_cond.shape
    rows = c_all.shape[0]
    tn = 1536
    return pl.pallas_call(
        _modulation_kernel,
        grid=(depth, n // tn),
        in_specs=[
            pl.BlockSpec((rows, d), lambda i, j: (0, 0)),
            pl.BlockSpec((1, d, tn), lambda i, j: (i, 0, j)),
            pl.BlockSpec((1, 1, tn), lambda i, j: (i, 0, j)),
        ],
        out_specs=pl.BlockSpec((1, rows, tn), lambda i, j: (i, 0, j)),
        out_shape=jax.ShapeDtypeStruct((depth, rows, n), F32),
        compiler_params=_params(2),
        name="modulation",
    )(c_all, w_cond, b_cond.reshape(depth, 1, n))


def _modmm_kernel(x_ref, sc_ref, sh_ref, w_ref, o_ref):
    bb, tt, d = x_ref.shape
    h = x_ref[...] * (1.0 + sc_ref[...]) + sh_ref[...]
    h = h.reshape(bb * tt, d).astype(BF16)
    acc = jnp.dot(h, w_ref[...], preferred_element_type=F32)
    o_ref[...] = acc.reshape(bb, tt, acc.shape[-1]).astype(o_ref.dtype)


def _modmm(x, mod, w, out_dtype, t_start=0):
    batch, seq, d = x.shape
    n = w.shape[1]
    seq_out = seq - t_start
    bb, tt = _row_blocks(batch, seq_out, ROW_TILE)
    assert t_start % tt == 0
    off = t_start // tt
    return pl.pallas_call(
        _modmm_kernel,
        grid=(batch // bb, seq_out // tt),
        in_specs=[
            pl.BlockSpec((bb, tt, d), lambda b, t: (b, t + off, 0)),
            _mod_spec(bb, 1, 2),
            _mod_spec(bb, 0, 2),
            pl.BlockSpec((d, n), lambda b, t: (0, 0)),
        ],
        out_specs=pl.BlockSpec((bb, tt, n), lambda b, t: (b, t, 0)),
        out_shape=jax.ShapeDtypeStruct((batch, seq_out, n), out_dtype),
        compiler_params=_params(2),
        name="a_project",
    )(x, mod, mod, w)


def _band_attn_kernel(*refs, n_kb, n_maybe_invalid):
    q_ref = refs[0]
    k_refs = refs[1:1 + n_kb]
    v_refs = refs[1 + n_kb:1 + 2 * n_kb]
    f_ref = refs[1 + 2 * n_kb]
    o_ref = refs[2 + 2 * n_kb]
    bias_ref = refs[3 + 2 * n_kb]
    q_tile = q_ref.shape[1]
    kb_sizes = [r.shape[1] for r in k_refs]
    k_tot = sum(kb_sizes)
    width = f_ref.shape[-1]
    i = pl.program_id(1)

    @pl.when((pl.program_id(0) == 0) & (i == 0))
    def _():
        rq = lax.broadcasted_iota(jnp.int32, (q_tile, k_tot), 0) // CHUNK
        ck = lax.broadcasted_iota(jnp.int32, (q_tile, k_tot), 1) // CHUNK
        in_band = (ck >= rq) & (ck <= rq + N_PAST_CHUNKS)
        for h in range(A_HEADS):
            rows = jnp.broadcast_to(f_ref[h], (q_tile, width))
            toep = pltpu.roll(rows, k_tot + 1, 1, stride=1, stride_axis=0)
            bias_ref[h] = jnp.where(in_band, toep[:, :k_tot], -jnp.inf)

    q = q_ref[0] * (HEAD_DIM ** -0.5)
    for h in range(A_HEADS):
        cols = slice(h * HEAD_DIM, (h + 1) * HEAD_DIM)
        qh = q[:, cols]
        parts = []
        for kb in range(n_kb):
            lg = lax.dot_general(qh, k_refs[kb][0, :, cols], (((1,), (1,)), ((), ())),
                                 preferred_element_type=F32)
            if kb < n_maybe_invalid:
                lg = jnp.where(i - n_maybe_invalid + kb >= 0, lg, -jnp.inf)
            parts.append(lg)
        logits = jnp.concatenate(parts, axis=1) + bias_ref[h]
        m = jnp.max(logits, axis=1, keepdims=True)
        p = jnp.exp(logits - m)
        l = jnp.sum(p, axis=1, keepdims=True)
        pb = p.astype(BF16)
        acc = jnp.zeros((q_tile, HEAD_DIM), F32)
        start = 0
        for kb in range(n_kb):
            acc = acc + jnp.dot(pb[:, start:start + kb_sizes[kb]], v_refs[kb][0, :, cols],
                                preferred_element_type=F32)
            start += kb_sizes[kb]
        o_ref[0, :, cols] = (acc / l).astype(o_ref.dtype)


def _bias_vector(table, q_tile, k_tot):
    width = k_tot + q_tile
    lo = -(q_tile - 1) - BAND_ROWS + REL_CLIP
    left = max(0, -lo)
    start = max(0, lo)
    n_mid = min(2 * REL_CLIP + 1 - start, width - left)
    right = width - left - n_mid
    mid = table.T[:, start:start + n_mid]
    f = jnp.pad(mid, ((0, 0), (left, right)), mode="edge")
    return f.reshape(A_HEADS, 1, width)


def _band_attn_prompt(qkv, table):
    batch, seq, _ = qkv.shape
    hd = A_HEADS * HEAD_DIM
    qt = A_Q_TILE
    n_prev = BAND_ROWS // qt
    n_kb = n_prev + 1
    k_tot = n_kb * qt
    f = _bias_vector(table, qt, k_tot)

    def kv_spec(kb, col):
        return pl.BlockSpec((1, qt, hd), lambda b, i: (b, jnp.maximum(i - n_prev + kb, 0), col))

    return pl.pallas_call(
        functools.partial(_band_attn_kernel, n_kb=n_kb, n_maybe_invalid=n_prev),
        grid=(batch, seq // qt),
        in_specs=[pl.BlockSpec((1, qt, hd), lambda b, i: (b, i, 0))]
        + [kv_spec(kb, 1) for kb in range(n_kb)]
        + [kv_spec(kb, 2) for kb in range(n_kb)]
        + [pl.BlockSpec(f.shape, lambda b, i: (0, 0, 0))],
        out_specs=pl.BlockSpec((1, qt, hd), lambda b, i: (b, i, 0)),
        out_shape=jax.ShapeDtypeStruct((batch, seq, hd), BF16),
        scratch_shapes=[pltpu.VMEM((A_HEADS, qt, k_tot), F32)],
        compiler_params=_params(2),
        name="band_attn_prompt",
    )(*([qkv] * (1 + 2 * n_kb)), f)


def _band_attn_sample(qkv, cache_k, cache_v, table):
    batch, seq, _ = qkv.shape
    hd = A_HEADS * HEAD_DIM
    win = cache_k.shape[1]
    assert seq == CHUNK and win == BAND_ROWS
    k_tot = win + seq
    f = _bias_vector(table, seq, k_tot)
    new = lambda col: pl.BlockSpec((1, seq, hd), lambda b, i: (b, 0, col))
    old = pl.BlockSpec((1, win, hd), lambda b, i: (b, 0, 0))
    return pl.pallas_call(
        functools.partial(_band_attn_kernel, n_kb=2, n_maybe_invalid=0),
        grid=(batch, 1),
        in_specs=[new(0), old, new(1), old, new(2), pl.BlockSpec(f.shape, lambda b, i: (0, 0, 0))],
        out_specs=pl.BlockSpec((1, seq, hd), lambda b, i: (b, 0, 0)),
        out_shape=jax.ShapeDtypeStruct((batch, seq, hd), BF16),
        scratch_shapes=[pltpu.VMEM((A_HEADS, seq, k_tot), F32)],
        compiler_params=_params(2),
        name="band_attn_sample",
    )(qkv, cache_k, qkv, cache_v, qkv, f)


def _mm_postnorm_kernel(o_ref, w_ref, x_ref, gate_ref, g_ref, b_ref, y_ref, *, alpha):
    bb, tt, d = x_ref.shape
    o = o_ref[...].reshape(bb * tt, o_ref.shape[-1])
    sub = jnp.dot(o, w_ref[...], preferred_element_type=F32).reshape(bb, tt, d)
    z = alpha * x_ref[...] + (1.0 + gate_ref[...]) * sub
    y_ref[...] = _layer_norm_rows(z, g_ref[...], b_ref[...])


def _mm_postnorm(o, w, x, mod, gate_chunk, g, b, alpha):
    batch, seq, d = x.shape
    bb, tt = _row_blocks(batch, seq, ROW_TILE)
    k = o.shape[-1]
    return pl.pallas_call(
        functools.partial(_mm_postnorm_kernel, alpha=alpha),
        grid=(batch // bb, seq // tt),
        in_specs=[
            pl.BlockSpec((bb, tt, k), lambda b_, t: (b_, t, 0)),
            pl.BlockSpec((k, d), lambda b_, t: (0, 0)),
            pl.BlockSpec((bb, tt, d), lambda b_, t: (b_, t, 0)),
            _mod_spec(bb, gate_chunk, 2),
            pl.BlockSpec((1, 1, d), lambda b_, t: (0, 0, 0)),
            pl.BlockSpec((1, 1, d), lambda b_, t: (0, 0, 0)),
        ],
        out_specs=pl.BlockSpec((bb, tt, d), lambda b_, t: (b_, t, 0)),
        out_shape=jax.ShapeDtypeStruct((batch, seq, d), F32),
        compiler_params=_params(2),
        name="out_proj_postnorm",
    )(o, w, x, mod, g.reshape(1, 1, d), b.reshape(1, 1, d))


def _ffn_kernel(x_ref, sc_ref, sh_ref, gate_ref, wr_ref, br_ref, wg_ref, wu_ref, wd_ref,
                g_ref, b_ref, y_ref, h_ref, comb_ref, acc_ref, *, routed, alpha):
    bb, tt, d = x_ref.shape
    rows = bb * tt
    e = pl.program_id(2)
    c = pl.program_id(3)
    lane = lax.broadcasted_iota(jnp.int32, (rows, LANES), 1)

    @pl.when((e == 0) & (c == 0))
    def _():
        h = (x_ref[...] * (1.0 + sc_ref[...]) + sh_ref[...]).reshape(rows, d)
        h_ref[...] = h.astype(BF16)
        acc_ref[...] = jnp.zeros_like(acc_ref)
        if routed:
            logits = jnp.dot(h, wr_ref[...], preferred_element_type=F32,
                             precision=lax.Precision.HIGHEST) + br_ref[...]
            logits = jnp.where(lane < N_EXPERTS, logits, -jnp.inf)
            m1 = jnp.max(logits, axis=1, keepdims=True)
            i1 = jnp.min(jnp.where(logits == m1, lane, LANES), axis=1, keepdims=True)
            rest = jnp.where(lane == i1, -jnp.inf, logits)
            m2 = jnp.max(rest, axis=1, keepdims=True)
            i2 = jnp.min(jnp.where(rest == m2, lane, LANES), axis=1, keepdims=True)
            e2 = jnp.exp(m2 - m1)
            den = 1.0 + e2
            comb_ref[...] = jnp.where(lane == i1, 1.0 / den, 0.0) + jnp.where(lane == i2, e2 / den, 0.0)

    h = h_ref[...]
    gt = jnp.dot(h, wg_ref[0], preferred_element_type=F32)
    up = jnp.dot(h, wu_ref[0], preferred_element_type=F32)
    act = (_silu(gt) * up).astype(BF16)
    part = jnp.dot(act, wd_ref[0], preferred_element_type=F32)
    if routed:
        ce = jnp.sum(jnp.where(lane == e, comb_ref[...], 0.0), axis=1, keepdims=True)
        part = ce * part
    acc_ref[...] += part

    @pl.when((e == pl.num_programs(2) - 1) & (c == pl.num_programs(3) - 1))
    def _():
        z = alpha * x_ref[...] + (1.0 + gate_ref[...]) * acc_ref[...].reshape(bb, tt, d)
        y_ref[...] = _layer_norm_rows(z, g_ref[...], b_ref[...])


def _ffn(x, mod, w_r, b_r, w_gu, w_down, g, b, *, routed, ff_chunk, alpha):
    batch, seq, d = x.shape
    n_e, ff, _ = w_down.shape
    assert ff % ff_chunk == 0
    n_c = ff // ff_chunk
    bb, tt = _row_blocks(batch, seq, FFN_ROW_TILE)
    rows = bb * tt
    vec = pl.BlockSpec((1, 1, d), lambda b_, t, e, c: (0, 0, 0))
    return pl.pallas_call(
        functools.partial(_ffn_kernel, routed=routed, alpha=alpha),
        grid=(batch // bb, seq // tt, n_e, n_c),
        in_specs=[
            pl.BlockSpec((bb, tt, d), lambda b_, t, e, c: (b_, t, 0)),
            _mod_spec(bb, 4, 4),
            _mod_spec(bb, 3, 4),
            _mod_spec(bb, 5, 4),
            pl.BlockSpec((d, LANES), lambda b_, t, e, c: (0, 0)),
            pl.BlockSpec((1, LANES), lambda b_, t, e, c: (0, 0)),
            pl.BlockSpec((1, d, ff_chunk), lambda b_, t, e, c: (e, 0, c)),
            pl.BlockSpec((1, d, ff_chunk), lambda b_, t, e, c: (e, 0, n_c + c)),
            pl.BlockSpec((1, ff_chunk, d), lambda b_, t, e, c: (e, c, 0)),
            vec,
            vec,
        ],
        out_specs=pl.BlockSpec((bb, tt, d), lambda b_, t, e, c: (b_, t, 0)),
        out_shape=jax.ShapeDtypeStruct((batch, seq, d), F32),
        scratch_shapes=[
            pltpu.VMEM((rows, d), BF16),
            pltpu.VMEM((rows, LANES), F32),
            pltpu.VMEM((rows, d), F32),
        ],
        compiler_params=_params(4),
        name="moe_ffn" if routed else "dense_ffn",
    )(x, mod, mod, mod, w_r, b_r, w_gu, w_gu, w_down, g.reshape(1, 1, d), b.reshape(1, 1, d))


def _moe_kernel(x_ref, sc_ref, sh_ref, gate_ref, wr_ref, br_ref, wg_ref, wu_ref, wd_ref, g_ref, b_ref,
                y_ref, h_ref, comb_ref, slot_ref, cnt_ref, tri_ref, hc_ref, acc_ref, out_ref, *, alpha):
    bb, tt, d = x_ref.shape
    rows = bb * tt
    cap = hc_ref.shape[1]
    e = pl.program_id(2)
    c = pl.program_id(3)
    last_c = pl.num_programs(3) - 1
    lane = lax.broadcasted_iota(jnp.int32, (rows, LANES), 1)

    @pl.when((pl.program_id(0) == 0) & (pl.program_id(1) == 0) & (e == 0) & (c == 0))
    def _():
        r = lax.broadcasted_iota(jnp.int32, (rows, rows), 0)
        col = lax.broadcasted_iota(jnp.int32, (rows, rows), 1)
        tri_ref[...] = jnp.where(r <= col, 1.0, 0.0).astype(BF16)

    @pl.when((e == 0) & (c == 0))
    def _():
        h = (x_ref[...] * (1.0 + sc_ref[...]) + sh_ref[...]).reshape(rows, d)
        h_ref[...] = h.astype(BF16)
        out_ref[...] = jnp.zeros_like(out_ref)
        logits = jnp.dot(h, wr_ref[...], preferred_element_type=F32,
                         precision=lax.Precision.HIGHEST) + br_ref[...]
        logits = jnp.where(lane < N_EXPERTS, logits, -jnp.inf)
        m1 = jnp.max(logits, axis=1, keepdims=True)
        i1 = jnp.min(jnp.where(logits == m1, lane, LANES), axis=1, keepdims=True)
        rest = jnp.where(lane == i1, -jnp.inf, logits)
        m2 = jnp.max(rest, axis=1, keepdims=True)
        i2 = jnp.min(jnp.where(rest == m2, lane, LANES), axis=1, keepdims=True)
        e2 = jnp.exp(m2 - m1)
        den = 1.0 + e2
        comb_ref[...] = jnp.where(lane == i1, 1.0 / den, 0.0) + jnp.where(lane == i2, e2 / den, 0.0)
        member = jnp.where((lane == i1) | (lane == i2), 1.0, 0.0)
        member_t = member.T[:2 * N_EXPERTS]
        upto = jnp.dot(member_t.astype(BF16), tri_ref[...], preferred_element_type=F32)
        slot_ref[...] = jnp.where(member_t > 0, upto - 1.0, -1.0)
        cnt_ref[...] = jnp.broadcast_to(upto[:, rows - 1:rows], cnt_ref.shape)

    n_sub = (jnp.max(cnt_ref[pl.ds(e, 1), :]).astype(jnp.int32) + cap - 1) // cap

    def selection(s):
        slot = slot_ref[pl.ds(e, 1), :] - (s * cap).astype(F32)
        r = lax.broadcasted_iota(jnp.int32, (cap, rows), 0).astype(F32)
        return jnp.where(slot == r, 1.0, 0.0).astype(BF16)

    @pl.when(c == 0)
    def _():
        def pack(s, carry):
            hc_ref[s] = jnp.dot(selection(s), h_ref[...], preferred_element_type=F32).astype(BF16)
            acc_ref[s] = jnp.zeros(acc_ref.shape[1:], F32)
            return carry
        lax.fori_loop(0, n_sub, pack, 0)

    def expert(s, carry):
        hc = hc_ref[s]
        gt = jnp.dot(hc, wg_ref[0, 0], preferred_element_type=F32)
        up = jnp.dot(hc, wu_ref[0, 0], preferred_element_type=F32)
        act = (_silu(gt) * up).astype(BF16)
        acc_ref[s] += jnp.dot(act, wd_ref[0], preferred_element_type=F32)
        return carry

    lax.fori_loop(0, n_sub, expert, 0)

    @pl.when(c == last_c)
    def _():
        ce = jnp.sum(jnp.where(lane == e, comb_ref[...], 0.0), axis=1, keepdims=True)

        def unpack(s, carry):
            back = lax.dot_general(selection(s), acc_ref[s].astype(BF16), (((0,), (0,)), ((), ())),
                                   preferred_element_type=F32)
            out_ref[...] += ce * back
            return carry
        lax.fori_loop(0, n_sub, unpack, 0)

    @pl.when((e == pl.num_programs(2) - 1) & (c == last_c))
    def _():
        z = alpha * x_ref[...] + (1.0 + gate_ref[...]) * out_ref[...].reshape(bb, tt, d)
        y_ref[...] = _layer_norm_rows(z, g_ref[...], b_ref[...])


def _moe(x, mod, w_r, b_r, w_gu, w_down, g, b, *, ff_chunk, alpha):
    batch, seq, d = x.shape
    n_e, ff, _ = w_down.shape
    assert ff % ff_chunk == 0
    n_c = ff // ff_chunk
    bb, tt = _row_blocks(batch, seq, FFN_ROW_TILE)
    rows = bb * tt
    cap = MOE_CAPACITY
    max_sub = -(-rows // cap)
    vec = pl.BlockSpec((1, 1, d), lambda b_, t, e, c: (0, 0, 0))
    return pl.pallas_call(
        functools.partial(_moe_kernel, alpha=alpha),
        grid=(batch // bb, seq // tt, n_e, n_c),
        in_specs=[
            pl.BlockSpec((bb, tt, d), lambda b_, t, e, c: (b_, t, 0), pipeline_mode=pl.Buffered(1)),
            _mod_spec(bb, 4, 4),
            _mod_spec(bb, 3, 4),
            _mod_spec(bb, 5, 4),
            pl.BlockSpec((d, LANES), lambda b_, t, e, c: (0, 0)),
            pl.BlockSpec((1, LANES), lambda b_, t, e, c: (0, 0)),
            pl.BlockSpec((1, 1, d, ff_chunk), lambda b_, t, e, c: (e, c, 0, 0)),
            pl.BlockSpec((1, 1, d, ff_chunk), lambda b_, t, e, c: (e, n_c + c, 0, 0)),
            pl.BlockSpec((1, ff_chunk, d), lambda b_, t, e, c: (e, c, 0)),
            vec,
            vec,
        ],
        out_specs=pl.BlockSpec((bb, tt, d), lambda b_, t, e, c: (b_, t, 0)),
        out_shape=jax.ShapeDtypeStruct((batch, seq, d), F32),
        scratch_shapes=[
            pltpu.VMEM((rows, d), BF16),
            pltpu.VMEM((rows, LANES), F32),
            pltpu.VMEM((2 * N_EXPERTS, rows), F32),
            pltpu.VMEM((2 * N_EXPERTS, LANES), F32),
            pltpu.VMEM((rows, rows), BF16),
            pltpu.VMEM((max_sub, cap, d), BF16),
            pltpu.VMEM((max_sub, cap, d), F32),
            pltpu.VMEM((rows, d), F32),
        ],
        compiler_params=_params(4),
        name="moe_ffn",
    )(x, mod, mod, mod, w_r, b_r, w_gu, w_gu, w_down, g.reshape(1, 1, d), b.reshape(1, 1, d))


def _rope_lanes(seg, cos, sin_signed, first_half):
    width = seg.shape[-1]
    reps = width // LANES
    if reps > 1:
        cos = jnp.concatenate([cos] * reps, axis=1)
        sin_signed = jnp.concatenate([sin_signed] * reps, axis=1)
        first_half = jnp.concatenate([first_half] * reps, axis=1)
    half = HEAD_DIM // 2
    swapped = jnp.where(first_half, pltpu.roll(seg, width - half, 1), pltpu.roll(seg, half, 1))
    return seg * cos + swapped * sin_signed


def _b_project_kernel(x_ref, sc_ref, sh_ref, w_ref, cos_ref, sin_ref, kg_ref, kb_ref,
                      q_ref, qi_ref, k_ref, v_ref, kw_ref):
    bb, tt, d = x_ref.shape
    rows = bb * tt
    h = x_ref[...] * (1.0 + sc_ref[...]) + sh_ref[...]
    h = h.reshape(rows, d).astype(BF16)
    acc = jnp.dot(h, w_ref[...], preferred_element_type=F32)
    cos = cos_ref[...]
    sin_signed = sin_ref[...]
    lane = lax.broadcasted_iota(jnp.int32, (1, LANES), 1)
    first_half = (lane % HEAD_DIM) < (HEAD_DIM // 2)

    def out(ref, val):
        ref[...] = val.reshape(bb, tt, val.shape[-1]).astype(ref.dtype)

    o_k, o_v, o_qi, o_ki = B_Q, B_Q + B_KV, B_Q + 2 * B_KV, B_Q + 2 * B_KV + B_QI
    out(q_ref, _rope_lanes(acc[:, :o_k], cos, sin_signed, first_half) * QK_SCALE_LOG2)
    out(k_ref, _rope_lanes(acc[:, o_k:o_v], cos, sin_signed, first_half))
    out(v_ref, acc[:, o_v:o_qi])
    out(qi_ref, _rope_lanes(acc[:, o_qi:o_ki], cos, sin_signed, first_half))
    seg = acc[:, o_ki:]
    is_ki = lane < IDX_DIM
    mu = jnp.sum(jnp.where(is_ki, seg, 0.0), axis=1, keepdims=True) / IDX_DIM
    cen = jnp.where(is_ki, seg - mu, 0.0)
    var = jnp.sum(cen * cen, axis=1, keepdims=True) / IDX_DIM
    ki = cen * lax.rsqrt(var + LN_EPS) * kg_ref[...] + kb_ref[...]
    ki = _rope_lanes(ki, cos, sin_signed, first_half)
    out(kw_ref, jnp.where(is_ki, ki, seg * (IDX_HEADS ** -0.5)))


def _b_project(x, mod, w, cos, sin_signed, kn_g, kn_b, table_per_tile):
    batch, seq, d = x.shape
    bb, tt = _row_blocks(batch, seq, ROW_TILE)
    rows = bb * tt
    n = w.shape[1]
    tab = pl.BlockSpec((rows, LANES), (lambda b, t: (0, 0)) if table_per_tile else (lambda b, t: (t, 0)))
    vec = pl.BlockSpec((1, LANES), lambda b, t: (0, 0))

    def o_spec(width):
        return pl.BlockSpec((bb, tt, width), lambda b, t: (b, t, 0))

    def o_shape(width, dtype):
        return jax.ShapeDtypeStruct((batch, seq, width), dtype)

    return pl.pallas_call(
        _b_project_kernel,
        grid=(batch // bb, seq // tt),
        in_specs=[
            pl.BlockSpec((bb, tt, d), lambda b, t: (b, t, 0)),
            _mod_spec(bb, 1, 2),
            _mod_spec(bb, 0, 2),
            pl.BlockSpec((d, n), lambda b, t: (0, 0)),
            tab,
            tab,
            vec,
            vec,
        ],
        out_specs=[o_spec(B_Q), o_spec(B_QI), o_spec(B_KV), o_spec(B_KV), o_spec(LANES)],
        out_shape=[o_shape(B_Q, BF16), o_shape(B_QI, BF16), o_shape(B_KV, F32), o_shape(B_KV, F32),
                   o_shape(LANES, F32)],
        compiler_params=_params(2),
        name="b_project",
    )(x, mod, mod, w, cos, sin_signed, kn_g, kn_b)


def _rope_tables(pos):
    half = HEAD_DIM // 2
    inv = ROPE_THETA ** (-jnp.arange(half, dtype=F32) / half)
    ang = pos.astype(F32)[:, None] * inv[None, :]
    cos, sin = jnp.cos(ang), jnp.sin(ang)
    return (jnp.concatenate([cos, cos, cos, cos], axis=1),
            jnp.concatenate([-sin, sin, -sin, sin], axis=1))


def _key_norm_kernel(k_ref, grp_ref, o_ref):
    k = k_ref[0].astype(F32)
    n2 = jnp.dot((k * k).astype(BF16), grp_ref[...], preferred_element_type=F32)
    mx = jnp.broadcast_to(jnp.max(n2, axis=0, keepdims=True), o_ref.shape[1:])

    @pl.when(pl.program_id(1) == 0)
    def _():
        o_ref[0] = mx

    @pl.when(pl.program_id(1) > 0)
    def _():
        o_ref[0] = jnp.maximum(o_ref[0], mx)


def _key_norm_max(k):
    batch, lp, width = k.shape
    tile = DSA_KEY_TILE
    grp = (jnp.arange(width)[:, None] // HEAD_DIM == jnp.arange(LANES)[None, :]).astype(BF16)
    return pl.pallas_call(
        _key_norm_kernel,
        grid=(batch, lp // tile),
        in_specs=[pl.BlockSpec((1, tile, width), lambda b, t: (b, t, 0)),
                  pl.BlockSpec((width, LANES), lambda b, t: (0, 0))],
        out_specs=pl.BlockSpec((1, 8, LANES), lambda b, t: (b, 0, 0)),
        out_shape=jax.ShapeDtypeStruct((batch, 8, LANES), F32),
        compiler_params=_params(2),
        name="key_norm_max",
    )(k, grp)


def _dsa_kernel(qt_ref, qit_ref, kwt_ref, ki_ref, k_ref, vta_ref, kmax_ref, o_ref,
                s_ref, s16_ref, qpad_ref, bound_ref, m_ref, acc_ref, *, causal, n_keys, k_sel):
    qn = qt_ref.shape[2]
    lt = DSA_KEY_TILE
    i = pl.program_id(1)
    if causal:
        n_tiles = ((i + 1) * qn + lt - 1) // lt
        qpos = i * qn + lax.broadcasted_iota(jnp.int32, (1, qn), 1)
        limit = (qpos // CHUNK + 1) * CHUNK
    else:
        n_tiles = ki_ref.shape[1] // lt
        limit = jnp.full((1, qn), n_keys, jnp.int32)

    def key_slice(t):
        return pl.ds(pl.multiple_of(t * lt, lt), lt)

    def key_pos(t):
        return t * lt + lax.broadcasted_iota(jnp.int32, (lt, 1), 0)

    def fold_rows(x, op):
        return op(op(x.reshape(lt // 32, 4, 8, x.shape[-1]), axis=0), axis=0)

    def paired_tile_loop(body, init):
        def pair(t2, carry):
            return body(2 * t2 + 1, body(2 * t2, carry))
        carry = lax.fori_loop(0, n_tiles // 2, pair, init)
        return lax.cond(n_tiles % 2 == 1, lambda c: body(n_tiles - 1, c), lambda c: c, carry)

    qit = qit_ref[0]
    kwt = kwt_ref[0]
    qi_w = jnp.concatenate([qit[h * IDX_DIM:(h + 1) * IDX_DIM, :] for h in range(IDX_HEADS)], axis=1)
    w_row = jnp.concatenate([kwt[IDX_DIM + h:IDX_DIM + h + 1, :] for h in range(IDX_HEADS)], axis=1)
    w_row = w_row * (IDX_DIM ** -0.5)

    def score_body(t, carry):
        rmax, rmin = carry
        s = jnp.dot(ki_ref[0, key_slice(t), :], qi_w, preferred_element_type=F32)
        s = jnp.maximum(s, 0.0) * w_row
        sc = s[:, 0:qn]
        for h in range(1, IDX_HEADS):
            sc = sc + s[:, h * qn:(h + 1) * qn]
        adm = key_pos(t) < limit
        masked = jnp.where(adm, sc, -jnp.inf)
        s_ref[key_slice(t), :] = masked
        bits = lax.bitcast_convert_type(masked, jnp.int32)
        bits = (bits + ((bits >> 31) & 0xFFFF)) & -65536
        s16_ref[key_slice(t), :] = lax.bitcast_convert_type(bits, F32).astype(BF16)
        rmax = jnp.maximum(rmax, jnp.max(masked, axis=0, keepdims=True))
        rmin = jnp.minimum(rmin, jnp.min(jnp.where(adm, sc, jnp.inf), axis=0, keepdims=True))
        return rmax, rmin

    rmax, rmin = paired_tile_loop(score_body,
                                  (jnp.full((1, qn), -jnp.inf, F32), jnp.full((1, qn), jnp.inf, F32)))

    def count(pred):
        def body(t, acc):
            return acc + fold_rows(jnp.where(pred(s_ref[key_slice(t), :], key_pos(t)), 1.0, 0.0), jnp.sum)
        acc = lax.fori_loop(0, n_tiles, body, jnp.zeros((8, qn), F32))
        return jnp.sum(acc, axis=0, keepdims=True)

    def to_key(x):
        bits = lax.bitcast_convert_type(x, jnp.int32)
        return bits ^ ((bits >> 31) & 0x7FFFFFFF)

    def from_key(key):
        return lax.bitcast_convert_type(key ^ ((key >> 31) & 0x7FFFFFFF), F32)

    n_adm = limit.astype(F32)
    target = jnp.minimum(n_adm, float(k_sel))
    log_target = jnp.log(target)

    def any_left(done):
        return jnp.sum(done) < qn

    c_ge0 = count(lambda s, _: s >= 0.0)
    c_gt0 = count(lambda s, _: s > 0.0)
    is_pos = c_gt0 >= target
    is_neg = c_ge0 < target
    zero_key = jnp.zeros((1, qn), jnp.int32)
    lo0 = jnp.where(is_neg, to_key(rmin), zero_key)
    c_lo0 = jnp.where(is_neg, n_adm, c_ge0)
    c_hi0 = jnp.where(is_pos, 0.0, c_ge0)
    done0 = jnp.where((n_adm == target) | jnp.logical_not(is_pos | is_neg) | (c_lo0 == target), 1.0, 0.0)
    lo0 = jnp.where(n_adm == target, to_key(rmin), lo0)
    c_lo0 = jnp.where(n_adm == target, n_adm, c_lo0)

    ones = jnp.ones((1, qn), F32)

    def search(st0, key_to_value, value_to_key, count_ge):
        def cond(st):
            return any_left(st[7])

        def body(st):
            it, lo, hi, c_lo, c_hi, w_lo, w_hi, done, side = st
            t_lo, t_hi = key_to_value(lo), key_to_value(hi)
            f_lo = (jnp.log(c_lo) - log_target) * w_lo
            f_hi = (log_target - jnp.log(jnp.maximum(c_hi, 0.5))) * w_hi
            guess = value_to_key(t_lo + f_lo / (f_lo + f_hi) * (t_hi - t_lo))
            halve = (lo >> 1) + (hi >> 1) + (lo & hi & 1)
            mid = jnp.where(it >= SECANT_STEPS, halve, jnp.clip(guess, lo + 1, hi - 1))
            stuck = halve == lo
            c = count_ge(key_to_value(mid))
            live = (done < 0.5) & jnp.logical_not(stuck)
            up = live & (c >= target)
            dn = live & (c < target)
            w_hi = jnp.where(up, jnp.where(side > 0, 0.5 * w_hi, 1.0), jnp.where(dn, 1.0, w_hi))
            w_lo = jnp.where(dn, jnp.where(side < 0, 0.5 * w_lo, 1.0), jnp.where(up, 1.0, w_lo))
            side = jnp.where(up, 1.0, jnp.where(dn, -1.0, side))
            lo = jnp.where(up, mid, lo)
            c_lo = jnp.where(up, c, c_lo)
            hi = jnp.where(dn, mid, hi)
            c_hi = jnp.where(dn, c, c_hi)
            done = jnp.where(stuck | (c_lo == target), 1.0, done)
            return it + 1, lo, hi, c_lo, c_hi, w_lo, w_hi, done, side

        lo, hi, c_lo, c_hi, done = st0
        st = lax.while_loop(cond, body, (jnp.int32(0), lo, hi, c_lo, c_hi, ones, ones, done, 0.0 * ones))
        return st[1], st[2], st[3], st[4]

    def to_key16(x):
        key = to_key(x)
        return (key + ((key >> 31) & 1)) >> 16

    def grid_value(k16):
        return from_key((k16 << 16) - ((k16 >> 31) & 1))

    def count16_ge(value):
        thr16 = value.astype(BF16)
        one, zero = jnp.ones((), BF16), jnp.zeros((), BF16)

        def body(t, acc):
            hit = jnp.where(s16_ref[key_slice(t), :] >= thr16, one, zero)
            parts = [hit[j * 16:(j + 1) * 16] for j in range(lt // 16)]
            while len(parts) > 1:
                parts = [parts[j] + parts[j + 1] for j in range(0, len(parts), 2)]
            return acc + parts[0].astype(F32)

        acc = lax.fori_loop(0, n_tiles, body, jnp.zeros((16, qn), F32))
        return jnp.sum(acc, axis=0, keepdims=True)

    hi16 = jnp.where(is_pos, to_key16(rmax) + 1, zero_key)
    lo16, hi16, c_lo1, c_hi1 = search((to_key16(from_key(lo0)), hi16, c_lo0, c_hi0, done0),
                                      grid_value, to_key16, count16_ge)
    done1 = jnp.where((done0 > 0.5) | (c_lo1 == target), 1.0, 0.0)
    k_thr, _, c_thr, _ = search((to_key(grid_value(lo16)), to_key(grid_value(hi16)), c_lo1, c_hi1, done1),
                                from_key, to_key, lambda v: count(lambda s, _: s >= v))
    thr = from_key(k_thr)

    tied = c_thr > target

    @pl.when(jnp.sum(jnp.where(tied, 1.0, 0.0)) > 0)
    def _():
        need = target - count(lambda s, _: s > thr)

        def idx_cond(st):
            return any_left(st[5])

        def idx_body(st):
            it, lo_j, hi_j, c_lo_j, c_hi_j, done = st
            span = (hi_j - lo_j).astype(F32)
            step = jnp.ceil((need - c_lo_j) / (c_hi_j - c_lo_j) * span).astype(jnp.int32)
            step = jnp.where(it % 2 == 0, step, (hi_j - lo_j) // 2)
            mid = lo_j + jnp.clip(step, 1, jnp.maximum(hi_j - lo_j - 1, 1))
            c = count(lambda s, kp: (s == thr) & (kp <= mid))
            live = (done < 0.5) & (hi_j - lo_j > 1)
            below = live & (c < need)
            above = live & (c >= need)
            lo_j = jnp.where(below, mid, lo_j)
            c_lo_j = jnp.where(below, c, c_lo_j)
            hi_j = jnp.where(above, mid, hi_j)
            c_hi_j = jnp.where(above, c, c_hi_j)
            done = jnp.where((c_hi_j == need) | (hi_j - lo_j <= 1), 1.0, done)
            return it + 1, lo_j, hi_j, c_lo_j, c_hi_j, done

        first = jnp.full((1, qn), -1, jnp.int32)
        final = jnp.full((1, qn), n_tiles * lt - 1, jnp.int32)
        n_ties = c_thr - (target - need)
        idx_done0 = jnp.where(tied & (n_ties > need), 0.0, 1.0)
        idx = lax.while_loop(idx_cond, idx_body,
                             (jnp.int32(0), first, final, 0.0 * ones, n_ties, idx_done0))
        last = idx[2]

        def drop_body(t, carry):
            s = s_ref[key_slice(t), :]
            s_ref[key_slice(t), :] = jnp.where(tied & (s == thr) & (key_pos(t) > last), -jnp.inf, s)
            return carry

        lax.fori_loop(0, n_tiles, drop_body, 0)

    qt = qt_ref[0]
    zeros = jnp.zeros((HEAD_DIM, B_GROUP * qn), BF16)
    for g in range(B_KV_HEADS):
        q_g = jnp.concatenate(
            [qt[(g * B_GROUP + j) * HEAD_DIM:(g * B_GROUP + j + 1) * HEAD_DIM, :] for j in range(B_GROUP)],
            axis=1)
        qpad_ref[g] = jnp.concatenate([zeros] * g + [q_g] + [zeros] * (B_KV_HEADS - 1 - g), axis=0)

    qf = qt.astype(F32)
    q_norm2 = jnp.sum((qf * qf).reshape(B_HEADS, HEAD_DIM, qn), axis=1)
    for g in range(B_KV_HEADS):
        b = jnp.sqrt(q_norm2[g * B_GROUP:(g + 1) * B_GROUP] * kmax_ref[0, 0:1, g:g + 1])
        b = b * BOUND_SLACK + BOUND_SLACK_ABS
        bound_ref[g] = jnp.concatenate([b[j:j + 1] for j in range(B_GROUP)], axis=1)
    acc_ref[...] = jnp.zeros(acc_ref.shape, F32)

    def fast_body(t, carry):
        keys = k_ref[0, key_slice(t), :]
        sel = jnp.where(s_ref[key_slice(t), :] >= thr, 1.0, 0.0).astype(BF16)
        sel = jnp.concatenate([sel] * B_GROUP, axis=1)
        def logits(g):
            return jnp.dot(keys, qpad_ref[g], preferred_element_type=F32)

        lg = logits(0)
        for g in range(B_KV_HEADS):
            lg_next = logits(g + 1) if g + 1 < B_KV_HEADS else None
            p = jnp.exp2(lg - bound_ref[g]).astype(BF16) * sel
            acc_ref[g] += jnp.dot(vta_ref[0, g, :, key_slice(t)], p, preferred_element_type=F32)
            lg = lg_next
        return carry

    paired_tile_loop(fast_body, 0)
    den_min = jnp.min(acc_ref[:, HEAD_DIM:HEAD_DIM + 1, :])

    @pl.when(jnp.logical_not(den_min >= MIN_TRUSTED_DENOMINATOR))
    def _():
        m_ref[...] = jnp.full(m_ref.shape, NEG_BIG, F32)
        acc_ref[...] = jnp.zeros(acc_ref.shape, F32)

        def exact_body(t, carry):
            keys = k_ref[0, key_slice(t), :]
            mb = jnp.where(s_ref[key_slice(t), :] >= thr, 0.0, -jnp.inf)
            mb = jnp.concatenate([mb] * B_GROUP, axis=1)
            for g in range(B_KV_HEADS):
                lg = jnp.dot(keys, qpad_ref[g], preferred_element_type=F32) + mb
                m_old = m_ref[g]
                m_new = jnp.maximum(m_old, jnp.max(fold_rows(lg, jnp.max), axis=0, keepdims=True))
                p = jnp.exp2(lg - m_new).astype(BF16)
                pv = jnp.dot(vta_ref[0, g, :, key_slice(t)], p, preferred_element_type=F32)
                acc_ref[g] = jnp.exp2(m_old - m_new) * acc_ref[g] + pv
                m_ref[g] = m_new
            return carry

        lax.fori_loop(0, n_tiles, exact_body, 0)

    for g in range(B_KV_HEADS):
        acc = acc_ref[g]
        o_g = acc[:HEAD_DIM] / acc[HEAD_DIM:HEAD_DIM + 1]
        for j in range(B_GROUP):
            hh = g * B_GROUP + j
            o_ref[0, hh * HEAD_DIM:(hh + 1) * HEAD_DIM, :] = o_g[:, j * qn:(j + 1) * qn].astype(o_ref.dtype)


V_AUG_ROWS = HEAD_DIM + 16


def _dsa(q, qi, kw, k, v, ki, *, q_tile, causal, k_sel):
    batch, seq, _ = q.shape
    n_keys = k.shape[1]
    lp = -(-n_keys // DSA_KEY_TILE) * DSA_KEY_TILE
    assert seq % q_tile == 0 and q_tile % LANES == 0
    pad = ((0, 0), (0, lp - n_keys), (0, 0))
    k, v, ki = (jnp.pad(a.astype(BF16), pad) for a in (k, v, ki))
    vt = jnp.swapaxes(v, 1, 2).reshape(batch, B_KV_HEADS, HEAD_DIM, lp)
    vta = jnp.concatenate([vt, jnp.ones((batch, B_KV_HEADS, 1, lp), BF16),
                           jnp.zeros((batch, B_KV_HEADS, V_AUG_ROWS - HEAD_DIM - 1, lp), BF16)], axis=2)
    qt, qit, kwt = (jnp.swapaxes(a, 1, 2) for a in (q, qi, kw))

    def qspec(rows):
        return pl.BlockSpec((1, rows, q_tile), lambda b, i: (b, 0, i))

    def resident(shape):
        zero = (0,) * len(shape)
        return pl.BlockSpec((1,) + shape, lambda b, i: (b,) + zero, pipeline_mode=pl.Buffered(1))

    ot = pl.pallas_call(
        functools.partial(_dsa_kernel, causal=causal, n_keys=n_keys, k_sel=k_sel),
        grid=(batch, seq // q_tile),
        in_specs=[qspec(B_Q), qspec(B_QI), qspec(LANES),
                  resident((lp, IDX_DIM)), resident((lp, B_KV)),
                  resident((B_KV_HEADS, V_AUG_ROWS, lp)),
                  pl.BlockSpec((1, 8, LANES), lambda b, i: (b, 0, 0))],
        out_specs=qspec(B_Q),
        out_shape=jax.ShapeDtypeStruct((batch, B_Q, seq), BF16),
        scratch_shapes=[pltpu.VMEM((lp, q_tile), F32),
                        pltpu.VMEM((lp, q_tile), BF16),
                        pltpu.VMEM((B_KV_HEADS, B_KV, B_GROUP * q_tile), BF16),
                        pltpu.VMEM((B_KV_HEADS, 1, B_GROUP * q_tile), F32),
                        pltpu.VMEM((B_KV_HEADS, 1, B_GROUP * q_tile), F32),
                        pltpu.VMEM((B_KV_HEADS, V_AUG_ROWS, B_GROUP * q_tile), F32)],
        compiler_params=_params(2),
        name="dsa_prompt" if causal else "dsa_sample",
    )(qt, qit, kwt, ki, k, vta, _key_norm_max(k))
    return jnp.swapaxes(ot, 1, 2)


def kernel(x_prompt, x_sample, cache_k_a, cache_v_a, cache_k_b, cache_v_b, cache_kidx_b,
           c_prompt, c_sample, w_cond, b_cond, ln_g, ln_b, a_w_in, a_w_o, a_rel_bias,
           b_w_in, b_w_o, b_kidx_ln_g, b_kidx_ln_b, ffn_w_gu, ffn_w_down,
           moe_w_router, moe_b_router, moe_w_gu, moe_w_down):
    depth = w_cond.shape[0]
    alpha = (2 * depth) ** 0.25
    n_p, seq, d = x_prompt.shape
    n_s, dec_seq, _ = x_sample.shape
    past = cache_k_b.shape[2]
    a_hd = A_HEADS * HEAD_DIM

    rows = n_p + n_s
    rows_pad = -(-rows // 8) * 8
    c_all = jnp.pad(jnp.concatenate([c_prompt, c_sample], axis=0), ((0, rows_pad - rows), (0, 0)))
    mod_all = _modulation(c_all, w_cond, b_cond)

    xp, xs = x_prompt, x_sample
    outs = {k: [] for k in ("ka_p", "va_p", "kb_p", "vb_p", "ib_p", "ka_s", "va_s", "kb_s", "vb_s", "ib_s")}
    dummy_wr = jnp.zeros((d, LANES), F32)
    dummy_br = jnp.zeros((1, LANES), F32)
    for i in range(depth):
        j = i // 2
        mod_p = mod_all[i, :n_p].reshape(n_p, 1, 6 * d)
        mod_s = mod_all[i, n_p:rows].reshape(n_s, 1, 6 * d)
        g1, b1, g2, b2 = ln_g[i, 0], ln_b[i, 0], ln_g[i, 1], ln_b[i, 1]
        if i % 2 == 0:
            w_in = a_w_in[j].astype(BF16)
            keep = min(BAND_ROWS, seq)
            qkv_p = _modmm(xp, mod_p, w_in, BF16)
            kv_p = _modmm(xp, mod_p, w_in[:, a_hd:], F32, t_start=seq - keep)
            qkv_s = _modmm(xs, mod_s, w_in, BF16)
            kv_s = _modmm(xs, mod_s, w_in[:, a_hd:], F32)
            outs["ka_p"].append(kv_p[..., :a_hd].reshape(n_p, keep, A_HEADS, HEAD_DIM))
            outs["va_p"].append(kv_p[..., a_hd:].reshape(n_p, keep, A_HEADS, HEAD_DIM))
            outs["ka_s"].append(kv_s[..., :a_hd].reshape(n_s, dec_seq, A_HEADS, HEAD_DIM))
            outs["va_s"].append(kv_s[..., a_hd:].reshape(n_s, dec_seq, A_HEADS, HEAD_DIM))
            o_p = _band_attn_prompt(qkv_p, a_rel_bias[j])
            ck = cache_k_a[j].reshape(n_s, -1, a_hd).astype(BF16)
            cv = cache_v_a[j].reshape(n_s, -1, a_hd).astype(BF16)
            o_s = _band_attn_sample(qkv_s, ck, cv, a_rel_bias[j])
            w_o = a_w_o[j].astype(BF16)
        else:
            w_in = jnp.pad(b_w_in[j], ((0, 0), (0, B_PROJ_PAD - B_PROJ))).astype(BF16)
            kn_g = jnp.pad(b_kidx_ln_g[j], (0, LANES - IDX_DIM)).reshape(1, LANES)
            kn_b = jnp.pad(b_kidx_ln_b[j], (0, LANES - IDX_DIM)).reshape(1, LANES)
            cos_p, sin_p = _rope_tables(jnp.arange(seq))
            cos_s, sin_s = _rope_tables(past + jnp.arange(dec_seq))
            bb_s, _ = _row_blocks(n_s, dec_seq, ROW_TILE)
            cos_s, sin_s = jnp.tile(cos_s, (bb_s, 1)), jnp.tile(sin_s, (bb_s, 1))
            q_p, qi_p, k_p, v_p, kw_p = _b_project(xp, mod_p, w_in, cos_p, sin_p, kn_g, kn_b, False)
            q_s, qi_s, k_s, v_s, kw_s = _b_project(xs, mod_s, w_in, cos_s, sin_s, kn_g, kn_b, True)
            outs["kb_p"].append(k_p.reshape(n_p, seq, B_KV_HEADS, HEAD_DIM))
            outs["vb_p"].append(v_p.reshape(n_p, seq, B_KV_HEADS, HEAD_DIM))
            outs["ib_p"].append(kw_p[..., :IDX_DIM])
            outs["kb_s"].append(k_s.reshape(n_s, dec_seq, B_KV_HEADS, HEAD_DIM))
            outs["vb_s"].append(v_s.reshape(n_s, dec_seq, B_KV_HEADS, HEAD_DIM))
            outs["ib_s"].append(kw_s[..., :IDX_DIM])
            o_p = _dsa(q_p, qi_p, kw_p, k_p, v_p, kw_p[..., :IDX_DIM], q_tile=DSA_Q_TILE, causal=True,
                       k_sel=min(TOPK_MAX, seq // 4))
            n_keys = past + dec_seq
            kk = jnp.concatenate([cache_k_b[j].reshape(n_s, past, B_KV), k_s], axis=1)
            vc = jnp.concatenate([cache_v_b[j].reshape(n_s, past, B_KV), v_s], axis=1)
            kki = jnp.concatenate([cache_kidx_b[j], kw_s[..., :IDX_DIM]], axis=1)
            rep = LANES // dec_seq
            q_r, qi_r, kw_r = (jnp.concatenate([a] * rep, axis=1) for a in (q_s, qi_s, kw_s))
            o_s = _dsa(q_r, qi_r, kw_r, kk, vc, kki, q_tile=LANES, causal=False,
                       k_sel=min(TOPK_MAX, n_keys // 4))[:, :dec_seq]
            w_o = b_w_o[j].astype(BF16)
        xp = _mm_postnorm(o_p, w_o, xp, mod_p, 2, g1, b1, alpha)
        xs = _mm_postnorm(o_s, w_o, xs, mod_s, 2, g1, b1, alpha)
        if i % 2 == 0:
            w_gu = ffn_w_gu[j].astype(BF16)[None]
            w_dn = ffn_w_down[j].astype(BF16)[None]
            dense = dict(routed=False, ff_chunk=D_FF // 2, alpha=alpha)
            xp = _ffn(xp, mod_p, dummy_wr, dummy_br, w_gu, w_dn, g2, b2, **dense)
            xs = _ffn(xs, mod_s, dummy_wr, dummy_br, w_gu, w_dn, g2, b2, **dense)
        else:
            fc = MOE_FF_CHUNK
            w_gu = moe_w_gu[j].astype(BF16).reshape(N_EXPERTS, d, 2 * D_FF_EXPERT // fc, fc)
            w_gu = jnp.swapaxes(w_gu, 1, 2)
            w_dn = moe_w_down[j].astype(BF16)
            w_r = jnp.pad(moe_w_router[j], ((0, 0), (0, LANES - N_EXPERTS)))
            b_r = jnp.pad(moe_b_router[j], (0, LANES - N_EXPERTS)).reshape(1, LANES)
            moe = dict(ff_chunk=fc, alpha=alpha)
            xp = _moe(xp, mod_p, w_r, b_r, w_gu, w_dn, g2, b2, **moe)
            xs = _moe(xs, mod_s, w_r, b_r, w_gu, w_dn, g2, b2, **moe)

    st = lambda name: jnp.stack(outs[name])
    return (xp, xs, st("ka_p"), st("va_p"), st("kb_p"), st("vb_p"), st("ib_p"),
            st("ka_s"), st("va_s"), st("kb_s"), st("vb_s"), st("ib_s"))
```

```python
import functools

import jax
import jax.numpy as jnp
from jax import lax
from jax.experimental import pallas as pl
from jax.experimental.pallas import tpu as pltpu

F32 = jnp.float32
BF16 = jnp.bfloat16

D_MODEL = 1024
CHUNK = 64
N_PAST_CHUNKS = 8
BAND_ROWS = N_PAST_CHUNKS * CHUNK
REL_CLIP = 2 * CHUNK
HEAD_DIM = 64
A_HEADS = 16
B_HEADS = 16
B_KV_HEADS = 4
B_GROUP = B_HEADS // B_KV_HEADS
IDX_HEADS = 8
IDX_DIM = 64
TOPK_MAX = 256
D_FF = 2816
N_EXPERTS = 8
D_FF_EXPERT = 3584
ROPE_THETA = 10000.0
LN_EPS = 1e-5
B_Q = B_HEADS * HEAD_DIM
B_KV = B_KV_HEADS * HEAD_DIM
B_QI = IDX_HEADS * IDX_DIM
B_PROJ = B_Q + 2 * B_KV + B_QI + IDX_DIM + IDX_HEADS

LANES = 128
VMEM_LIMIT_BYTES = 58 * 1024 * 1024

A_Q_TILE = 4 * CHUNK
DSA_Q_TILE = 4 * CHUNK
DSA_KEY_TILE = 512
ROW_TILE = 512
FFN_ROW_TILE = 1024
MOE_CAPACITY = 288
MOE_FF_CHUNK = 1792
B_PROJ_PAD = B_Q + 2 * B_KV + B_QI + LANES
NEG_BIG = -1e30
QK_SCALE_LOG2 = HEAD_DIM ** -0.5 * 1.4426950408889634
BOUND_SLACK = 1.02
BOUND_SLACK_ABS = 0.01
MIN_TRUSTED_DENOMINATOR = 2.0 ** -100
SECANT_STEPS = 12


def _params(n_grid):
    return pltpu.CompilerParams(
        dimension_semantics=("arbitrary",) * n_grid,
        vmem_limit_bytes=VMEM_LIMIT_BYTES,
    )


def _row_blocks(batch, seq, target):
    if seq >= target:
        assert seq % target == 0
        return 1, target
    bb = max(1, min(batch, target // seq))
    while batch % bb:
        bb -= 1
    return bb, seq


def _mod_spec(bb, chunk, n_grid):
    if n_grid == 2:
        return pl.BlockSpec((bb, 1, D_MODEL), lambda b, t: (b, 0, chunk))
    return pl.BlockSpec((bb, 1, D_MODEL), lambda b, t, e, c: (b, 0, chunk))


def _silu(x):
    return x / (1.0 + jnp.exp(-x))


def _layer_norm_rows(z, g, b):
    mu = jnp.mean(z, axis=-1, keepdims=True)
    zc = z - mu
    var = jnp.mean(zc * zc, axis=-1, keepdims=True)
    return zc * lax.rsqrt(var + LN_EPS) * g + b


def _modulation_kernel(c_ref, w_ref, b_ref, o_ref):
    a = _silu(c_ref[...]).astype(BF16)
    w = w_ref[0].astype(BF16)
    o_ref[0] = jnp.dot(a, w, preferred_element_type=F32) + b_ref[0]


def _modulation(c_all, w_cond, b_cond):
    depth, d, n = w_cond.shape
    rows = c_all.shape[0]
    tn = 1536
    return pl.pallas_call(
        _modulation_kernel,
        grid=(depth, n // tn),
        in_specs=[
            pl.BlockSpec((rows, d), lambda i, j: (0, 0)),
            pl.BlockSpec((1, d, tn), lambda i, j: (i, 0, j)),
            pl.BlockSpec((1, 1, tn), lambda i, j: (i, 0, j)),
        ],
        out_specs=pl.BlockSpec((1, rows, tn), lambda i, j: (i, 0, j)),
        out_shape=jax.ShapeDtypeStruct((depth, rows, n), F32),
        compiler_params=_params(2),
        name="modulation",
    )(c_all, w_cond, b_cond.reshape(depth, 1, n))


def _modmm_kernel(x_ref, sc_ref, sh_ref, w_ref, o_ref):
    bb, tt, d = x_ref.shape
    h = x_ref[...] * (1.0 + sc_ref[...]) + sh_ref[...]
    h = h.reshape(bb * tt, d).astype(BF16)
    acc = jnp.dot(h, w_ref[...], preferred_element_type=F32)
    o_ref[...] = acc.reshape(bb, tt, acc.shape[-1]).astype(o_ref.dtype)


def _modmm(x, mod, w, out_dtype, t_start=0):
    batch, seq, d = x.shape
    n = w.shape[1]
    seq_out = seq - t_start
    bb, tt = _row_blocks(batch, seq_out, ROW_TILE)
    assert t_start % tt == 0
    off = t_start // tt
    return pl.pallas_call(
        _modmm_kernel,
        grid=(batch // bb, seq_out // tt),
        in_specs=[
            pl.BlockSpec((bb, tt, d), lambda b, t: (b, t + off, 0)),
            _mod_spec(bb, 1, 2),
            _mod_spec(bb, 0, 2),
            pl.BlockSpec((d, n), lambda b, t: (0, 0)),
        ],
        out_specs=pl.BlockSpec((bb, tt, n), lambda b, t: (b, t, 0)),
        out_shape=jax.ShapeDtypeStruct((batch, seq_out, n), out_dtype),
        compiler_params=_params(2),
        name="a_project",
    )(x, mod, mod, w)


def _band_attn_kernel(*refs, n_kb, n_maybe_invalid):
    q_ref = refs[0]
    k_refs = refs[1:1 + n_kb]
    v_refs = refs[1 + n_kb:1 + 2 * n_kb]
    f_ref = refs[1 + 2 * n_kb]
    o_ref = refs[2 + 2 * n_kb]
    bias_ref = refs[3 + 2 * n_kb]
    q_tile = q_ref.shape[1]
    kb_sizes = [r.shape[1] for r in k_refs]
    k_tot = sum(kb_sizes)
    width = f_ref.shape[-1]
    i = pl.program_id(1)

    @pl.when((pl.program_id(0) == 0) & (i == 0))
    def _():
        rq = lax.broadcasted_iota(jnp.int32, (q_tile, k_tot), 0) // CHUNK
        ck = lax.broadcasted_iota(jnp.int32, (q_tile, k_tot), 1) // CHUNK
        in_band = (ck >= rq) & (ck <= rq + N_PAST_CHUNKS)
        for h in range(A_HEADS):
            rows = jnp.broadcast_to(f_ref[h], (q_tile, width))
            toep = pltpu.roll(rows, k_tot + 1, 1, stride=1, stride_axis=0)
            bias_ref[h] = jnp.where(in_band, toep[:, :k_tot], -jnp.inf)

    q = q_ref[0] * (HEAD_DIM ** -0.5)
    for h in range(A_HEADS):
        cols = slice(h * HEAD_DIM, (h + 1) * HEAD_DIM)
        qh = q[:, cols]
        parts = []
        for kb in range(n_kb):
            lg = lax.dot_general(qh, k_refs[kb][0, :, cols], (((1,), (1,)), ((), ())),
                                 preferred_element_type=F32)
            if kb < n_maybe_invalid:
                lg = jnp.where(i - n_maybe_invalid + kb >= 0, lg, -jnp.inf)
            parts.append(lg)
        logits = jnp.concatenate(parts, axis=1) + bias_ref[h]
        m = jnp.max(logits, axis=1, keepdims=True)
        p = jnp.exp(logits - m)
        l = jnp.sum(p, axis=1, keepdims=True)
        pb = p.astype(BF16)
        acc = jnp.zeros((q_tile, HEAD_DIM), F32)
        start = 0
        for kb in range(n_kb):
            acc = acc + jnp.dot(pb[:, start:start + kb_sizes[kb]], v_refs[kb][0, :, cols],
                                preferred_element_type=F32)
            start += kb_sizes[kb]
        o_ref[0, :, cols] = (acc / l).astype(o_ref.dtype)


def _bias_vector(table, q_tile, k_tot):
    width = k_tot + q_tile
    lo = -(q_tile - 1) - BAND_ROWS + REL_CLIP
    left = max(0, -lo)
    start = max(0, lo)
    n_mid = min(2 * REL_CLIP + 1 - start, width - left)
    right = width - left - n_mid
    mid = table.T[:, start:start + n_mid]
    f = jnp.pad(mid, ((0, 0), (left, right)), mode="edge")
    return f.reshape(A_HEADS, 1, width)


def _band_attn_prompt(qkv, table):
    batch, seq, _ = qkv.shape
    hd = A_HEADS * HEAD_DIM
    qt = A_Q_TILE
    n_prev = BAND_ROWS // qt
    n_kb = n_prev + 1
    k_tot = n_kb * qt
    f = _bias_vector(table, qt, k_tot)

    def kv_spec(kb, col):
        return pl.BlockSpec((1, qt, hd), lambda b, i: (b, jnp.maximum(i - n_prev + kb, 0), col))

    return pl.pallas_call(
        functools.partial(_band_attn_kernel, n_kb=n_kb, n_maybe_invalid=n_prev),
        grid=(batch, seq // qt),
        in_specs=[pl.BlockSpec((1, qt, hd), lambda b, i: (b, i, 0))]
        + [kv_spec(kb, 1) for kb in range(n_kb)]
        + [kv_spec(kb, 2) for kb in range(n_kb)]
        + [pl.BlockSpec(f.shape, lambda b, i: (0, 0, 0))],
        out_specs=pl.BlockSpec((1, qt, hd), lambda b, i: (b, i, 0)),
        out_shape=jax.ShapeDtypeStruct((batch, seq, hd), BF16),
        scratch_shapes=[pltpu.VMEM((A_HEADS, qt, k_tot), F32)],
        compiler_params=_params(2),
        name="band_attn_prompt",
    )(*([qkv] * (1 + 2 * n_kb)), f)


def _band_attn_sample(qkv, cache_k, cache_v, table):
    batch, seq, _ = qkv.shape
    hd = A_HEADS * HEAD_DIM
    win = cache_k.shape[1]
    assert seq == CHUNK and win == BAND_ROWS
    k_tot = win + seq
    f = _bias_vector(table, seq, k_tot)
    new = lambda col: pl.BlockSpec((1, seq, hd), lambda b, i: (b, 0, col))
    old = pl.BlockSpec((1, win, hd), lambda b, i: (b, 0, 0))
    return pl.pallas_call(
        functools.partial(_band_attn_kernel, n_kb=2, n_maybe_invalid=0),
        grid=(batch, 1),
        in_specs=[new(0), old, new(1), old, new(2), pl.BlockSpec(f.shape, lambda b, i: (0, 0, 0))],
        out_specs=pl.BlockSpec((1, seq, hd), lambda b, i: (b, 0, 0)),
        out_shape=jax.ShapeDtypeStruct((batch, seq, hd), BF16),
        scratch_shapes=[pltpu.VMEM((A_HEADS, seq, k_tot), F32)],
        compiler_params=_params(2),
        name="band_attn_sample",
    )(qkv, cache_k, qkv, cache_v, qkv, f)


def _mm_postnorm_kernel(o_ref, w_ref, x_ref, gate_ref, g_ref, b_ref, y_ref, *, alpha):
    bb, tt, d = x_ref.shape
    o = o_ref[...].reshape(bb * tt, o_ref.shape[-1])
    sub = jnp.dot(o, w_ref[...], preferred_element_type=F32).reshape(bb, tt, d)
    z = alpha * x_ref[...] + (1.0 + gate_ref[...]) * sub
    y_ref[...] = _layer_norm_rows(z, g_ref[...], b_ref[...])


def _mm_postnorm(o, w, x, mod, gate_chunk, g, b, alpha):
    batch, seq, d = x.shape
    bb, tt = _row_blocks(batch, seq, ROW_TILE)
    k = o.shape[-1]
    return pl.pallas_call(
        functools.partial(_mm_postnorm_kernel, alpha=alpha),
        grid=(batch // bb, seq // tt),
        in_specs=[
            pl.BlockSpec((bb, tt, k), lambda b_, t: (b_, t, 0)),
            pl.BlockSpec((k, d), lambda b_, t: (0, 0)),
            pl.BlockSpec((bb, tt, d), lambda b_, t: (b_, t, 0)),
            _mod_spec(bb, gate_chunk, 2),
            pl.BlockSpec((1, 1, d), lambda b_, t: (0, 0, 0)),
            pl.BlockSpec((1, 1, d), lambda b_, t: (0, 0, 0)),
        ],
        out_specs=pl.BlockSpec((bb, tt, d), lambda b_, t: (b_, t, 0)),
        out_shape=jax.ShapeDtypeStruct((batch, seq, d), F32),
        compiler_params=_params(2),
        name="out_proj_postnorm",
    )(o, w, x, mod, g.reshape(1, 1, d), b.reshape(1, 1, d))


def _dense_ffn_kernel(x_ref, sc_ref, sh_ref, gate_ref, wg_ref, wu_ref, wd_ref, g_ref, b_ref,
                      y_ref, h_ref, acc_ref, *, alpha):
    bb, tt, d = x_ref.shape
    rows = bb * tt
    c = pl.program_id(2)

    @pl.when(c == 0)
    def _():
        h = (x_ref[...] * (1.0 + sc_ref[...]) + sh_ref[...]).reshape(rows, d)
        h_ref[...] = h.astype(BF16)
        acc_ref[...] = jnp.zeros_like(acc_ref)

    h = h_ref[...]
    gt = jnp.dot(h, wg_ref[...], preferred_element_type=F32)
    up = jnp.dot(h, wu_ref[...], preferred_element_type=F32)
    act = (_silu(gt) * up).astype(BF16)
    acc_ref[...] += jnp.dot(act, wd_ref[...], preferred_element_type=F32)

    @pl.when(c == pl.num_programs(2) - 1)
    def _():
        z = alpha * x_ref[...] + (1.0 + gate_ref[...]) * acc_ref[...].reshape(bb, tt, d)
        y_ref[...] = _layer_norm_rows(z, g_ref[...], b_ref[...])


def _dense_ffn(x, mod, w_gu, w_down, g, b, *, ff_chunk, alpha):
    batch, seq, d = x.shape
    ff = w_down.shape[0]
    assert ff % ff_chunk == 0
    n_c = ff // ff_chunk
    bb, tt = _row_blocks(batch, seq, FFN_ROW_TILE)
    rows = bb * tt
    vec = pl.BlockSpec((1, 1, d), lambda b_, t, c: (0, 0, 0))

    def mod_spec(chunk):
        return pl.BlockSpec((bb, 1, d), lambda b_, t, c: (b_, 0, chunk))

    return pl.pallas_call(
        functools.partial(_dense_ffn_kernel, alpha=alpha),
        grid=(batch // bb, seq // tt, n_c),
        in_specs=[
            pl.BlockSpec((bb, tt, d), lambda b_, t, c: (b_, t, 0)),
            mod_spec(4),
            mod_spec(3),
            mod_spec(5),
            pl.BlockSpec((d, ff_chunk), lambda b_, t, c: (0, c)),
            pl.BlockSpec((d, ff_chunk), lambda b_, t, c: (0, n_c + c)),
            pl.BlockSpec((ff_chunk, d), lambda b_, t, c: (c, 0)),
            vec,
            vec,
        ],
        out_specs=pl.BlockSpec((bb, tt, d), lambda b_, t, c: (b_, t, 0)),
        out_shape=jax.ShapeDtypeStruct((batch, seq, d), F32),
        scratch_shapes=[pltpu.VMEM((rows, d), BF16), pltpu.VMEM((rows, d), F32)],
        compiler_params=_params(3),
        name="dense_ffn",
    )(x, mod, mod, mod, w_gu, w_gu, w_down, g.reshape(1, 1, d), b.reshape(1, 1, d))


def _moe_kernel(x_ref, sc_ref, sh_ref, gate_ref, wr_ref, br_ref, wg_ref, wu_ref, wd_ref, g_ref, b_ref,
                y_ref, h_ref, comb_ref, slot_ref, cnt_ref, tri_ref, hc_ref, acc_ref, out_ref, *, alpha):
    bb, tt, d = x_ref.shape
    rows = bb * tt
    cap = hc_ref.shape[1]
    e = pl.program_id(2)
    c = pl.program_id(3)
    last_c = pl.num_programs(3) - 1
    lane = lax.broadcasted_iota(jnp.int32, (rows, LANES), 1)

    @pl.when((pl.program_id(0) == 0) & (pl.program_id(1) == 0) & (e == 0) & (c == 0))
    def _():
        r = lax.broadcasted_iota(jnp.int32, (rows, rows), 0)
        col = lax.broadcasted_iota(jnp.int32, (rows, rows), 1)
        tri_ref[...] = jnp.where(r <= col, 1.0, 0.0).astype(BF16)

    @pl.when((e == 0) & (c == 0))
    def _():
        h = (x_ref[...] * (1.0 + sc_ref[...]) + sh_ref[...]).reshape(rows, d)
        h_ref[...] = h.astype(BF16)
        out_ref[...] = jnp.zeros_like(out_ref)
        logits = jnp.dot(h, wr_ref[...], preferred_element_type=F32,
                         precision=lax.Precision.HIGHEST) + br_ref[...]
        logits = jnp.where(lane < N_EXPERTS, logits, -jnp.inf)
        m1 = jnp.max(logits, axis=1, keepdims=True)
        i1 = jnp.min(jnp.where(logits == m1, lane, LANES), axis=1, keepdims=True)
        rest = jnp.where(lane == i1, -jnp.inf, logits)
        m2 = jnp.max(rest, axis=1, keepdims=True)
        i2 = jnp.min(jnp.where(rest == m2, lane, LANES), axis=1, keepdims=True)
        e2 = jnp.exp(m2 - m1)
        den = 1.0 + e2
        comb_ref[...] = jnp.where(lane == i1, 1.0 / den, 0.0) + jnp.where(lane == i2, e2 / den, 0.0)
        member = jnp.where((lane == i1) | (lane == i2), 1.0, 0.0)
        member_t = member.T[:2 * N_EXPERTS]
        upto = jnp.dot(member_t.astype(BF16), tri_ref[...], preferred_element_type=F32)
        slot_ref[...] = jnp.where(member_t > 0, upto - 1.0, -1.0)
        cnt_ref[...] = jnp.broadcast_to(upto[:, rows - 1:rows], cnt_ref.shape)

    n_sub = (jnp.max(cnt_ref[pl.ds(e, 1), :]).astype(jnp.int32) + cap - 1) // cap

    def selection(s):
        slot = slot_ref[pl.ds(e, 1), :] - (s * cap).astype(F32)
        r = lax.broadcasted_iota(jnp.int32, (cap, rows), 0).astype(F32)
        return jnp.where(slot == r, 1.0, 0.0).astype(BF16)

    @pl.when(c == 0)
    def _():
        def pack(s, carry):
            hc_ref[s] = jnp.dot(selection(s), h_ref[...], preferred_element_type=F32).astype(BF16)
            acc_ref[s] = jnp.zeros(acc_ref.shape[1:], F32)
            return carry
        lax.fori_loop(0, n_sub, pack, 0)

    def expert(s, carry):
        hc = hc_ref[s]
        gt = jnp.dot(hc, wg_ref[0, 0], preferred_element_type=F32)
        up = jnp.dot(hc, wu_ref[0, 0], preferred_element_type=F32)
        act = (_silu(gt) * up).astype(BF16)
        acc_ref[s] += jnp.dot(act, wd_ref[0], preferred_element_type=F32)
        return carry

    lax.fori_loop(0, n_sub, expert, 0)

    @pl.when(c == last_c)
    def _():
        ce = jnp.sum(jnp.where(lane == e, comb_ref[...], 0.0), axis=1, keepdims=True)

        def unpack(s, carry):
            back = lax.dot_general(selection(s), acc_ref[s].astype(BF16), (((0,), (0,)), ((), ())),
                                   preferred_element_type=F32)
            out_ref[...] += ce * back
            return carry
        lax.fori_loop(0, n_sub, unpack, 0)

    @pl.when((e == pl.num_programs(2) - 1) & (c == last_c))
    def _():
        z = alpha * x_ref[...] + (1.0 + gate_ref[...]) * out_ref[...].reshape(bb, tt, d)
        y_ref[...] = _layer_norm_rows(z, g_ref[...], b_ref[...])


def _moe(x, mod, w_r, b_r, w_gu, w_down, g, b, *, ff_chunk, alpha):
    batch, seq, d = x.shape
    n_e, ff, _ = w_down.shape
    assert ff % ff_chunk == 0
    n_c = ff // ff_chunk
    bb, tt = _row_blocks(batch, seq, FFN_ROW_TILE)
    rows = bb * tt
    cap = MOE_CAPACITY
    max_sub = -(-rows // cap)
    vec = pl.BlockSpec((1, 1, d), lambda b_, t, e, c: (0, 0, 0))
    return pl.pallas_call(
        functools.partial(_moe_kernel, alpha=alpha),
        grid=(batch // bb, seq // tt, n_e, n_c),
        in_specs=[
            pl.BlockSpec((bb, tt, d), lambda b_, t, e, c: (b_, t, 0), pipeline_mode=pl.Buffered(1)),
            _mod_spec(bb, 4, 4),
            _mod_spec(bb, 3, 4),
            _mod_spec(bb, 5, 4),
            pl.BlockSpec((d, LANES), lambda b_, t, e, c: (0, 0)),
            pl.BlockSpec((1, LANES), lambda b_, t, e, c: (0, 0)),
            pl.BlockSpec((1, 1, d, ff_chunk), lambda b_, t, e, c: (e, c, 0, 0)),
            pl.BlockSpec((1, 1, d, ff_chunk), lambda b_, t, e, c: (e, n_c + c, 0, 0)),
            pl.BlockSpec((1, ff_chunk, d), lambda b_, t, e, c: (e, c, 0)),
            vec,
            vec,
        ],
        out_specs=pl.BlockSpec((bb, tt, d), lambda b_, t, e, c: (b_, t, 0)),
        out_shape=jax.ShapeDtypeStruct((batch, seq, d), F32),
        scratch_shapes=[
            pltpu.VMEM((rows, d), BF16),
            pltpu.VMEM((rows, LANES), F32),
            pltpu.VMEM((2 * N_EXPERTS, rows), F32),
            pltpu.VMEM((2 * N_EXPERTS, LANES), F32),
            pltpu.VMEM((rows, rows), BF16),
            pltpu.VMEM((max_sub, cap, d), BF16),
            pltpu.VMEM((max_sub, cap, d), F32),
            pltpu.VMEM((rows, d), F32),
        ],
        compiler_params=_params(4),
        name="moe_ffn",
    )(x, mod, mod, mod, w_r, b_r, w_gu, w_gu, w_down, g.reshape(1, 1, d), b.reshape(1, 1, d))


def _rope_lanes(seg, cos, sin_signed, first_half):
    width = seg.shape[-1]
    reps = width // LANES
    if reps > 1:
        cos = jnp.concatenate([cos] * reps, axis=1)
        sin_signed = jnp.concatenate([sin_signed] * reps, axis=1)
        first_half = jnp.concatenate([first_half] * reps, axis=1)
    half = HEAD_DIM // 2
    swapped = jnp.where(first_half, pltpu.roll(seg, width - half, 1), pltpu.roll(seg, half, 1))
    return seg * cos + swapped * sin_signed


def _b_project_kernel(x_ref, sc_ref, sh_ref, w_ref, cos_ref, sin_ref, kg_ref, kb_ref,
                      q_ref, qi_ref, k_ref, v_ref, kw_ref):
    bb, tt, d = x_ref.shape
    rows = bb * tt
    h = x_ref[...] * (1.0 + sc_ref[...]) + sh_ref[...]
    h = h.reshape(rows, d).astype(BF16)
    acc = jnp.dot(h, w_ref[...], preferred_element_type=F32)
    cos = cos_ref[...]
    sin_signed = sin_ref[...]
    lane = lax.broadcasted_iota(jnp.int32, (1, LANES), 1)
    first_half = (lane % HEAD_DIM) < (HEAD_DIM // 2)

    def out(ref, val):
        ref[...] = val.reshape(bb, tt, val.shape[-1]).astype(ref.dtype)

    o_k, o_v, o_qi, o_ki = B_Q, B_Q + B_KV, B_Q + 2 * B_KV, B_Q + 2 * B_KV + B_QI
    out(q_ref, _rope_lanes(acc[:, :o_k], cos, sin_signed, first_half) * QK_SCALE_LOG2)
    out(k_ref, _rope_lanes(acc[:, o_k:o_v], cos, sin_signed, first_half))
    out(v_ref, acc[:, o_v:o_qi])
    out(qi_ref, _rope_lanes(acc[:, o_qi:o_ki], cos, sin_signed, first_half))
    seg = acc[:, o_ki:]
    is_ki = lane < IDX_DIM
    mu = jnp.sum(jnp.where(is_ki, seg, 0.0), axis=1, keepdims=True) / IDX_DIM
    cen = jnp.where(is_ki, seg - mu, 0.0)
    var = jnp.sum(cen * cen, axis=1, keepdims=True) / IDX_DIM
    ki = cen * lax.rsqrt(var + LN_EPS) * kg_ref[...] + kb_ref[...]
    ki = _rope_lanes(ki, cos, sin_signed, first_half)
    out(kw_ref, jnp.where(is_ki, ki, seg * (IDX_HEADS ** -0.5)))


def _b_project(x, mod, w, cos, sin_signed, kn_g, kn_b, table_per_tile):
    batch, seq, d = x.shape
    bb, tt = _row_blocks(batch, seq, ROW_TILE)
    rows = bb * tt
    n = w.shape[1]
    tab = pl.BlockSpec((rows, LANES), (lambda b, t: (0, 0)) if table_per_tile else (lambda b, t: (t, 0)))
    vec = pl.BlockSpec((1, LANES), lambda b, t: (0, 0))

    def o_spec(width):
        return pl.BlockSpec((bb, tt, width), lambda b, t: (b, t, 0))

    def o_shape(width, dtype):
        return jax.ShapeDtypeStruct((batch, seq, width), dtype)

    return pl.pallas_call(
        _b_project_kernel,
        grid=(batch // bb, seq // tt),
        in_specs=[
            pl.BlockSpec((bb, tt, d), lambda b, t: (b, t, 0)),
            _mod_spec(bb, 1, 2),
            _mod_spec(bb, 0, 2),
            pl.BlockSpec((d, n), lambda b, t: (0, 0)),
            tab,
            tab,
            vec,
            vec,
        ],
        out_specs=[o_spec(B_Q), o_spec(B_QI), o_spec(B_KV), o_spec(B_KV), o_spec(LANES)],
        out_shape=[o_shape(B_Q, BF16), o_shape(B_QI, BF16), o_shape(B_KV, F32), o_shape(B_KV, F32),
                   o_shape(LANES, F32)],
        compiler_params=_params(2),
        name="b_project",
    )(x, mod, mod, w, cos, sin_signed, kn_g, kn_b)


def _rope_tables(pos):
    half = HEAD_DIM // 2
    inv = ROPE_THETA ** (-jnp.arange(half, dtype=F32) / half)
    ang = pos.astype(F32)[:, None] * inv[None, :]
    cos, sin = jnp.cos(ang), jnp.sin(ang)
    return (jnp.concatenate([cos, cos, cos, cos], axis=1),
            jnp.concatenate([-sin, sin, -sin, sin], axis=1))


def _key_norm_kernel(k_ref, grp_ref, o_ref):
    k = k_ref[0].astype(F32)
    n2 = jnp.dot((k * k).astype(BF16), grp_ref[...], preferred_element_type=F32)
    mx = jnp.broadcast_to(jnp.max(n2, axis=0, keepdims=True), o_ref.shape[1:])

    @pl.when(pl.program_id(1) == 0)
    def _():
        o_ref[0] = mx

    @pl.when(pl.program_id(1) > 0)
    def _():
        o_ref[0] = jnp.maximum(o_ref[0], mx)


def _key_norm_max(k):
    batch, lp, width = k.shape
    tile = DSA_KEY_TILE
    grp = (jnp.arange(width)[:, None] // HEAD_DIM == jnp.arange(LANES)[None, :]).astype(BF16)
    return pl.pallas_call(
        _key_norm_kernel,
        grid=(batch, lp // tile),
        in_specs=[pl.BlockSpec((1, tile, width), lambda b, t: (b, t, 0)),
                  pl.BlockSpec((width, LANES), lambda b, t: (0, 0))],
        out_specs=pl.BlockSpec((1, 8, LANES), lambda b, t: (b, 0, 0)),
        out_shape=jax.ShapeDtypeStruct((batch, 8, LANES), F32),
        compiler_params=_params(2),
        name="key_norm_max",
    )(k, grp)


def _dsa_kernel(qt_ref, qit_ref, kwt_ref, ki_ref, k_ref, vta_ref, kmax_ref, o_ref,
                s_ref, qpad_ref, bound_ref, m_ref, acc_ref, *, causal, n_keys, k_sel):
    qn = qt_ref.shape[2]
    lt = DSA_KEY_TILE
    i = pl.program_id(1)
    if causal:
        n_tiles = ((i + 1) * qn + lt - 1) // lt
        qpos = i * qn + lax.broadcasted_iota(jnp.int32, (1, qn), 1)
        limit = (qpos // CHUNK + 1) * CHUNK
    else:
        n_tiles = ki_ref.shape[1] // lt
        limit = jnp.full((1, qn), n_keys, jnp.int32)

    def key_slice(t):
        return pl.ds(pl.multiple_of(t * lt, lt), lt)

    def key_pos(t):
        return t * lt + lax.broadcasted_iota(jnp.int32, (lt, 1), 0)

    def fold_rows(x, op):
        return op(op(x.reshape(lt // 32, 4, 8, x.shape[-1]), axis=0), axis=0)

    def paired_tile_loop(body, init):
        def pair(t2, carry):
            return body(2 * t2 + 1, body(2 * t2, carry))
        carry = lax.fori_loop(0, n_tiles // 2, pair, init)
        return lax.cond(n_tiles % 2 == 1, lambda c: body(n_tiles - 1, c), lambda c: c, carry)

    qit = qit_ref[0]
    kwt = kwt_ref[0]
    qi_w = jnp.concatenate([qit[h * IDX_DIM:(h + 1) * IDX_DIM, :] for h in range(IDX_HEADS)], axis=1)
    w_row = jnp.concatenate([kwt[IDX_DIM + h:IDX_DIM + h + 1, :] for h in range(IDX_HEADS)], axis=1)
    w_row = w_row * (IDX_DIM ** -0.5)

    def score_body(t, carry):
        rmax, rmin = carry
        s = jnp.dot(ki_ref[0, key_slice(t), :], qi_w, preferred_element_type=F32)
        s = jnp.maximum(s, 0.0) * w_row
        sc = s[:, 0:qn]
        for h in range(1, IDX_HEADS):
            sc = sc + s[:, h * qn:(h + 1) * qn]
        adm = key_pos(t) < limit
        masked = jnp.where(adm, sc, -jnp.inf)
        s_ref[key_slice(t), :] = masked
        rmax = jnp.maximum(rmax, jnp.max(masked, axis=0, keepdims=True))
        rmin = jnp.minimum(rmin, jnp.min(jnp.where(adm, sc, jnp.inf), axis=0, keepdims=True))
        return rmax, rmin

    rmax, rmin = paired_tile_loop(score_body,
                                  (jnp.full((1, qn), -jnp.inf, F32), jnp.full((1, qn), jnp.inf, F32)))

    def count(pred):
        def body(t, acc):
            return acc + fold_rows(jnp.where(pred(s_ref[key_slice(t), :], key_pos(t)), 1.0, 0.0), jnp.sum)
        acc = lax.fori_loop(0, n_tiles, body, jnp.zeros((8, qn), F32))
        return jnp.sum(acc, axis=0, keepdims=True)

    def to_key(x):
        bits = lax.bitcast_convert_type(x, jnp.int32)
        return bits ^ ((bits >> 31) & 0x7FFFFFFF)

    def from_key(key):
        return lax.bitcast_convert_type(key ^ ((key >> 31) & 0x7FFFFFFF), F32)

    n_adm = limit.astype(F32)
    target = jnp.minimum(n_adm, float(k_sel))
    log_target = jnp.log(target)

    def any_left(done):
        return jnp.sum(done) < qn

    c_ge0 = count(lambda s, _: s >= 0.0)
    c_gt0 = count(lambda s, _: s > 0.0)
    is_pos = c_gt0 >= target
    is_neg = c_ge0 < target
    zero_key = jnp.zeros((1, qn), jnp.int32)
    lo0 = jnp.where(is_neg, to_key(rmin), zero_key)
    c_lo0 = jnp.where(is_neg, n_adm, c_ge0)
    c_hi0 = jnp.where(is_pos, 0.0, c_ge0)
    done0 = jnp.where((n_adm == target) | jnp.logical_not(is_pos | is_neg) | (c_lo0 == target), 1.0, 0.0)
    lo0 = jnp.where(n_adm == target, to_key(rmin), lo0)
    c_lo0 = jnp.where(n_adm == target, n_adm, c_lo0)

    ones = jnp.ones((1, qn), F32)

    def search(st0, key_to_value, value_to_key, count_ge):
        def cond(st):
            return any_left(st[7])

        def body(st):
            it, lo, hi, c_lo, c_hi, w_lo, w_hi, done, side = st
            t_lo, t_hi = key_to_value(lo), key_to_value(hi)
            f_lo = (jnp.log(c_lo) - log_target) * w_lo
            f_hi = (log_target - jnp.log(jnp.maximum(c_hi, 0.5))) * w_hi
            guess = value_to_key(t_lo + f_lo / (f_lo + f_hi) * (t_hi - t_lo))
            halve = (lo >> 1) + (hi >> 1) + (lo & hi & 1)
            mid = jnp.where(it >= SECANT_STEPS, halve, jnp.clip(guess, lo + 1, hi - 1))
            stuck = halve == lo
            c = count_ge(key_to_value(mid))
            live = (done < 0.5) & jnp.logical_not(stuck)
            up = live & (c >= target)
            dn = live & (c < target)
            w_hi = jnp.where(up, jnp.where(side > 0, 0.5 * w_hi, 1.0), jnp.where(dn, 1.0, w_hi))
            w_lo = jnp.where(dn, jnp.where(side < 0, 0.5 * w_lo, 1.0), jnp.where(up, 1.0, w_lo))
            side = jnp.where(up, 1.0, jnp.where(dn, -1.0, side))
            lo = jnp.where(up, mid, lo)
            c_lo = jnp.where(up, c, c_lo)
            hi = jnp.where(dn, mid, hi)
            c_hi = jnp.where(dn, c, c_hi)
            done = jnp.where(stuck | (c_lo == target) | (c_lo - c_hi <= 2.0), 1.0, done)
            return it + 1, lo, hi, c_lo, c_hi, w_lo, w_hi, done, side

        lo, hi, c_lo, c_hi, done = st0
        st = lax.while_loop(cond, body, (jnp.int32(0), lo, hi, c_lo, c_hi, ones, ones, done, 0.0 * ones))
        return st[1], st[2], st[3], st[4]

    hi0 = jnp.where(is_pos, to_key(rmax) + 1, zero_key)
    k_lo, k_hi, c_thr, c_above = search((lo0, hi0, c_lo0, c_hi0, done0),
                                        from_key, to_key, lambda v: count(lambda s, _: s >= v))
    thr = from_key(k_lo)

    two_left = (c_thr != target) & (c_thr - c_above == 2.0)

    def settle(_):
        t_lo, t_hi = from_key(k_lo), from_key(k_hi)

        def body(t, carry):
            top, bottom = carry
            s = s_ref[key_slice(t), :]
            top = jnp.maximum(top, fold_rows(jnp.where(s < t_hi, s, -jnp.inf), jnp.max))
            bottom = jnp.minimum(bottom, fold_rows(jnp.where(s >= t_lo, s, jnp.inf), jnp.min))
            return top, bottom

        top, bottom = lax.fori_loop(0, n_tiles, body, (jnp.full((8, qn), -jnp.inf, F32),
                                                       jnp.full((8, qn), jnp.inf, F32)))
        top = jnp.max(top, axis=0, keepdims=True)
        bottom = jnp.min(bottom, axis=0, keepdims=True)
        return (jnp.where(two_left, top, thr),
                jnp.where(two_left & (top != bottom), target, c_thr))

    thr, c_thr = lax.cond(jnp.sum(jnp.where(two_left, 1.0, 0.0)) > 0, settle, lambda _: (thr, c_thr), 0)

    tied = c_thr > target

    @pl.when(jnp.sum(jnp.where(tied, 1.0, 0.0)) > 0)
    def _():
        need = target - count(lambda s, _: s > thr)

        def idx_cond(st):
            return any_left(st[5])

        def idx_body(st):
            it, lo_j, hi_j, c_lo_j, c_hi_j, done = st
            span = (hi_j - lo_j).astype(F32)
            step = jnp.ceil((need - c_lo_j) / (c_hi_j - c_lo_j) * span).astype(jnp.int32)
            step = jnp.where(it % 2 == 0, step, (hi_j - lo_j) // 2)
            mid = lo_j + jnp.clip(step, 1, jnp.maximum(hi_j - lo_j - 1, 1))
            c = count(lambda s, kp: (s == thr) & (kp <= mid))
            live = (done < 0.5) & (hi_j - lo_j > 1)
            below = live & (c < need)
            above = live & (c >= need)
            lo_j = jnp.where(below, mid, lo_j)
            c_lo_j = jnp.where(below, c, c_lo_j)
            hi_j = jnp.where(above, mid, hi_j)
            c_hi_j = jnp.where(above, c, c_hi_j)
            done = jnp.where((c_hi_j == need) | (hi_j - lo_j <= 1), 1.0, done)
            return it + 1, lo_j, hi_j, c_lo_j, c_hi_j, done

        first = jnp.full((1, qn), -1, jnp.int32)
        final = jnp.full((1, qn), n_tiles * lt - 1, jnp.int32)
        n_ties = c_thr - (target - need)
        idx_done0 = jnp.where(tied & (n_ties > need), 0.0, 1.0)
        idx = lax.while_loop(idx_cond, idx_body,
                             (jnp.int32(0), first, final, 0.0 * ones, n_ties, idx_done0))
        last = idx[2]

        def drop_body(t, carry):
            s = s_ref[key_slice(t), :]
            s_ref[key_slice(t), :] = jnp.where(tied & (s == thr) & (key_pos(t) > last), -jnp.inf, s)
            return carry

        lax.fori_loop(0, n_tiles, drop_body, 0)

    qt = qt_ref[0]
    zeros = jnp.zeros((HEAD_DIM, B_GROUP * qn), BF16)
    for g in range(B_KV_HEADS):
        q_g = jnp.concatenate(
            [qt[(g * B_GROUP + j) * HEAD_DIM:(g * B_GROUP + j + 1) * HEAD_DIM, :] for j in range(B_GROUP)],
            axis=1)
        qpad_ref[g] = jnp.concatenate([zeros] * g + [q_g] + [zeros] * (B_KV_HEADS - 1 - g), axis=0)

    qf = qt.astype(F32)
    q_norm2 = jnp.sum((qf * qf).reshape(B_HEADS, HEAD_DIM, qn), axis=1)
    for g in range(B_KV_HEADS):
        b = jnp.sqrt(q_norm2[g * B_GROUP:(g + 1) * B_GROUP] * kmax_ref[0, 0:1, g:g + 1])
        b = b * BOUND_SLACK + BOUND_SLACK_ABS
        bound_ref[g] = jnp.concatenate([b[j:j + 1] for j in range(B_GROUP)], axis=1)
    acc_ref[...] = jnp.zeros(acc_ref.shape, F32)

    def fast_body(t, carry):
        keys = k_ref[0, key_slice(t), :]
        sel = jnp.where(s_ref[key_slice(t), :] >= thr, 1.0, 0.0).astype(BF16)
        sel = jnp.concatenate([sel] * B_GROUP, axis=1)
        def logits(g):
            return jnp.dot(keys, qpad_ref[g], preferred_element_type=F32)

        lg = logits(0)
        for g in range(B_KV_HEADS):
            lg_next = logits(g + 1) if g + 1 < B_KV_HEADS else None
            p = jnp.exp2(lg - bound_ref[g]).astype(BF16) * sel
            acc_ref[g] += jnp.dot(vta_ref[0, g, :, key_slice(t)], p, preferred_element_type=F32)
            lg = lg_next
        return carry

    paired_tile_loop(fast_body, 0)
    den_min = jnp.min(acc_ref[:, HEAD_DIM:HEAD_DIM + 1, :])

    @pl.when(jnp.logical_not(den_min >= MIN_TRUSTED_DENOMINATOR))
    def _():
        m_ref[...] = jnp.full(m_ref.shape, NEG_BIG, F32)
        acc_ref[...] = jnp.zeros(acc_ref.shape, F32)

        def exact_body(t, carry):
            keys = k_ref[0, key_slice(t), :]
            mb = jnp.where(s_ref[key_slice(t), :] >= thr, 0.0, -jnp.inf)
            mb = jnp.concatenate([mb] * B_GROUP, axis=1)
            for g in range(B_KV_HEADS):
                lg = jnp.dot(keys, qpad_ref[g], preferred_element_type=F32) + mb
                m_old = m_ref[g]
                m_new = jnp.maximum(m_old, jnp.max(fold_rows(lg, jnp.max), axis=0, keepdims=True))
                p = jnp.exp2(lg - m_new).astype(BF16)
                pv = jnp.dot(vta_ref[0, g, :, key_slice(t)], p, preferred_element_type=F32)
                acc_ref[g] = jnp.exp2(m_old - m_new) * acc_ref[g] + pv
                m_ref[g] = m_new
            return carry

        lax.fori_loop(0, n_tiles, exact_body, 0)

    for g in range(B_KV_HEADS):
        acc = acc_ref[g]
        o_g = acc[:HEAD_DIM] / acc[HEAD_DIM:HEAD_DIM + 1]
        for j in range(B_GROUP):
            hh = g * B_GROUP + j
            o_ref[0, hh * HEAD_DIM:(hh + 1) * HEAD_DIM, :] = o_g[:, j * qn:(j + 1) * qn].astype(o_ref.dtype)


V_AUG_ROWS = HEAD_DIM + 16


def _dsa(q, qi, kw, k, v, ki, *, q_tile, causal, k_sel):
    batch, seq, _ = q.shape
    n_keys = k.shape[1]
    lp = -(-n_keys // DSA_KEY_TILE) * DSA_KEY_TILE
    assert seq % q_tile == 0 and q_tile % LANES == 0
    pad = ((0, 0), (0, lp - n_keys), (0, 0))
    k, v, ki = (jnp.pad(a.astype(BF16), pad) for a in (k, v, ki))
    vt = jnp.swapaxes(v, 1, 2).reshape(batch, B_KV_HEADS, HEAD_DIM, lp)
    vta = jnp.concatenate([vt, jnp.ones((batch, B_KV_HEADS, 1, lp), BF16),
                           jnp.zeros((batch, B_KV_HEADS, V_AUG_ROWS - HEAD_DIM - 1, lp), BF16)], axis=2)
    qt, qit, kwt = (jnp.swapaxes(a, 1, 2) for a in (q, qi, kw))

    def qspec(rows):
        return pl.BlockSpec((1, rows, q_tile), lambda b, i: (b, 0, i))

    def resident(shape):
        zero = (0,) * len(shape)
        return pl.BlockSpec((1,) + shape, lambda b, i: (b,) + zero, pipeline_mode=pl.Buffered(1))

    ot = pl.pallas_call(
        functools.partial(_dsa_kernel, causal=causal, n_keys=n_keys, k_sel=k_sel),
        grid=(batch, seq // q_tile),
        in_specs=[qspec(B_Q), qspec(B_QI), qspec(LANES),
                  resident((lp, IDX_DIM)), resident((lp, B_KV)),
                  resident((B_KV_HEADS, V_AUG_ROWS, lp)),
                  pl.BlockSpec((1, 8, LANES), lambda b, i: (b, 0, 0))],
        out_specs=qspec(B_Q),
        out_shape=jax.ShapeDtypeStruct((batch, B_Q, seq), BF16),
        scratch_shapes=[pltpu.VMEM((lp, q_tile), F32),
                        pltpu.VMEM((B_KV_HEADS, B_KV, B_GROUP * q_tile), BF16),
                        pltpu.VMEM((B_KV_HEADS, 1, B_GROUP * q_tile), F32),
                        pltpu.VMEM((B_KV_HEADS, 1, B_GROUP * q_tile), F32),
                        pltpu.VMEM((B_KV_HEADS, V_AUG_ROWS, B_GROUP * q_tile), F32)],
        compiler_params=_params(2),
        name="dsa_prompt" if causal else "dsa_sample",
    )(qt, qit, kwt, ki, k, vta, _key_norm_max(k))
    return jnp.swapaxes(ot, 1, 2)


def kernel(x_prompt, x_sample, cache_k_a, cache_v_a, cache_k_b, cache_v_b, cache_kidx_b,
           c_prompt, c_sample, w_cond, b_cond, ln_g, ln_b, a_w_in, a_w_o, a_rel_bias,
           b_w_in, b_w_o, b_kidx_ln_g, b_kidx_ln_b, ffn_w_gu, ffn_w_down,
           moe_w_router, moe_b_router, moe_w_gu, moe_w_down):
    depth = w_cond.shape[0]
    alpha = (2 * depth) ** 0.25
    n_p, seq, d = x_prompt.shape
    n_s, dec_seq, _ = x_sample.shape
    past = cache_k_b.shape[2]
    a_hd = A_HEADS * HEAD_DIM

    rows = n_p + n_s
    rows_pad = -(-rows // 8) * 8
    c_all = jnp.pad(jnp.concatenate([c_prompt, c_sample], axis=0), ((0, rows_pad - rows), (0, 0)))
    mod_all = _modulation(c_all, w_cond, b_cond)

    xp, xs = x_prompt, x_sample
    outs = {k: [] for k in ("ka_p", "va_p", "kb_p", "vb_p", "ib_p", "ka_s", "va_s", "kb_s", "vb_s", "ib_s")}
    for i in range(depth):
        j = i // 2
        mod_p = mod_all[i, :n_p].reshape(n_p, 1, 6 * d)
        mod_s = mod_all[i, n_p:rows].reshape(n_s, 1, 6 * d)
        g1, b1, g2, b2 = ln_g[i, 0], ln_b[i, 0], ln_g[i, 1], ln_b[i, 1]
        if i % 2 == 0:
            w_in = a_w_in[j].astype(BF16)
            keep = min(BAND_ROWS, seq)
            qkv_p = _modmm(xp, mod_p, w_in, BF16)
            kv_p = _modmm(xp, mod_p, w_in[:, a_hd:], F32, t_start=seq - keep)
            qkv_s = _modmm(xs, mod_s, w_in, BF16)
            kv_s = _modmm(xs, mod_s, w_in[:, a_hd:], F32)
            outs["ka_p"].append(kv_p[..., :a_hd].reshape(n_p, keep, A_HEADS, HEAD_DIM))
            outs["va_p"].append(kv_p[..., a_hd:].reshape(n_p, keep, A_HEADS, HEAD_DIM))
            outs["ka_s"].append(kv_s[..., :a_hd].reshape(n_s, dec_seq, A_HEADS, HEAD_DIM))
            outs["va_s"].append(kv_s[..., a_hd:].reshape(n_s, dec_seq, A_HEADS, HEAD_DIM))
            o_p = _band_attn_prompt(qkv_p, a_rel_bias[j])
            ck = cache_k_a[j].reshape(n_s, -1, a_hd).astype(BF16)
            cv = cache_v_a[j].reshape(n_s, -1, a_hd).astype(BF16)
            o_s = _band_attn_sample(qkv_s, ck, cv, a_rel_bias[j])
            w_o = a_w_o[j].astype(BF16)
        else:
            w_in = jnp.pad(b_w_in[j], ((0, 0), (0, B_PROJ_PAD - B_PROJ))).astype(BF16)
            kn_g = jnp.pad(b_kidx_ln_g[j], (0, LANES - IDX_DIM)).reshape(1, LANES)
            kn_b = jnp.pad(b_kidx_ln_b[j], (0, LANES - IDX_DIM)).reshape(1, LANES)
            cos_p, sin_p = _rope_tables(jnp.arange(seq))
            cos_s, sin_s = _rope_tables(past + jnp.arange(dec_seq))
            bb_s, _ = _row_blocks(n_s, dec_seq, ROW_TILE)
            cos_s, sin_s = jnp.tile(cos_s, (bb_s, 1)), jnp.tile(sin_s, (bb_s, 1))
            q_p, qi_p, k_p, v_p, kw_p = _b_project(xp, mod_p, w_in, cos_p, sin_p, kn_g, kn_b, False)
            q_s, qi_s, k_s, v_s, kw_s = _b_project(xs, mod_s, w_in, cos_s, sin_s, kn_g, kn_b, True)
            outs["kb_p"].append(k_p.reshape(n_p, seq, B_KV_HEADS, HEAD_DIM))
            outs["vb_p"].append(v_p.reshape(n_p, seq, B_KV_HEADS, HEAD_DIM))
            outs["ib_p"].append(kw_p[..., :IDX_DIM])
            outs["kb_s"].append(k_s.reshape(n_s, dec_seq, B_KV_HEADS, HEAD_DIM))
            outs["vb_s"].append(v_s.reshape(n_s, dec_seq, B_KV_HEADS, HEAD_DIM))
            outs["ib_s"].append(kw_s[..., :IDX_DIM])
            o_p = _dsa(q_p, qi_p, kw_p, k_p, v_p, kw_p[..., :IDX_DIM], q_tile=DSA_Q_TILE, causal=True,
                       k_sel=min(TOPK_MAX, seq // 4))
            n_keys = past + dec_seq
            kk = jnp.concatenate([cache_k_b[j].reshape(n_s, past, B_KV), k_s], axis=1)
            vc = jnp.concatenate([cache_v_b[j].reshape(n_s, past, B_KV), v_s], axis=1)
            kki = jnp.concatenate([cache_kidx_b[j], kw_s[..., :IDX_DIM]], axis=1)
            rep = LANES // dec_seq
            q_r, qi_r, kw_r = (jnp.concatenate([a] * rep, axis=1) for a in (q_s, qi_s, kw_s))
            o_s = _dsa(q_r, qi_r, kw_r, kk, vc, kki, q_tile=LANES, causal=False,
                       k_sel=min(TOPK_MAX, n_keys // 4))[:, :dec_seq]
            w_o = b_w_o[j].astype(BF16)
        xp = _mm_postnorm(o_p, w_o, xp, mod_p, 2, g1, b1, alpha)
        xs = _mm_postnorm(o_s, w_o, xs, mod_s, 2, g1, b1, alpha)
        if i % 2 == 0:
            w_gu = ffn_w_gu[j].astype(BF16)
            w_dn = ffn_w_down[j].astype(BF16)
            dense = dict(ff_chunk=D_FF // 2, alpha=alpha)
            xp = _dense_ffn(xp, mod_p, w_gu, w_dn, g2, b2, **dense)
            xs = _dense_ffn(xs, mod_s, w_gu, w_dn, g2, b2, **dense)
        else:
            fc = MOE_FF_CHUNK
            w_gu = moe_w_gu[j].astype(BF16).reshape(N_EXPERTS, d, 2 * D_FF_EXPERT // fc, fc)
            w_gu = jnp.swapaxes(w_gu, 1, 2)
            w_dn = moe_w_down[j].astype(BF16)
            w_r = jnp.pad(moe_w_router[j], ((0, 0), (0, LANES - N_EXPERTS)))
            b_r = jnp.pad(moe_b_router[j], (0, LANES - N_EXPERTS)).reshape(1, LANES)
            moe = dict(ff_chunk=fc, alpha=alpha)
            xp = _moe(xp, mod_p, w_r, b_r, w_gu, w_dn, g2, b2, **moe)
            xs = _moe(xs, mod_s, w_r, b_r, w_gu, w_dn, g2, b2, **moe)

    st = lambda name: jnp.stack(outs[name])
    return (xp, xs, st("ka_p"), st("va_p"), st("kb_p"), st("vb_p"), st("ib_p"),
            st("ka_s"), st("va_s"), st("kb_s"), st("vb_s"), st("ib_s"))
```

```python
import functools

import jax
import jax.numpy as jnp
from jax import lax
from jax.experimental import pallas as pl
from jax.experimental.pallas import tpu as pltpu

F32 = jnp.float32
BF16 = jnp.bfloat16

D_MODEL = 1024
CHUNK = 64
N_PAST_CHUNKS = 8
BAND_ROWS = N_PAST_CHUNKS * CHUNK
REL_CLIP = 2 * CHUNK
HEAD_DIM = 64
A_HEADS = 16
B_HEADS = 16
B_KV_HEADS = 4
B_GROUP = B_HEADS // B_KV_HEADS
IDX_HEADS = 8
IDX_DIM = 64
TOPK_MAX = 256
D_FF = 2816
N_EXPERTS = 8
D_FF_EXPERT = 3584
ROPE_THETA = 10000.0
LN_EPS = 1e-5
B_Q = B_HEADS * HEAD_DIM
B_KV = B_KV_HEADS * HEAD_DIM
B_QI = IDX_HEADS * IDX_DIM
B_PROJ = B_Q + 2 * B_KV + B_QI + IDX_DIM + IDX_HEADS

LANES = 128
VMEM_LIMIT_BYTES = 58 * 1024 * 1024

A_Q_TILE = 4 * CHUNK
DSA_Q_TILE = 4 * CHUNK
DSA_KEY_TILE = 512
ROW_TILE = 512
FFN_ROW_TILE = 1024
MOE_CAPACITY = 288
MOE_FF_CHUNK = 1792
B_PROJ_PAD = B_Q + 2 * B_KV + B_QI + LANES
NEG_BIG = -1e30
QK_SCALE_LOG2 = HEAD_DIM ** -0.5 * 1.4426950408889634
BOUND_SLACK = 1.02
BOUND_SLACK_ABS = 0.01
MIN_TRUSTED_DENOMINATOR = 2.0 ** -100
SECANT_STEPS = 12


def _params(n_grid):
    return pltpu.CompilerParams(
        dimension_semantics=("arbitrary",) * n_grid,
        vmem_limit_bytes=VMEM_LIMIT_BYTES,
    )


def _row_blocks(batch, seq, target):
    if seq >= target:
        assert seq % target == 0
        return 1, target
    bb = max(1, min(batch, target // seq))
    while batch % bb:
        bb -= 1
    return bb, seq


def _mod_spec(bb, chunk, n_grid):
    if n_grid == 2:
        return pl.BlockSpec((bb, 1, D_MODEL), lambda b, t: (b, 0, chunk))
    return pl.BlockSpec((bb, 1, D_MODEL), lambda b, t, e, c: (b, 0, chunk))


def _silu(x):
    return x / (1.0 + jnp.exp(-x))


def _layer_norm_rows(z, g, b):
    mu = jnp.mean(z, axis=-1, keepdims=True)
    zc = z - mu
    var = jnp.mean(zc * zc, axis=-1, keepdims=True)
    return zc * lax.rsqrt(var + LN_EPS) * g + b


def _modulation_kernel(c_ref, w_ref, b_ref, o_ref):
    a = _silu(c_ref[...]).astype(BF16)
    w = w_ref[0].astype(BF16)
    o_ref[0] = jnp.dot(a, w, preferred_element_type=F32) + b_ref[0]


def _modulation(c_all, w_cond, b_cond):
    depth, d, n = w_cond.shape
    rows = c_all.shape[0]
    tn = 1536
    return pl.pallas_call(
        _modulation_kernel,
        grid=(depth, n // tn),
        in_specs=[
            pl.BlockSpec((rows, d), lambda i, j: (0, 0)),
            pl.BlockSpec((1, d, tn), lambda i, j: (i, 0, j)),
            pl.BlockSpec((1, 1, tn), lambda i, j: (i, 0, j)),
        ],
        out_specs=pl.BlockSpec((1, rows, tn), lambda i, j: (i, 0, j)),
        out_shape=jax.ShapeDtypeStruct((depth, rows, n), F32),
        compiler_params=_params(2),
        name="modulation",
    )(c_all, w_cond, b_cond.reshape(depth, 1, n))


def _modmm_kernel(x_ref, sc_ref, sh_ref, w_ref, o_ref):
    bb, tt, d = x_ref.shape
    h = x_ref[...] * (1.0 + sc_ref[...]) + sh_ref[...]
    h = h.reshape(bb * tt, d).astype(BF16)
    acc = jnp.dot(h, w_ref[...], preferred_element_type=F32)
    o_ref[...] = acc.reshape(bb, tt, acc.shape[-1]).astype(o_ref.dtype)


def _modmm(x, mod, w, out_dtype, t_start=0):
    batch, seq, d = x.shape
    n = w.shape[1]
    seq_out = seq - t_start
    bb, tt = _row_blocks(batch, seq_out, ROW_TILE)
    assert t_start % tt == 0
    off = t_start // tt
    return pl.pallas_call(
        _modmm_kernel,
        grid=(batch // bb, seq_out // tt),
        in_specs=[
            pl.BlockSpec((bb, tt, d), lambda b, t: (b, t + off, 0)),
            _mod_spec(bb, 1, 2),
            _mod_spec(bb, 0, 2),
            pl.BlockSpec((d, n), lambda b, t: (0, 0)),
        ],
        out_specs=pl.BlockSpec((bb, tt, n), lambda b, t: (b, t, 0)),
        out_shape=jax.ShapeDtypeStruct((batch, seq_out, n), out_dtype),
        compiler_params=_params(2),
        name="a_project",
    )(x, mod, mod, w)


def _band_attn_kernel(*refs, n_kb, n_maybe_invalid):
    q_ref = refs[0]
    k_refs = refs[1:1 + n_kb]
    v_refs = refs[1 + n_kb:1 + 2 * n_kb]
    f_ref = refs[1 + 2 * n_kb]
    o_ref = refs[2 + 2 * n_kb]
    bias_ref = refs[3 + 2 * n_kb]
    q_tile = q_ref.shape[1]
    kb_sizes = [r.shape[1] for r in k_refs]
    k_tot = sum(kb_sizes)
    width = f_ref.shape[-1]
    i = pl.program_id(1)

    @pl.when((pl.program_id(0) == 0) & (i == 0))
    def _():
        rq = lax.broadcasted_iota(jnp.int32, (q_tile, k_tot), 0) // CHUNK
        ck = lax.broadcasted_iota(jnp.int32, (q_tile, k_tot), 1) // CHUNK
        in_band = (ck >= rq) & (ck <= rq + N_PAST_CHUNKS)
        for h in range(A_HEADS):
            rows = jnp.broadcast_to(f_ref[h], (q_tile, width))
            toep = pltpu.roll(rows, k_tot + 1, 1, stride=1, stride_axis=0)
            bias_ref[h] = jnp.where(in_band, toep[:, :k_tot], -jnp.inf)

    q = q_ref[0] * (HEAD_DIM ** -0.5)
    for h in range(A_HEADS):
        cols = slice(h * HEAD_DIM, (h + 1) * HEAD_DIM)
        qh = q[:, cols]
        parts = []
        for kb in range(n_kb):
            lg = lax.dot_general(qh, k_refs[kb][0, :, cols], (((1,), (1,)), ((), ())),
                                 preferred_element_type=F32)
            if kb < n_maybe_invalid:
                lg = jnp.where(i - n_maybe_invalid + kb >= 0, lg, -jnp.inf)
            parts.append(lg)
        logits = jnp.concatenate(parts, axis=1) + bias_ref[h]
        m = jnp.max(logits, axis=1, keepdims=True)
        p = jnp.exp(logits - m)
        l = jnp.sum(p, axis=1, keepdims=True)
        pb = p.astype(BF16)
        acc = jnp.zeros((q_tile, HEAD_DIM), F32)
        start = 0
        for kb in range(n_kb):
            acc = acc + jnp.dot(pb[:, start:start + kb_sizes[kb]], v_refs[kb][0, :, cols],
                                preferred_element_type=F32)
            start += kb_sizes[kb]
        o_ref[0, :, cols] = (acc / l).astype(o_ref.dtype)


def _bias_vector(table, q_tile, k_tot):
    width = k_tot + q_tile
    lo = -(q_tile - 1) - BAND_ROWS + REL_CLIP
    left = max(0, -lo)
    start = max(0, lo)
    n_mid = min(2 * REL_CLIP + 1 - start, width - left)
    right = width - left - n_mid
    mid = table.T[:, start:start + n_mid]
    f = jnp.pad(mid, ((0, 0), (left, right)), mode="edge")
    return f.reshape(A_HEADS, 1, width)


def _band_attn_prompt(qkv, table):
    batch, seq, _ = qkv.shape
    hd = A_HEADS * HEAD_DIM
    qt = A_Q_TILE
    n_prev = BAND_ROWS // qt
    n_kb = n_prev + 1
    k_tot = n_kb * qt
    f = _bias_vector(table, qt, k_tot)

    def kv_spec(kb, col):
        return pl.BlockSpec((1, qt, hd), lambda b, i: (b, jnp.maximum(i - n_prev + kb, 0), col))

    return pl.pallas_call(
        functools.partial(_band_attn_kernel, n_kb=n_kb, n_maybe_invalid=n_prev),
        grid=(batch, seq // qt),
        in_specs=[pl.BlockSpec((1, qt, hd), lambda b, i: (b, i, 0))]
        + [kv_spec(kb, 1) for kb in range(n_kb)]
        + [kv_spec(kb, 2) for kb in range(n_kb)]
        + [pl.BlockSpec(f.shape, lambda b, i: (0, 0, 0))],
        out_specs=pl.BlockSpec((1, qt, hd), lambda b, i: (b, i, 0)),
        out_shape=jax.ShapeDtypeStruct((batch, seq, hd), BF16),
        scratch_shapes=[pltpu.VMEM((A_HEADS, qt, k_tot), F32)],
        compiler_params=_params(2),
        name="band_attn_prompt",
    )(*([qkv] * (1 + 2 * n_kb)), f)


def _band_attn_sample(qkv, cache_k, cache_v, table):
    batch, seq, _ = qkv.shape
    hd = A_HEADS * HEAD_DIM
    win = cache_k.shape[1]
    assert seq == CHUNK and win == BAND_ROWS
    k_tot = win + seq
    f = _bias_vector(table, seq, k_tot)
    new = lambda col: pl.BlockSpec((1, seq, hd), lambda b, i: (b, 0, col))
    old = pl.BlockSpec((1, win, hd), lambda b, i: (b, 0, 0))
    return pl.pallas_call(
        functools.partial(_band_attn_kernel, n_kb=2, n_maybe_invalid=0),
        grid=(batch, 1),
        in_specs=[new(0), old, new(1), old, new(2), pl.BlockSpec(f.shape, lambda b, i: (0, 0, 0))],
        out_specs=pl.BlockSpec((1, seq, hd), lambda b, i: (b, 0, 0)),
        out_shape=jax.ShapeDtypeStruct((batch, seq, hd), BF16),
        scratch_shapes=[pltpu.VMEM((A_HEADS, seq, k_tot), F32)],
        compiler_params=_params(2),
        name="band_attn_sample",
    )(qkv, cache_k, qkv, cache_v, qkv, f)


def _mm_postnorm_kernel(o_ref, w_ref, x_ref, gate_ref, g_ref, b_ref, y_ref, *, alpha):
    bb, tt, d = x_ref.shape
    o = o_ref[...].reshape(bb * tt, o_ref.shape[-1])
    sub = jnp.dot(o, w_ref[...], preferred_element_type=F32).reshape(bb, tt, d)
    z = alpha * x_ref[...] + (1.0 + gate_ref[...]) * sub
    y_ref[...] = _layer_norm_rows(z, g_ref[...], b_ref[...])


def _mm_postnorm(o, w, x, mod, gate_chunk, g, b, alpha):
    batch, seq, d = x.shape
    bb, tt = _row_blocks(batch, seq, ROW_TILE)
    k = o.shape[-1]
    return pl.pallas_call(
        functools.partial(_mm_postnorm_kernel, alpha=alpha),
        grid=(batch // bb, seq // tt),
        in_specs=[
            pl.BlockSpec((bb, tt, k), lambda b_, t: (b_, t, 0)),
            pl.BlockSpec((k, d), lambda b_, t: (0, 0)),
            pl.BlockSpec((bb, tt, d), lambda b_, t: (b_, t, 0)),
            _mod_spec(bb, gate_chunk, 2),
            pl.BlockSpec((1, 1, d), lambda b_, t: (0, 0, 0)),
            pl.BlockSpec((1, 1, d), lambda b_, t: (0, 0, 0)),
        ],
        out_specs=pl.BlockSpec((bb, tt, d), lambda b_, t: (b_, t, 0)),
        out_shape=jax.ShapeDtypeStruct((batch, seq, d), F32),
        compiler_params=_params(2),
        name="out_proj_postnorm",
    )(o, w, x, mod, g.reshape(1, 1, d), b.reshape(1, 1, d))


def _dense_ffn_kernel(x_ref, sc_ref, sh_ref, gate_ref, wg_ref, wu_ref, wd_ref, g_ref, b_ref,
                      y_ref, h_ref, acc_ref, *, alpha):
    bb, tt, d = x_ref.shape
    rows = bb * tt
    c = pl.program_id(2)

    @pl.when(c == 0)
    def _():
        h = (x_ref[...] * (1.0 + sc_ref[...]) + sh_ref[...]).reshape(rows, d)
        h_ref[...] = h.astype(BF16)
        acc_ref[...] = jnp.zeros_like(acc_ref)

    h = h_ref[...]
    gt = jnp.dot(h, wg_ref[...], preferred_element_type=F32)
    up = jnp.dot(h, wu_ref[...], preferred_element_type=F32)
    act = (_silu(gt) * up).astype(BF16)
    acc_ref[...] += jnp.dot(act, wd_ref[...], preferred_element_type=F32)

    @pl.when(c == pl.num_programs(2) - 1)
    def _():
        z = alpha * x_ref[...] + (1.0 + gate_ref[...]) * acc_ref[...].reshape(bb, tt, d)
        y_ref[...] = _layer_norm_rows(z, g_ref[...], b_ref[...])


def _dense_ffn(x, mod, w_gu, w_down, g, b, *, ff_chunk, alpha):
    batch, seq, d = x.shape
    ff = w_down.shape[0]
    assert ff % ff_chunk == 0
    n_c = ff // ff_chunk
    bb, tt = _row_blocks(batch, seq, FFN_ROW_TILE)
    rows = bb * tt
    vec = pl.BlockSpec((1, 1, d), lambda b_, t, c: (0, 0, 0))

    def mod_spec(chunk):
        return pl.BlockSpec((bb, 1, d), lambda b_, t, c: (b_, 0, chunk))

    return pl.pallas_call(
        functools.partial(_dense_ffn_kernel, alpha=alpha),
        grid=(batch // bb, seq // tt, n_c),
        in_specs=[
            pl.BlockSpec((bb, tt, d), lambda b_, t, c: (b_, t, 0)),
            mod_spec(4),
            mod_spec(3),
            mod_spec(5),
            pl.BlockSpec((d, ff_chunk), lambda b_, t, c: (0, c)),
            pl.BlockSpec((d, ff_chunk), lambda b_, t, c: (0, n_c + c)),
            pl.BlockSpec((ff_chunk, d), lambda b_, t, c: (c, 0)),
            vec,
            vec,
        ],
        out_specs=pl.BlockSpec((bb, tt, d), lambda b_, t, c: (b_, t, 0)),
        out_shape=jax.ShapeDtypeStruct((batch, seq, d), F32),
        scratch_shapes=[pltpu.VMEM((rows, d), BF16), pltpu.VMEM((rows, d), F32)],
        compiler_params=_params(3),
        name="dense_ffn",
    )(x, mod, mod, mod, w_gu, w_gu, w_down, g.reshape(1, 1, d), b.reshape(1, 1, d))


def _moe_kernel(x_ref, sc_ref, sh_ref, gate_ref, wr_ref, br_ref, wg_ref, wu_ref, wd_ref, g_ref, b_ref,
                y_ref, h_ref, comb_ref, slot_ref, cnt_ref, tri_ref, hc_ref, acc_ref, out_ref, *, alpha):
    bb, tt, d = x_ref.shape
    rows = bb * tt
    cap = hc_ref.shape[1]
    e = pl.program_id(2)
    c = pl.program_id(3)
    last_c = pl.num_programs(3) - 1
    lane = lax.broadcasted_iota(jnp.int32, (rows, LANES), 1)

    @pl.when((pl.program_id(0) == 0) & (pl.program_id(1) == 0) & (e == 0) & (c == 0))
    def _():
        r = lax.broadcasted_iota(jnp.int32, (rows, rows), 0)
        col = lax.broadcasted_iota(jnp.int32, (rows, rows), 1)
        tri_ref[...] = jnp.where(r <= col, 1.0, 0.0).astype(BF16)

    @pl.when((e == 0) & (c == 0))
    def _():
        h = (x_ref[...] * (1.0 + sc_ref[...]) + sh_ref[...]).reshape(rows, d)
        h_ref[...] = h.astype(BF16)
        out_ref[...] = jnp.zeros_like(out_ref)
        logits = jnp.dot(h, wr_ref[...], preferred_element_type=F32,
                         precision=lax.Precision.HIGHEST) + br_ref[...]
        logits = jnp.where(lane < N_EXPERTS, logits, -jnp.inf)
        m1 = jnp.max(logits, axis=1, keepdims=True)
        i1 = jnp.min(jnp.where(logits == m1, lane, LANES), axis=1, keepdims=True)
        rest = jnp.where(lane == i1, -jnp.inf, logits)
        m2 = jnp.max(rest, axis=1, keepdims=True)
        i2 = jnp.min(jnp.where(rest == m2, lane, LANES), axis=1, keepdims=True)
        e2 = jnp.exp(m2 - m1)
        den = 1.0 + e2
        comb_ref[...] = jnp.where(lane == i1, 1.0 / den, 0.0) + jnp.where(lane == i2, e2 / den, 0.0)
        member = jnp.where((lane == i1) | (lane == i2), 1.0, 0.0)
        member_t = member.T[:2 * N_EXPERTS]
        upto = jnp.dot(member_t.astype(BF16), tri_ref[...], preferred_element_type=F32)
        slot_ref[...] = jnp.where(member_t > 0, upto - 1.0, -1.0)
        cnt_ref[...] = jnp.broadcast_to(upto[:, rows - 1:rows], cnt_ref.shape)

    n_sub = (jnp.max(cnt_ref[pl.ds(e, 1), :]).astype(jnp.int32) + cap - 1) // cap

    def selection(s):
        slot = slot_ref[pl.ds(e, 1), :] - (s * cap).astype(F32)
        r = lax.broadcasted_iota(jnp.int32, (cap, rows), 0).astype(F32)
        return jnp.where(slot == r, 1.0, 0.0).astype(BF16)

    @pl.when(c == 0)
    def _():
        def pack(s, carry):
            hc_ref[s] = jnp.dot(selection(s), h_ref[...], preferred_element_type=F32).astype(BF16)
            acc_ref[s] = jnp.zeros(acc_ref.shape[1:], F32)
            return carry
        lax.fori_loop(0, n_sub, pack, 0)

    def expert(s, carry):
        hc = hc_ref[s]
        gt = jnp.dot(hc, wg_ref[0], preferred_element_type=F32)
        up = jnp.dot(hc, wu_ref[0], preferred_element_type=F32)
        act = (_silu(gt) * up).astype(BF16)
        acc_ref[s] += jnp.dot(act, wd_ref[0], preferred_element_type=F32)
        return carry

    lax.fori_loop(0, n_sub, expert, 0)

    @pl.when(c == last_c)
    def _():
        ce = jnp.sum(jnp.where(lane == e, comb_ref[...], 0.0), axis=1, keepdims=True)

        def unpack(s, carry):
            back = lax.dot_general(selection(s), acc_ref[s].astype(BF16), (((0,), (0,)), ((), ())),
                                   preferred_element_type=F32)
            out_ref[...] += ce * back
            return carry
        lax.fori_loop(0, n_sub, unpack, 0)

    @pl.when((e == pl.num_programs(2) - 1) & (c == last_c))
    def _():
        z = alpha * x_ref[...] + (1.0 + gate_ref[...]) * out_ref[...].reshape(bb, tt, d)
        y_ref[...] = _layer_norm_rows(z, g_ref[...], b_ref[...])


def _moe(x, mod, w_r, b_r, w_gu, w_down, g, b, *, ff_chunk, alpha):
    batch, seq, d = x.shape
    n_e, ff, _ = w_down.shape
    assert ff % ff_chunk == 0
    n_c = ff // ff_chunk
    bb, tt = _row_blocks(batch, seq, FFN_ROW_TILE)
    rows = bb * tt
    cap = MOE_CAPACITY
    max_sub = -(-rows // cap)
    vec = pl.BlockSpec((1, 1, d), lambda b_, t, e, c: (0, 0, 0))
    return pl.pallas_call(
        functools.partial(_moe_kernel, alpha=alpha),
        grid=(batch // bb, seq // tt, n_e, n_c),
        in_specs=[
            pl.BlockSpec((bb, tt, d), lambda b_, t, e, c: (b_, t, 0), pipeline_mode=pl.Buffered(1)),
            _mod_spec(bb, 4, 4),
            _mod_spec(bb, 3, 4),
            _mod_spec(bb, 5, 4),
            pl.BlockSpec((d, LANES), lambda b_, t, e, c: (0, 0)),
            pl.BlockSpec((1, LANES), lambda b_, t, e, c: (0, 0)),
            pl.BlockSpec((1, d, ff_chunk), lambda b_, t, e, c: (e, 0, c)),
            pl.BlockSpec((1, d, ff_chunk), lambda b_, t, e, c: (e, 0, n_c + c)),
            pl.BlockSpec((1, ff_chunk, d), lambda b_, t, e, c: (e, c, 0)),
            vec,
            vec,
        ],
        out_specs=pl.BlockSpec((bb, tt, d), lambda b_, t, e, c: (b_, t, 0)),
        out_shape=jax.ShapeDtypeStruct((batch, seq, d), F32),
        scratch_shapes=[
            pltpu.VMEM((rows, d), BF16),
            pltpu.VMEM((rows, LANES), F32),
            pltpu.VMEM((2 * N_EXPERTS, rows), F32),
            pltpu.VMEM((2 * N_EXPERTS, LANES), F32),
            pltpu.VMEM((rows, rows), BF16),
            pltpu.VMEM((max_sub, cap, d), BF16),
            pltpu.VMEM((max_sub, cap, d), F32),
            pltpu.VMEM((rows, d), F32),
        ],
        compiler_params=_params(4),
        name="moe_ffn",
    )(x, mod, mod, mod, w_r, b_r, w_gu, w_gu, w_down, g.reshape(1, 1, d), b.reshape(1, 1, d))


def _rope_lanes(seg, cos, sin_signed, first_half):
    width = seg.shape[-1]
    reps = width // LANES
    if reps > 1:
        cos = jnp.concatenate([cos] * reps, axis=1)
        sin_signed = jnp.concatenate([sin_signed] * reps, axis=1)
        first_half = jnp.concatenate([first_half] * reps, axis=1)
    half = HEAD_DIM // 2
    swapped = jnp.where(first_half, pltpu.roll(seg, width - half, 1), pltpu.roll(seg, half, 1))
    return seg * cos + swapped * sin_signed


def _b_project_kernel(x_ref, sc_ref, sh_ref, w_ref, cos_ref, sin_ref, kg_ref, kb_ref,
                      q_ref, qi_ref, k_ref, v_ref, kw_ref):
    bb, tt, d = x_ref.shape
    rows = bb * tt
    h = x_ref[...] * (1.0 + sc_ref[...]) + sh_ref[...]
    h = h.reshape(rows, d).astype(BF16)
    acc = jnp.dot(h, w_ref[...], preferred_element_type=F32)
    cos = cos_ref[...]
    sin_signed = sin_ref[...]
    lane = lax.broadcasted_iota(jnp.int32, (1, LANES), 1)
    first_half = (lane % HEAD_DIM) < (HEAD_DIM // 2)

    def out(ref, val):
        ref[...] = val.reshape(bb, tt, val.shape[-1]).astype(ref.dtype)

    o_k, o_v, o_qi, o_ki = B_Q, B_Q + B_KV, B_Q + 2 * B_KV, B_Q + 2 * B_KV + B_QI
    out(q_ref, _rope_lanes(acc[:, :o_k], cos, sin_signed, first_half) * QK_SCALE_LOG2)
    out(k_ref, _rope_lanes(acc[:, o_k:o_v], cos, sin_signed, first_half))
    out(v_ref, acc[:, o_v:o_qi])
    out(qi_ref, _rope_lanes(acc[:, o_qi:o_ki], cos, sin_signed, first_half))
    seg = acc[:, o_ki:]
    is_ki = lane < IDX_DIM
    mu = jnp.sum(jnp.where(is_ki, seg, 0.0), axis=1, keepdims=True) / IDX_DIM
    cen = jnp.where(is_ki, seg - mu, 0.0)
    var = jnp.sum(cen * cen, axis=1, keepdims=True) / IDX_DIM
    ki = cen * lax.rsqrt(var + LN_EPS) * kg_ref[...] + kb_ref[...]
    ki = _rope_lanes(ki, cos, sin_signed, first_half)
    out(kw_ref, jnp.where(is_ki, ki, seg * (IDX_HEADS ** -0.5)))


def _b_project(x, mod, w, cos, sin_signed, kn_g, kn_b, table_per_tile):
    batch, seq, d = x.shape
    bb, tt = _row_blocks(batch, seq, ROW_TILE)
    rows = bb * tt
    n = w.shape[1]
    tab = pl.BlockSpec((rows, LANES), (lambda b, t: (0, 0)) if table_per_tile else (lambda b, t: (t, 0)))
    vec = pl.BlockSpec((1, LANES), lambda b, t: (0, 0))

    def o_spec(width):
        return pl.BlockSpec((bb, tt, width), lambda b, t: (b, t, 0))

    def o_shape(width, dtype):
        return jax.ShapeDtypeStruct((batch, seq, width), dtype)

    return pl.pallas_call(
        _b_project_kernel,
        grid=(batch // bb, seq // tt),
        in_specs=[
            pl.BlockSpec((bb, tt, d), lambda b, t: (b, t, 0)),
            _mod_spec(bb, 1, 2),
            _mod_spec(bb, 0, 2),
            pl.BlockSpec((d, n), lambda b, t: (0, 0)),
            tab,
            tab,
            vec,
            vec,
        ],
        out_specs=[o_spec(B_Q), o_spec(B_QI), o_spec(B_KV), o_spec(B_KV), o_spec(LANES)],
        out_shape=[o_shape(B_Q, BF16), o_shape(B_QI, BF16), o_shape(B_KV, F32), o_shape(B_KV, F32),
                   o_shape(LANES, F32)],
        compiler_params=_params(2),
        name="b_project",
    )(x, mod, mod, w, cos, sin_signed, kn_g, kn_b)


def _rope_tables(pos):
    half = HEAD_DIM // 2
    inv = ROPE_THETA ** (-jnp.arange(half, dtype=F32) / half)
    ang = pos.astype(F32)[:, None] * inv[None, :]
    cos, sin = jnp.cos(ang), jnp.sin(ang)
    return (jnp.concatenate([cos, cos, cos, cos], axis=1),
            jnp.concatenate([-sin, sin, -sin, sin], axis=1))


def _key_norm_kernel(k_ref, grp_ref, o_ref):
    k = k_ref[0].astype(F32)
    n2 = jnp.dot((k * k).astype(BF16), grp_ref[...], preferred_element_type=F32)
    mx = jnp.broadcast_to(jnp.max(n2, axis=0, keepdims=True), o_ref.shape[1:])

    @pl.when(pl.program_id(1) == 0)
    def _():
        o_ref[0] = mx

    @pl.when(pl.program_id(1) > 0)
    def _():
        o_ref[0] = jnp.maximum(o_ref[0], mx)


def _key_norm_max(k):
    batch, lp, width = k.shape
    tile = DSA_KEY_TILE
    grp = (jnp.arange(width)[:, None] // HEAD_DIM == jnp.arange(LANES)[None, :]).astype(BF16)
    return pl.pallas_call(
        _key_norm_kernel,
        grid=(batch, lp // tile),
        in_specs=[pl.BlockSpec((1, tile, width), lambda b, t: (b, t, 0)),
                  pl.BlockSpec((width, LANES), lambda b, t: (0, 0))],
        out_specs=pl.BlockSpec((1, 8, LANES), lambda b, t: (b, 0, 0)),
        out_shape=jax.ShapeDtypeStruct((batch, 8, LANES), F32),
        compiler_params=_params(2),
        name="key_norm_max",
    )(k, grp)


def _dsa_kernel(qt_ref, qit_ref, kwt_ref, ki_ref, k_ref, vta_ref, kmax_ref, o_ref,
                s_ref, qpad_ref, bound_ref, m_ref, acc_ref, *, causal, n_keys, k_sel):
    qn = qt_ref.shape[2]
    lt = DSA_KEY_TILE
    i = pl.program_id(1)
    if causal:
        n_tiles = ((i + 1) * qn + lt - 1) // lt
        qpos = i * qn + lax.broadcasted_iota(jnp.int32, (1, qn), 1)
        limit = (qpos // CHUNK + 1) * CHUNK
    else:
        n_tiles = ki_ref.shape[1] // lt
        limit = jnp.full((1, qn), n_keys, jnp.int32)

    def key_slice(t):
        return pl.ds(pl.multiple_of(t * lt, lt), lt)

    def key_pos(t):
        return t * lt + lax.broadcasted_iota(jnp.int32, (lt, 1), 0)

    def fold_rows(x, op):
        return op(op(x.reshape(lt // 32, 4, 8, x.shape[-1]), axis=0), axis=0)

    def paired_tile_loop(body, init):
        def pair(t2, carry):
            return body(2 * t2 + 1, body(2 * t2, carry))
        carry = lax.fori_loop(0, n_tiles // 2, pair, init)
        return lax.cond(n_tiles % 2 == 1, lambda c: body(n_tiles - 1, c), lambda c: c, carry)

    qit = qit_ref[0]
    kwt = kwt_ref[0]
    qi_w = jnp.concatenate([qit[h * IDX_DIM:(h + 1) * IDX_DIM, :] for h in range(IDX_HEADS)], axis=1)
    w_row = jnp.concatenate([kwt[IDX_DIM + h:IDX_DIM + h + 1, :] for h in range(IDX_HEADS)], axis=1)
    w_row = w_row * (IDX_DIM ** -0.5)

    def score_body(t, carry):
        rmax, rmin = carry
        s = jnp.dot(ki_ref[0, key_slice(t), :], qi_w, preferred_element_type=F32)
        s = jnp.maximum(s, 0.0) * w_row
        sc = s[:, 0:qn]
        for h in range(1, IDX_HEADS):
            sc = sc + s[:, h * qn:(h + 1) * qn]
        adm = key_pos(t) < limit
        masked = jnp.where(adm, sc, -jnp.inf)
        s_ref[key_slice(t), :] = masked
        rmax = jnp.maximum(rmax, jnp.max(masked, axis=0, keepdims=True))
        rmin = jnp.minimum(rmin, jnp.min(jnp.where(adm, sc, jnp.inf), axis=0, keepdims=True))
        return rmax, rmin

    rmax, rmin = paired_tile_loop(score_body,
                                  (jnp.full((1, qn), -jnp.inf, F32), jnp.full((1, qn), jnp.inf, F32)))

    def count(pred):
        def body(t, acc):
            return acc + fold_rows(jnp.where(pred(s_ref[key_slice(t), :], key_pos(t)), 1.0, 0.0), jnp.sum)
        acc = lax.fori_loop(0, n_tiles, body, jnp.zeros((8, qn), F32))
        return jnp.sum(acc, axis=0, keepdims=True)

    def to_key(x):
        bits = lax.bitcast_convert_type(x, jnp.int32)
        return bits ^ ((bits >> 31) & 0x7FFFFFFF)

    def from_key(key):
        return lax.bitcast_convert_type(key ^ ((key >> 31) & 0x7FFFFFFF), F32)

    n_adm = limit.astype(F32)
    target = jnp.minimum(n_adm, float(k_sel))
    log_target = jnp.log(target)

    def any_left(done):
        return jnp.sum(done) < qn

    c_ge0 = count(lambda s, _: s >= 0.0)
    c_gt0 = count(lambda s, _: s > 0.0)
    is_pos = c_gt0 >= target
    is_neg = c_ge0 < target
    zero_key = jnp.zeros((1, qn), jnp.int32)
    lo0 = jnp.where(is_neg, to_key(rmin), zero_key)
    c_lo0 = jnp.where(is_neg, n_adm, c_ge0)
    c_hi0 = jnp.where(is_pos, 0.0, c_ge0)
    done0 = jnp.where((n_adm == target) | jnp.logical_not(is_pos | is_neg) | (c_lo0 == target), 1.0, 0.0)
    lo0 = jnp.where(n_adm == target, to_key(rmin), lo0)
    c_lo0 = jnp.where(n_adm == target, n_adm, c_lo0)

    ones = jnp.ones((1, qn), F32)

    def search(st0, key_to_value, value_to_key, count_ge):
        def cond(st):
            return any_left(st[7])

        def body(st):
            it, lo, hi, c_lo, c_hi, w_lo, w_hi, done, side = st
            t_lo, t_hi = key_to_value(lo), key_to_value(hi)
            f_lo = (jnp.log(c_lo) - log_target) * w_lo
            f_hi = (log_target - jnp.log(jnp.maximum(c_hi, 0.5))) * w_hi
            guess = value_to_key(t_lo + f_lo / (f_lo + f_hi) * (t_hi - t_lo))
            halve = (lo >> 1) + (hi >> 1) + (lo & hi & 1)
            mid = jnp.where(it >= SECANT_STEPS, halve, jnp.clip(guess, lo + 1, hi - 1))
            stuck = halve == lo
            c = count_ge(key_to_value(mid))
            live = (done < 0.5) & jnp.logical_not(stuck)
            up = live & (c >= target)
            dn = live & (c < target)
            w_hi = jnp.where(up, jnp.where(side > 0, 0.5 * w_hi, 1.0), jnp.where(dn, 1.0, w_hi))
            w_lo = jnp.where(dn, jnp.where(side < 0, 0.5 * w_lo, 1.0), jnp.where(up, 1.0, w_lo))
            side = jnp.where(up, 1.0, jnp.where(dn, -1.0, side))
            lo = jnp.where(up, mid, lo)
            c_lo = jnp.where(up, c, c_lo)
            hi = jnp.where(dn, mid, hi)
            c_hi = jnp.where(dn, c, c_hi)
            done = jnp.where(stuck | (c_lo == target) | (c_lo - c_hi <= 2.0), 1.0, done)
            return it + 1, lo, hi, c_lo, c_hi, w_lo, w_hi, done, side

        lo, hi, c_lo, c_hi, done = st0
        st = lax.while_loop(cond, body, (jnp.int32(0), lo, hi, c_lo, c_hi, ones, ones, done, 0.0 * ones))
        return st[1], st[2], st[3], st[4]

    hi0 = jnp.where(is_pos, to_key(rmax) + 1, zero_key)
    k_lo, k_hi, c_thr, c_above = search((lo0, hi0, c_lo0, c_hi0, done0),
                                        from_key, to_key, lambda v: count(lambda s, _: s >= v))
    thr = from_key(k_lo)

    two_left = (c_thr != target) & (c_thr - c_above == 2.0)

    def settle(_):
        t_lo, t_hi = from_key(k_lo), from_key(k_hi)

        def body(t, carry):
            top, bottom = carry
            s = s_ref[key_slice(t), :]
            top = jnp.maximum(top, fold_rows(jnp.where(s < t_hi, s, -jnp.inf), jnp.max))
            bottom = jnp.minimum(bottom, fold_rows(jnp.where(s >= t_lo, s, jnp.inf), jnp.min))
            return top, bottom

        top, bottom = lax.fori_loop(0, n_tiles, body, (jnp.full((8, qn), -jnp.inf, F32),
                                                       jnp.full((8, qn), jnp.inf, F32)))
        top = jnp.max(top, axis=0, keepdims=True)
        bottom = jnp.min(bottom, axis=0, keepdims=True)
        return (jnp.where(two_left, top, thr),
                jnp.where(two_left & (top != bottom), target, c_thr))

    thr, c_thr = lax.cond(jnp.sum(jnp.where(two_left, 1.0, 0.0)) > 0, settle, lambda _: (thr, c_thr), 0)

    tied = c_thr > target

    @pl.when(jnp.sum(jnp.where(tied, 1.0, 0.0)) > 0)
    def _():
        need = target - count(lambda s, _: s > thr)

        def idx_cond(st):
            return any_left(st[5])

        def idx_body(st):
            it, lo_j, hi_j, c_lo_j, c_hi_j, done = st
            span = (hi_j - lo_j).astype(F32)
            step = jnp.ceil((need - c_lo_j) / (c_hi_j - c_lo_j) * span).astype(jnp.int32)
            step = jnp.where(it % 2 == 0, step, (hi_j - lo_j) // 2)
            mid = lo_j + jnp.clip(step, 1, jnp.maximum(hi_j - lo_j - 1, 1))
            c = count(lambda s, kp: (s == thr) & (kp <= mid))
            live = (done < 0.5) & (hi_j - lo_j > 1)
            below = live & (c < need)
            above = live & (c >= need)
            lo_j = jnp.where(below, mid, lo_j)
            c_lo_j = jnp.where(below, c, c_lo_j)
            hi_j = jnp.where(above, mid, hi_j)
            c_hi_j = jnp.where(above, c, c_hi_j)
            done = jnp.where((c_hi_j == need) | (hi_j - lo_j <= 1), 1.0, done)
            return it + 1, lo_j, hi_j, c_lo_j, c_hi_j, done

        first = jnp.full((1, qn), -1, jnp.int32)
        final = jnp.full((1, qn), n_tiles * lt - 1, jnp.int32)
        n_ties = c_thr - (target - need)
        idx_done0 = jnp.where(tied & (n_ties > need), 0.0, 1.0)
        idx = lax.while_loop(idx_cond, idx_body,
                             (jnp.int32(0), first, final, 0.0 * ones, n_ties, idx_done0))
        last = idx[2]

        def drop_body(t, carry):
            s = s_ref[key_slice(t), :]
            s_ref[key_slice(t), :] = jnp.where(tied & (s == thr) & (key_pos(t) > last), -jnp.inf, s)
            return carry

        lax.fori_loop(0, n_tiles, drop_body, 0)

    qt = qt_ref[0]
    zeros = jnp.zeros((HEAD_DIM, B_GROUP * qn), BF16)
    for g in range(B_KV_HEADS):
        q_g = jnp.concatenate(
            [qt[(g * B_GROUP + j) * HEAD_DIM:(g * B_GROUP + j + 1) * HEAD_DIM, :] for j in range(B_GROUP)],
            axis=1)
        qpad_ref[g] = jnp.concatenate([zeros] * g + [q_g] + [zeros] * (B_KV_HEADS - 1 - g), axis=0)

    qf = qt.astype(F32)
    q_norm2 = jnp.sum((qf * qf).reshape(B_HEADS, HEAD_DIM, qn), axis=1)
    for g in range(B_KV_HEADS):
        b = jnp.sqrt(q_norm2[g * B_GROUP:(g + 1) * B_GROUP] * kmax_ref[0, 0:1, g:g + 1])
        b = b * BOUND_SLACK + BOUND_SLACK_ABS
        bound_ref[g] = jnp.concatenate([b[j:j + 1] for j in range(B_GROUP)], axis=1)
    acc_ref[...] = jnp.zeros(acc_ref.shape, F32)

    def fast_body(t, carry):
        keys = k_ref[0, key_slice(t), :]
        sel = jnp.where(s_ref[key_slice(t), :] >= thr, 1.0, 0.0).astype(BF16)
        sel = jnp.concatenate([sel] * B_GROUP, axis=1)
        def logits(g):
            return jnp.dot(keys, qpad_ref[g], preferred_element_type=F32)

        lg = logits(0)
        for g in range(B_KV_HEADS):
            lg_next = logits(g + 1) if g + 1 < B_KV_HEADS else None
            p = jnp.exp2(lg - bound_ref[g]).astype(BF16) * sel
            acc_ref[g] += jnp.dot(vta_ref[0, g, :, key_slice(t)], p, preferred_element_type=F32)
            lg = lg_next
        return carry

    paired_tile_loop(fast_body, 0)
    den_min = jnp.min(acc_ref[:, HEAD_DIM:HEAD_DIM + 1, :])

    @pl.when(jnp.logical_not(den_min >= MIN_TRUSTED_DENOMINATOR))
    def _():
        m_ref[...] = jnp.full(m_ref.shape, NEG_BIG, F32)
        acc_ref[...] = jnp.zeros(acc_ref.shape, F32)

        def exact_body(t, carry):
            keys = k_ref[0, key_slice(t), :]
            mb = jnp.where(s_ref[key_slice(t), :] >= thr, 0.0, -jnp.inf)
            mb = jnp.concatenate([mb] * B_GROUP, axis=1)
            for g in range(B_KV_HEADS):
                lg = jnp.dot(keys, qpad_ref[g], preferred_element_type=F32) + mb
                m_old = m_ref[g]
                m_new = jnp.maximum(m_old, jnp.max(fold_rows(lg, jnp.max), axis=0, keepdims=True))
                p = jnp.exp2(lg - m_new).astype(BF16)
                pv = jnp.dot(vta_ref[0, g, :, key_slice(t)], p, preferred_element_type=F32)
                acc_ref[g] = jnp.exp2(m_old - m_new) * acc_ref[g] + pv
                m_ref[g] = m_new
            return carry

        lax.fori_loop(0, n_tiles, exact_body, 0)

    for g in range(B_KV_HEADS):
        acc = acc_ref[g]
        o_g = acc[:HEAD_DIM] / acc[HEAD_DIM:HEAD_DIM + 1]
        for j in range(B_GROUP):
            hh = g * B_GROUP + j
            o_ref[0, hh * HEAD_DIM:(hh + 1) * HEAD_DIM, :] = o_g[:, j * qn:(j + 1) * qn].astype(o_ref.dtype)


V_AUG_ROWS = HEAD_DIM + 16


def _dsa(q, qi, kw, k, v, ki, *, q_tile, causal, k_sel):
    batch, seq, _ = q.shape
    n_keys = k.shape[1]
    lp = -(-n_keys // DSA_KEY_TILE) * DSA_KEY_TILE
    assert seq % q_tile == 0 and q_tile % LANES == 0
    pad = ((0, 0), (0, lp - n_keys), (0, 0))
    k, v, ki = (jnp.pad(a.astype(BF16), pad) for a in (k, v, ki))
    vt = jnp.swapaxes(v, 1, 2).reshape(batch, B_KV_HEADS, HEAD_DIM, lp)
    vta = jnp.concatenate([vt, jnp.ones((batch, B_KV_HEADS, 1, lp), BF16),
                           jnp.zeros((batch, B_KV_HEADS, V_AUG_ROWS - HEAD_DIM - 1, lp), BF16)], axis=2)
    qt, qit, kwt = (jnp.swapaxes(a, 1, 2) for a in (q, qi, kw))

    def qspec(rows):
        return pl.BlockSpec((1, rows, q_tile), lambda b, i: (b, 0, i))

    def resident(shape):
        zero = (0,) * len(shape)
        return pl.BlockSpec((1,) + shape, lambda b, i: (b,) + zero, pipeline_mode=pl.Buffered(1))

    ot = pl.pallas_call(
        functools.partial(_dsa_kernel, causal=causal, n_keys=n_keys, k_sel=k_sel),
        grid=(batch, seq // q_tile),
        in_specs=[qspec(B_Q), qspec(B_QI), qspec(LANES),
                  resident((lp, IDX_DIM)), resident((lp, B_KV)),
                  resident((B_KV_HEADS, V_AUG_ROWS, lp)),
                  pl.BlockSpec((1, 8, LANES), lambda b, i: (b, 0, 0))],
        out_specs=qspec(B_Q),
        out_shape=jax.ShapeDtypeStruct((batch, B_Q, seq), BF16),
        scratch_shapes=[pltpu.VMEM((lp, q_tile), F32),
                        pltpu.VMEM((B_KV_HEADS, B_KV, B_GROUP * q_tile), BF16),
                        pltpu.VMEM((B_KV_HEADS, 1, B_GROUP * q_tile), F32),
                        pltpu.VMEM((B_KV_HEADS, 1, B_GROUP * q_tile), F32),
                        pltpu.VMEM((B_KV_HEADS, V_AUG_ROWS, B_GROUP * q_tile), F32)],
        compiler_params=_params(2),
        name="dsa_prompt" if causal else "dsa_sample",
    )(qt, qit, kwt, ki, k, vta, _key_norm_max(k))
    return jnp.swapaxes(ot, 1, 2)


def kernel(x_prompt, x_sample, cache_k_a, cache_v_a, cache_k_b, cache_v_b, cache_kidx_b,
           c_prompt, c_sample, w_cond, b_cond, ln_g, ln_b, a_w_in, a_w_o, a_rel_bias,
           b_w_in, b_w_o, b_kidx_ln_g, b_kidx_ln_b, ffn_w_gu, ffn_w_down,
           moe_w_router, moe_b_router, moe_w_gu, moe_w_down):
    depth = w_cond.shape[0]
    alpha = (2 * depth) ** 0.25
    n_p, seq, d = x_prompt.shape
    n_s, dec_seq, _ = x_sample.shape
    past = cache_k_b.shape[2]
    a_hd = A_HEADS * HEAD_DIM

    rows = n_p + n_s
    rows_pad = -(-rows // 8) * 8
    c_all = jnp.pad(jnp.concatenate([c_prompt, c_sample], axis=0), ((0, rows_pad - rows), (0, 0)))
    mod_all = _modulation(c_all, w_cond, b_cond)

    xp, xs = x_prompt, x_sample
    outs = {k: [] for k in ("ka_p", "va_p", "kb_p", "vb_p", "ib_p", "ka_s", "va_s", "kb_s", "vb_s", "ib_s")}
    for i in range(depth):
        j = i // 2
        mod_p = mod_all[i, :n_p].reshape(n_p, 1, 6 * d)
        mod_s = mod_all[i, n_p:rows].reshape(n_s, 1, 6 * d)
        g1, b1, g2, b2 = ln_g[i, 0], ln_b[i, 0], ln_g[i, 1], ln_b[i, 1]
        if i % 2 == 0:
            w_in = a_w_in[j].astype(BF16)
            keep = min(BAND_ROWS, seq)
            qkv_p = _modmm(xp, mod_p, w_in, BF16)
            kv_p = _modmm(xp, mod_p, w_in[:, a_hd:], F32, t_start=seq - keep)
            qkv_s = _modmm(xs, mod_s, w_in, BF16)
            kv_s = _modmm(xs, mod_s, w_in[:, a_hd:], F32)
            outs["ka_p"].append(kv_p[..., :a_hd].reshape(n_p, keep, A_HEADS, HEAD_DIM))
            outs["va_p"].append(kv_p[..., a_hd:].reshape(n_p, keep, A_HEADS, HEAD_DIM))
            outs["ka_s"].append(kv_s[..., :a_hd].reshape(n_s, dec_seq, A_HEADS, HEAD_DIM))
            outs["va_s"].append(kv_s[..., a_hd:].reshape(n_s, dec_seq, A_HEADS, HEAD_DIM))
            o_p = _band_attn_prompt(qkv_p, a_rel_bias[j])
            ck = cache_k_a[j].reshape(n_s, -1, a_hd).astype(BF16)
            cv = cache_v_a[j].reshape(n_s, -1, a_hd).astype(BF16)
            o_s = _band_attn_sample(qkv_s, ck, cv, a_rel_bias[j])
            w_o = a_w_o[j].astype(BF16)
        else:
            w_in = jnp.pad(b_w_in[j], ((0, 0), (0, B_PROJ_PAD - B_PROJ))).astype(BF16)
            kn_g = jnp.pad(b_kidx_ln_g[j], (0, LANES - IDX_DIM)).reshape(1, LANES)
            kn_b = jnp.pad(b_kidx_ln_b[j], (0, LANES - IDX_DIM)).reshape(1, LANES)
            cos_p, sin_p = _rope_tables(jnp.arange(seq))
            cos_s, sin_s = _rope_tables(past + jnp.arange(dec_seq))
            bb_s, _ = _row_blocks(n_s, dec_seq, ROW_TILE)
            cos_s, sin_s = jnp.tile(cos_s, (bb_s, 1)), jnp.tile(sin_s, (bb_s, 1))
            q_p, qi_p, k_p, v_p, kw_p = _b_project(xp, mod_p, w_in, cos_p, sin_p, kn_g, kn_b, False)
            q_s, qi_s, k_s, v_s, kw_s = _b_project(xs, mod_s, w_in, cos_s, sin_s, kn_g, kn_b, True)
            outs["kb_p"].append(k_p.reshape(n_p, seq, B_KV_HEADS, HEAD_DIM))
            outs["vb_p"].append(v_p.reshape(n_p, seq, B_KV_HEADS, HEAD_DIM))
            outs["ib_p"].append(kw_p[..., :IDX_DIM])
            outs["kb_s"].append(k_s.reshape(n_s, dec_seq, B_KV_HEADS, HEAD_DIM))
            outs["vb_s"].append(v_s.reshape(n_s, dec_seq, B_KV_HEADS, HEAD_DIM))
            outs["ib_s"].append(kw_s[..., :IDX_DIM])
            o_p = _dsa(q_p, qi_p, kw_p, k_p, v_p, kw_p[..., :IDX_DIM], q_tile=DSA_Q_TILE, causal=True,
                       k_sel=min(TOPK_MAX, seq // 4))
            n_keys = past + dec_seq
            kk = jnp.concatenate([cache_k_b[j].reshape(n_s, past, B_KV), k_s], axis=1)
            vc = jnp.concatenate([cache_v_b[j].reshape(n_s, past, B_KV), v_s], axis=1)
            kki = jnp.concatenate([cache_kidx_b[j], kw_s[..., :IDX_DIM]], axis=1)
            rep = LANES // dec_seq
            q_r, qi_r, kw_r = (jnp.concatenate([a] * rep, axis=1) for a in (q_s, qi_s, kw_s))
            o_s = _dsa(q_r, qi_r, kw_r, kk, vc, kki, q_tile=LANES, causal=False,
                       k_sel=min(TOPK_MAX, n_keys // 4))[:, :dec_seq]
            w_o = b_w_o[j].astype(BF16)
        xp = _mm_postnorm(o_p, w_o, xp, mod_p, 2, g1, b1, alpha)
        xs = _mm_postnorm(o_s, w_o, xs, mod_s, 2, g1, b1, alpha)
        if i % 2 == 0:
            w_gu = ffn_w_gu[j].astype(BF16)
            w_dn = ffn_w_down[j].astype(BF16)
            dense = dict(ff_chunk=D_FF // 2, alpha=alpha)
            xp = _dense_ffn(xp, mod_p, w_gu, w_dn, g2, b2, **dense)
            xs = _dense_ffn(xs, mod_s, w_gu, w_dn, g2, b2, **dense)
        else:
            w_gu = moe_w_gu[j].astype(BF16)
            w_dn = moe_w_down[j].astype(BF16)
            w_r = jnp.pad(moe_w_router[j], ((0, 0), (0, LANES - N_EXPERTS)))
            b_r = jnp.pad(moe_b_router[j], (0, LANES - N_EXPERTS)).reshape(1, LANES)
            moe = dict(ff_chunk=MOE_FF_CHUNK, alpha=alpha)
            xp = _moe(xp, mod_p, w_r, b_r, w_gu, w_dn, g2, b2, **moe)
            xs = _moe(xs, mod_s, w_r, b_r, w_gu, w_dn, g2, b2, **moe)

    st = lambda name: jnp.stack(outs[name])
    return (xp, xs, st("ka_p"), st("va_p"), st("kb_p"), st("vb_p"), st("ib_p"),
            st("ka_s"), st("va_s"), st("kb_s"), st("vb_s"), st("ib_s"))
```

```python
import functools

import jax
import jax.numpy as jnp
from jax import lax
from jax.experimental import pallas as pl
from jax.experimental.pallas import tpu as pltpu

F32 = jnp.float32
BF16 = jnp.bfloat16

D_MODEL = 1024
CHUNK = 64
N_PAST_CHUNKS = 8
BAND_ROWS = N_PAST_CHUNKS * CHUNK
REL_CLIP = 2 * CHUNK
HEAD_DIM = 64
A_HEADS = 16
B_HEADS = 16
B_KV_HEADS = 4
B_GROUP = B_HEADS // B_KV_HEADS
IDX_HEADS = 8
IDX_DIM = 64
TOPK_MAX = 256
D_FF = 2816
N_EXPERTS = 8
D_FF_EXPERT = 3584
ROPE_THETA = 10000.0
LN_EPS = 1e-5
B_Q = B_HEADS * HEAD_DIM
B_KV = B_KV_HEADS * HEAD_DIM
B_QI = IDX_HEADS * IDX_DIM
B_PROJ = B_Q + 2 * B_KV + B_QI + IDX_DIM + IDX_HEADS

LANES = 128
VMEM_LIMIT_BYTES = 58 * 1024 * 1024

A_Q_TILE = 4 * CHUNK
DSA_Q_TILE = 4 * CHUNK
DSA_KEY_TILE = 512
ROW_TILE = 512
FFN_ROW_TILE = 1024
MOE_CAPACITY = 288
MOE_FF_CHUNK = 1792
B_PROJ_PAD = B_Q + 2 * B_KV + B_QI + LANES
NEG_BIG = -1e30
QK_SCALE_LOG2 = HEAD_DIM ** -0.5 * 1.4426950408889634
BOUND_SLACK = 1.02
BOUND_SLACK_ABS = 0.01
MIN_TRUSTED_DENOMINATOR = 2.0 ** -100
SECANT_STEPS = 12


def _params(n_grid):
    return pltpu.CompilerParams(
        dimension_semantics=("arbitrary",) * n_grid,
        vmem_limit_bytes=VMEM_LIMIT_BYTES,
    )


def _row_blocks(batch, seq, target):
    if seq >= target:
        assert seq % target == 0
        return 1, target
    bb = max(1, min(batch, target // seq))
    while batch % bb:
        bb -= 1
    return bb, seq


def _mod_spec(bb, chunk, n_grid):
    if n_grid == 2:
        return pl.BlockSpec((bb, 1, D_MODEL), lambda b, t: (b, 0, chunk))
    return pl.BlockSpec((bb, 1, D_MODEL), lambda b, t, e, c: (b, 0, chunk))


def _silu(x):
    return x / (1.0 + jnp.exp(-x))


def _layer_norm_rows(z, g, b):
    mu = jnp.mean(z, axis=-1, keepdims=True)
    zc = z - mu
    var = jnp.mean(zc * zc, axis=-1, keepdims=True)
    return zc * lax.rsqrt(var + LN_EPS) * g + b


def _modulation_kernel(c_ref, w_ref, b_ref, o_ref):
    a = _silu(c_ref[...]).astype(BF16)
    w = w_ref[0].astype(BF16)
    o_ref[0] = jnp.dot(a, w, preferred_element_type=F32) + b_ref[0]


def _modulation(c_all, w_cond, b_cond):
    depth, d, n = w_cond.shape
    rows = c_all.shape[0]
    tn = 1536
    return pl.pallas_call(
        _modulation_kernel,
        grid=(depth, n // tn),
        in_specs=[
            pl.BlockSpec((rows, d), lambda i, j: (0, 0)),
            pl.BlockSpec((1, d, tn), lambda i, j: (i, 0, j)),
            pl.BlockSpec((1, 1, tn), lambda i, j: (i, 0, j)),
        ],
        out_specs=pl.BlockSpec((1, rows, tn), lambda i, j: (i, 0, j)),
        out_shape=jax.ShapeDtypeStruct((depth, rows, n), F32),
        compiler_params=_params(2),
        name="modulation",
    )(c_all, w_cond, b_cond.reshape(depth, 1, n))


def _modmm_kernel(x_ref, sc_ref, sh_ref, w_ref, o_ref):
    bb, tt, d = x_ref.shape
    h = x_ref[...] * (1.0 + sc_ref[...]) + sh_ref[...]
    h = h.reshape(bb * tt, d).astype(BF16)
    acc = jnp.dot(h, w_ref[...], preferred_element_type=F32)
    o_ref[...] = acc.reshape(bb, tt, acc.shape[-1]).astype(o_ref.dtype)


def _modmm(x, mod, w, out_dtype, t_start=0):
    batch, seq, d = x.shape
    n = w.shape[1]
    seq_out = seq - t_start
    bb, tt = _row_blocks(batch, seq_out, ROW_TILE)
    assert t_start % tt == 0
    off = t_start // tt
    return pl.pallas_call(
        _modmm_kernel,
        grid=(batch // bb, seq_out // tt),
        in_specs=[
            pl.BlockSpec((bb, tt, d), lambda b, t: (b, t + off, 0)),
            _mod_spec(bb, 1, 2),
            _mod_spec(bb, 0, 2),
            pl.BlockSpec((d, n), lambda b, t: (0, 0)),
        ],
        out_specs=pl.BlockSpec((bb, tt, n), lambda b, t: (b, t, 0)),
        out_shape=jax.ShapeDtypeStruct((batch, seq_out, n), out_dtype),
        compiler_params=_params(2),
        name="a_project",
    )(x, mod, mod, w)


def _band_attn_kernel(*refs, n_kb, n_maybe_invalid):
    q_ref = refs[0]
    k_refs = refs[1:1 + n_kb]
    v_refs = refs[1 + n_kb:1 + 2 * n_kb]
    f_ref = refs[1 + 2 * n_kb]
    o_ref = refs[2 + 2 * n_kb]
    bias_ref = refs[3 + 2 * n_kb]
    q_tile = q_ref.shape[1]
    kb_sizes = [r.shape[1] for r in k_refs]
    k_tot = sum(kb_sizes)
    width = f_ref.shape[-1]
    i = pl.program_id(1)

    @pl.when((pl.program_id(0) == 0) & (i == 0))
    def _():
        rq = lax.broadcasted_iota(jnp.int32, (q_tile, k_tot), 0) // CHUNK
        ck = lax.broadcasted_iota(jnp.int32, (q_tile, k_tot), 1) // CHUNK
        in_band = (ck >= rq) & (ck <= rq + N_PAST_CHUNKS)
        for h in range(A_HEADS):
            rows = jnp.broadcast_to(f_ref[h], (q_tile, width))
            toep = pltpu.roll(rows, k_tot + 1, 1, stride=1, stride_axis=0)
            bias_ref[h] = jnp.where(in_band, toep[:, :k_tot], -jnp.inf)

    q = q_ref[0] * (HEAD_DIM ** -0.5)
    for h in range(A_HEADS):
        cols = slice(h * HEAD_DIM, (h + 1) * HEAD_DIM)
        qh = q[:, cols]
        parts = []
        for kb in range(n_kb):
            lg = lax.dot_general(qh, k_refs[kb][0, :, cols], (((1,), (1,)), ((), ())),
                                 preferred_element_type=F32)
            if kb < n_maybe_invalid:
                lg = jnp.where(i - n_maybe_invalid + kb >= 0, lg, -jnp.inf)
            parts.append(lg)
        logits = jnp.concatenate(parts, axis=1) + bias_ref[h]
        m = jnp.max(logits, axis=1, keepdims=True)
        p = jnp.exp(logits - m)
        l = jnp.sum(p, axis=1, keepdims=True)
        pb = p.astype(BF16)
        acc = jnp.zeros((q_tile, HEAD_DIM), F32)
        start = 0
        for kb in range(n_kb):
            acc = acc + jnp.dot(pb[:, start:start + kb_sizes[kb]], v_refs[kb][0, :, cols],
                                preferred_element_type=F32)
            start += kb_sizes[kb]
        o_ref[0, :, cols] = (acc / l).astype(o_ref.dtype)


def _bias_vector(table, q_tile, k_tot):
    width = k_tot + q_tile
    lo = -(q_tile - 1) - BAND_ROWS + REL_CLIP
    left = max(0, -lo)
    start = max(0, lo)
    n_mid = min(2 * REL_CLIP + 1 - start, width - left)
    right = width - left - n_mid
    mid = table.T[:, start:start + n_mid]
    f = jnp.pad(mid, ((0, 0), (left, right)), mode="edge")
    return f.reshape(A_HEADS, 1, width)


def _band_attn_prompt(qkv, table):
    batch, seq, _ = qkv.shape
    hd = A_HEADS * HEAD_DIM
    qt = A_Q_TILE
    n_prev = BAND_ROWS // qt
    n_kb = n_prev + 1
    k_tot = n_kb * qt
    f = _bias_vector(table, qt, k_tot)

    def kv_spec(kb, col):
        return pl.BlockSpec((1, qt, hd), lambda b, i: (b, jnp.maximum(i - n_prev + kb, 0), col))

    return pl.pallas_call(
        functools.partial(_band_attn_kernel, n_kb=n_kb, n_maybe_invalid=n_prev),
        grid=(batch, seq // qt),
        in_specs=[pl.BlockSpec((1, qt, hd), lambda b, i: (b, i, 0))]
        + [kv_spec(kb, 1) for kb in range(n_kb)]
        + [kv_spec(kb, 2) for kb in range(n_kb)]
        + [pl.BlockSpec(f.shape, lambda b, i: (0, 0, 0))],
        out_specs=pl.BlockSpec((1, qt, hd), lambda b, i: (b, i, 0)),
        out_shape=jax.ShapeDtypeStruct((batch, seq, hd), BF16),
        scratch_shapes=[pltpu.VMEM((A_HEADS, qt, k_tot), F32)],
        compiler_params=_params(2),
        name="band_attn_prompt",
    )(*([qkv] * (1 + 2 * n_kb)), f)


def _band_attn_sample(qkv, cache_k, cache_v, table):
    batch, seq, _ = qkv.shape
    hd = A_HEADS * HEAD_DIM
    win = cache_k.shape[1]
    assert seq == CHUNK and win == BAND_ROWS
    k_tot = win + seq
    f = _bias_vector(table, seq, k_tot)
    new = lambda col: pl.BlockSpec((1, seq, hd), lambda b, i: (b, 0, col))
    old = pl.BlockSpec((1, win, hd), lambda b, i: (b, 0, 0))
    return pl.pallas_call(
        functools.partial(_band_attn_kernel, n_kb=2, n_maybe_invalid=0),
        grid=(batch, 1),
        in_specs=[new(0), old, new(1), old, new(2), pl.BlockSpec(f.shape, lambda b, i: (0, 0, 0))],
        out_specs=pl.BlockSpec((1, seq, hd), lambda b, i: (b, 0, 0)),
        out_shape=jax.ShapeDtypeStruct((batch, seq, hd), BF16),
        scratch_shapes=[pltpu.VMEM((A_HEADS, seq, k_tot), F32)],
        compiler_params=_params(2),
        name="band_attn_sample",
    )(qkv, cache_k, qkv, cache_v, qkv, f)


def _mm_postnorm_kernel(o_ref, w_ref, x_ref, gate_ref, g_ref, b_ref, y_ref, *, alpha):
    bb, tt, d = x_ref.shape
    o = o_ref[...].reshape(bb * tt, o_ref.shape[-1])
    sub = jnp.dot(o, w_ref[...], preferred_element_type=F32).reshape(bb, tt, d)
    z = alpha * x_ref[...] + (1.0 + gate_ref[...]) * sub
    y_ref[...] = _layer_norm_rows(z, g_ref[...], b_ref[...])


def _mm_postnorm(o, w, x, mod, gate_chunk, g, b, alpha):
    batch, seq, d = x.shape
    bb, tt = _row_blocks(batch, seq, ROW_TILE)
    k = o.shape[-1]
    return pl.pallas_call(
        functools.partial(_mm_postnorm_kernel, alpha=alpha),
        grid=(batch // bb, seq // tt),
        in_specs=[
            pl.BlockSpec((bb, tt, k), lambda b_, t: (b_, t, 0)),
            pl.BlockSpec((k, d), lambda b_, t: (0, 0)),
            pl.BlockSpec((bb, tt, d), lambda b_, t: (b_, t, 0)),
            _mod_spec(bb, gate_chunk, 2),
            pl.BlockSpec((1, 1, d), lambda b_, t: (0, 0, 0)),
            pl.BlockSpec((1, 1, d), lambda b_, t: (0, 0, 0)),
        ],
        out_specs=pl.BlockSpec((bb, tt, d), lambda b_, t: (b_, t, 0)),
        out_shape=jax.ShapeDtypeStruct((batch, seq, d), F32),
        compiler_params=_params(2),
        name="out_proj_postnorm",
    )(o, w, x, mod, g.reshape(1, 1, d), b.reshape(1, 1, d))


def _dense_ffn_kernel(x_ref, sc_ref, sh_ref, gate_ref, wg_ref, wu_ref, wd_ref, g_ref, b_ref,
                      y_ref, h_ref, acc_ref, *, alpha):
    bb, tt, d = x_ref.shape
    rows = bb * tt
    c = pl.program_id(2)

    @pl.when(c == 0)
    def _():
        h = (x_ref[...] * (1.0 + sc_ref[...]) + sh_ref[...]).reshape(rows, d)
        h_ref[...] = h.astype(BF16)
        acc_ref[...] = jnp.zeros_like(acc_ref)

    h = h_ref[...]
    gt = jnp.dot(h, wg_ref[...], preferred_element_type=F32)
    up = jnp.dot(h, wu_ref[...], preferred_element_type=F32)
    act = (_silu(gt) * up).astype(BF16)
    acc_ref[...] += jnp.dot(act, wd_ref[...], preferred_element_type=F32)

    @pl.when(c == pl.num_programs(2) - 1)
    def _():
        z = alpha * x_ref[...] + (1.0 + gate_ref[...]) * acc_ref[...].reshape(bb, tt, d)
        y_ref[...] = _layer_norm_rows(z, g_ref[...], b_ref[...])


def _dense_ffn(x, mod, w_gu, w_down, g, b, *, ff_chunk, alpha):
    batch, seq, d = x.shape
    ff = w_down.shape[0]
    assert ff % ff_chunk == 0
    n_c = ff // ff_chunk
    bb, tt = _row_blocks(batch, seq, FFN_ROW_TILE)
    rows = bb * tt
    vec = pl.BlockSpec((1, 1, d), lambda b_, t, c: (0, 0, 0))

    def mod_spec(chunk):
        return pl.BlockSpec((bb, 1, d), lambda b_, t, c: (b_, 0, chunk))

    return pl.pallas_call(
        functools.partial(_dense_ffn_kernel, alpha=alpha),
        grid=(batch // bb, seq // tt, n_c),
        in_specs=[
            pl.BlockSpec((bb, tt, d), lambda b_, t, c: (b_, t, 0)),
            mod_spec(4),
            mod_spec(3),
            mod_spec(5),
            pl.BlockSpec((d, ff_chunk), lambda b_, t, c: (0, c)),
            pl.BlockSpec((d, ff_chunk), lambda b_, t, c: (0, n_c + c)),
            pl.BlockSpec((ff_chunk, d), lambda b_, t, c: (c, 0)),
            vec,
            vec,
        ],
        out_specs=pl.BlockSpec((bb, tt, d), lambda b_, t, c: (b_, t, 0)),
        out_shape=jax.ShapeDtypeStruct((batch, seq, d), F32),
        scratch_shapes=[pltpu.VMEM((rows, d), BF16), pltpu.VMEM((rows, d), F32)],
        compiler_params=_params(3),
        name="dense_ffn",
    )(x, mod, mod, mod, w_gu, w_gu, w_down, g.reshape(1, 1, d), b.reshape(1, 1, d))


def _moe_kernel(x_ref, sc_ref, sh_ref, gate_ref, wr_ref, br_ref, wg_ref, wu_ref, wd_ref, g_ref, b_ref,
                y_ref, h_ref, comb_ref, slot_ref, cnt_ref, tri_ref, hc_ref, acc_ref, out_ref, *, alpha):
    bb, tt, d = x_ref.shape
    rows = bb * tt
    cap = hc_ref.shape[1]
    e = pl.program_id(2)
    c = pl.program_id(3)
    last_c = pl.num_programs(3) - 1
    lane = lax.broadcasted_iota(jnp.int32, (rows, LANES), 1)

    @pl.when((pl.program_id(0) == 0) & (pl.program_id(1) == 0) & (e == 0) & (c == 0))
    def _():
        r = lax.broadcasted_iota(jnp.int32, (rows, rows), 0)
        col = lax.broadcasted_iota(jnp.int32, (rows, rows), 1)
        tri_ref[...] = jnp.where(r <= col, 1.0, 0.0).astype(BF16)

    @pl.when((e == 0) & (c == 0))
    def _():
        h = (x_ref[...] * (1.0 + sc_ref[...]) + sh_ref[...]).reshape(rows, d)
        h_ref[...] = h.astype(BF16)
        out_ref[...] = jnp.zeros_like(out_ref)
        logits = jnp.dot(h, wr_ref[...], preferred_element_type=F32,
                         precision=lax.Precision.HIGHEST) + br_ref[...]
        logits = jnp.where(lane < N_EXPERTS, logits, -jnp.inf)
        m1 = jnp.max(logits, axis=1, keepdims=True)
        i1 = jnp.min(jnp.where(logits == m1, lane, LANES), axis=1, keepdims=True)
        rest = jnp.where(lane == i1, -jnp.inf, logits)
        m2 = jnp.max(rest, axis=1, keepdims=True)
        i2 = jnp.min(jnp.where(rest == m2, lane, LANES), axis=1, keepdims=True)
        e2 = jnp.exp(m2 - m1)
        den = 1.0 + e2
        comb_ref[...] = jnp.where(lane == i1, 1.0 / den, 0.0) + jnp.where(lane == i2, e2 / den, 0.0)
        member = jnp.where((lane == i1) | (lane == i2), 1.0, 0.0)
        member_t = member.T[:2 * N_EXPERTS]
        upto = jnp.dot(member_t.astype(BF16), tri_ref[...], preferred_element_type=F32)
        slot_ref[...] = jnp.where(member_t > 0, upto - 1.0, -1.0)
        cnt_ref[...] = jnp.broadcast_to(upto[:, rows - 1:rows], cnt_ref.shape)

    n_sub = (jnp.max(cnt_ref[pl.ds(e, 1), :]).astype(jnp.int32) + cap - 1) // cap

    def selection(s):
        slot = slot_ref[pl.ds(e, 1), :] - (s * cap).astype(F32)
        r = lax.broadcasted_iota(jnp.int32, (cap, rows), 0).astype(F32)
        return jnp.where(slot == r, 1.0, 0.0).astype(BF16)

    @pl.when(c == 0)
    def _():
        def pack(s, carry):
            hc_ref[s] = jnp.dot(selection(s), h_ref[...], preferred_element_type=F32).astype(BF16)
            acc_ref[s] = jnp.zeros(acc_ref.shape[1:], F32)
            return carry
        lax.fori_loop(0, n_sub, pack, 0)

    def expert(s, carry):
        hc = hc_ref[s]
        gt = jnp.dot(hc, wg_ref[0], preferred_element_type=F32)
        up = jnp.dot(hc, wu_ref[0], preferred_element_type=F32)
        act = (_silu(gt) * up).astype(BF16)
        acc_ref[s] += jnp.dot(act, wd_ref[0], preferred_element_type=F32)
        return carry

    lax.fori_loop(0, n_sub, expert, 0)

    @pl.when(c == last_c)
    def _():
        ce = jnp.sum(jnp.where(lane == e, comb_ref[...], 0.0), axis=1, keepdims=True)

        def unpack(s, carry):
            back = lax.dot_general(selection(s), acc_ref[s].astype(BF16), (((0,), (0,)), ((), ())),
                                   preferred_element_type=F32)
            out_ref[...] += ce * back
            return carry
        lax.fori_loop(0, n_sub, unpack, 0)

    @pl.when((e == pl.num_programs(2) - 1) & (c == last_c))
    def _():
        z = alpha * x_ref[...] + (1.0 + gate_ref[...]) * out_ref[...].reshape(bb, tt, d)
        y_ref[...] = _layer_norm_rows(z, g_ref[...], b_ref[...])


def _moe(x, mod, w_r, b_r, w_gu, w_down, g, b, *, ff_chunk, alpha):
    batch, seq, d = x.shape
    n_e, ff, _ = w_down.shape
    assert ff % ff_chunk == 0
    n_c = ff // ff_chunk
    bb, tt = _row_blocks(batch, seq, FFN_ROW_TILE)
    rows = bb * tt
    cap = MOE_CAPACITY
    max_sub = -(-rows // cap)
    vec = pl.BlockSpec((1, 1, d), lambda b_, t, e, c: (0, 0, 0))
    return pl.pallas_call(
        functools.partial(_moe_kernel, alpha=alpha),
        grid=(batch // bb, seq // tt, n_e, n_c),
        in_specs=[
            pl.BlockSpec((bb, tt, d), lambda b_, t, e, c: (b_, t, 0), pipeline_mode=pl.Buffered(1)),
            _mod_spec(bb, 4, 4),
            _mod_spec(bb, 3, 4),
            _mod_spec(bb, 5, 4),
            pl.BlockSpec((d, LANES), lambda b_, t, e, c: (0, 0)),
            pl.BlockSpec((1, LANES), lambda b_, t, e, c: (0, 0)),
            pl.BlockSpec((1, d, ff_chunk), lambda b_, t, e, c: (e, 0, c)),
            pl.BlockSpec((1, d, ff_chunk), lambda b_, t, e, c: (e, 0, n_c + c)),
            pl.BlockSpec((1, ff_chunk, d), lambda b_, t, e, c: (e, c, 0)),
            vec,
            vec,
        ],
        out_specs=pl.BlockSpec((bb, tt, d), lambda b_, t, e, c: (b_, t, 0)),
        out_shape=jax.ShapeDtypeStruct((batch, seq, d), F32),
        scratch_shapes=[
            pltpu.VMEM((rows, d), BF16),
            pltpu.VMEM((rows, LANES), F32),
            pltpu.VMEM((2 * N_EXPERTS, rows), F32),
            pltpu.VMEM((2 * N_EXPERTS, LANES), F32),
            pltpu.VMEM((rows, rows), BF16),
            pltpu.VMEM((max_sub, cap, d), BF16),
            pltpu.VMEM((max_sub, cap, d), F32),
            pltpu.VMEM((rows, d), F32),
        ],
        compiler_params=_params(4),
        name="moe_ffn",
    )(x, mod, mod, mod, w_r, b_r, w_gu, w_gu, w_down, g.reshape(1, 1, d), b.reshape(1, 1, d))


def _rope_lanes(seg, cos, sin_signed, first_half):
    width = seg.shape[-1]
    reps = width // LANES
    if reps > 1:
        cos = jnp.concatenate([cos] * reps, axis=1)
        sin_signed = jnp.concatenate([sin_signed] * reps, axis=1)
        first_half = jnp.concatenate([first_half] * reps, axis=1)
    half = HEAD_DIM // 2
    swapped = jnp.where(first_half, pltpu.roll(seg, width - half, 1), pltpu.roll(seg, half, 1))
    return seg * cos + swapped * sin_signed


def _b_project_kernel(x_ref, sc_ref, sh_ref, w_ref, cos_ref, sin_ref, kg_ref, kb_ref,
                      q_ref, qi_ref, k_ref, v_ref, kw_ref):
    bb, tt, d = x_ref.shape
    rows = bb * tt
    h = x_ref[...] * (1.0 + sc_ref[...]) + sh_ref[...]
    h = h.reshape(rows, d).astype(BF16)
    acc = jnp.dot(h, w_ref[...], preferred_element_type=F32)
    cos = cos_ref[...]
    sin_signed = sin_ref[...]
    lane = lax.broadcasted_iota(jnp.int32, (1, LANES), 1)
    first_half = (lane % HEAD_DIM) < (HEAD_DIM // 2)

    def out(ref, val):
        ref[...] = val.reshape(bb, tt, val.shape[-1]).astype(ref.dtype)

    o_k, o_v, o_qi, o_ki = B_Q, B_Q + B_KV, B_Q + 2 * B_KV, B_Q + 2 * B_KV + B_QI
    out(q_ref, _rope_lanes(acc[:, :o_k], cos, sin_signed, first_half) * QK_SCALE_LOG2)
    out(k_ref, _rope_lanes(acc[:, o_k:o_v], cos, sin_signed, first_half))
    out(v_ref, acc[:, o_v:o_qi])
    out(qi_ref, _rope_lanes(acc[:, o_qi:o_ki], cos, sin_signed, first_half))
    seg = acc[:, o_ki:]
    is_ki = lane < IDX_DIM
    mu = jnp.sum(jnp.where(is_ki, seg, 0.0), axis=1, keepdims=True) / IDX_DIM
    cen = jnp.where(is_ki, seg - mu, 0.0)
    var = jnp.sum(cen * cen, axis=1, keepdims=True) / IDX_DIM
    ki = cen * lax.rsqrt(var + LN_EPS) * kg_ref[...] + kb_ref[...]
    ki = _rope_lanes(ki, cos, sin_signed, first_half)
    out(kw_ref, jnp.where(is_ki, ki, seg * (IDX_HEADS ** -0.5)))


def _b_project(x, mod, w, cos, sin_signed, kn_g, kn_b, table_per_tile):
    batch, seq, d = x.shape
    bb, tt = _row_blocks(batch, seq, ROW_TILE)
    rows = bb * tt
    n = w.shape[1]
    tab = pl.BlockSpec((rows, LANES), (lambda b, t: (0, 0)) if table_per_tile else (lambda b, t: (t, 0)))
    vec = pl.BlockSpec((1, LANES), lambda b, t: (0, 0))

    def o_spec(width):
        return pl.BlockSpec((bb, tt, width), lambda b, t: (b, t, 0))

    def o_shape(width, dtype):
        return jax.ShapeDtypeStruct((batch, seq, width), dtype)

    return pl.pallas_call(
        _b_project_kernel,
        grid=(batch // bb, seq // tt),
        in_specs=[
            pl.BlockSpec((bb, tt, d), lambda b, t: (b, t, 0)),
            _mod_spec(bb, 1, 2),
            _mod_spec(bb, 0, 2),
            pl.BlockSpec((d, n), lambda b, t: (0, 0)),
            tab,
            tab,
            vec,
            vec,
        ],
        out_specs=[o_spec(B_Q), o_spec(B_QI), o_spec(B_KV), o_spec(B_KV), o_spec(LANES)],
        out_shape=[o_shape(B_Q, BF16), o_shape(B_QI, BF16), o_shape(B_KV, F32), o_shape(B_KV, F32),
                   o_shape(LANES, F32)],
        compiler_params=_params(2),
        name="b_project",
    )(x, mod, mod, w, cos, sin_signed, kn_g, kn_b)


def _rope_tables(pos):
    half = HEAD_DIM // 2
    inv = ROPE_THETA ** (-jnp.arange(half, dtype=F32) / half)
    ang = pos.astype(F32)[:, None] * inv[None, :]
    cos, sin = jnp.cos(ang), jnp.sin(ang)
    return (jnp.concatenate([cos, cos, cos, cos], axis=1),
            jnp.concatenate([-sin, sin, -sin, sin], axis=1))


def _key_norm_kernel(k_ref, grp_ref, o_ref):
    k = k_ref[0].astype(F32)
    n2 = jnp.dot((k * k).astype(BF16), grp_ref[...], preferred_element_type=F32)
    mx = jnp.broadcast_to(jnp.max(n2, axis=0, keepdims=True), o_ref.shape[1:])

    @pl.when(pl.program_id(1) == 0)
    def _():
        o_ref[0] = mx

    @pl.when(pl.program_id(1) > 0)
    def _():
        o_ref[0] = jnp.maximum(o_ref[0], mx)


def _key_norm_max(k):
    batch, lp, width = k.shape
    tile = DSA_KEY_TILE
    grp = (jnp.arange(width)[:, None] // HEAD_DIM == jnp.arange(LANES)[None, :]).astype(BF16)
    return pl.pallas_call(
        _key_norm_kernel,
        grid=(batch, lp // tile),
        in_specs=[pl.BlockSpec((1, tile, width), lambda b, t: (b, t, 0)),
                  pl.BlockSpec((width, LANES), lambda b, t: (0, 0))],
        out_specs=pl.BlockSpec((1, 8, LANES), lambda b, t: (b, 0, 0)),
        out_shape=jax.ShapeDtypeStruct((batch, 8, LANES), F32),
        compiler_params=_params(2),
        name="key_norm_max",
    )(k, grp)


def _dsa_kernel(qt_ref, qit_ref, kwt_ref, ki_ref, k_ref, vta_ref, kmax_ref, o_ref,
                s_ref, qpad_ref, bound_ref, m_ref, acc_ref, *, causal, n_keys, k_sel):
    qn = qt_ref.shape[2]
    lt = DSA_KEY_TILE
    i = pl.program_id(1)
    if causal:
        n_tiles = ((i + 1) * qn + lt - 1) // lt
        qpos = i * qn + lax.broadcasted_iota(jnp.int32, (1, qn), 1)
        limit = (qpos // CHUNK + 1) * CHUNK
    else:
        n_tiles = ki_ref.shape[1] // lt
        limit = jnp.full((1, qn), n_keys, jnp.int32)

    def key_slice(t):
        return pl.ds(pl.multiple_of(t * lt, lt), lt)

    def key_pos(t):
        return t * lt + lax.broadcasted_iota(jnp.int32, (lt, 1), 0)

    def fold_rows(x, op):
        return op(op(x.reshape(lt // 32, 4, 8, x.shape[-1]), axis=0), axis=0)

    def paired_tile_loop(body, init):
        def pair(t2, carry):
            return body(2 * t2 + 1, body(2 * t2, carry))
        carry = lax.fori_loop(0, n_tiles // 2, pair, init)
        return lax.cond(n_tiles % 2 == 1, lambda c: body(n_tiles - 1, c), lambda c: c, carry)

    qit = qit_ref[0]
    kwt = kwt_ref[0]
    qi_w = jnp.concatenate([qit[h * IDX_DIM:(h + 1) * IDX_DIM, :] for h in range(IDX_HEADS)], axis=1)
    w_row = jnp.concatenate([kwt[IDX_DIM + h:IDX_DIM + h + 1, :] for h in range(IDX_HEADS)], axis=1)
    w_row = w_row * (IDX_DIM ** -0.5)

    def score_body(t, carry):
        rmax, rmin = carry
        s = jnp.dot(ki_ref[0, key_slice(t), :], qi_w, preferred_element_type=F32)
        s = jnp.maximum(s, 0.0) * w_row
        sc = s[:, 0:qn]
        for h in range(1, IDX_HEADS):
            sc = sc + s[:, h * qn:(h + 1) * qn]
        adm = key_pos(t) < limit
        masked = jnp.where(adm, sc, -jnp.inf)
        s_ref[key_slice(t), :] = masked
        rmax = jnp.maximum(rmax, jnp.max(masked, axis=0, keepdims=True))
        rmin = jnp.minimum(rmin, jnp.min(jnp.where(adm, sc, jnp.inf), axis=0, keepdims=True))
        return rmax, rmin

    rmax, rmin = paired_tile_loop(score_body,
                                  (jnp.full((1, qn), -jnp.inf, F32), jnp.full((1, qn), jnp.inf, F32)))

    def count(pred):
        def body(t, acc):
            return acc + fold_rows(jnp.where(pred(s_ref[key_slice(t), :], key_pos(t)), 1.0, 0.0), jnp.sum)
        acc = paired_tile_loop(body, jnp.zeros((8, qn), F32))
        return jnp.sum(acc, axis=0, keepdims=True)

    def to_key(x):
        bits = lax.bitcast_convert_type(x, jnp.int32)
        return bits ^ ((bits >> 31) & 0x7FFFFFFF)

    def from_key(key):
        return lax.bitcast_convert_type(key ^ ((key >> 31) & 0x7FFFFFFF), F32)

    n_adm = limit.astype(F32)
    target = jnp.minimum(n_adm, float(k_sel))
    log_target = jnp.log(target)

    def any_left(done):
        return jnp.sum(done) < qn

    c_ge0 = count(lambda s, _: s >= 0.0)
    c_gt0 = count(lambda s, _: s > 0.0)
    is_pos = c_gt0 >= target
    is_neg = c_ge0 < target
    zero_key = jnp.zeros((1, qn), jnp.int32)
    lo0 = jnp.where(is_neg, to_key(rmin), zero_key)
    c_lo0 = jnp.where(is_neg, n_adm, c_ge0)
    c_hi0 = jnp.where(is_pos, 0.0, c_ge0)
    done0 = jnp.where((n_adm == target) | jnp.logical_not(is_pos | is_neg) | (c_lo0 == target), 1.0, 0.0)
    lo0 = jnp.where(n_adm == target, to_key(rmin), lo0)
    c_lo0 = jnp.where(n_adm == target, n_adm, c_lo0)

    ones = jnp.ones((1, qn), F32)

    def search(st0, key_to_value, value_to_key, count_ge):
        def cond(st):
            return any_left(st[7])

        def body(st):
            it, lo, hi, c_lo, c_hi, w_lo, w_hi, done, side = st
            t_lo, t_hi = key_to_value(lo), key_to_value(hi)
            f_lo = (jnp.log(c_lo) - log_target) * w_lo
            f_hi = (log_target - jnp.log(jnp.maximum(c_hi, 0.5))) * w_hi
            guess = value_to_key(t_lo + f_lo / (f_lo + f_hi) * (t_hi - t_lo))
            halve = (lo >> 1) + (hi >> 1) + (lo & hi & 1)
            mid = jnp.where(it >= SECANT_STEPS, halve, jnp.clip(guess, lo + 1, hi - 1))
            stuck = halve == lo
            c = count_ge(key_to_value(mid))
            live = (done < 0.5) & jnp.logical_not(stuck)
            up = live & (c >= target)
            dn = live & (c < target)
            w_hi = jnp.where(up, jnp.where(side > 0, 0.5 * w_hi, 1.0), jnp.where(dn, 1.0, w_hi))
            w_lo = jnp.where(dn, jnp.where(side < 0, 0.5 * w_lo, 1.0), jnp.where(up, 1.0, w_lo))
            side = jnp.where(up, 1.0, jnp.where(dn, -1.0, side))
            lo = jnp.where(up, mid, lo)
            c_lo = jnp.where(up, c, c_lo)
            hi = jnp.where(dn, mid, hi)
            c_hi = jnp.where(dn, c, c_hi)
            done = jnp.where(stuck | (c_lo == target) | (c_lo - c_hi <= 2.0), 1.0, done)
            return it + 1, lo, hi, c_lo, c_hi, w_lo, w_hi, done, side

        lo, hi, c_lo, c_hi, done = st0
        st = lax.while_loop(cond, body, (jnp.int32(0), lo, hi, c_lo, c_hi, ones, ones, done, 0.0 * ones))
        return st[1], st[2], st[3], st[4]

    hi0 = jnp.where(is_pos, to_key(rmax) + 1, zero_key)
    k_lo, k_hi, c_thr, c_above = search((lo0, hi0, c_lo0, c_hi0, done0),
                                        from_key, to_key, lambda v: count(lambda s, _: s >= v))
    thr = from_key(k_lo)

    two_left = (c_thr != target) & (c_thr - c_above == 2.0)

    def settle(_):
        t_lo, t_hi = from_key(k_lo), from_key(k_hi)

        def body(t, carry):
            top, bottom = carry
            s = s_ref[key_slice(t), :]
            top = jnp.maximum(top, fold_rows(jnp.where(s < t_hi, s, -jnp.inf), jnp.max))
            bottom = jnp.minimum(bottom, fold_rows(jnp.where(s >= t_lo, s, jnp.inf), jnp.min))
            return top, bottom

        top, bottom = lax.fori_loop(0, n_tiles, body, (jnp.full((8, qn), -jnp.inf, F32),
                                                       jnp.full((8, qn), jnp.inf, F32)))
        top = jnp.max(top, axis=0, keepdims=True)
        bottom = jnp.min(bottom, axis=0, keepdims=True)
        return (jnp.where(two_left, top, thr),
                jnp.where(two_left & (top != bottom), target, c_thr))

    thr, c_thr = lax.cond(jnp.sum(jnp.where(two_left, 1.0, 0.0)) > 0, settle, lambda _: (thr, c_thr), 0)

    tied = c_thr > target

    @pl.when(jnp.sum(jnp.where(tied, 1.0, 0.0)) > 0)
    def _():
        need = target - count(lambda s, _: s > thr)

        def idx_cond(st):
            return any_left(st[5])

        def idx_body(st):
            it, lo_j, hi_j, c_lo_j, c_hi_j, done = st
            span = (hi_j - lo_j).astype(F32)
            step = jnp.ceil((need - c_lo_j) / (c_hi_j - c_lo_j) * span).astype(jnp.int32)
            step = jnp.where(it % 2 == 0, step, (hi_j - lo_j) // 2)
            mid = lo_j + jnp.clip(step, 1, jnp.maximum(hi_j - lo_j - 1, 1))
            c = count(lambda s, kp: (s == thr) & (kp <= mid))
            live = (done < 0.5) & (hi_j - lo_j > 1)
            below = live & (c < need)
            above = live & (c >= need)
            lo_j = jnp.where(below, mid, lo_j)
            c_lo_j = jnp.where(below, c, c_lo_j)
            hi_j = jnp.where(above, mid, hi_j)
            c_hi_j = jnp.where(above, c, c_hi_j)
            done = jnp.where((c_hi_j == need) | (hi_j - lo_j <= 1), 1.0, done)
            return it + 1, lo_j, hi_j, c_lo_j, c_hi_j, done

        first = jnp.full((1, qn), -1, jnp.int32)
        final = jnp.full((1, qn), n_tiles * lt - 1, jnp.int32)
        n_ties = c_thr - (target - need)
        idx_done0 = jnp.where(tied & (n_ties > need), 0.0, 1.0)
        idx = lax.while_loop(idx_cond, idx_body,
                             (jnp.int32(0), first, final, 0.0 * ones, n_ties, idx_done0))
        last = idx[2]

        def drop_body(t, carry):
            s = s_ref[key_slice(t), :]
            s_ref[key_slice(t), :] = jnp.where(tied & (s == thr) & (key_pos(t) > last), -jnp.inf, s)
            return carry

        lax.fori_loop(0, n_tiles, drop_body, 0)

    qt = qt_ref[0]
    zeros = jnp.zeros((HEAD_DIM, B_GROUP * qn), BF16)
    for g in range(B_KV_HEADS):
        q_g = jnp.concatenate(
            [qt[(g * B_GROUP + j) * HEAD_DIM:(g * B_GROUP + j + 1) * HEAD_DIM, :] for j in range(B_GROUP)],
            axis=1)
        qpad_ref[g] = jnp.concatenate([zeros] * g + [q_g] + [zeros] * (B_KV_HEADS - 1 - g), axis=0)

    qf = qt.astype(F32)
    q_norm2 = jnp.sum((qf * qf).reshape(B_HEADS, HEAD_DIM, qn), axis=1)
    for g in range(B_KV_HEADS):
        b = jnp.sqrt(q_norm2[g * B_GROUP:(g + 1) * B_GROUP] * kmax_ref[0, 0:1, g:g + 1])
        b = b * BOUND_SLACK + BOUND_SLACK_ABS
        bound_ref[g] = jnp.concatenate([b[j:j + 1] for j in range(B_GROUP)], axis=1)
    acc_ref[...] = jnp.zeros(acc_ref.shape, F32)

    def fast_body(t, carry):
        keys = k_ref[0, key_slice(t), :]
        sel = jnp.where(s_ref[key_slice(t), :] >= thr, 1.0, 0.0).astype(BF16)
        sel = jnp.concatenate([sel] * B_GROUP, axis=1)
        def logits(g):
            return jnp.dot(keys, qpad_ref[g], preferred_element_type=F32)

        lg = logits(0)
        for g in range(B_KV_HEADS):
            lg_next = logits(g + 1) if g + 1 < B_KV_HEADS else None
            p = jnp.exp2(lg - bound_ref[g]).astype(BF16) * sel
            acc_ref[g] += jnp.dot(vta_ref[0, g, :, key_slice(t)], p, preferred_element_type=F32)
            lg = lg_next
        return carry

    paired_tile_loop(fast_body, 0)
    den_min = jnp.min(acc_ref[:, HEAD_DIM:HEAD_DIM + 1, :])

    @pl.when(jnp.logical_not(den_min >= MIN_TRUSTED_DENOMINATOR))
    def _():
        m_ref[...] = jnp.full(m_ref.shape, NEG_BIG, F32)
        acc_ref[...] = jnp.zeros(acc_ref.shape, F32)

        def exact_body(t, carry):
            keys = k_ref[0, key_slice(t), :]
            mb = jnp.where(s_ref[key_slice(t), :] >= thr, 0.0, -jnp.inf)
            mb = jnp.concatenate([mb] * B_GROUP, axis=1)
            for g in range(B_KV_HEADS):
                lg = jnp.dot(keys, qpad_ref[g], preferred_element_type=F32) + mb
                m_old = m_ref[g]
                m_new = jnp.maximum(m_old, jnp.max(fold_rows(lg, jnp.max), axis=0, keepdims=True))
                p = jnp.exp2(lg - m_new).astype(BF16)
                pv = jnp.dot(vta_ref[0, g, :, key_slice(t)], p, preferred_element_type=F32)
                acc_ref[g] = jnp.exp2(m_old - m_new) * acc_ref[g] + pv
                m_ref[g] = m_new
            return carry

        lax.fori_loop(0, n_tiles, exact_body, 0)

    for g in range(B_KV_HEADS):
        acc = acc_ref[g]
        o_g = acc[:HEAD_DIM] / acc[HEAD_DIM:HEAD_DIM + 1]
        for j in range(B_GROUP):
            hh = g * B_GROUP + j
            o_ref[0, hh * HEAD_DIM:(hh + 1) * HEAD_DIM, :] = o_g[:, j * qn:(j + 1) * qn].astype(o_ref.dtype)


V_AUG_ROWS = HEAD_DIM + 16


def _dsa(q, qi, kw, k, v, ki, *, q_tile, causal, k_sel):
    batch, seq, _ = q.shape
    n_keys = k.shape[1]
    lp = -(-n_keys // DSA_KEY_TILE) * DSA_KEY_TILE
    assert seq % q_tile == 0 and q_tile % LANES == 0
    pad = ((0, 0), (0, lp - n_keys), (0, 0))
    k, v, ki = (jnp.pad(a.astype(BF16), pad) for a in (k, v, ki))
    vt = jnp.swapaxes(v, 1, 2).reshape(batch, B_KV_HEADS, HEAD_DIM, lp)
    vta = jnp.concatenate([vt, jnp.ones((batch, B_KV_HEADS, 1, lp), BF16),
                           jnp.zeros((batch, B_KV_HEADS, V_AUG_ROWS - HEAD_DIM - 1, lp), BF16)], axis=2)
    qt, qit, kwt = (jnp.swapaxes(a, 1, 2) for a in (q, qi, kw))

    def qspec(rows):
        return pl.BlockSpec((1, rows, q_tile), lambda b, i: (b, 0, i))

    def resident(shape):
        zero = (0,) * len(shape)
        return pl.BlockSpec((1,) + shape, lambda b, i: (b,) + zero, pipeline_mode=pl.Buffered(1))

    ot = pl.pallas_call(
        functools.partial(_dsa_kernel, causal=causal, n_keys=n_keys, k_sel=k_sel),
        grid=(batch, seq // q_tile),
        in_specs=[qspec(B_Q), qspec(B_QI), qspec(LANES),
                  resident((lp, IDX_DIM)), resident((lp, B_KV)),
                  resident((B_KV_HEADS, V_AUG_ROWS, lp)),
                  pl.BlockSpec((1, 8, LANES), lambda b, i: (b, 0, 0))],
        out_specs=qspec(B_Q),
        out_shape=jax.ShapeDtypeStruct((batch, B_Q, seq), BF16),
        scratch_shapes=[pltpu.VMEM((lp, q_tile), F32),
                        pltpu.VMEM((B_KV_HEADS, B_KV, B_GROUP * q_tile), BF16),
                        pltpu.VMEM((B_KV_HEADS, 1, B_GROUP * q_tile), F32),
                        pltpu.VMEM((B_KV_HEADS, 1, B_GROUP * q_tile), F32),
                        pltpu.VMEM((B_KV_HEADS, V_AUG_ROWS, B_GROUP * q_tile), F32)],
        compiler_params=_params(2),
        name="dsa_prompt" if causal else "dsa_sample",
    )(qt, qit, kwt, ki, k, vta, _key_norm_max(k))
    return jnp.swapaxes(ot, 1, 2)


def kernel(x_prompt, x_sample, cache_k_a, cache_v_a, cache_k_b, cache_v_b, cache_kidx_b,
           c_prompt, c_sample, w_cond, b_cond, ln_g, ln_b, a_w_in, a_w_o, a_rel_bias,
           b_w_in, b_w_o, b_kidx_ln_g, b_kidx_ln_b, ffn_w_gu, ffn_w_down,
           moe_w_router, moe_b_router, moe_w_gu, moe_w_down):
    depth = w_cond.shape[0]
    alpha = (2 * depth) ** 0.25
    n_p, seq, d = x_prompt.shape
    n_s, dec_seq, _ = x_sample.shape
    past = cache_k_b.shape[2]
    a_hd = A_HEADS * HEAD_DIM

    rows = n_p + n_s
    rows_pad = -(-rows // 8) * 8
    c_all = jnp.pad(jnp.concatenate([c_prompt, c_sample], axis=0), ((0, rows_pad - rows), (0, 0)))
    mod_all = _modulation(c_all, w_cond, b_cond)

    xp, xs = x_prompt, x_sample
    outs = {k: [] for k in ("ka_p", "va_p", "kb_p", "vb_p", "ib_p", "ka_s", "va_s", "kb_s", "vb_s", "ib_s")}
    for i in range(depth):
        j = i // 2
        mod_p = mod_all[i, :n_p].reshape(n_p, 1, 6 * d)
        mod_s = mod_all[i, n_p:rows].reshape(n_s, 1, 6 * d)
        g1, b1, g2, b2 = ln_g[i, 0], ln_b[i, 0], ln_g[i, 1], ln_b[i, 1]
        if i % 2 == 0:
            w_in = a_w_in[j].astype(BF16)
            keep = min(BAND_ROWS, seq)
            qkv_p = _modmm(xp, mod_p, w_in, BF16)
            kv_p = _modmm(xp, mod_p, w_in[:, a_hd:], F32, t_start=seq - keep)
            qkv_s = _modmm(xs, mod_s, w_in, BF16)
            kv_s = _modmm(xs, mod_s, w_in[:, a_hd:], F32)
            outs["ka_p"].append(kv_p[..., :a_hd].reshape(n_p, keep, A_HEADS, HEAD_DIM))
            outs["va_p"].append(kv_p[..., a_hd:].reshape(n_p, keep, A_HEADS, HEAD_DIM))
            outs["ka_s"].append(kv_s[..., :a_hd].reshape(n_s, dec_seq, A_HEADS, HEAD_DIM))
            outs["va_s"].append(kv_s[..., a_hd:].reshape(n_s, dec_seq, A_HEADS, HEAD_DIM))
            o_p = _band_attn_prompt(qkv_p, a_rel_bias[j])
            ck = cache_k_a[j].reshape(n_s, -1, a_hd).astype(BF16)
            cv = cache_v_a[j].reshape(n_s, -1, a_hd).astype(BF16)
            o_s = _band_attn_sample(qkv_s, ck, cv, a_rel_bias[j])
            w_o = a_w_o[j].astype(BF16)
        else:
            w_in = jnp.pad(b_w_in[j], ((0, 0), (0, B_PROJ_PAD - B_PROJ))).astype(BF16)
            kn_g = jnp.pad(b_kidx_ln_g[j], (0, LANES - IDX_DIM)).reshape(1, LANES)
            kn_b = jnp.pad(b_kidx_ln_b[j], (0, LANES - IDX_DIM)).reshape(1, LANES)
            cos_p, sin_p = _rope_tables(jnp.arange(seq))
            cos_s, sin_s = _rope_tables(past + jnp.arange(dec_seq))
            bb_s, _ = _row_blocks(n_s, dec_seq, ROW_TILE)
            cos_s, sin_s = jnp.tile(cos_s, (bb_s, 1)), jnp.tile(sin_s, (bb_s, 1))
            q_p, qi_p, k_p, v_p, kw_p = _b_project(xp, mod_p, w_in, cos_p, sin_p, kn_g, kn_b, False)
            q_s, qi_s, k_s, v_s, kw_s = _b_project(xs, mod_s, w_in, cos_s, sin_s, kn_g, kn_b, True)
            outs["kb_p"].append(k_p.reshape(n_p, seq, B_KV_HEADS, HEAD_DIM))
            outs["vb_p"].append(v_p.reshape(n_p, seq, B_KV_HEADS, HEAD_DIM))
            outs["ib_p"].append(kw_p[..., :IDX_DIM])
            outs["kb_s"].append(k_s.reshape(n_s, dec_seq, B_KV_HEADS, HEAD_DIM))
            outs["vb_s"].append(v_s.reshape(n_s, dec_seq, B_KV_HEADS, HEAD_DIM))
            outs["ib_s"].append(kw_s[..., :IDX_DIM])
            o_p = _dsa(q_p, qi_p, kw_p, k_p, v_p, kw_p[..., :IDX_DIM], q_tile=DSA_Q_TILE, causal=True,
                       k_sel=min(TOPK_MAX, seq // 4))
            n_keys = past + dec_seq
            kk = jnp.concatenate([cache_k_b[j].reshape(n_s, past, B_KV), k_s], axis=1)
            vc = jnp.concatenate([cache_v_b[j].reshape(n_s, past, B_KV), v_s], axis=1)
            kki = jnp.concatenate([cache_kidx_b[j], kw_s[..., :IDX_DIM]], axis=1)
            rep = LANES // dec_seq
            q_r, qi_r, kw_r = (jnp.concatenate([a] * rep, axis=1) for a in (q_s, qi_s, kw_s))
            o_s = _dsa(q_r, qi_r, kw_r, kk, vc, kki, q_tile=LANES, causal=False,
                       k_sel=min(TOPK_MAX, n_keys // 4))[:, :dec_seq]
            w_o = b_w_o[j].astype(BF16)
        xp = _mm_postnorm(o_p, w_o, xp, mod_p, 2, g1, b1, alpha)
        xs = _mm_postnorm(o_s, w_o, xs, mod_s, 2, g1, b1, alpha)
        if i % 2 == 0:
            w_gu = ffn_w_gu[j].astype(BF16)
            w_dn = ffn_w_down[j].astype(BF16)
            dense = dict(ff_chunk=D_FF // 2, alpha=alpha)
            xp = _dense_ffn(xp, mod_p, w_gu, w_dn, g2, b2, **dense)
            xs = _dense_ffn(xs, mod_s, w_gu, w_dn, g2, b2, **dense)
        else:
            w_gu = moe_w_gu[j].astype(BF16)
            w_dn = moe_w_down[j].astype(BF16)
            w_r = jnp.pad(moe_w_router[j], ((0, 0), (0, LANES - N_EXPERTS)))
            b_r = jnp.pad(moe_b_router[j], (0, LANES - N_EXPERTS)).reshape(1, LANES)
            moe = dict(ff_chunk=MOE_FF_CHUNK, alpha=alpha)
            xp = _moe(xp, mod_p, w_r, b_r, w_gu, w_dn, g2, b2, **moe)
            xs = _moe(xs, mod_s, w_r, b_r, w_gu, w_dn, g2, b2, **moe)

    st = lambda name: jnp.stack(outs[name])
    return (xp, xs, st("ka_p"), st("va_p"), st("kb_p"), st("vb_p"), st("ib_p"),
            st("ka_s"), st("va_s"), st("kb_s"), st("vb_s"), st("ib_s"))
```

```python
import functools

import jax
import jax.numpy as jnp
from jax import lax
from jax.experimental import pallas as pl
from jax.experimental.pallas import tpu as pltpu

F32 = jnp.float32
BF16 = jnp.bfloat16

D_MODEL = 1024
CHUNK = 64
N_PAST_CHUNKS = 8
BAND_ROWS = N_PAST_CHUNKS * CHUNK
REL_CLIP = 2 * CHUNK
HEAD_DIM = 64
A_HEADS = 16
B_HEADS = 16
B_KV_HEADS = 4
B_GROUP = B_HEADS // B_KV_HEADS
IDX_HEADS = 8
IDX_DIM = 64
TOPK_MAX = 256
D_FF = 2816
N_EXPERTS = 8
D_FF_EXPERT = 3584
ROPE_THETA = 10000.0
LN_EPS = 1e-5
B_Q = B_HEADS * HEAD_DIM
B_KV = B_KV_HEADS * HEAD_DIM
B_QI = IDX_HEADS * IDX_DIM
B_PROJ = B_Q + 2 * B_KV + B_QI + IDX_DIM + IDX_HEADS

LANES = 128
VMEM_LIMIT_BYTES = 58 * 1024 * 1024

A_Q_TILE = 4 * CHUNK
DSA_Q_TILE = 4 * CHUNK
DSA_KEY_TILE = 512
ROW_TILE = 512
FFN_ROW_TILE = 1024
MOE_CAPACITY = 288
MOE_FF_CHUNK = 1792
B_PROJ_PAD = B_Q + 2 * B_KV + B_QI + LANES
NEG_BIG = -1e30
QK_SCALE_LOG2 = HEAD_DIM ** -0.5 * 1.4426950408889634
BOUND_SLACK = 1.02
BOUND_SLACK_ABS = 0.01
MIN_TRUSTED_DENOMINATOR = 2.0 ** -100
SECANT_STEPS = 12


def _params(n_grid):
    return pltpu.CompilerParams(
        dimension_semantics=("arbitrary",) * n_grid,
        vmem_limit_bytes=VMEM_LIMIT_BYTES,
    )


def _row_blocks(batch, seq, target):
    if seq >= target:
        assert seq % target == 0
        return 1, target
    bb = max(1, min(batch, target // seq))
    while batch % bb:
        bb -= 1
    return bb, seq


def _mod_spec(bb, chunk, n_grid):
    if n_grid == 2:
        return pl.BlockSpec((bb, 1, D_MODEL), lambda b, t: (b, 0, chunk))
    return pl.BlockSpec((bb, 1, D_MODEL), lambda b, t, e, c: (b, 0, chunk))


def _silu(x):
    return x / (1.0 + jnp.exp(-x))


def _layer_norm_rows(z, g, b):
    mu = jnp.mean(z, axis=-1, keepdims=True)
    zc = z - mu
    var = jnp.mean(zc * zc, axis=-1, keepdims=True)
    return zc * lax.rsqrt(var + LN_EPS) * g + b


def _modulation_kernel(c_ref, w_ref, b_ref, o_ref):
    a = _silu(c_ref[...]).astype(BF16)
    w = w_ref[0].astype(BF16)
    o_ref[0] = jnp.dot(a, w, preferred_element_type=F32) + b_ref[0]


def _modulation(c_all, w_cond, b_cond):
    depth, d, n = w_cond.shape
    rows = c_all.shape[0]
    tn = 1536
    return pl.pallas_call(
        _modulation_kernel,
        grid=(depth, n // tn),
        in_specs=[
            pl.BlockSpec((rows, d), lambda i, j: (0, 0)),
            pl.BlockSpec((1, d, tn), lambda i, j: (i, 0, j)),
            pl.BlockSpec((1, 1, tn), lambda i, j: (i, 0, j)),
        ],
        out_specs=pl.BlockSpec((1, rows, tn), lambda i, j: (i, 0, j)),
        out_shape=jax.ShapeDtypeStruct((depth, rows, n), F32),
        compiler_params=_params(2),
        name="modulation",
    )(c_all, w_cond, b_cond.reshape(depth, 1, n))


def _modmm_kernel(x_ref, sc_ref, sh_ref, w_ref, o_ref):
    bb, tt, d = x_ref.shape
    h = x_ref[...] * (1.0 + sc_ref[...]) + sh_ref[...]
    h = h.reshape(bb * tt, d).astype(BF16)
    acc = jnp.dot(h, w_ref[...], preferred_element_type=F32)
    o_ref[...] = acc.reshape(bb, tt, acc.shape[-1]).astype(o_ref.dtype)


def _modmm(x, mod, w, out_dtype, t_start=0):
    batch, seq, d = x.shape
    n = w.shape[1]
    seq_out = seq - t_start
    bb, tt = _row_blocks(batch, seq_out, ROW_TILE)
    assert t_start % tt == 0
    off = t_start // tt
    return pl.pallas_call(
        _modmm_kernel,
        grid=(batch // bb, seq_out // tt),
        in_specs=[
            pl.BlockSpec((bb, tt, d), lambda b, t: (b, t + off, 0)),
            _mod_spec(bb, 1, 2),
            _mod_spec(bb, 0, 2),
            pl.BlockSpec((d, n), lambda b, t: (0, 0)),
        ],
        out_specs=pl.BlockSpec((bb, tt, n), lambda b, t: (b, t, 0)),
        out_shape=jax.ShapeDtypeStruct((batch, seq_out, n), out_dtype),
        compiler_params=_params(2),
        name="a_project",
    )(x, mod, mod, w)


def _band_attn_kernel(*refs, n_kb, n_maybe_invalid):
    q_ref = refs[0]
    k_refs = refs[1:1 + n_kb]
    v_refs = refs[1 + n_kb:1 + 2 * n_kb]
    f_ref = refs[1 + 2 * n_kb]
    o_ref = refs[2 + 2 * n_kb]
    bias_ref = refs[3 + 2 * n_kb]
    q_tile = q_ref.shape[1]
    kb_sizes = [r.shape[1] for r in k_refs]
    k_tot = sum(kb_sizes)
    width = f_ref.shape[-1]
    i = pl.program_id(1)

    @pl.when((pl.program_id(0) == 0) & (i == 0))
    def _():
        rq = lax.broadcasted_iota(jnp.int32, (q_tile, k_tot), 0) // CHUNK
        ck = lax.broadcasted_iota(jnp.int32, (q_tile, k_tot), 1) // CHUNK
        in_band = (ck >= rq) & (ck <= rq + N_PAST_CHUNKS)
        for h in range(A_HEADS):
            rows = jnp.broadcast_to(f_ref[h], (q_tile, width))
            toep = pltpu.roll(rows, k_tot + 1, 1, stride=1, stride_axis=0)
            bias_ref[h] = jnp.where(in_band, toep[:, :k_tot], -jnp.inf)

    q = q_ref[0] * (HEAD_DIM ** -0.5)
    for h in range(A_HEADS):
        cols = slice(h * HEAD_DIM, (h + 1) * HEAD_DIM)
        qh = q[:, cols]
        parts = []
        for kb in range(n_kb):
            lg = lax.dot_general(qh, k_refs[kb][0, :, cols], (((1,), (1,)), ((), ())),
                                 preferred_element_type=F32)
            if kb < n_maybe_invalid:
                lg = jnp.where(i - n_maybe_invalid + kb >= 0, lg, -jnp.inf)
            parts.append(lg)
        logits = jnp.concatenate(parts, axis=1) + bias_ref[h]
        m = jnp.max(logits, axis=1, keepdims=True)
        p = jnp.exp(logits - m)
        l = jnp.sum(p, axis=1, keepdims=True)
        pb = p.astype(BF16)
        acc = jnp.zeros((q_tile, HEAD_DIM), F32)
        start = 0
        for kb in range(n_kb):
            acc = acc + jnp.dot(pb[:, start:start + kb_sizes[kb]], v_refs[kb][0, :, cols],
                                preferred_element_type=F32)
            start += kb_sizes[kb]
        o_ref[0, :, cols] = (acc / l).astype(o_ref.dtype)


def _bias_vector(table, q_tile, k_tot):
    width = k_tot + q_tile
    lo = -(q_tile - 1) - BAND_ROWS + REL_CLIP
    left = max(0, -lo)
    start = max(0, lo)
    n_mid = min(2 * REL_CLIP + 1 - start, width - left)
    right = width - left - n_mid
    mid = table.T[:, start:start + n_mid]
    f = jnp.pad(mid, ((0, 0), (left, right)), mode="edge")
    return f.reshape(A_HEADS, 1, width)


def _band_attn_prompt(qkv, table):
    batch, seq, _ = qkv.shape
    hd = A_HEADS * HEAD_DIM
    qt = A_Q_TILE
    n_prev = BAND_ROWS // qt
    n_kb = n_prev + 1
    k_tot = n_kb * qt
    f = _bias_vector(table, qt, k_tot)

    def kv_spec(kb, col):
        return pl.BlockSpec((1, qt, hd), lambda b, i: (b, jnp.maximum(i - n_prev + kb, 0), col))

    return pl.pallas_call(
        functools.partial(_band_attn_kernel, n_kb=n_kb, n_maybe_invalid=n_prev),
        grid=(batch, seq // qt),
        in_specs=[pl.BlockSpec((1, qt, hd), lambda b, i: (b, i, 0))]
        + [kv_spec(kb, 1) for kb in range(n_kb)]
        + [kv_spec(kb, 2) for kb in range(n_kb)]
        + [pl.BlockSpec(f.shape, lambda b, i: (0, 0, 0))],
        out_specs=pl.BlockSpec((1, qt, hd), lambda b, i: (b, i, 0)),
        out_shape=jax.ShapeDtypeStruct((batch, seq, hd), BF16),
        scratch_shapes=[pltpu.VMEM((A_HEADS, qt, k_tot), F32)],
        compiler_params=_params(2),
        name="band_attn_prompt",
    )(*([qkv] * (1 + 2 * n_kb)), f)


def _band_attn_sample(qkv, cache_k, cache_v, table):
    batch, seq, _ = qkv.shape
    hd = A_HEADS * HEAD_DIM
    win = cache_k.shape[1]
    assert seq == CHUNK and win == BAND_ROWS
    k_tot = win + seq
    f = _bias_vector(table, seq, k_tot)
    new = lambda col: pl.BlockSpec((1, seq, hd), lambda b, i: (b, 0, col))
    old = pl.BlockSpec((1, win, hd), lambda b, i: (b, 0, 0))
    return pl.pallas_call(
        functools.partial(_band_attn_kernel, n_kb=2, n_maybe_invalid=0),
        grid=(batch, 1),
        in_specs=[new(0), old, new(1), old, new(2), pl.BlockSpec(f.shape, lambda b, i: (0, 0, 0))],
        out_specs=pl.BlockSpec((1, seq, hd), lambda b, i: (b, 0, 0)),
        out_shape=jax.ShapeDtypeStruct((batch, seq, hd), BF16),
        scratch_shapes=[pltpu.VMEM((A_HEADS, seq, k_tot), F32)],
        compiler_params=_params(2),
        name="band_attn_sample",
    )(qkv, cache_k, qkv, cache_v, qkv, f)


def _mm_postnorm_kernel(o_ref, w_ref, x_ref, gate_ref, g_ref, b_ref, y_ref, *, alpha, o_transposed):
    bb, tt, d = x_ref.shape
    if o_transposed:
        sub = lax.dot_general(o_ref[0], w_ref[...], (((0,), (0,)), ((), ())), preferred_element_type=F32)
    else:
        o = o_ref[...].reshape(bb * tt, o_ref.shape[-1])
        sub = jnp.dot(o, w_ref[...], preferred_element_type=F32)
    z = alpha * x_ref[...] + (1.0 + gate_ref[...]) * sub.reshape(bb, tt, d)
    y_ref[...] = _layer_norm_rows(z, g_ref[...], b_ref[...])


def _mm_postnorm(o, w, x, mod, gate_chunk, g, b, alpha, o_transposed=False):
    batch, seq, d = x.shape
    bb, tt = _row_blocks(batch, seq, ROW_TILE)
    k = w.shape[0]
    assert bb == 1 or not o_transposed
    o_spec = (pl.BlockSpec((1, k, tt), lambda b_, t: (b_, 0, t)) if o_transposed
              else pl.BlockSpec((bb, tt, k), lambda b_, t: (b_, t, 0)))
    return pl.pallas_call(
        functools.partial(_mm_postnorm_kernel, alpha=alpha, o_transposed=o_transposed),
        grid=(batch // bb, seq // tt),
        in_specs=[
            o_spec,
            pl.BlockSpec((k, d), lambda b_, t: (0, 0)),
            pl.BlockSpec((bb, tt, d), lambda b_, t: (b_, t, 0)),
            _mod_spec(bb, gate_chunk, 2),
            pl.BlockSpec((1, 1, d), lambda b_, t: (0, 0, 0)),
            pl.BlockSpec((1, 1, d), lambda b_, t: (0, 0, 0)),
        ],
        out_specs=pl.BlockSpec((bb, tt, d), lambda b_, t: (b_, t, 0)),
        out_shape=jax.ShapeDtypeStruct((batch, seq, d), F32),
        compiler_params=_params(2),
        name="out_proj_postnorm",
    )(o, w, x, mod, g.reshape(1, 1, d), b.reshape(1, 1, d))


def _dense_ffn_kernel(x_ref, sc_ref, sh_ref, gate_ref, wg_ref, wu_ref, wd_ref, g_ref, b_ref,
                      y_ref, h_ref, acc_ref, *, alpha):
    bb, tt, d = x_ref.shape
    rows = bb * tt
    c = pl.program_id(2)

    @pl.when(c == 0)
    def _():
        h = (x_ref[...] * (1.0 + sc_ref[...]) + sh_ref[...]).reshape(rows, d)
        h_ref[...] = h.astype(BF16)
        acc_ref[...] = jnp.zeros_like(acc_ref)

    h = h_ref[...]
    gt = jnp.dot(h, wg_ref[...], preferred_element_type=F32)
    up = jnp.dot(h, wu_ref[...], preferred_element_type=F32)
    act = (_silu(gt) * up).astype(BF16)
    acc_ref[...] += jnp.dot(act, wd_ref[...], preferred_element_type=F32)

    @pl.when(c == pl.num_programs(2) - 1)
    def _():
        z = alpha * x_ref[...] + (1.0 + gate_ref[...]) * acc_ref[...].reshape(bb, tt, d)
        y_ref[...] = _layer_norm_rows(z, g_ref[...], b_ref[...])


def _dense_ffn(x, mod, w_gu, w_down, g, b, *, ff_chunk, alpha):
    batch, seq, d = x.shape
    ff = w_down.shape[0]
    assert ff % ff_chunk == 0
    n_c = ff // ff_chunk
    bb, tt = _row_blocks(batch, seq, FFN_ROW_TILE)
    rows = bb * tt
    vec = pl.BlockSpec((1, 1, d), lambda b_, t, c: (0, 0, 0))

    def mod_spec(chunk):
        return pl.BlockSpec((bb, 1, d), lambda b_, t, c: (b_, 0, chunk))

    return pl.pallas_call(
        functools.partial(_dense_ffn_kernel, alpha=alpha),
        grid=(batch // bb, seq // tt, n_c),
        in_specs=[
            pl.BlockSpec((bb, tt, d), lambda b_, t, c: (b_, t, 0)),
            mod_spec(4),
            mod_spec(3),
            mod_spec(5),
            pl.BlockSpec((d, ff_chunk), lambda b_, t, c: (0, c)),
            pl.BlockSpec((d, ff_chunk), lambda b_, t, c: (0, n_c + c)),
            pl.BlockSpec((ff_chunk, d), lambda b_, t, c: (c, 0)),
            vec,
            vec,
        ],
        out_specs=pl.BlockSpec((bb, tt, d), lambda b_, t, c: (b_, t, 0)),
        out_shape=jax.ShapeDtypeStruct((batch, seq, d), F32),
        scratch_shapes=[pltpu.VMEM((rows, d), BF16), pltpu.VMEM((rows, d), F32)],
        compiler_params=_params(3),
        name="dense_ffn",
    )(x, mod, mod, mod, w_gu, w_gu, w_down, g.reshape(1, 1, d), b.reshape(1, 1, d))


def _moe_kernel(x_ref, sc_ref, sh_ref, gate_ref, wr_ref, br_ref, wg_ref, wu_ref, wd_ref, g_ref, b_ref,
                y_ref, h_ref, comb_ref, slot_ref, cnt_ref, tri_ref, hc_ref, acc_ref, out_ref, *, alpha):
    bb, tt, d = x_ref.shape
    rows = bb * tt
    cap = hc_ref.shape[1]
    e = pl.program_id(2)
    c = pl.program_id(3)
    last_c = pl.num_programs(3) - 1
    lane = lax.broadcasted_iota(jnp.int32, (rows, LANES), 1)

    @pl.when((pl.program_id(0) == 0) & (pl.program_id(1) == 0) & (e == 0) & (c == 0))
    def _():
        r = lax.broadcasted_iota(jnp.int32, (rows, rows), 0)
        col = lax.broadcasted_iota(jnp.int32, (rows, rows), 1)
        tri_ref[...] = jnp.where(r <= col, 1.0, 0.0).astype(BF16)

    @pl.when((e == 0) & (c == 0))
    def _():
        h = (x_ref[...] * (1.0 + sc_ref[...]) + sh_ref[...]).reshape(rows, d)
        h_ref[...] = h.astype(BF16)
        out_ref[...] = jnp.zeros_like(out_ref)
        logits = jnp.dot(h, wr_ref[...], preferred_element_type=F32,
                         precision=lax.Precision.HIGHEST) + br_ref[...]
        logits = jnp.where(lane < N_EXPERTS, logits, -jnp.inf)
        m1 = jnp.max(logits, axis=1, keepdims=True)
        i1 = jnp.min(jnp.where(logits == m1, lane, LANES), axis=1, keepdims=True)
        rest = jnp.where(lane == i1, -jnp.inf, logits)
        m2 = jnp.max(rest, axis=1, keepdims=True)
        i2 = jnp.min(jnp.where(rest == m2, lane, LANES), axis=1, keepdims=True)
        e2 = jnp.exp(m2 - m1)
        den = 1.0 + e2
        comb_ref[...] = jnp.where(lane == i1, 1.0 / den, 0.0) + jnp.where(lane == i2, e2 / den, 0.0)
        member = jnp.where((lane == i1) | (lane == i2), 1.0, 0.0)
        member_t = member.T[:2 * N_EXPERTS]
        upto = jnp.dot(member_t.astype(BF16), tri_ref[...], preferred_element_type=F32)
        slot_ref[...] = jnp.where(member_t > 0, upto - 1.0, -1.0)
        cnt_ref[...] = jnp.broadcast_to(upto[:, rows - 1:rows], cnt_ref.shape)

    n_sub = (jnp.max(cnt_ref[pl.ds(e, 1), :]).astype(jnp.int32) + cap - 1) // cap

    def selection(s):
        slot = slot_ref[pl.ds(e, 1), :] - (s * cap).astype(F32)
        r = lax.broadcasted_iota(jnp.int32, (cap, rows), 0).astype(F32)
        return jnp.where(slot == r, 1.0, 0.0).astype(BF16)

    @pl.when(c == 0)
    def _():
        def pack(s, carry):
            hc_ref[s] = jnp.dot(selection(s), h_ref[...], preferred_element_type=F32).astype(BF16)
            acc_ref[s] = jnp.zeros(acc_ref.shape[1:], F32)
            return carry
        lax.fori_loop(0, n_sub, pack, 0)

    def expert(s, carry):
        hc = hc_ref[s]
        gt = jnp.dot(hc, wg_ref[0], preferred_element_type=F32)
        up = jnp.dot(hc, wu_ref[0], preferred_element_type=F32)
        act = (_silu(gt) * up).astype(BF16)
        acc_ref[s] += jnp.dot(act, wd_ref[0], preferred_element_type=F32)
        return carry

    lax.fori_loop(0, n_sub, expert, 0)

    @pl.when(c == last_c)
    def _():
        ce = jnp.sum(jnp.where(lane == e, comb_ref[...], 0.0), axis=1, keepdims=True)

        def unpack(s, carry):
            back = lax.dot_general(selection(s), acc_ref[s].astype(BF16), (((0,), (0,)), ((), ())),
                                   preferred_element_type=F32)
            out_ref[...] += ce * back
            return carry
        lax.fori_loop(0, n_sub, unpack, 0)

    @pl.when((e == pl.num_programs(2) - 1) & (c == last_c))
    def _():
        z = alpha * x_ref[...] + (1.0 + gate_ref[...]) * out_ref[...].reshape(bb, tt, d)
        y_ref[...] = _layer_norm_rows(z, g_ref[...], b_ref[...])


def _moe(x, mod, w_r, b_r, w_gu, w_down, g, b, *, ff_chunk, alpha):
    batch, seq, d = x.shape
    n_e, ff, _ = w_down.shape
    assert ff % ff_chunk == 0
    n_c = ff // ff_chunk
    bb, tt = _row_blocks(batch, seq, FFN_ROW_TILE)
    rows = bb * tt
    cap = MOE_CAPACITY
    max_sub = -(-rows // cap)
    vec = pl.BlockSpec((1, 1, d), lambda b_, t, e, c: (0, 0, 0))
    return pl.pallas_call(
        functools.partial(_moe_kernel, alpha=alpha),
        grid=(batch // bb, seq // tt, n_e, n_c),
        in_specs=[
            pl.BlockSpec((bb, tt, d), lambda b_, t, e, c: (b_, t, 0), pipeline_mode=pl.Buffered(1)),
            _mod_spec(bb, 4, 4),
            _mod_spec(bb, 3, 4),
            _mod_spec(bb, 5, 4),
            pl.BlockSpec((d, LANES), lambda b_, t, e, c: (0, 0)),
            pl.BlockSpec((1, LANES), lambda b_, t, e, c: (0, 0)),
            pl.BlockSpec((1, d, ff_chunk), lambda b_, t, e, c: (e, 0, c)),
            pl.BlockSpec((1, d, ff_chunk), lambda b_, t, e, c: (e, 0, n_c + c)),
            pl.BlockSpec((1, ff_chunk, d), lambda b_, t, e, c: (e, c, 0)),
            vec,
            vec,
        ],
        out_specs=pl.BlockSpec((bb, tt, d), lambda b_, t, e, c: (b_, t, 0)),
        out_shape=jax.ShapeDtypeStruct((batch, seq, d), F32),
        scratch_shapes=[
            pltpu.VMEM((rows, d), BF16),
            pltpu.VMEM((rows, LANES), F32),
            pltpu.VMEM((2 * N_EXPERTS, rows), F32),
            pltpu.VMEM((2 * N_EXPERTS, LANES), F32),
            pltpu.VMEM((rows, rows), BF16),
            pltpu.VMEM((max_sub, cap, d), BF16),
            pltpu.VMEM((max_sub, cap, d), F32),
            pltpu.VMEM((rows, d), F32),
        ],
        compiler_params=_params(4),
        name="moe_ffn",
    )(x, mod, mod, mod, w_r, b_r, w_gu, w_gu, w_down, g.reshape(1, 1, d), b.reshape(1, 1, d))


def _rope_lanes(seg, cos, sin_signed, first_half):
    width = seg.shape[-1]
    reps = width // LANES
    if reps > 1:
        cos = jnp.concatenate([cos] * reps, axis=1)
        sin_signed = jnp.concatenate([sin_signed] * reps, axis=1)
        first_half = jnp.concatenate([first_half] * reps, axis=1)
    half = HEAD_DIM // 2
    swapped = jnp.where(first_half, pltpu.roll(seg, width - half, 1), pltpu.roll(seg, half, 1))
    return seg * cos + swapped * sin_signed


def _b_project_kernel(x_ref, sc_ref, sh_ref, w_ref, cos_ref, sin_ref, kg_ref, kb_ref,
                      q_ref, qi_ref, k_ref, v_ref, kw_ref, *maybe_kwt_ref, transposed):
    bb, tt, d = x_ref.shape
    rows = bb * tt
    h = x_ref[...] * (1.0 + sc_ref[...]) + sh_ref[...]
    h = h.reshape(rows, d).astype(BF16)
    acc = jnp.dot(h, w_ref[...], preferred_element_type=F32)
    cos = cos_ref[...]
    sin_signed = sin_ref[...]
    lane = lax.broadcasted_iota(jnp.int32, (1, LANES), 1)
    first_half = (lane % HEAD_DIM) < (HEAD_DIM // 2)

    def out(ref, val):
        ref[...] = val.reshape(bb, tt, val.shape[-1]).astype(ref.dtype)

    def out_query_side(ref, val):
        if transposed:
            ref[0] = val.T.astype(ref.dtype)
        else:
            out(ref, val)

    o_k, o_v, o_qi, o_ki = B_Q, B_Q + B_KV, B_Q + 2 * B_KV, B_Q + 2 * B_KV + B_QI
    out_query_side(q_ref, _rope_lanes(acc[:, :o_k], cos, sin_signed, first_half) * QK_SCALE_LOG2)
    out(k_ref, _rope_lanes(acc[:, o_k:o_v], cos, sin_signed, first_half))
    out(v_ref, acc[:, o_v:o_qi])
    out_query_side(qi_ref, _rope_lanes(acc[:, o_qi:o_ki], cos, sin_signed, first_half))
    seg = acc[:, o_ki:]
    is_ki = lane < IDX_DIM
    mu = jnp.sum(jnp.where(is_ki, seg, 0.0), axis=1, keepdims=True) / IDX_DIM
    cen = jnp.where(is_ki, seg - mu, 0.0)
    var = jnp.sum(cen * cen, axis=1, keepdims=True) / IDX_DIM
    ki = cen * lax.rsqrt(var + LN_EPS) * kg_ref[...] + kb_ref[...]
    ki = _rope_lanes(ki, cos, sin_signed, first_half)
    kw = jnp.where(is_ki, ki, seg * (IDX_HEADS ** -0.5))
    out(kw_ref, kw)
    if transposed:
        maybe_kwt_ref[0][0] = kw.T


def _b_project(x, mod, w, cos, sin_signed, kn_g, kn_b, table_per_tile, transposed=False):
    batch, seq, d = x.shape
    bb, tt = _row_blocks(batch, seq, ROW_TILE)
    rows = bb * tt
    n = w.shape[1]
    assert bb == 1 or not transposed
    tab = pl.BlockSpec((rows, LANES), (lambda b, t: (0, 0)) if table_per_tile else (lambda b, t: (t, 0)))
    vec = pl.BlockSpec((1, LANES), lambda b, t: (0, 0))

    def o_spec(width):
        return pl.BlockSpec((bb, tt, width), lambda b, t: (b, t, 0))

    def o_shape(width, dtype):
        return jax.ShapeDtypeStruct((batch, seq, width), dtype)

    def t_spec(width):
        return pl.BlockSpec((1, width, tt), lambda b, t: (b, 0, t))

    def t_shape(width, dtype):
        return jax.ShapeDtypeStruct((batch, width, seq), dtype)

    q_spec, q_shape = (t_spec, t_shape) if transposed else (o_spec, o_shape)
    extra_specs = [t_spec(LANES)] if transposed else []
    extra_shapes = [t_shape(LANES, F32)] if transposed else []
    return pl.pallas_call(
        functools.partial(_b_project_kernel, transposed=transposed),
        grid=(batch // bb, seq // tt),
        in_specs=[
            pl.BlockSpec((bb, tt, d), lambda b, t: (b, t, 0)),
            _mod_spec(bb, 1, 2),
            _mod_spec(bb, 0, 2),
            pl.BlockSpec((d, n), lambda b, t: (0, 0)),
            tab,
            tab,
            vec,
            vec,
        ],
        out_specs=[q_spec(B_Q), q_spec(B_QI), o_spec(B_KV), o_spec(B_KV), o_spec(LANES)] + extra_specs,
        out_shape=[q_shape(B_Q, BF16), q_shape(B_QI, BF16), o_shape(B_KV, F32), o_shape(B_KV, F32),
                   o_shape(LANES, F32)] + extra_shapes,
        compiler_params=_params(2),
        name="b_project",
    )(x, mod, mod, w, cos, sin_signed, kn_g, kn_b)


def _rope_tables(pos):
    half = HEAD_DIM // 2
    inv = ROPE_THETA ** (-jnp.arange(half, dtype=F32) / half)
    ang = pos.astype(F32)[:, None] * inv[None, :]
    cos, sin = jnp.cos(ang), jnp.sin(ang)
    return (jnp.concatenate([cos, cos, cos, cos], axis=1),
            jnp.concatenate([-sin, sin, -sin, sin], axis=1))


def _key_norm_kernel(k_ref, grp_ref, o_ref):
    k = k_ref[0].astype(F32)
    n2 = jnp.dot((k * k).astype(BF16), grp_ref[...], preferred_element_type=F32)
    mx = jnp.broadcast_to(jnp.max(n2, axis=0, keepdims=True), o_ref.shape[1:])

    @pl.when(pl.program_id(1) == 0)
    def _():
        o_ref[0] = mx

    @pl.when(pl.program_id(1) > 0)
    def _():
        o_ref[0] = jnp.maximum(o_ref[0], mx)


def _key_norm_max(k):
    batch, lp, width = k.shape
    tile = DSA_KEY_TILE
    grp = (jnp.arange(width)[:, None] // HEAD_DIM == jnp.arange(LANES)[None, :]).astype(BF16)
    return pl.pallas_call(
        _key_norm_kernel,
        grid=(batch, lp // tile),
        in_specs=[pl.BlockSpec((1, tile, width), lambda b, t: (b, t, 0)),
                  pl.BlockSpec((width, LANES), lambda b, t: (0, 0))],
        out_specs=pl.BlockSpec((1, 8, LANES), lambda b, t: (b, 0, 0)),
        out_shape=jax.ShapeDtypeStruct((batch, 8, LANES), F32),
        compiler_params=_params(2),
        name="key_norm_max",
    )(k, grp)


def _dsa_kernel(qt_ref, qit_ref, kwt_ref, ki_ref, k_ref, vta_ref, kmax_ref, o_ref,
                s_ref, qpad_ref, bound_ref, m_ref, acc_ref, *, causal, n_keys, k_sel):
    qn = qt_ref.shape[2]
    lt = DSA_KEY_TILE
    i = pl.program_id(1)
    if causal:
        n_tiles = ((i + 1) * qn + lt - 1) // lt
        qpos = i * qn + lax.broadcasted_iota(jnp.int32, (1, qn), 1)
        limit = (qpos // CHUNK + 1) * CHUNK
    else:
        n_tiles = ki_ref.shape[1] // lt
        limit = jnp.full((1, qn), n_keys, jnp.int32)

    def key_slice(t):
        return pl.ds(pl.multiple_of(t * lt, lt), lt)

    def key_pos(t):
        return t * lt + lax.broadcasted_iota(jnp.int32, (lt, 1), 0)

    def fold_rows(x, op):
        return op(op(x.reshape(lt // 32, 4, 8, x.shape[-1]), axis=0), axis=0)

    def paired_tile_loop(body, init):
        def pair(t2, carry):
            return body(2 * t2 + 1, body(2 * t2, carry))
        carry = lax.fori_loop(0, n_tiles // 2, pair, init)
        return lax.cond(n_tiles % 2 == 1, lambda c: body(n_tiles - 1, c), lambda c: c, carry)

    qit = qit_ref[0]
    kwt = kwt_ref[0]
    qi_w = jnp.concatenate([qit[h * IDX_DIM:(h + 1) * IDX_DIM, :] for h in range(IDX_HEADS)], axis=1)
    w_row = jnp.concatenate([kwt[IDX_DIM + h:IDX_DIM + h + 1, :] for h in range(IDX_HEADS)], axis=1)
    w_row = w_row * (IDX_DIM ** -0.5)

    def score_body(t, carry):
        rmax, rmin = carry
        s = jnp.dot(ki_ref[0, key_slice(t), :], qi_w, preferred_element_type=F32)
        s = jnp.maximum(s, 0.0) * w_row
        sc = s[:, 0:qn]
        for h in range(1, IDX_HEADS):
            sc = sc + s[:, h * qn:(h + 1) * qn]
        adm = key_pos(t) < limit
        masked = jnp.where(adm, sc, -jnp.inf)
        s_ref[key_slice(t), :] = masked
        rmax = jnp.maximum(rmax, jnp.max(masked, axis=0, keepdims=True))
        rmin = jnp.minimum(rmin, jnp.min(jnp.where(adm, sc, jnp.inf), axis=0, keepdims=True))
        return rmax, rmin

    rmax, rmin = paired_tile_loop(score_body,
                                  (jnp.full((1, qn), -jnp.inf, F32), jnp.full((1, qn), jnp.inf, F32)))

    def count(pred):
        def body(t, acc):
            return acc + fold_rows(jnp.where(pred(s_ref[key_slice(t), :], key_pos(t)), 1.0, 0.0), jnp.sum)
        acc = paired_tile_loop(body, jnp.zeros((8, qn), F32))
        return jnp.sum(acc, axis=0, keepdims=True)

    def to_key(x):
        bits = lax.bitcast_convert_type(x, jnp.int32)
        return bits ^ ((bits >> 31) & 0x7FFFFFFF)

    def from_key(key):
        return lax.bitcast_convert_type(key ^ ((key >> 31) & 0x7FFFFFFF), F32)

    n_adm = limit.astype(F32)
    target = jnp.minimum(n_adm, float(k_sel))
    log_target = jnp.log(target)

    def any_left(done):
        return jnp.sum(done) < qn

    c_ge0 = count(lambda s, _: s >= 0.0)
    c_gt0 = count(lambda s, _: s > 0.0)
    is_pos = c_gt0 >= target
    is_neg = c_ge0 < target
    zero_key = jnp.zeros((1, qn), jnp.int32)
    lo0 = jnp.where(is_neg, to_key(rmin), zero_key)
    c_lo0 = jnp.where(is_neg, n_adm, c_ge0)
    c_hi0 = jnp.where(is_pos, 0.0, c_ge0)
    done0 = jnp.where((n_adm == target) | jnp.logical_not(is_pos | is_neg) | (c_lo0 == target), 1.0, 0.0)
    lo0 = jnp.where(n_adm == target, to_key(rmin), lo0)
    c_lo0 = jnp.where(n_adm == target, n_adm, c_lo0)

    ones = jnp.ones((1, qn), F32)

    def search(st0, key_to_value, value_to_key, count_ge):
        def cond(st):
            return any_left(st[7])

        def body(st):
            it, lo, hi, c_lo, c_hi, w_lo, w_hi, done, side = st
            t_lo, t_hi = key_to_value(lo), key_to_value(hi)
            f_lo = (jnp.log(c_lo) - log_target) * w_lo
            f_hi = (log_target - jnp.log(jnp.maximum(c_hi, 0.5))) * w_hi
            guess = value_to_key(t_lo + f_lo / (f_lo + f_hi) * (t_hi - t_lo))
            halve = (lo >> 1) + (hi >> 1) + (lo & hi & 1)
            mid = jnp.where(it >= SECANT_STEPS, halve, jnp.clip(guess, lo + 1, hi - 1))
            stuck = halve == lo
            c = count_ge(key_to_value(mid))
            live = (done < 0.5) & jnp.logical_not(stuck)
            up = live & (c >= target)
            dn = live & (c < target)
            w_hi = jnp.where(up, jnp.where(side > 0, 0.5 * w_hi, 1.0), jnp.where(dn, 1.0, w_hi))
            w_lo = jnp.where(dn, jnp.where(side < 0, 0.5 * w_lo, 1.0), jnp.where(up, 1.0, w_lo))
            side = jnp.where(up, 1.0, jnp.where(dn, -1.0, side))
            lo = jnp.where(up, mid, lo)
            c_lo = jnp.where(up, c, c_lo)
            hi = jnp.where(dn, mid, hi)
            c_hi = jnp.where(dn, c, c_hi)
            done = jnp.where(stuck | (c_lo == target) | (c_lo - c_hi <= 2.0), 1.0, done)
            return it + 1, lo, hi, c_lo, c_hi, w_lo, w_hi, done, side

        lo, hi, c_lo, c_hi, done = st0
        st = lax.while_loop(cond, body, (jnp.int32(0), lo, hi, c_lo, c_hi, ones, ones, done, 0.0 * ones))
        return st[1], st[2], st[3], st[4]

    hi0 = jnp.where(is_pos, to_key(rmax) + 1, zero_key)
    k_lo, k_hi, c_thr, c_above = search((lo0, hi0, c_lo0, c_hi0, done0),
                                        from_key, to_key, lambda v: count(lambda s, _: s >= v))
    thr = from_key(k_lo)

    two_left = (c_thr != target) & (c_thr - c_above == 2.0)

    def settle(_):
        t_lo, t_hi = from_key(k_lo), from_key(k_hi)

        def body(t, carry):
            top, bottom = carry
            s = s_ref[key_slice(t), :]
            top = jnp.maximum(top, fold_rows(jnp.where(s < t_hi, s, -jnp.inf), jnp.max))
            bottom = jnp.minimum(bottom, fold_rows(jnp.where(s >= t_lo, s, jnp.inf), jnp.min))
            return top, bottom

        top, bottom = lax.fori_loop(0, n_tiles, body, (jnp.full((8, qn), -jnp.inf, F32),
                                                       jnp.full((8, qn), jnp.inf, F32)))
        top = jnp.max(top, axis=0, keepdims=True)
        bottom = jnp.min(bottom, axis=0, keepdims=True)
        return (jnp.where(two_left, top, thr),
                jnp.where(two_left & (top != bottom), target, c_thr))

    thr, c_thr = lax.cond(jnp.sum(jnp.where(two_left, 1.0, 0.0)) > 0, settle, lambda _: (thr, c_thr), 0)

    tied = c_thr > target

    @pl.when(jnp.sum(jnp.where(tied, 1.0, 0.0)) > 0)
    def _():
        need = target - count(lambda s, _: s > thr)

        def idx_cond(st):
            return any_left(st[5])

        def idx_body(st):
            it, lo_j, hi_j, c_lo_j, c_hi_j, done = st
            span = (hi_j - lo_j).astype(F32)
            step = jnp.ceil((need - c_lo_j) / (c_hi_j - c_lo_j) * span).astype(jnp.int32)
            step = jnp.where(it % 2 == 0, step, (hi_j - lo_j) // 2)
            mid = lo_j + jnp.clip(step, 1, jnp.maximum(hi_j - lo_j - 1, 1))
            c = count(lambda s, kp: (s == thr) & (kp <= mid))
            live = (done < 0.5) & (hi_j - lo_j > 1)
            below = live & (c < need)
            above = live & (c >= need)
            lo_j = jnp.where(below, mid, lo_j)
            c_lo_j = jnp.where(below, c, c_lo_j)
            hi_j = jnp.where(above, mid, hi_j)
            c_hi_j = jnp.where(above, c, c_hi_j)
            done = jnp.where((c_hi_j == need) | (hi_j - lo_j <= 1), 1.0, done)
            return it + 1, lo_j, hi_j, c_lo_j, c_hi_j, done

        first = jnp.full((1, qn), -1, jnp.int32)
        final = jnp.full((1, qn), n_tiles * lt - 1, jnp.int32)
        n_ties = c_thr - (target - need)
        idx_done0 = jnp.where(tied & (n_ties > need), 0.0, 1.0)
        idx = lax.while_loop(idx_cond, idx_body,
                             (jnp.int32(0), first, final, 0.0 * ones, n_ties, idx_done0))
        last = idx[2]

        def drop_body(t, carry):
            s = s_ref[key_slice(t), :]
            s_ref[key_slice(t), :] = jnp.where(tied & (s == thr) & (key_pos(t) > last), -jnp.inf, s)
            return carry

        lax.fori_loop(0, n_tiles, drop_body, 0)

    qt = qt_ref[0]
    zeros = jnp.zeros((HEAD_DIM, B_GROUP * qn), BF16)
    for g in range(B_KV_HEADS):
        q_g = jnp.concatenate(
            [qt[(g * B_GROUP + j) * HEAD_DIM:(g * B_GROUP + j + 1) * HEAD_DIM, :] for j in range(B_GROUP)],
            axis=1)
        qpad_ref[g] = jnp.concatenate([zeros] * g + [q_g] + [zeros] * (B_KV_HEADS - 1 - g), axis=0)

    qf = qt.astype(F32)
    q_norm2 = jnp.sum((qf * qf).reshape(B_HEADS, HEAD_DIM, qn), axis=1)
    for g in range(B_KV_HEADS):
        b = jnp.sqrt(q_norm2[g * B_GROUP:(g + 1) * B_GROUP] * kmax_ref[0, 0:1, g:g + 1])
        b = b * BOUND_SLACK + BOUND_SLACK_ABS
        bound_ref[g] = jnp.concatenate([b[j:j + 1] for j in range(B_GROUP)], axis=1)
    acc_ref[...] = jnp.zeros(acc_ref.shape, F32)

    def fast_body(t, carry):
        keys = k_ref[0, key_slice(t), :]
        sel = jnp.where(s_ref[key_slice(t), :] >= thr, 1.0, 0.0).astype(BF16)
        sel = jnp.concatenate([sel] * B_GROUP, axis=1)
        def logits(g):
            return jnp.dot(keys, qpad_ref[g], preferred_element_type=F32)

        lg = logits(0)
        for g in range(B_KV_HEADS):
            lg_next = logits(g + 1) if g + 1 < B_KV_HEADS else None
            p = jnp.exp2(lg - bound_ref[g]).astype(BF16) * sel
            acc_ref[g] += jnp.dot(vta_ref[0, g, :, key_slice(t)], p, preferred_element_type=F32)
            lg = lg_next
        return carry

    paired_tile_loop(fast_body, 0)
    den_min = jnp.min(acc_ref[:, HEAD_DIM:HEAD_DIM + 1, :])

    @pl.when(jnp.logical_not(den_min >= MIN_TRUSTED_DENOMINATOR))
    def _():
        m_ref[...] = jnp.full(m_ref.shape, NEG_BIG, F32)
        acc_ref[...] = jnp.zeros(acc_ref.shape, F32)

        def exact_body(t, carry):
            keys = k_ref[0, key_slice(t), :]
            mb = jnp.where(s_ref[key_slice(t), :] >= thr, 0.0, -jnp.inf)
            mb = jnp.concatenate([mb] * B_GROUP, axis=1)
            for g in range(B_KV_HEADS):
                lg = jnp.dot(keys, qpad_ref[g], preferred_element_type=F32) + mb
                m_old = m_ref[g]
                m_new = jnp.maximum(m_old, jnp.max(fold_rows(lg, jnp.max), axis=0, keepdims=True))
                p = jnp.exp2(lg - m_new).astype(BF16)
                pv = jnp.dot(vta_ref[0, g, :, key_slice(t)], p, preferred_element_type=F32)
                acc_ref[g] = jnp.exp2(m_old - m_new) * acc_ref[g] + pv
                m_ref[g] = m_new
            return carry

        lax.fori_loop(0, n_tiles, exact_body, 0)

    for g in range(B_KV_HEADS):
        acc = acc_ref[g]
        o_g = acc[:HEAD_DIM] / acc[HEAD_DIM:HEAD_DIM + 1]
        for j in range(B_GROUP):
            hh = g * B_GROUP + j
            o_ref[0, hh * HEAD_DIM:(hh + 1) * HEAD_DIM, :] = o_g[:, j * qn:(j + 1) * qn].astype(o_ref.dtype)


V_AUG_ROWS = HEAD_DIM + 16


def _dsa(q, qi, kw, k, v, ki, *, q_tile, causal, k_sel, transposed=False):
    batch = q.shape[0]
    seq = q.shape[2] if transposed else q.shape[1]
    n_keys = k.shape[1]
    lp = -(-n_keys // DSA_KEY_TILE) * DSA_KEY_TILE
    assert seq % q_tile == 0 and q_tile % LANES == 0
    pad = ((0, 0), (0, lp - n_keys), (0, 0))
    k, v, ki = (jnp.pad(a.astype(BF16), pad) for a in (k, v, ki))
    vt = jnp.swapaxes(v, 1, 2).reshape(batch, B_KV_HEADS, HEAD_DIM, lp)
    vta = jnp.concatenate([vt, jnp.ones((batch, B_KV_HEADS, 1, lp), BF16),
                           jnp.zeros((batch, B_KV_HEADS, V_AUG_ROWS - HEAD_DIM - 1, lp), BF16)], axis=2)
    qt, qit, kwt = (q, qi, kw) if transposed else (jnp.swapaxes(a, 1, 2) for a in (q, qi, kw))

    def qspec(rows):
        return pl.BlockSpec((1, rows, q_tile), lambda b, i: (b, 0, i))

    def resident(shape):
        zero = (0,) * len(shape)
        return pl.BlockSpec((1,) + shape, lambda b, i: (b,) + zero, pipeline_mode=pl.Buffered(1))

    ot = pl.pallas_call(
        functools.partial(_dsa_kernel, causal=causal, n_keys=n_keys, k_sel=k_sel),
        grid=(batch, seq // q_tile),
        in_specs=[qspec(B_Q), qspec(B_QI), qspec(LANES),
                  resident((lp, IDX_DIM)), resident((lp, B_KV)),
                  resident((B_KV_HEADS, V_AUG_ROWS, lp)),
                  pl.BlockSpec((1, 8, LANES), lambda b, i: (b, 0, 0))],
        out_specs=qspec(B_Q),
        out_shape=jax.ShapeDtypeStruct((batch, B_Q, seq), BF16),
        scratch_shapes=[pltpu.VMEM((lp, q_tile), F32),
                        pltpu.VMEM((B_KV_HEADS, B_KV, B_GROUP * q_tile), BF16),
                        pltpu.VMEM((B_KV_HEADS, 1, B_GROUP * q_tile), F32),
                        pltpu.VMEM((B_KV_HEADS, 1, B_GROUP * q_tile), F32),
                        pltpu.VMEM((B_KV_HEADS, V_AUG_ROWS, B_GROUP * q_tile), F32)],
        compiler_params=_params(2),
        name="dsa_prompt" if causal else "dsa_sample",
    )(qt, qit, kwt, ki, k, vta, _key_norm_max(k))
    return ot if transposed else jnp.swapaxes(ot, 1, 2)


def kernel(x_prompt, x_sample, cache_k_a, cache_v_a, cache_k_b, cache_v_b, cache_kidx_b,
           c_prompt, c_sample, w_cond, b_cond, ln_g, ln_b, a_w_in, a_w_o, a_rel_bias,
           b_w_in, b_w_o, b_kidx_ln_g, b_kidx_ln_b, ffn_w_gu, ffn_w_down,
           moe_w_router, moe_b_router, moe_w_gu, moe_w_down):
    depth = w_cond.shape[0]
    alpha = (2 * depth) ** 0.25
    n_p, seq, d = x_prompt.shape
    n_s, dec_seq, _ = x_sample.shape
    past = cache_k_b.shape[2]
    a_hd = A_HEADS * HEAD_DIM

    rows = n_p + n_s
    rows_pad = -(-rows // 8) * 8
    c_all = jnp.pad(jnp.concatenate([c_prompt, c_sample], axis=0), ((0, rows_pad - rows), (0, 0)))
    mod_all = _modulation(c_all, w_cond, b_cond)

    xp, xs = x_prompt, x_sample
    outs = {k: [] for k in ("ka_p", "va_p", "kb_p", "vb_p", "ib_p", "ka_s", "va_s", "kb_s", "vb_s", "ib_s")}
    for i in range(depth):
        j = i // 2
        mod_p = mod_all[i, :n_p].reshape(n_p, 1, 6 * d)
        mod_s = mod_all[i, n_p:rows].reshape(n_s, 1, 6 * d)
        g1, b1, g2, b2 = ln_g[i, 0], ln_b[i, 0], ln_g[i, 1], ln_b[i, 1]
        if i % 2 == 0:
            w_in = a_w_in[j].astype(BF16)
            keep = min(BAND_ROWS, seq)
            qkv_p = _modmm(xp, mod_p, w_in, BF16)
            kv_p = _modmm(xp, mod_p, w_in[:, a_hd:], F32, t_start=seq - keep)
            qkv_s = _modmm(xs, mod_s, w_in, BF16)
            kv_s = _modmm(xs, mod_s, w_in[:, a_hd:], F32)
            outs["ka_p"].append(kv_p[..., :a_hd].reshape(n_p, keep, A_HEADS, HEAD_DIM))
            outs["va_p"].append(kv_p[..., a_hd:].reshape(n_p, keep, A_HEADS, HEAD_DIM))
            outs["ka_s"].append(kv_s[..., :a_hd].reshape(n_s, dec_seq, A_HEADS, HEAD_DIM))
            outs["va_s"].append(kv_s[..., a_hd:].reshape(n_s, dec_seq, A_HEADS, HEAD_DIM))
            o_p = _band_attn_prompt(qkv_p, a_rel_bias[j])
            ck = cache_k_a[j].reshape(n_s, -1, a_hd).astype(BF16)
            cv = cache_v_a[j].reshape(n_s, -1, a_hd).astype(BF16)
            o_s = _band_attn_sample(qkv_s, ck, cv, a_rel_bias[j])
            w_o = a_w_o[j].astype(BF16)
        else:
            w_in = jnp.pad(b_w_in[j], ((0, 0), (0, B_PROJ_PAD - B_PROJ))).astype(BF16)
            kn_g = jnp.pad(b_kidx_ln_g[j], (0, LANES - IDX_DIM)).reshape(1, LANES)
            kn_b = jnp.pad(b_kidx_ln_b[j], (0, LANES - IDX_DIM)).reshape(1, LANES)
            cos_p, sin_p = _rope_tables(jnp.arange(seq))
            cos_s, sin_s = _rope_tables(past + jnp.arange(dec_seq))
            bb_s, _ = _row_blocks(n_s, dec_seq, ROW_TILE)
            cos_s, sin_s = jnp.tile(cos_s, (bb_s, 1)), jnp.tile(sin_s, (bb_s, 1))
            q_p, qi_p, k_p, v_p, kw_p, kwt_p = _b_project(xp, mod_p, w_in, cos_p, sin_p, kn_g, kn_b, False,
                                                          transposed=True)
            q_s, qi_s, k_s, v_s, kw_s = _b_project(xs, mod_s, w_in, cos_s, sin_s, kn_g, kn_b, True)
            outs["kb_p"].append(k_p.reshape(n_p, seq, B_KV_HEADS, HEAD_DIM))
            outs["vb_p"].append(v_p.reshape(n_p, seq, B_KV_HEADS, HEAD_DIM))
            outs["ib_p"].append(kw_p[..., :IDX_DIM])
            outs["kb_s"].append(k_s.reshape(n_s, dec_seq, B_KV_HEADS, HEAD_DIM))
            outs["vb_s"].append(v_s.reshape(n_s, dec_seq, B_KV_HEADS, HEAD_DIM))
            outs["ib_s"].append(kw_s[..., :IDX_DIM])
            o_p = _dsa(q_p, qi_p, kwt_p, k_p, v_p, kw_p[..., :IDX_DIM], q_tile=DSA_Q_TILE, causal=True,
                       k_sel=min(TOPK_MAX, seq // 4), transposed=True)
            n_keys = past + dec_seq
            kk = jnp.concatenate([cache_k_b[j].reshape(n_s, past, B_KV), k_s], axis=1)
            vc = jnp.concatenate([cache_v_b[j].reshape(n_s, past, B_KV), v_s], axis=1)
            kki = jnp.concatenate([cache_kidx_b[j], kw_s[..., :IDX_DIM]], axis=1)
            rep = LANES // dec_seq
            q_r, qi_r, kw_r = (jnp.concatenate([a] * rep, axis=1) for a in (q_s, qi_s, kw_s))
            o_s = _dsa(q_r, qi_r, kw_r, kk, vc, kki, q_tile=LANES, causal=False,
                       k_sel=min(TOPK_MAX, n_keys // 4))[:, :dec_seq]
            w_o = b_w_o[j].astype(BF16)
        xp = _mm_postnorm(o_p, w_o, xp, mod_p, 2, g1, b1, alpha, o_transposed=(i % 2 == 1))
        xs = _mm_postnorm(o_s, w_o, xs, mod_s, 2, g1, b1, alpha)
        if i % 2 == 0:
            w_gu = ffn_w_gu[j].astype(BF16)
            w_dn = ffn_w_down[j].astype(BF16)
            dense = dict(ff_chunk=D_FF // 2, alpha=alpha)
            xp = _dense_ffn(xp, mod_p, w_gu, w_dn, g2, b2, **dense)
            xs = _dense_ffn(xs, mod_s, w_gu, w_dn, g2, b2, **dense)
        else:
            w_gu = moe_w_gu[j].astype(BF16)
            w_dn = moe_w_down[j].astype(BF16)
            w_r = jnp.pad(moe_w_router[j], ((0, 0), (0, LANES - N_EXPERTS)))
            b_r = jnp.pad(moe_b_router[j], (0, LANES - N_EXPERTS)).reshape(1, LANES)
            moe = dict(ff_chunk=MOE_FF_CHUNK, alpha=alpha)
            xp = _moe(xp, mod_p, w_r, b_r, w_gu, w_dn, g2, b2, **moe)
            xs = _moe(xs, mod_s, w_r, b_r, w_gu, w_dn, g2, b2, **moe)

    st = lambda name: jnp.stack(outs[name])
    return (xp, xs, st("ka_p"), st("va_p"), st("kb_p"), st("vb_p"), st("ib_p"),
            st("ka_s"), st("va_s"), st("kb_s"), st("vb_s"), st("ib_s"))
```

```python
import functools

import jax
import jax.numpy as jnp
from jax import lax
from jax.experimental import pallas as pl
from jax.experimental.pallas import tpu as pltpu

F32 = jnp.float32
BF16 = jnp.bfloat16

D_MODEL = 1024
CHUNK = 64
N_PAST_CHUNKS = 8
BAND_ROWS = N_PAST_CHUNKS * CHUNK
REL_CLIP = 2 * CHUNK
HEAD_DIM = 64
A_HEADS = 16
B_HEADS = 16
B_KV_HEADS = 4
B_GROUP = B_HEADS // B_KV_HEADS
IDX_HEADS = 8
IDX_DIM = 64
TOPK_MAX = 256
D_FF = 2816
N_EXPERTS = 8
D_FF_EXPERT = 3584
ROPE_THETA = 10000.0
LN_EPS = 1e-5
B_Q = B_HEADS * HEAD_DIM
B_KV = B_KV_HEADS * HEAD_DIM
B_QI = IDX_HEADS * IDX_DIM
B_PROJ = B_Q + 2 * B_KV + B_QI + IDX_DIM + IDX_HEADS

LANES = 128
VMEM_LIMIT_BYTES = 58 * 1024 * 1024

A_Q_TILE = 4 * CHUNK
DSA_Q_TILE = 4 * CHUNK
DSA_KEY_TILE = 512
ROW_TILE = 512
FFN_ROW_TILE = 1024
MOE_CAPACITY = 288
MOE_FF_CHUNK = 1792
B_PROJ_PAD = B_Q + 2 * B_KV + B_QI + LANES
NEG_BIG = -1e30
QK_SCALE_LOG2 = HEAD_DIM ** -0.5 * 1.4426950408889634
BOUND_SLACK = 1.02
BOUND_SLACK_ABS = 0.01
MIN_TRUSTED_DENOMINATOR = 2.0 ** -100
SECANT_STEPS = 12


def _params(n_grid):
    return pltpu.CompilerParams(
        dimension_semantics=("arbitrary",) * n_grid,
        vmem_limit_bytes=VMEM_LIMIT_BYTES,
    )


def _row_blocks(batch, seq, target):
    if seq >= target:
        assert seq % target == 0
        return 1, target
    bb = max(1, min(batch, target // seq))
    while batch % bb:
        bb -= 1
    return bb, seq


def _mod_spec(bb, chunk, n_grid):
    if n_grid == 2:
        return pl.BlockSpec((bb, 1, D_MODEL), lambda b, t: (b, 0, chunk))
    return pl.BlockSpec((bb, 1, D_MODEL), lambda b, t, e, c: (b, 0, chunk))


def _silu(x):
    return x / (1.0 + jnp.exp(-x))


def _layer_norm_rows(z, g, b):
    mu = jnp.mean(z, axis=-1, keepdims=True)
    zc = z - mu
    var = jnp.mean(zc * zc, axis=-1, keepdims=True)
    return zc * lax.rsqrt(var + LN_EPS) * g + b


def _modulation_kernel(c_ref, w_ref, b_ref, o_ref):
    a = _silu(c_ref[...]).astype(BF16)
    w = w_ref[0].astype(BF16)
    o_ref[0] = jnp.dot(a, w, preferred_element_type=F32) + b_ref[0]


def _modulation(c_all, w_cond, b_cond):
    depth, d, n = w_cond.shape
    rows = c_all.shape[0]
    tn = 1536
    return pl.pallas_call(
        _modulation_kernel,
        grid=(depth, n // tn),
        in_specs=[
            pl.BlockSpec((rows, d), lambda i, j: (0, 0)),
            pl.BlockSpec((1, d, tn), lambda i, j: (i, 0, j)),
            pl.BlockSpec((1, 1, tn), lambda i, j: (i, 0, j)),
        ],
        out_specs=pl.BlockSpec((1, rows, tn), lambda i, j: (i, 0, j)),
        out_shape=jax.ShapeDtypeStruct((depth, rows, n), F32),
        compiler_params=_params(2),
        name="modulation",
    )(c_all, w_cond, b_cond.reshape(depth, 1, n))


def _modmm_kernel(x_ref, sc_ref, sh_ref, w_ref, o_ref):
    bb, tt, d = x_ref.shape
    h = x_ref[...] * (1.0 + sc_ref[...]) + sh_ref[...]
    h = h.reshape(bb * tt, d).astype(BF16)
    acc = jnp.dot(h, w_ref[...], preferred_element_type=F32)
    o_ref[...] = acc.reshape(bb, tt, acc.shape[-1]).astype(o_ref.dtype)


def _modmm(x, mod, w, out_dtype, t_start=0):
    batch, seq, d = x.shape
    n = w.shape[1]
    seq_out = seq - t_start
    bb, tt = _row_blocks(batch, seq_out, ROW_TILE)
    assert t_start % tt == 0
    off = t_start // tt
    return pl.pallas_call(
        _modmm_kernel,
        grid=(batch // bb, seq_out // tt),
        in_specs=[
            pl.BlockSpec((bb, tt, d), lambda b, t: (b, t + off, 0)),
            _mod_spec(bb, 1, 2),
            _mod_spec(bb, 0, 2),
            pl.BlockSpec((d, n), lambda b, t: (0, 0)),
        ],
        out_specs=pl.BlockSpec((bb, tt, n), lambda b, t: (b, t, 0)),
        out_shape=jax.ShapeDtypeStruct((batch, seq_out, n), out_dtype),
        compiler_params=_params(2),
        name="a_project",
    )(x, mod, mod, w)


def _band_attn_kernel(*refs, n_kb, n_maybe_invalid):
    q_ref = refs[0]
    k_refs = refs[1:1 + n_kb]
    v_refs = refs[1 + n_kb:1 + 2 * n_kb]
    f_ref = refs[1 + 2 * n_kb]
    o_ref = refs[2 + 2 * n_kb]
    bias_ref = refs[3 + 2 * n_kb]
    q_tile = q_ref.shape[1]
    kb_sizes = [r.shape[1] for r in k_refs]
    k_tot = sum(kb_sizes)
    width = f_ref.shape[-1]
    i = pl.program_id(1)

    @pl.when((pl.program_id(0) == 0) & (i == 0))
    def _():
        rq = lax.broadcasted_iota(jnp.int32, (q_tile, k_tot), 0) // CHUNK
        ck = lax.broadcasted_iota(jnp.int32, (q_tile, k_tot), 1) // CHUNK
        in_band = (ck >= rq) & (ck <= rq + N_PAST_CHUNKS)
        for h in range(A_HEADS):
            rows = jnp.broadcast_to(f_ref[h], (q_tile, width))
            toep = pltpu.roll(rows, k_tot + 1, 1, stride=1, stride_axis=0)
            bias_ref[h] = jnp.where(in_band, toep[:, :k_tot], -jnp.inf)

    q = q_ref[0] * (HEAD_DIM ** -0.5)
    for h in range(A_HEADS):
        cols = slice(h * HEAD_DIM, (h + 1) * HEAD_DIM)
        qh = q[:, cols]
        parts = []
        for kb in range(n_kb):
            lg = lax.dot_general(qh, k_refs[kb][0, :, cols], (((1,), (1,)), ((), ())),
                                 preferred_element_type=F32)
            if kb < n_maybe_invalid:
                lg = jnp.where(i - n_maybe_invalid + kb >= 0, lg, -jnp.inf)
            parts.append(lg)
        logits = jnp.concatenate(parts, axis=1) + bias_ref[h]
        m = jnp.max(logits, axis=1, keepdims=True)
        p = jnp.exp(logits - m)
        l = jnp.sum(p, axis=1, keepdims=True)
        pb = p.astype(BF16)
        acc = jnp.zeros((q_tile, HEAD_DIM), F32)
        start = 0
        for kb in range(n_kb):
            acc = acc + jnp.dot(pb[:, start:start + kb_sizes[kb]], v_refs[kb][0, :, cols],
                                preferred_element_type=F32)
            start += kb_sizes[kb]
        o_ref[0, :, cols] = (acc / l).astype(o_ref.dtype)


def _bias_vector(table, q_tile, k_tot):
    width = k_tot + q_tile
    lo = -(q_tile - 1) - BAND_ROWS + REL_CLIP
    left = max(0, -lo)
    start = max(0, lo)
    n_mid = min(2 * REL_CLIP + 1 - start, width - left)
    right = width - left - n_mid
    mid = table.T[:, start:start + n_mid]
    f = jnp.pad(mid, ((0, 0), (left, right)), mode="edge")
    return f.reshape(A_HEADS, 1, width)


def _band_attn_prompt(qkv, table):
    batch, seq, _ = qkv.shape
    hd = A_HEADS * HEAD_DIM
    qt = A_Q_TILE
    n_prev = BAND_ROWS // qt
    n_kb = n_prev + 1
    k_tot = n_kb * qt
    f = _bias_vector(table, qt, k_tot)

    def kv_spec(kb, col):
        return pl.BlockSpec((1, qt, hd), lambda b, i: (b, jnp.maximum(i - n_prev + kb, 0), col))

    return pl.pallas_call(
        functools.partial(_band_attn_kernel, n_kb=n_kb, n_maybe_invalid=n_prev),
        grid=(batch, seq // qt),
        in_specs=[pl.BlockSpec((1, qt, hd), lambda b, i: (b, i, 0))]
        + [kv_spec(kb, 1) for kb in range(n_kb)]
        + [kv_spec(kb, 2) for kb in range(n_kb)]
        + [pl.BlockSpec(f.shape, lambda b, i: (0, 0, 0))],
        out_specs=pl.BlockSpec((1, qt, hd), lambda b, i: (b, i, 0)),
        out_shape=jax.ShapeDtypeStruct((batch, seq, hd), BF16),
        scratch_shapes=[pltpu.VMEM((A_HEADS, qt, k_tot), F32)],
        compiler_params=_params(2),
        name="band_attn_prompt",
    )(*([qkv] * (1 + 2 * n_kb)), f)


def _band_attn_sample(qkv, cache_k, cache_v, table):
    batch, seq, _ = qkv.shape
    hd = A_HEADS * HEAD_DIM
    win = cache_k.shape[1]
    assert seq == CHUNK and win == BAND_ROWS
    k_tot = win + seq
    f = _bias_vector(table, seq, k_tot)
    new = lambda col: pl.BlockSpec((1, seq, hd), lambda b, i: (b, 0, col))
    old = pl.BlockSpec((1, win, hd), lambda b, i: (b, 0, 0))
    return pl.pallas_call(
        functools.partial(_band_attn_kernel, n_kb=2, n_maybe_invalid=0),
        grid=(batch, 1),
        in_specs=[new(0), old, new(1), old, new(2), pl.BlockSpec(f.shape, lambda b, i: (0, 0, 0))],
        out_specs=pl.BlockSpec((1, seq, hd), lambda b, i: (b, 0, 0)),
        out_shape=jax.ShapeDtypeStruct((batch, seq, hd), BF16),
        scratch_shapes=[pltpu.VMEM((A_HEADS, seq, k_tot), F32)],
        compiler_params=_params(2),
        name="band_attn_sample",
    )(qkv, cache_k, qkv, cache_v, qkv, f)


def _mm_postnorm_kernel(o_ref, w_ref, x_ref, gate_ref, g_ref, b_ref, y_ref, *, alpha, o_transposed):
    bb, tt, d = x_ref.shape
    if o_transposed:
        sub = lax.dot_general(o_ref[0], w_ref[...], (((0,), (0,)), ((), ())), preferred_element_type=F32)
    else:
        o = o_ref[...].reshape(bb * tt, o_ref.shape[-1])
        sub = jnp.dot(o, w_ref[...], preferred_element_type=F32)
    z = alpha * x_ref[...] + (1.0 + gate_ref[...]) * sub.reshape(bb, tt, d)
    y_ref[...] = _layer_norm_rows(z, g_ref[...], b_ref[...])


def _mm_postnorm(o, w, x, mod, gate_chunk, g, b, alpha, o_transposed=False):
    batch, seq, d = x.shape
    bb, tt = _row_blocks(batch, seq, ROW_TILE)
    k = w.shape[0]
    assert bb == 1 or not o_transposed
    o_spec = (pl.BlockSpec((1, k, tt), lambda b_, t: (b_, 0, t)) if o_transposed
              else pl.BlockSpec((bb, tt, k), lambda b_, t: (b_, t, 0)))
    return pl.pallas_call(
        functools.partial(_mm_postnorm_kernel, alpha=alpha, o_transposed=o_transposed),
        grid=(batch // bb, seq // tt),
        in_specs=[
            o_spec,
            pl.BlockSpec((k, d), lambda b_, t: (0, 0)),
            pl.BlockSpec((bb, tt, d), lambda b_, t: (b_, t, 0)),
            _mod_spec(bb, gate_chunk, 2),
            pl.BlockSpec((1, 1, d), lambda b_, t: (0, 0, 0)),
            pl.BlockSpec((1, 1, d), lambda b_, t: (0, 0, 0)),
        ],
        out_specs=pl.BlockSpec((bb, tt, d), lambda b_, t: (b_, t, 0)),
        out_shape=jax.ShapeDtypeStruct((batch, seq, d), F32),
        compiler_params=_params(2),
        name="out_proj_postnorm",
    )(o, w, x, mod, g.reshape(1, 1, d), b.reshape(1, 1, d))


def _dense_ffn_kernel(x_ref, sc_ref, sh_ref, gate_ref, wg_ref, wu_ref, wd_ref, g_ref, b_ref,
                      y_ref, h_ref, acc_ref, *, alpha):
    bb, tt, d = x_ref.shape
    rows = bb * tt
    c = pl.program_id(2)

    @pl.when(c == 0)
    def _():
        h = (x_ref[...] * (1.0 + sc_ref[...]) + sh_ref[...]).reshape(rows, d)
        h_ref[...] = h.astype(BF16)
        acc_ref[...] = jnp.zeros_like(acc_ref)

    h = h_ref[...]
    gt = jnp.dot(h, wg_ref[...], preferred_element_type=F32)
    up = jnp.dot(h, wu_ref[...], preferred_element_type=F32)
    act = (_silu(gt) * up).astype(BF16)
    acc_ref[...] += jnp.dot(act, wd_ref[...], preferred_element_type=F32)

    @pl.when(c == pl.num_programs(2) - 1)
    def _():
        z = alpha * x_ref[...] + (1.0 + gate_ref[...]) * acc_ref[...].reshape(bb, tt, d)
        y_ref[...] = _layer_norm_rows(z, g_ref[...], b_ref[...])


def _dense_ffn(x, mod, w_gu, w_down, g, b, *, ff_chunk, alpha):
    batch, seq, d = x.shape
    ff = w_down.shape[0]
    assert ff % ff_chunk == 0
    n_c = ff // ff_chunk
    bb, tt = _row_blocks(batch, seq, FFN_ROW_TILE)
    rows = bb * tt
    vec = pl.BlockSpec((1, 1, d), lambda b_, t, c: (0, 0, 0))

    def mod_spec(chunk):
        return pl.BlockSpec((bb, 1, d), lambda b_, t, c: (b_, 0, chunk))

    return pl.pallas_call(
        functools.partial(_dense_ffn_kernel, alpha=alpha),
        grid=(batch // bb, seq // tt, n_c),
        in_specs=[
            pl.BlockSpec((bb, tt, d), lambda b_, t, c: (b_, t, 0)),
            mod_spec(4),
            mod_spec(3),
            mod_spec(5),
            pl.BlockSpec((d, ff_chunk), lambda b_, t, c: (0, c)),
            pl.BlockSpec((d, ff_chunk), lambda b_, t, c: (0, n_c + c)),
            pl.BlockSpec((ff_chunk, d), lambda b_, t, c: (c, 0)),
            vec,
            vec,
        ],
        out_specs=pl.BlockSpec((bb, tt, d), lambda b_, t, c: (b_, t, 0)),
        out_shape=jax.ShapeDtypeStruct((batch, seq, d), F32),
        scratch_shapes=[pltpu.VMEM((rows, d), BF16), pltpu.VMEM((rows, d), F32)],
        compiler_params=_params(3),
        name="dense_ffn",
    )(x, mod, mod, mod, w_gu, w_gu, w_down, g.reshape(1, 1, d), b.reshape(1, 1, d))


def _moe_kernel(x_ref, sc_ref, sh_ref, gate_ref, wr_ref, br_ref, wg_ref, wu_ref, wd_ref, g_ref, b_ref,
                y_ref, h_ref, comb_ref, slot_ref, cnt_ref, tri_ref, hc_ref, acc_ref, out_ref, *, alpha):
    bb, tt, d = x_ref.shape
    rows = bb * tt
    cap = hc_ref.shape[1]
    e = pl.program_id(2)
    c = pl.program_id(3)
    last_c = pl.num_programs(3) - 1
    lane = lax.broadcasted_iota(jnp.int32, (rows, LANES), 1)

    @pl.when((pl.program_id(0) == 0) & (pl.program_id(1) == 0) & (e == 0) & (c == 0))
    def _():
        r = lax.broadcasted_iota(jnp.int32, (rows, rows), 0)
        col = lax.broadcasted_iota(jnp.int32, (rows, rows), 1)
        tri_ref[...] = jnp.where(r <= col, 1.0, 0.0).astype(BF16)

    @pl.when((e == 0) & (c == 0))
    def _():
        h = (x_ref[...] * (1.0 + sc_ref[...]) + sh_ref[...]).reshape(rows, d)
        h_ref[...] = h.astype(BF16)
        out_ref[...] = jnp.zeros_like(out_ref)
        logits = jnp.dot(h, wr_ref[...], preferred_element_type=F32,
                         precision=lax.Precision.HIGHEST) + br_ref[...]
        logits = jnp.where(lane < N_EXPERTS, logits, -jnp.inf)
        m1 = jnp.max(logits, axis=1, keepdims=True)
        i1 = jnp.min(jnp.where(logits == m1, lane, LANES), axis=1, keepdims=True)
        rest = jnp.where(lane == i1, -jnp.inf, logits)
        m2 = jnp.max(rest, axis=1, keepdims=True)
        i2 = jnp.min(jnp.where(rest == m2, lane, LANES), axis=1, keepdims=True)
        e2 = jnp.exp(m2 - m1)
        den = 1.0 + e2
        comb_ref[...] = jnp.where(lane == i1, 1.0 / den, 0.0) + jnp.where(lane == i2, e2 / den, 0.0)
        member = jnp.where((lane == i1) | (lane == i2), 1.0, 0.0)
        member_t = member.T[:2 * N_EXPERTS]
        upto = jnp.dot(member_t.astype(BF16), tri_ref[...], preferred_element_type=F32)
        slot_ref[...] = jnp.where(member_t > 0, upto - 1.0, -1.0)
        cnt_ref[...] = jnp.broadcast_to(upto[:, rows - 1:rows], cnt_ref.shape)

    n_sub = (jnp.max(cnt_ref[pl.ds(e, 1), :]).astype(jnp.int32) + cap - 1) // cap

    def selection(s):
        slot = slot_ref[pl.ds(e, 1), :] - (s * cap).astype(F32)
        r = lax.broadcasted_iota(jnp.int32, (cap, rows), 0).astype(F32)
        return jnp.where(slot == r, 1.0, 0.0).astype(BF16)

    @pl.when(c == 0)
    def _():
        def pack(s, carry):
            hc_ref[s] = jnp.dot(selection(s), h_ref[...], preferred_element_type=F32).astype(BF16)
            acc_ref[s] = jnp.zeros(acc_ref.shape[1:], F32)
            return carry
        lax.fori_loop(0, n_sub, pack, 0)

    def expert(s, carry):
        hc = hc_ref[s]
        gt = jnp.dot(hc, wg_ref[0], preferred_element_type=F32)
        up = jnp.dot(hc, wu_ref[0], preferred_element_type=F32)
        act = (_silu(gt) * up).astype(BF16)
        acc_ref[s] += jnp.dot(act, wd_ref[0], preferred_element_type=F32)
        return carry

    lax.fori_loop(0, n_sub, expert, 0)

    @pl.when(c == last_c)
    def _():
        ce = jnp.sum(jnp.where(lane == e, comb_ref[...], 0.0), axis=1, keepdims=True)

        def unpack(s, carry):
            back = lax.dot_general(selection(s), acc_ref[s].astype(BF16), (((0,), (0,)), ((), ())),
                                   preferred_element_type=F32)
            out_ref[...] += ce * back
            return carry
        lax.fori_loop(0, n_sub, unpack, 0)

    @pl.when((e == pl.num_programs(2) - 1) & (c == last_c))
    def _():
        z = alpha * x_ref[...] + (1.0 + gate_ref[...]) * out_ref[...].reshape(bb, tt, d)
        y_ref[...] = _layer_norm_rows(z, g_ref[...], b_ref[...])


def _moe(x, mod, w_r, b_r, w_gu, w_down, g, b, *, ff_chunk, alpha):
    batch, seq, d = x.shape
    n_e, ff, _ = w_down.shape
    assert ff % ff_chunk == 0
    n_c = ff // ff_chunk
    bb, tt = _row_blocks(batch, seq, FFN_ROW_TILE)
    rows = bb * tt
    cap = MOE_CAPACITY
    max_sub = -(-rows // cap)
    vec = pl.BlockSpec((1, 1, d), lambda b_, t, e, c: (0, 0, 0))
    return pl.pallas_call(
        functools.partial(_moe_kernel, alpha=alpha),
        grid=(batch // bb, seq // tt, n_e, n_c),
        in_specs=[
            pl.BlockSpec((bb, tt, d), lambda b_, t, e, c: (b_, t, 0), pipeline_mode=pl.Buffered(1)),
            _mod_spec(bb, 4, 4),
            _mod_spec(bb, 3, 4),
            _mod_spec(bb, 5, 4),
            pl.BlockSpec((d, LANES), lambda b_, t, e, c: (0, 0)),
            pl.BlockSpec((1, LANES), lambda b_, t, e, c: (0, 0)),
            pl.BlockSpec((1, d, ff_chunk), lambda b_, t, e, c: (e, 0, c)),
            pl.BlockSpec((1, d, ff_chunk), lambda b_, t, e, c: (e, 0, n_c + c)),
            pl.BlockSpec((1, ff_chunk, d), lambda b_, t, e, c: (e, c, 0)),
            vec,
            vec,
        ],
        out_specs=pl.BlockSpec((bb, tt, d), lambda b_, t, e, c: (b_, t, 0)),
        out_shape=jax.ShapeDtypeStruct((batch, seq, d), F32),
        scratch_shapes=[
            pltpu.VMEM((rows, d), BF16),
            pltpu.VMEM((rows, LANES), F32),
            pltpu.VMEM((2 * N_EXPERTS, rows), F32),
            pltpu.VMEM((2 * N_EXPERTS, LANES), F32),
            pltpu.VMEM((rows, rows), BF16),
            pltpu.VMEM((max_sub, cap, d), BF16),
            pltpu.VMEM((max_sub, cap, d), F32),
            pltpu.VMEM((rows, d), F32),
        ],
        compiler_params=_params(4),
        name="moe_ffn",
    )(x, mod, mod, mod, w_r, b_r, w_gu, w_gu, w_down, g.reshape(1, 1, d), b.reshape(1, 1, d))


def _rope_lanes(seg, cos, sin_signed, first_half):
    width = seg.shape[-1]
    reps = width // LANES
    if reps > 1:
        cos = jnp.concatenate([cos] * reps, axis=1)
        sin_signed = jnp.concatenate([sin_signed] * reps, axis=1)
        first_half = jnp.concatenate([first_half] * reps, axis=1)
    half = HEAD_DIM // 2
    swapped = jnp.where(first_half, pltpu.roll(seg, width - half, 1), pltpu.roll(seg, half, 1))
    return seg * cos + swapped * sin_signed


def _b_project_kernel(x_ref, sc_ref, sh_ref, w_ref, cos_ref, sin_ref, kg_ref, kb_ref,
                      q_ref, qi_ref, k_ref, v_ref, kw_ref, *extra_refs, transposed):
    bb, tt, d = x_ref.shape
    rows = bb * tt
    h = x_ref[...] * (1.0 + sc_ref[...]) + sh_ref[...]
    h = h.reshape(rows, d).astype(BF16)
    acc = jnp.dot(h, w_ref[...], preferred_element_type=F32)
    cos = cos_ref[...]
    sin_signed = sin_ref[...]
    lane = lax.broadcasted_iota(jnp.int32, (1, LANES), 1)
    first_half = (lane % HEAD_DIM) < (HEAD_DIM // 2)

    def out(ref, val):
        ref[...] = val.reshape(bb, tt, val.shape[-1]).astype(ref.dtype)

    def out_query_side(ref, val):
        if transposed:
            ref[0] = val.T.astype(ref.dtype)
        else:
            out(ref, val)

    o_k, o_v, o_qi, o_ki = B_Q, B_Q + B_KV, B_Q + 2 * B_KV, B_Q + 2 * B_KV + B_QI
    out_query_side(q_ref, _rope_lanes(acc[:, :o_k], cos, sin_signed, first_half) * QK_SCALE_LOG2)
    k_rot = _rope_lanes(acc[:, o_k:o_v], cos, sin_signed, first_half)
    out(k_ref, k_rot)
    out(v_ref, acc[:, o_v:o_qi])
    out_query_side(qi_ref, _rope_lanes(acc[:, o_qi:o_ki], cos, sin_signed, first_half))
    seg = acc[:, o_ki:]
    is_ki = lane < IDX_DIM
    mu = jnp.sum(jnp.where(is_ki, seg, 0.0), axis=1, keepdims=True) / IDX_DIM
    cen = jnp.where(is_ki, seg - mu, 0.0)
    var = jnp.sum(cen * cen, axis=1, keepdims=True) / IDX_DIM
    ki = cen * lax.rsqrt(var + LN_EPS) * kg_ref[...] + kb_ref[...]
    ki = _rope_lanes(ki, cos, sin_signed, first_half)
    kw = jnp.where(is_ki, ki, seg * (IDX_HEADS ** -0.5))
    out(kw_ref, kw)
    if transposed:
        kwt_ref, k16_ref, ki16_ref, vta_ref = extra_refs
        kwt_ref[0] = kw.T
        out(k16_ref, k_rot)
        out(ki16_ref, kw[:, :IDX_DIM])
        v_t = acc[:, o_v:o_qi].T.astype(BF16).reshape(B_KV_HEADS, HEAD_DIM, tt)
        tail_rows = vta_ref.shape[2] - HEAD_DIM
        first_row = lax.broadcasted_iota(jnp.int32, (B_KV_HEADS, tail_rows, tt), 1) == 0
        vta_ref[0] = jnp.concatenate([v_t, jnp.where(first_row, 1.0, 0.0).astype(BF16)], axis=1)


def _b_project(x, mod, w, cos, sin_signed, kn_g, kn_b, table_per_tile, transposed=False):
    batch, seq, d = x.shape
    bb, tt = _row_blocks(batch, seq, ROW_TILE)
    rows = bb * tt
    n = w.shape[1]
    assert bb == 1 or not transposed
    tab = pl.BlockSpec((rows, LANES), (lambda b, t: (0, 0)) if table_per_tile else (lambda b, t: (t, 0)))
    vec = pl.BlockSpec((1, LANES), lambda b, t: (0, 0))

    def o_spec(width):
        return pl.BlockSpec((bb, tt, width), lambda b, t: (b, t, 0))

    def o_shape(width, dtype):
        return jax.ShapeDtypeStruct((batch, seq, width), dtype)

    def t_spec(width):
        return pl.BlockSpec((1, width, tt), lambda b, t: (b, 0, t))

    def t_shape(width, dtype):
        return jax.ShapeDtypeStruct((batch, width, seq), dtype)

    q_spec, q_shape = (t_spec, t_shape) if transposed else (o_spec, o_shape)
    extra_specs, extra_shapes = [], []
    if transposed:
        extra_specs = [t_spec(LANES), o_spec(B_KV), o_spec(IDX_DIM),
                       pl.BlockSpec((1, B_KV_HEADS, V_AUG_ROWS, tt), lambda b, t: (b, 0, 0, t))]
        extra_shapes = [t_shape(LANES, F32), o_shape(B_KV, BF16), o_shape(IDX_DIM, BF16),
                        jax.ShapeDtypeStruct((batch, B_KV_HEADS, V_AUG_ROWS, seq), BF16)]
    return pl.pallas_call(
        functools.partial(_b_project_kernel, transposed=transposed),
        grid=(batch // bb, seq // tt),
        in_specs=[
            pl.BlockSpec((bb, tt, d), lambda b, t: (b, t, 0)),
            _mod_spec(bb, 1, 2),
            _mod_spec(bb, 0, 2),
            pl.BlockSpec((d, n), lambda b, t: (0, 0)),
            tab,
            tab,
            vec,
            vec,
        ],
        out_specs=[q_spec(B_Q), q_spec(B_QI), o_spec(B_KV), o_spec(B_KV), o_spec(LANES)] + extra_specs,
        out_shape=[q_shape(B_Q, BF16), q_shape(B_QI, BF16), o_shape(B_KV, F32), o_shape(B_KV, F32),
                   o_shape(LANES, F32)] + extra_shapes,
        compiler_params=_params(2),
        name="b_project",
    )(x, mod, mod, w, cos, sin_signed, kn_g, kn_b)


def _rope_tables(pos):
    half = HEAD_DIM // 2
    inv = ROPE_THETA ** (-jnp.arange(half, dtype=F32) / half)
    ang = pos.astype(F32)[:, None] * inv[None, :]
    cos, sin = jnp.cos(ang), jnp.sin(ang)
    return (jnp.concatenate([cos, cos, cos, cos], axis=1),
            jnp.concatenate([-sin, sin, -sin, sin], axis=1))


def _key_norm_kernel(k_ref, grp_ref, o_ref):
    k = k_ref[0].astype(F32)
    n2 = jnp.dot((k * k).astype(BF16), grp_ref[...], preferred_element_type=F32)
    mx = jnp.broadcast_to(jnp.max(n2, axis=0, keepdims=True), o_ref.shape[1:])

    @pl.when(pl.program_id(1) == 0)
    def _():
        o_ref[0] = mx

    @pl.when(pl.program_id(1) > 0)
    def _():
        o_ref[0] = jnp.maximum(o_ref[0], mx)


def _key_norm_max(k):
    batch, lp, width = k.shape
    tile = DSA_KEY_TILE
    grp = (jnp.arange(width)[:, None] // HEAD_DIM == jnp.arange(LANES)[None, :]).astype(BF16)
    return pl.pallas_call(
        _key_norm_kernel,
        grid=(batch, lp // tile),
        in_specs=[pl.BlockSpec((1, tile, width), lambda b, t: (b, t, 0)),
                  pl.BlockSpec((width, LANES), lambda b, t: (0, 0))],
        out_specs=pl.BlockSpec((1, 8, LANES), lambda b, t: (b, 0, 0)),
        out_shape=jax.ShapeDtypeStruct((batch, 8, LANES), F32),
        compiler_params=_params(2),
        name="key_norm_max",
    )(k, grp)


def _dsa_kernel(qt_ref, qit_ref, kwt_ref, ki_ref, k_ref, vta_ref, kmax_ref, o_ref,
                s_ref, qpad_ref, bound_ref, m_ref, acc_ref, *, causal, n_keys, k_sel):
    qn = qt_ref.shape[2]
    lt = DSA_KEY_TILE
    i = pl.program_id(1)
    if causal:
        n_tiles = ((i + 1) * qn + lt - 1) // lt
        qpos = i * qn + lax.broadcasted_iota(jnp.int32, (1, qn), 1)
        limit = (qpos // CHUNK + 1) * CHUNK
    else:
        n_tiles = ki_ref.shape[1] // lt
        limit = jnp.full((1, qn), n_keys, jnp.int32)

    def key_slice(t):
        return pl.ds(pl.multiple_of(t * lt, lt), lt)

    def key_pos(t):
        return t * lt + lax.broadcasted_iota(jnp.int32, (lt, 1), 0)

    def fold_rows(x, op):
        return op(op(x.reshape(lt // 32, 4, 8, x.shape[-1]), axis=0), axis=0)

    def paired_tile_loop(body, init):
        def pair(t2, carry):
            return body(2 * t2 + 1, body(2 * t2, carry))
        carry = lax.fori_loop(0, n_tiles // 2, pair, init)
        return lax.cond(n_tiles % 2 == 1, lambda c: body(n_tiles - 1, c), lambda c: c, carry)

    qit = qit_ref[0]
    kwt = kwt_ref[0]
    qi_w = jnp.concatenate([qit[h * IDX_DIM:(h + 1) * IDX_DIM, :] for h in range(IDX_HEADS)], axis=1)
    w_row = jnp.concatenate([kwt[IDX_DIM + h:IDX_DIM + h + 1, :] for h in range(IDX_HEADS)], axis=1)
    w_row = w_row * (IDX_DIM ** -0.5)

    def score_body(t, carry):
        rmax, rmin = carry
        s = jnp.dot(ki_ref[0, key_slice(t), :], qi_w, preferred_element_type=F32)
        s = jnp.maximum(s, 0.0) * w_row
        sc = s[:, 0:qn]
        for h in range(1, IDX_HEADS):
            sc = sc + s[:, h * qn:(h + 1) * qn]
        adm = key_pos(t) < limit
        masked = jnp.where(adm, sc, -jnp.inf)
        s_ref[key_slice(t), :] = masked
        rmax = jnp.maximum(rmax, jnp.max(masked, axis=0, keepdims=True))
        rmin = jnp.minimum(rmin, jnp.min(jnp.where(adm, sc, jnp.inf), axis=0, keepdims=True))
        return rmax, rmin

    rmax, rmin = paired_tile_loop(score_body,
                                  (jnp.full((1, qn), -jnp.inf, F32), jnp.full((1, qn), jnp.inf, F32)))

    def count(pred):
        def body(t, acc):
            return acc + fold_rows(jnp.where(pred(s_ref[key_slice(t), :], key_pos(t)), 1.0, 0.0), jnp.sum)
        acc = paired_tile_loop(body, jnp.zeros((8, qn), F32))
        return jnp.sum(acc, axis=0, keepdims=True)

    def to_key(x):
        bits = lax.bitcast_convert_type(x, jnp.int32)
        return bits ^ ((bits >> 31) & 0x7FFFFFFF)

    def from_key(key):
        return lax.bitcast_convert_type(key ^ ((key >> 31) & 0x7FFFFFFF), F32)

    n_adm = limit.astype(F32)
    target = jnp.minimum(n_adm, float(k_sel))
    log_target = jnp.log(target)

    def any_left(done):
        return jnp.sum(done) < qn

    c_ge0 = count(lambda s, _: s >= 0.0)
    c_gt0 = count(lambda s, _: s > 0.0)
    is_pos = c_gt0 >= target
    is_neg = c_ge0 < target
    zero_key = jnp.zeros((1, qn), jnp.int32)
    lo0 = jnp.where(is_neg, to_key(rmin), zero_key)
    c_lo0 = jnp.where(is_neg, n_adm, c_ge0)
    c_hi0 = jnp.where(is_pos, 0.0, c_ge0)
    done0 = jnp.where((n_adm == target) | jnp.logical_not(is_pos | is_neg) | (c_lo0 == target), 1.0, 0.0)
    lo0 = jnp.where(n_adm == target, to_key(rmin), lo0)
    c_lo0 = jnp.where(n_adm == target, n_adm, c_lo0)

    ones = jnp.ones((1, qn), F32)

    def search(st0, key_to_value, value_to_key, count_ge):
        def cond(st):
            return any_left(st[7])

        def body(st):
            it, lo, hi, c_lo, c_hi, w_lo, w_hi, done, side = st
            t_lo, t_hi = key_to_value(lo), key_to_value(hi)
            f_lo = (jnp.log(c_lo) - log_target) * w_lo
            f_hi = (log_target - jnp.log(jnp.maximum(c_hi, 0.5))) * w_hi
            guess = value_to_key(t_lo + f_lo / (f_lo + f_hi) * (t_hi - t_lo))
            halve = (lo >> 1) + (hi >> 1) + (lo & hi & 1)
            mid = jnp.where(it >= SECANT_STEPS, halve, jnp.clip(guess, lo + 1, hi - 1))
            stuck = halve == lo
            c = count_ge(key_to_value(mid))
            live = (done < 0.5) & jnp.logical_not(stuck)
            up = live & (c >= target)
            dn = live & (c < target)
            w_hi = jnp.where(up, jnp.where(side > 0, 0.5 * w_hi, 1.0), jnp.where(dn, 1.0, w_hi))
            w_lo = jnp.where(dn, jnp.where(side < 0, 0.5 * w_lo, 1.0), jnp.where(up, 1.0, w_lo))
            side = jnp.where(up, 1.0, jnp.where(dn, -1.0, side))
            lo = jnp.where(up, mid, lo)
            c_lo = jnp.where(up, c, c_lo)
            hi = jnp.where(dn, mid, hi)
            c_hi = jnp.where(dn, c, c_hi)
            done = jnp.where(stuck | (c_lo == target) | (c_lo - c_hi <= 2.0), 1.0, done)
            return it + 1, lo, hi, c_lo, c_hi, w_lo, w_hi, done, side

        lo, hi, c_lo, c_hi, done = st0
        st = lax.while_loop(cond, body, (jnp.int32(0), lo, hi, c_lo, c_hi, ones, ones, done, 0.0 * ones))
        return st[1], st[2], st[3], st[4]

    hi0 = jnp.where(is_pos, to_key(rmax) + 1, zero_key)
    k_lo, k_hi, c_thr, c_above = search((lo0, hi0, c_lo0, c_hi0, done0),
                                        from_key, to_key, lambda v: count(lambda s, _: s >= v))
    thr = from_key(k_lo)

    two_left = (c_thr != target) & (c_thr - c_above == 2.0)

    def settle(_):
        t_lo, t_hi = from_key(k_lo), from_key(k_hi)

        def body(t, carry):
            top, bottom = carry
            s = s_ref[key_slice(t), :]
            top = jnp.maximum(top, fold_rows(jnp.where(s < t_hi, s, -jnp.inf), jnp.max))
            bottom = jnp.minimum(bottom, fold_rows(jnp.where(s >= t_lo, s, jnp.inf), jnp.min))
            return top, bottom

        top, bottom = lax.fori_loop(0, n_tiles, body, (jnp.full((8, qn), -jnp.inf, F32),
                                                       jnp.full((8, qn), jnp.inf, F32)))
        top = jnp.max(top, axis=0, keepdims=True)
        bottom = jnp.min(bottom, axis=0, keepdims=True)
        return (jnp.where(two_left, top, thr),
                jnp.where(two_left & (top != bottom), target, c_thr))

    thr, c_thr = lax.cond(jnp.sum(jnp.where(two_left, 1.0, 0.0)) > 0, settle, lambda _: (thr, c_thr), 0)

    tied = c_thr > target

    @pl.when(jnp.sum(jnp.where(tied, 1.0, 0.0)) > 0)
    def _():
        need = target - count(lambda s, _: s > thr)

        def idx_cond(st):
            return any_left(st[5])

        def idx_body(st):
            it, lo_j, hi_j, c_lo_j, c_hi_j, done = st
            span = (hi_j - lo_j).astype(F32)
            step = jnp.ceil((need - c_lo_j) / (c_hi_j - c_lo_j) * span).astype(jnp.int32)
            step = jnp.where(it % 2 == 0, step, (hi_j - lo_j) // 2)
            mid = lo_j + jnp.clip(step, 1, jnp.maximum(hi_j - lo_j - 1, 1))
            c = count(lambda s, kp: (s == thr) & (kp <= mid))
            live = (done < 0.5) & (hi_j - lo_j > 1)
            below = live & (c < need)
            above = live & (c >= need)
            lo_j = jnp.where(below, mid, lo_j)
            c_lo_j = jnp.where(below, c, c_lo_j)
            hi_j = jnp.where(above, mid, hi_j)
            c_hi_j = jnp.where(above, c, c_hi_j)
            done = jnp.where((c_hi_j == need) | (hi_j - lo_j <= 1), 1.0, done)
            return it + 1, lo_j, hi_j, c_lo_j, c_hi_j, done

        first = jnp.full((1, qn), -1, jnp.int32)
        final = jnp.full((1, qn), n_tiles * lt - 1, jnp.int32)
        n_ties = c_thr - (target - need)
        idx_done0 = jnp.where(tied & (n_ties > need), 0.0, 1.0)
        idx = lax.while_loop(idx_cond, idx_body,
                             (jnp.int32(0), first, final, 0.0 * ones, n_ties, idx_done0))
        last = idx[2]

        def drop_body(t, carry):
            s = s_ref[key_slice(t), :]
            s_ref[key_slice(t), :] = jnp.where(tied & (s == thr) & (key_pos(t) > last), -jnp.inf, s)
            return carry

        lax.fori_loop(0, n_tiles, drop_body, 0)

    qt = qt_ref[0]
    zeros = jnp.zeros((HEAD_DIM, B_GROUP * qn), BF16)
    for g in range(B_KV_HEADS):
        q_g = jnp.concatenate(
            [qt[(g * B_GROUP + j) * HEAD_DIM:(g * B_GROUP + j + 1) * HEAD_DIM, :] for j in range(B_GROUP)],
            axis=1)
        qpad_ref[g] = jnp.concatenate([zeros] * g + [q_g] + [zeros] * (B_KV_HEADS - 1 - g), axis=0)

    qf = qt.astype(F32)
    q_norm2 = jnp.sum((qf * qf).reshape(B_HEADS, HEAD_DIM, qn), axis=1)
    for g in range(B_KV_HEADS):
        b = jnp.sqrt(q_norm2[g * B_GROUP:(g + 1) * B_GROUP] * kmax_ref[0, 0:1, g:g + 1])
        b = b * BOUND_SLACK + BOUND_SLACK_ABS
        bound_ref[g] = jnp.concatenate([b[j:j + 1] for j in range(B_GROUP)], axis=1)
    acc_ref[...] = jnp.zeros(acc_ref.shape, F32)

    def fast_body(t, carry):
        keys = k_ref[0, key_slice(t), :]
        sel = jnp.where(s_ref[key_slice(t), :] >= thr, 1.0, 0.0).astype(BF16)
        sel = jnp.concatenate([sel] * B_GROUP, axis=1)
        def logits(g):
            return jnp.dot(keys, qpad_ref[g], preferred_element_type=F32)

        lg = logits(0)
        for g in range(B_KV_HEADS):
            lg_next = logits(g + 1) if g + 1 < B_KV_HEADS else None
            p = jnp.exp2(lg - bound_ref[g]).astype(BF16) * sel
            acc_ref[g] += jnp.dot(vta_ref[0, g, :, key_slice(t)], p, preferred_element_type=F32)
            lg = lg_next
        return carry

    paired_tile_loop(fast_body, 0)
    den_min = jnp.min(acc_ref[:, HEAD_DIM:HEAD_DIM + 1, :])

    @pl.when(jnp.logical_not(den_min >= MIN_TRUSTED_DENOMINATOR))
    def _():
        m_ref[...] = jnp.full(m_ref.shape, NEG_BIG, F32)
        acc_ref[...] = jnp.zeros(acc_ref.shape, F32)

        def exact_body(t, carry):
            keys = k_ref[0, key_slice(t), :]
            mb = jnp.where(s_ref[key_slice(t), :] >= thr, 0.0, -jnp.inf)
            mb = jnp.concatenate([mb] * B_GROUP, axis=1)
            for g in range(B_KV_HEADS):
                lg = jnp.dot(keys, qpad_ref[g], preferred_element_type=F32) + mb
                m_old = m_ref[g]
                m_new = jnp.maximum(m_old, jnp.max(fold_rows(lg, jnp.max), axis=0, keepdims=True))
                p = jnp.exp2(lg - m_new).astype(BF16)
                pv = jnp.dot(vta_ref[0, g, :, key_slice(t)], p, preferred_element_type=F32)
                acc_ref[g] = jnp.exp2(m_old - m_new) * acc_ref[g] + pv
                m_ref[g] = m_new
            return carry

        lax.fori_loop(0, n_tiles, exact_body, 0)

    for g in range(B_KV_HEADS):
        acc = acc_ref[g]
        o_g = acc[:HEAD_DIM] / acc[HEAD_DIM:HEAD_DIM + 1]
        for j in range(B_GROUP):
            hh = g * B_GROUP + j
            o_ref[0, hh * HEAD_DIM:(hh + 1) * HEAD_DIM, :] = o_g[:, j * qn:(j + 1) * qn].astype(o_ref.dtype)


V_AUG_ROWS = HEAD_DIM + 16


def _dsa(q, qi, kw, k, v, ki, *, q_tile, causal, k_sel, transposed=False):
    batch = q.shape[0]
    seq = q.shape[2] if transposed else q.shape[1]
    n_keys = k.shape[1]
    lp = -(-n_keys // DSA_KEY_TILE) * DSA_KEY_TILE
    assert seq % q_tile == 0 and q_tile % LANES == 0
    if v.ndim == 4:
        assert lp == n_keys and k.dtype == BF16 and ki.dtype == BF16
        vta = v
    else:
        pad = ((0, 0), (0, lp - n_keys), (0, 0))
        k, v, ki = (jnp.pad(a.astype(BF16), pad) for a in (k, v, ki))
        vt = jnp.swapaxes(v, 1, 2).reshape(batch, B_KV_HEADS, HEAD_DIM, lp)
        vta = jnp.concatenate([vt, jnp.ones((batch, B_KV_HEADS, 1, lp), BF16),
                               jnp.zeros((batch, B_KV_HEADS, V_AUG_ROWS - HEAD_DIM - 1, lp), BF16)], axis=2)
    qt, qit, kwt = (q, qi, kw) if transposed else (jnp.swapaxes(a, 1, 2) for a in (q, qi, kw))

    def qspec(rows):
        return pl.BlockSpec((1, rows, q_tile), lambda b, i: (b, 0, i))

    def resident(shape):
        zero = (0,) * len(shape)
        return pl.BlockSpec((1,) + shape, lambda b, i: (b,) + zero, pipeline_mode=pl.Buffered(1))

    ot = pl.pallas_call(
        functools.partial(_dsa_kernel, causal=causal, n_keys=n_keys, k_sel=k_sel),
        grid=(batch, seq // q_tile),
        in_specs=[qspec(B_Q), qspec(B_QI), qspec(LANES),
                  resident((lp, IDX_DIM)), resident((lp, B_KV)),
                  resident((B_KV_HEADS, V_AUG_ROWS, lp)),
                  pl.BlockSpec((1, 8, LANES), lambda b, i: (b, 0, 0))],
        out_specs=qspec(B_Q),
        out_shape=jax.ShapeDtypeStruct((batch, B_Q, seq), BF16),
        scratch_shapes=[pltpu.VMEM((lp, q_tile), F32),
                        pltpu.VMEM((B_KV_HEADS, B_KV, B_GROUP * q_tile), BF16),
                        pltpu.VMEM((B_KV_HEADS, 1, B_GROUP * q_tile), F32),
                        pltpu.VMEM((B_KV_HEADS, 1, B_GROUP * q_tile), F32),
                        pltpu.VMEM((B_KV_HEADS, V_AUG_ROWS, B_GROUP * q_tile), F32)],
        compiler_params=_params(2),
        name="dsa_prompt" if causal else "dsa_sample",
    )(qt, qit, kwt, ki, k, vta, _key_norm_max(k))
    return ot if transposed else jnp.swapaxes(ot, 1, 2)


def kernel(x_prompt, x_sample, cache_k_a, cache_v_a, cache_k_b, cache_v_b, cache_kidx_b,
           c_prompt, c_sample, w_cond, b_cond, ln_g, ln_b, a_w_in, a_w_o, a_rel_bias,
           b_w_in, b_w_o, b_kidx_ln_g, b_kidx_ln_b, ffn_w_gu, ffn_w_down,
           moe_w_router, moe_b_router, moe_w_gu, moe_w_down):
    depth = w_cond.shape[0]
    alpha = (2 * depth) ** 0.25
    n_p, seq, d = x_prompt.shape
    n_s, dec_seq, _ = x_sample.shape
    past = cache_k_b.shape[2]
    a_hd = A_HEADS * HEAD_DIM

    rows = n_p + n_s
    rows_pad = -(-rows // 8) * 8
    c_all = jnp.pad(jnp.concatenate([c_prompt, c_sample], axis=0), ((0, rows_pad - rows), (0, 0)))
    mod_all = _modulation(c_all, w_cond, b_cond)

    xp, xs = x_prompt, x_sample
    outs = {k: [] for k in ("ka_p", "va_p", "kb_p", "vb_p", "ib_p", "ka_s", "va_s", "kb_s", "vb_s", "ib_s")}
    for i in range(depth):
        j = i // 2
        mod_p = mod_all[i, :n_p].reshape(n_p, 1, 6 * d)
        mod_s = mod_all[i, n_p:rows].reshape(n_s, 1, 6 * d)
        g1, b1, g2, b2 = ln_g[i, 0], ln_b[i, 0], ln_g[i, 1], ln_b[i, 1]
        if i % 2 == 0:
            w_in = a_w_in[j].astype(BF16)
            keep = min(BAND_ROWS, seq)
            qkv_p = _modmm(xp, mod_p, w_in, BF16)
            kv_p = _modmm(xp, mod_p, w_in[:, a_hd:], F32, t_start=seq - keep)
            qkv_s = _modmm(xs, mod_s, w_in, BF16)
            kv_s = _modmm(xs, mod_s, w_in[:, a_hd:], F32)
            outs["ka_p"].append(kv_p[..., :a_hd].reshape(n_p, keep, A_HEADS, HEAD_DIM))
            outs["va_p"].append(kv_p[..., a_hd:].reshape(n_p, keep, A_HEADS, HEAD_DIM))
            outs["ka_s"].append(kv_s[..., :a_hd].reshape(n_s, dec_seq, A_HEADS, HEAD_DIM))
            outs["va_s"].append(kv_s[..., a_hd:].reshape(n_s, dec_seq, A_HEADS, HEAD_DIM))
            o_p = _band_attn_prompt(qkv_p, a_rel_bias[j])
            ck = cache_k_a[j].reshape(n_s, -1, a_hd).astype(BF16)
            cv = cache_v_a[j].reshape(n_s, -1, a_hd).astype(BF16)
            o_s = _band_attn_sample(qkv_s, ck, cv, a_rel_bias[j])
            w_o = a_w_o[j].astype(BF16)
        else:
            w_in = jnp.pad(b_w_in[j], ((0, 0), (0, B_PROJ_PAD - B_PROJ))).astype(BF16)
            kn_g = jnp.pad(b_kidx_ln_g[j], (0, LANES - IDX_DIM)).reshape(1, LANES)
            kn_b = jnp.pad(b_kidx_ln_b[j], (0, LANES - IDX_DIM)).reshape(1, LANES)
            cos_p, sin_p = _rope_tables(jnp.arange(seq))
            cos_s, sin_s = _rope_tables(past + jnp.arange(dec_seq))
            bb_s, _ = _row_blocks(n_s, dec_seq, ROW_TILE)
            cos_s, sin_s = jnp.tile(cos_s, (bb_s, 1)), jnp.tile(sin_s, (bb_s, 1))
            q_p, qi_p, k_p, v_p, kw_p, kwt_p, k16_p, ki16_p, vta_p = _b_project(
                xp, mod_p, w_in, cos_p, sin_p, kn_g, kn_b, False, transposed=True)
            q_s, qi_s, k_s, v_s, kw_s = _b_project(xs, mod_s, w_in, cos_s, sin_s, kn_g, kn_b, True)
            outs["kb_p"].append(k_p.reshape(n_p, seq, B_KV_HEADS, HEAD_DIM))
            outs["vb_p"].append(v_p.reshape(n_p, seq, B_KV_HEADS, HEAD_DIM))
            outs["ib_p"].append(kw_p[..., :IDX_DIM])
            outs["kb_s"].append(k_s.reshape(n_s, dec_seq, B_KV_HEADS, HEAD_DIM))
            outs["vb_s"].append(v_s.reshape(n_s, dec_seq, B_KV_HEADS, HEAD_DIM))
            outs["ib_s"].append(kw_s[..., :IDX_DIM])
            o_p = _dsa(q_p, qi_p, kwt_p, k16_p, vta_p, ki16_p, q_tile=DSA_Q_TILE, causal=True,
                       k_sel=min(TOPK_MAX, seq // 4), transposed=True)
            n_keys = past + dec_seq
            kk = jnp.concatenate([cache_k_b[j].reshape(n_s, past, B_KV), k_s], axis=1)
            vc = jnp.concatenate([cache_v_b[j].reshape(n_s, past, B_KV), v_s], axis=1)
            kki = jnp.concatenate([cache_kidx_b[j], kw_s[..., :IDX_DIM]], axis=1)
            rep = LANES // dec_seq
            q_r, qi_r, kw_r = (jnp.concatenate([a] * rep, axis=1) for a in (q_s, qi_s, kw_s))
            o_s = _dsa(q_r, qi_r, kw_r, kk, vc, kki, q_tile=LANES, causal=False,
                       k_sel=min(TOPK_MAX, n_keys // 4))[:, :dec_seq]
            w_o = b_w_o[j].astype(BF16)
        xp = _mm_postnorm(o_p, w_o, xp, mod_p, 2, g1, b1, alpha, o_transposed=(i % 2 == 1))
        xs = _mm_postnorm(o_s, w_o, xs, mod_s, 2, g1, b1, alpha)
        if i % 2 == 0:
            w_gu = ffn_w_gu[j].astype(BF16)
            w_dn = ffn_w_down[j].astype(BF16)
            dense = dict(ff_chunk=D_FF // 2, alpha=alpha)
            xp = _dense_ffn(xp, mod_p, w_gu, w_dn, g2, b2, **dense)
            xs = _dense_ffn(xs, mod_s, w_gu, w_dn, g2, b2, **dense)
        else:
            w_gu = moe_w_gu[j].astype(BF16)
            w_dn = moe_w_down[j].astype(BF16)
            w_r = jnp.pad(moe_w_router[j], ((0, 0), (0, LANES - N_EXPERTS)))
            b_r = jnp.pad(moe_b_router[j], (0, LANES - N_EXPERTS)).reshape(1, LANES)
            moe = dict(ff_chunk=MOE_FF_CHUNK, alpha=alpha)
            xp = _moe(xp, mod_p, w_r, b_r, w_gu, w_dn, g2, b2, **moe)
            xs = _moe(xs, mod_s, w_r, b_r, w_gu, w_dn, g2, b2, **moe)

    st = lambda name: jnp.stack(outs[name])
    return (xp, xs, st("ka_p"), st("va_p"), st("kb_p"), st("vb_p"), st("ib_p"),
            st("ka_s"), st("va_s"), st("kb_s"), st("vb_s"), st("ib_s"))
```

```python
import functools

import jax
import jax.numpy as jnp
from jax import lax
from jax.experimental import pallas as pl
from jax.experimental.pallas import tpu as pltpu

F32 = jnp.float32
BF16 = jnp.bfloat16

D_MODEL = 1024
CHUNK = 64
N_PAST_CHUNKS = 8
BAND_ROWS = N_PAST_CHUNKS * CHUNK
REL_CLIP = 2 * CHUNK
HEAD_DIM = 64
A_HEADS = 16
B_HEADS = 16
B_KV_HEADS = 4
B_GROUP = B_HEADS // B_KV_HEADS
IDX_HEADS = 8
IDX_DIM = 64
TOPK_MAX = 256
D_FF = 2816
N_EXPERTS = 8
D_FF_EXPERT = 3584
ROPE_THETA = 10000.0
LN_EPS = 1e-5
B_Q = B_HEADS * HEAD_DIM
B_KV = B_KV_HEADS * HEAD_DIM
B_QI = IDX_HEADS * IDX_DIM
B_PROJ = B_Q + 2 * B_KV + B_QI + IDX_DIM + IDX_HEADS

LANES = 128
VMEM_LIMIT_BYTES = 58 * 1024 * 1024

A_Q_TILE = 4 * CHUNK
DSA_Q_TILE = 4 * CHUNK
DSA_KEY_TILE = 512
ROW_TILE = 512
FFN_ROW_TILE = 1024
MOE_CAPACITY = 288
MOE_FF_CHUNK = 1792
B_PROJ_PAD = B_Q + 2 * B_KV + B_QI + LANES
NEG_BIG = -1e30
QK_SCALE_LOG2 = HEAD_DIM ** -0.5 * 1.4426950408889634
BOUND_SLACK = 1.02
BOUND_SLACK_ABS = 0.01
MIN_TRUSTED_DENOMINATOR = 2.0 ** -100
SECANT_STEPS = 12


def _params(n_grid):
    return pltpu.CompilerParams(
        dimension_semantics=("arbitrary",) * n_grid,
        vmem_limit_bytes=VMEM_LIMIT_BYTES,
    )


def _row_blocks(batch, seq, target):
    if seq >= target:
        assert seq % target == 0
        return 1, target
    bb = max(1, min(batch, target // seq))
    while batch % bb:
        bb -= 1
    return bb, seq


def _mod_spec(bb, chunk, n_grid):
    if n_grid == 2:
        return pl.BlockSpec((bb, 1, D_MODEL), lambda b, t: (b, 0, chunk))
    return pl.BlockSpec((bb, 1, D_MODEL), lambda b, t, e, c: (b, 0, chunk))


def _silu(x):
    return x / (1.0 + jnp.exp(-x))


def _layer_norm_rows(z, g, b):
    mu = jnp.mean(z, axis=-1, keepdims=True)
    zc = z - mu
    var = jnp.mean(zc * zc, axis=-1, keepdims=True)
    return zc * lax.rsqrt(var + LN_EPS) * g + b


def _modulation_kernel(c_ref, w_ref, b_ref, o_ref):
    a = _silu(c_ref[...]).astype(BF16)
    w = w_ref[0].astype(BF16)
    o_ref[0] = jnp.dot(a, w, preferred_element_type=F32) + b_ref[0]


def _modulation(c_all, w_cond, b_cond):
    depth, d, n = w_cond.shape
    rows = c_all.shape[0]
    tn = 1536
    return pl.pallas_call(
        _modulation_kernel,
        grid=(depth, n // tn),
        in_specs=[
            pl.BlockSpec((rows, d), lambda i, j: (0, 0)),
            pl.BlockSpec((1, d, tn), lambda i, j: (i, 0, j)),
            pl.BlockSpec((1, 1, tn), lambda i, j: (i, 0, j)),
        ],
        out_specs=pl.BlockSpec((1, rows, tn), lambda i, j: (i, 0, j)),
        out_shape=jax.ShapeDtypeStruct((depth, rows, n), F32),
        compiler_params=_params(2),
        name="modulation",
    )(c_all, w_cond, b_cond.reshape(depth, 1, n))


def _modmm_kernel(x_ref, sc_ref, sh_ref, w_ref, o_ref):
    bb, tt, d = x_ref.shape
    h = x_ref[...] * (1.0 + sc_ref[...]) + sh_ref[...]
    h = h.reshape(bb * tt, d).astype(BF16)
    acc = jnp.dot(h, w_ref[...], preferred_element_type=F32)
    o_ref[...] = acc.reshape(bb, tt, acc.shape[-1]).astype(o_ref.dtype)


def _modmm(x, mod, w, out_dtype, t_start=0):
    batch, seq, d = x.shape
    n = w.shape[1]
    seq_out = seq - t_start
    bb, tt = _row_blocks(batch, seq_out, ROW_TILE)
    assert t_start % tt == 0
    off = t_start // tt
    return pl.pallas_call(
        _modmm_kernel,
        grid=(batch // bb, seq_out // tt),
        in_specs=[
            pl.BlockSpec((bb, tt, d), lambda b, t: (b, t + off, 0)),
            _mod_spec(bb, 1, 2),
            _mod_spec(bb, 0, 2),
            pl.BlockSpec((d, n), lambda b, t: (0, 0)),
        ],
        out_specs=pl.BlockSpec((bb, tt, n), lambda b, t: (b, t, 0)),
        out_shape=jax.ShapeDtypeStruct((batch, seq_out, n), out_dtype),
        compiler_params=_params(2),
        name="a_project",
    )(x, mod, mod, w)


def _band_attn_kernel(*refs, n_kb, n_maybe_invalid):
    q_ref = refs[0]
    k_refs = refs[1:1 + n_kb]
    v_refs = refs[1 + n_kb:1 + 2 * n_kb]
    f_ref = refs[1 + 2 * n_kb]
    o_ref = refs[2 + 2 * n_kb]
    bias_ref = refs[3 + 2 * n_kb]
    q_tile = q_ref.shape[1]
    kb_sizes = [r.shape[1] for r in k_refs]
    k_tot = sum(kb_sizes)
    width = f_ref.shape[-1]
    i = pl.program_id(1)

    @pl.when((pl.program_id(0) == 0) & (i == 0))
    def _():
        rq = lax.broadcasted_iota(jnp.int32, (q_tile, k_tot), 0) // CHUNK
        ck = lax.broadcasted_iota(jnp.int32, (q_tile, k_tot), 1) // CHUNK
        in_band = (ck >= rq) & (ck <= rq + N_PAST_CHUNKS)
        for h in range(A_HEADS):
            rows = jnp.broadcast_to(f_ref[h], (q_tile, width))
            toep = pltpu.roll(rows, k_tot + 1, 1, stride=1, stride_axis=0)
            bias_ref[h] = jnp.where(in_band, toep[:, :k_tot], -jnp.inf)

    q = q_ref[0] * (HEAD_DIM ** -0.5)
    for h in range(A_HEADS):
        cols = slice(h * HEAD_DIM, (h + 1) * HEAD_DIM)
        qh = q[:, cols]
        parts = []
        for kb in range(n_kb):
            lg = lax.dot_general(qh, k_refs[kb][0, :, cols], (((1,), (1,)), ((), ())),
                                 preferred_element_type=F32)
            if kb < n_maybe_invalid:
                lg = jnp.where(i - n_maybe_invalid + kb >= 0, lg, -jnp.inf)
            parts.append(lg)
        logits = jnp.concatenate(parts, axis=1) + bias_ref[h]
        m = jnp.max(logits, axis=1, keepdims=True)
        p = jnp.exp(logits - m)
        l = jnp.sum(p, axis=1, keepdims=True)
        pb = p.astype(BF16)
        acc = jnp.zeros((q_tile, HEAD_DIM), F32)
        start = 0
        for kb in range(n_kb):
            acc = acc + jnp.dot(pb[:, start:start + kb_sizes[kb]], v_refs[kb][0, :, cols],
                                preferred_element_type=F32)
            start += kb_sizes[kb]
        o_ref[0, :, cols] = (acc / l).astype(o_ref.dtype)


def _bias_vector(table, q_tile, k_tot):
    width = k_tot + q_tile
    lo = -(q_tile - 1) - BAND_ROWS + REL_CLIP
    left = max(0, -lo)
    start = max(0, lo)
    n_mid = min(2 * REL_CLIP + 1 - start, width - left)
    right = width - left - n_mid
    mid = table.T[:, start:start + n_mid]
    f = jnp.pad(mid, ((0, 0), (left, right)), mode="edge")
    return f.reshape(A_HEADS, 1, width)


def _band_attn_prompt(qkv, table):
    batch, seq, _ = qkv.shape
    hd = A_HEADS * HEAD_DIM
    qt = A_Q_TILE
    n_prev = BAND_ROWS // qt
    n_kb = n_prev + 1
    k_tot = n_kb * qt
    f = _bias_vector(table, qt, k_tot)

    def kv_spec(kb, col):
        return pl.BlockSpec((1, qt, hd), lambda b, i: (b, jnp.maximum(i - n_prev + kb, 0), col))

    return pl.pallas_call(
        functools.partial(_band_attn_kernel, n_kb=n_kb, n_maybe_invalid=n_prev),
        grid=(batch, seq // qt),
        in_specs=[pl.BlockSpec((1, qt, hd), lambda b, i: (b, i, 0))]
        + [kv_spec(kb, 1) for kb in range(n_kb)]
        + [kv_spec(kb, 2) for kb in range(n_kb)]
        + [pl.BlockSpec(f.shape, lambda b, i: (0, 0, 0))],
        out_specs=pl.BlockSpec((1, qt, hd), lambda b, i: (b, i, 0)),
        out_shape=jax.ShapeDtypeStruct((batch, seq, hd), BF16),
        scratch_shapes=[pltpu.VMEM((A_HEADS, qt, k_tot), F32)],
        compiler_params=_params(2),
        name="band_attn_prompt",
    )(*([qkv] * (1 + 2 * n_kb)), f)


def _band_attn_sample(qkv, cache_k, cache_v, table):
    batch, seq, _ = qkv.shape
    hd = A_HEADS * HEAD_DIM
    win = cache_k.shape[1]
    assert seq == CHUNK and win == BAND_ROWS
    k_tot = win + seq
    f = _bias_vector(table, seq, k_tot)
    new = lambda col: pl.BlockSpec((1, seq, hd), lambda b, i: (b, 0, col))
    old = pl.BlockSpec((1, win, hd), lambda b, i: (b, 0, 0))
    return pl.pallas_call(
        functools.partial(_band_attn_kernel, n_kb=2, n_maybe_invalid=0),
        grid=(batch, 1),
        in_specs=[new(0), old, new(1), old, new(2), pl.BlockSpec(f.shape, lambda b, i: (0, 0, 0))],
        out_specs=pl.BlockSpec((1, seq, hd), lambda b, i: (b, 0, 0)),
        out_shape=jax.ShapeDtypeStruct((batch, seq, hd), BF16),
        scratch_shapes=[pltpu.VMEM((A_HEADS, seq, k_tot), F32)],
        compiler_params=_params(2),
        name="band_attn_sample",
    )(qkv, cache_k, qkv, cache_v, qkv, f)


def _mm_postnorm_kernel(o_ref, w_ref, x_ref, gate_ref, g_ref, b_ref, y_ref, *, alpha, o_transposed):
    bb, tt, d = x_ref.shape
    if o_transposed:
        sub = lax.dot_general(o_ref[0], w_ref[...], (((0,), (0,)), ((), ())), preferred_element_type=F32)
    else:
        o = o_ref[...].reshape(bb * tt, o_ref.shape[-1])
        sub = jnp.dot(o, w_ref[...], preferred_element_type=F32)
    z = alpha * x_ref[...] + (1.0 + gate_ref[...]) * sub.reshape(bb, tt, d)
    y_ref[...] = _layer_norm_rows(z, g_ref[...], b_ref[...])


def _mm_postnorm(o, w, x, mod, gate_chunk, g, b, alpha, o_transposed=False):
    batch, seq, d = x.shape
    bb, tt = _row_blocks(batch, seq, ROW_TILE)
    k = w.shape[0]
    assert bb == 1 or not o_transposed
    o_spec = (pl.BlockSpec((1, k, tt), lambda b_, t: (b_, 0, t)) if o_transposed
              else pl.BlockSpec((bb, tt, k), lambda b_, t: (b_, t, 0)))
    return pl.pallas_call(
        functools.partial(_mm_postnorm_kernel, alpha=alpha, o_transposed=o_transposed),
        grid=(batch // bb, seq // tt),
        in_specs=[
            o_spec,
            pl.BlockSpec((k, d), lambda b_, t: (0, 0)),
            pl.BlockSpec((bb, tt, d), lambda b_, t: (b_, t, 0)),
            _mod_spec(bb, gate_chunk, 2),
            pl.BlockSpec((1, 1, d), lambda b_, t: (0, 0, 0)),
            pl.BlockSpec((1, 1, d), lambda b_, t: (0, 0, 0)),
        ],
        out_specs=pl.BlockSpec((bb, tt, d), lambda b_, t: (b_, t, 0)),
        out_shape=jax.ShapeDtypeStruct((batch, seq, d), F32),
        compiler_params=_params(2),
        name="out_proj_postnorm",
    )(o, w, x, mod, g.reshape(1, 1, d), b.reshape(1, 1, d))


def _dense_ffn_kernel(x_ref, sc_ref, sh_ref, gate_ref, wg_ref, wu_ref, wd_ref, g_ref, b_ref,
                      y_ref, h_ref, acc_ref, *, alpha):
    bb, tt, d = x_ref.shape
    rows = bb * tt
    c = pl.program_id(2)

    @pl.when(c == 0)
    def _():
        h = (x_ref[...] * (1.0 + sc_ref[...]) + sh_ref[...]).reshape(rows, d)
        h_ref[...] = h.astype(BF16)
        acc_ref[...] = jnp.zeros_like(acc_ref)

    h = h_ref[...]
    gt = jnp.dot(h, wg_ref[...], preferred_element_type=F32)
    up = jnp.dot(h, wu_ref[...], preferred_element_type=F32)
    act = (_silu(gt) * up).astype(BF16)
    acc_ref[...] += jnp.dot(act, wd_ref[...], preferred_element_type=F32)

    @pl.when(c == pl.num_programs(2) - 1)
    def _():
        z = alpha * x_ref[...] + (1.0 + gate_ref[...]) * acc_ref[...].reshape(bb, tt, d)
        y_ref[...] = _layer_norm_rows(z, g_ref[...], b_ref[...])


def _dense_ffn(x, mod, w_gu, w_down, g, b, *, ff_chunk, alpha):
    batch, seq, d = x.shape
    ff = w_down.shape[0]
    assert ff % ff_chunk == 0
    n_c = ff // ff_chunk
    bb, tt = _row_blocks(batch, seq, FFN_ROW_TILE)
    rows = bb * tt
    vec = pl.BlockSpec((1, 1, d), lambda b_, t, c: (0, 0, 0))

    def mod_spec(chunk):
        return pl.BlockSpec((bb, 1, d), lambda b_, t, c: (b_, 0, chunk))

    return pl.pallas_call(
        functools.partial(_dense_ffn_kernel, alpha=alpha),
        grid=(batch // bb, seq // tt, n_c),
        in_specs=[
            pl.BlockSpec((bb, tt, d), lambda b_, t, c: (b_, t, 0)),
            mod_spec(4),
            mod_spec(3),
            mod_spec(5),
            pl.BlockSpec((d, ff_chunk), lambda b_, t, c: (0, c)),
            pl.BlockSpec((d, ff_chunk), lambda b_, t, c: (0, n_c + c)),
            pl.BlockSpec((ff_chunk, d), lambda b_, t, c: (c, 0)),
            vec,
            vec,
        ],
        out_specs=pl.BlockSpec((bb, tt, d), lambda b_, t, c: (b_, t, 0)),
        out_shape=jax.ShapeDtypeStruct((batch, seq, d), F32),
        scratch_shapes=[pltpu.VMEM((rows, d), BF16), pltpu.VMEM((rows, d), F32)],
        compiler_params=_params(3),
        name="dense_ffn",
    )(x, mod, mod, mod, w_gu, w_gu, w_down, g.reshape(1, 1, d), b.reshape(1, 1, d))


def _moe_kernel(x_ref, sc_ref, sh_ref, gate_ref, wr_ref, br_ref, wg_ref, wu_ref, wd_ref, g_ref, b_ref,
                y_ref, h_ref, comb_ref, slot_ref, cnt_ref, tri_ref, hc_ref, acc_ref, out_ref, *, alpha):
    bb, tt, d = x_ref.shape
    rows = bb * tt
    cap = hc_ref.shape[1]
    e = pl.program_id(2)
    c = pl.program_id(3)
    last_c = pl.num_programs(3) - 1
    lane = lax.broadcasted_iota(jnp.int32, (rows, LANES), 1)

    @pl.when((pl.program_id(0) == 0) & (pl.program_id(1) == 0) & (e == 0) & (c == 0))
    def _():
        r = lax.broadcasted_iota(jnp.int32, (rows, rows), 0)
        col = lax.broadcasted_iota(jnp.int32, (rows, rows), 1)
        tri_ref[...] = jnp.where(r <= col, 1.0, 0.0).astype(BF16)

    @pl.when((e == 0) & (c == 0))
    def _():
        h = (x_ref[...] * (1.0 + sc_ref[...]) + sh_ref[...]).reshape(rows, d)
        h_ref[...] = h.astype(BF16)
        out_ref[...] = jnp.zeros_like(out_ref)
        logits = jnp.dot(h, wr_ref[...], preferred_element_type=F32,
                         precision=lax.Precision.HIGHEST) + br_ref[...]
        logits = jnp.where(lane < N_EXPERTS, logits, -jnp.inf)
        m1 = jnp.max(logits, axis=1, keepdims=True)
        i1 = jnp.min(jnp.where(logits == m1, lane, LANES), axis=1, keepdims=True)
        rest = jnp.where(lane == i1, -jnp.inf, logits)
        m2 = jnp.max(rest, axis=1, keepdims=True)
        i2 = jnp.min(jnp.where(rest == m2, lane, LANES), axis=1, keepdims=True)
        e2 = jnp.exp(m2 - m1)
        den = 1.0 + e2
        comb_ref[...] = jnp.where(lane == i1, 1.0 / den, 0.0) + jnp.where(lane == i2, e2 / den, 0.0)
        member = jnp.where((lane == i1) | (lane == i2), 1.0, 0.0)
        member_t = member.T[:2 * N_EXPERTS]
        upto = jnp.dot(member_t.astype(BF16), tri_ref[...], preferred_element_type=F32)
        slot_ref[...] = jnp.where(member_t > 0, upto - 1.0, -1.0)
        cnt_ref[...] = jnp.broadcast_to(upto[:, rows - 1:rows], cnt_ref.shape)

    n_sub = (jnp.max(cnt_ref[pl.ds(e, 1), :]).astype(jnp.int32) + cap - 1) // cap

    def selection(s):
        slot = slot_ref[pl.ds(e, 1), :] - (s * cap).astype(F32)
        r = lax.broadcasted_iota(jnp.int32, (cap, rows), 0).astype(F32)
        return jnp.where(slot == r, 1.0, 0.0).astype(BF16)

    @pl.when(c == 0)
    def _():
        def pack(s, carry):
            hc_ref[s] = jnp.dot(selection(s), h_ref[...], preferred_element_type=F32).astype(BF16)
            acc_ref[s] = jnp.zeros(acc_ref.shape[1:], F32)
            return carry
        lax.fori_loop(0, n_sub, pack, 0)

    def expert(s, carry):
        hc = hc_ref[s]
        gt = jnp.dot(hc, wg_ref[0], preferred_element_type=F32)
        up = jnp.dot(hc, wu_ref[0], preferred_element_type=F32)
        act = (_silu(gt) * up).astype(BF16)
        acc_ref[s] += jnp.dot(act, wd_ref[0], preferred_element_type=F32)
        return carry

    lax.fori_loop(0, n_sub, expert, 0)

    @pl.when(c == last_c)
    def _():
        ce = jnp.sum(jnp.where(lane == e, comb_ref[...], 0.0), axis=1, keepdims=True)

        def unpack(s, carry):
            back = lax.dot_general(selection(s), acc_ref[s].astype(BF16), (((0,), (0,)), ((), ())),
                                   preferred_element_type=F32)
            out_ref[...] += ce * back
            return carry
        lax.fori_loop(0, n_sub, unpack, 0)

    @pl.when((e == pl.num_programs(2) - 1) & (c == last_c))
    def _():
        z = alpha * x_ref[...] + (1.0 + gate_ref[...]) * out_ref[...].reshape(bb, tt, d)
        y_ref[...] = _layer_norm_rows(z, g_ref[...], b_ref[...])


def _moe(x, mod, w_r, b_r, w_gu, w_down, g, b, *, ff_chunk, alpha):
    batch, seq, d = x.shape
    n_e, ff, _ = w_down.shape
    assert ff % ff_chunk == 0
    n_c = ff // ff_chunk
    bb, tt = _row_blocks(batch, seq, FFN_ROW_TILE)
    rows = bb * tt
    cap = MOE_CAPACITY
    max_sub = -(-rows // cap)
    vec = pl.BlockSpec((1, 1, d), lambda b_, t, e, c: (0, 0, 0))
    return pl.pallas_call(
        functools.partial(_moe_kernel, alpha=alpha),
        grid=(batch // bb, seq // tt, n_e, n_c),
        in_specs=[
            pl.BlockSpec((bb, tt, d), lambda b_, t, e, c: (b_, t, 0), pipeline_mode=pl.Buffered(1)),
            _mod_spec(bb, 4, 4),
            _mod_spec(bb, 3, 4),
            _mod_spec(bb, 5, 4),
            pl.BlockSpec((d, LANES), lambda b_, t, e, c: (0, 0)),
            pl.BlockSpec((1, LANES), lambda b_, t, e, c: (0, 0)),
            pl.BlockSpec((1, d, ff_chunk), lambda b_, t, e, c: (e, 0, c)),
            pl.BlockSpec((1, d, ff_chunk), lambda b_, t, e, c: (e, 0, n_c + c)),
            pl.BlockSpec((1, ff_chunk, d), lambda b_, t, e, c: (e, c, 0)),
            vec,
            vec,
        ],
        out_specs=pl.BlockSpec((bb, tt, d), lambda b_, t, e, c: (b_, t, 0)),
        out_shape=jax.ShapeDtypeStruct((batch, seq, d), F32),
        scratch_shapes=[
            pltpu.VMEM((rows, d), BF16),
            pltpu.VMEM((rows, LANES), F32),
            pltpu.VMEM((2 * N_EXPERTS, rows), F32),
            pltpu.VMEM((2 * N_EXPERTS, LANES), F32),
            pltpu.VMEM((rows, rows), BF16),
            pltpu.VMEM((max_sub, cap, d), BF16),
            pltpu.VMEM((max_sub, cap, d), F32),
            pltpu.VMEM((rows, d), F32),
        ],
        compiler_params=_params(4),
        name="moe_ffn",
    )(x, mod, mod, mod, w_r, b_r, w_gu, w_gu, w_down, g.reshape(1, 1, d), b.reshape(1, 1, d))


def _rope_lanes(seg, cos, sin_signed, first_half):
    width = seg.shape[-1]
    reps = width // LANES
    if reps > 1:
        cos = jnp.concatenate([cos] * reps, axis=1)
        sin_signed = jnp.concatenate([sin_signed] * reps, axis=1)
        first_half = jnp.concatenate([first_half] * reps, axis=1)
    half = HEAD_DIM // 2
    swapped = jnp.where(first_half, pltpu.roll(seg, width - half, 1), pltpu.roll(seg, half, 1))
    return seg * cos + swapped * sin_signed


def _b_project_kernel(x_ref, sc_ref, sh_ref, w_ref, cos_ref, sin_ref, kg_ref, kb_ref,
                      q_ref, qi_ref, k_ref, v_ref, kw_ref, *extra_refs, transposed):
    bb, tt, d = x_ref.shape
    rows = bb * tt
    h = x_ref[...] * (1.0 + sc_ref[...]) + sh_ref[...]
    h = h.reshape(rows, d).astype(BF16)
    acc = jnp.dot(h, w_ref[...], preferred_element_type=F32)
    cos = cos_ref[...]
    sin_signed = sin_ref[...]
    lane = lax.broadcasted_iota(jnp.int32, (1, LANES), 1)
    first_half = (lane % HEAD_DIM) < (HEAD_DIM // 2)

    def out(ref, val):
        ref[...] = val.reshape(bb, tt, val.shape[-1]).astype(ref.dtype)

    def out_query_side(ref, val):
        if transposed:
            ref[0] = val.T.astype(ref.dtype)
        else:
            out(ref, val)

    o_k, o_v, o_qi, o_ki = B_Q, B_Q + B_KV, B_Q + 2 * B_KV, B_Q + 2 * B_KV + B_QI
    out_query_side(q_ref, _rope_lanes(acc[:, :o_k], cos, sin_signed, first_half) * QK_SCALE_LOG2)
    k_rot = _rope_lanes(acc[:, o_k:o_v], cos, sin_signed, first_half)
    out(k_ref, k_rot)
    out(v_ref, acc[:, o_v:o_qi])
    out_query_side(qi_ref, _rope_lanes(acc[:, o_qi:o_ki], cos, sin_signed, first_half))
    seg = acc[:, o_ki:]
    is_ki = lane < IDX_DIM
    mu = jnp.sum(jnp.where(is_ki, seg, 0.0), axis=1, keepdims=True) / IDX_DIM
    cen = jnp.where(is_ki, seg - mu, 0.0)
    var = jnp.sum(cen * cen, axis=1, keepdims=True) / IDX_DIM
    ki = cen * lax.rsqrt(var + LN_EPS) * kg_ref[...] + kb_ref[...]
    ki = _rope_lanes(ki, cos, sin_signed, first_half)
    kw = jnp.where(is_ki, ki, seg * (IDX_HEADS ** -0.5))
    out(kw_ref, kw)
    if transposed:
        kwt_ref, k16_ref, ki16_ref, vta_ref, kmax_ref = extra_refs
        kwt_ref[0] = kw.T
        out(k16_ref, k_rot)
        k_sq = k_rot.astype(BF16).astype(F32)
        k_sq = k_sq * k_sq
        norm_max = jnp.zeros((1, LANES), F32)
        for g in range(B_KV_HEADS):
            part = k_sq[:, (g // 2) * LANES:(g // 2 + 1) * LANES]
            n2 = jnp.sum(jnp.where(lane // HEAD_DIM == g % 2, part, 0.0), axis=1, keepdims=True)
            norm_max = jnp.where(lane == g, jnp.max(n2, axis=0, keepdims=True), norm_max)
        norm_max = jnp.broadcast_to(norm_max, kmax_ref.shape[1:])

        @pl.when(pl.program_id(1) == 0)
        def _():
            kmax_ref[0] = norm_max

        @pl.when(pl.program_id(1) > 0)
        def _():
            kmax_ref[0] = jnp.maximum(kmax_ref[0], norm_max)
        out(ki16_ref, kw[:, :IDX_DIM])
        v_t = acc[:, o_v:o_qi].T.astype(BF16).reshape(B_KV_HEADS, HEAD_DIM, tt)
        tail_rows = vta_ref.shape[2] - HEAD_DIM
        first_row = lax.broadcasted_iota(jnp.int32, (B_KV_HEADS, tail_rows, tt), 1) == 0
        vta_ref[0] = jnp.concatenate([v_t, jnp.where(first_row, 1.0, 0.0).astype(BF16)], axis=1)


def _b_project(x, mod, w, cos, sin_signed, kn_g, kn_b, table_per_tile, transposed=False):
    batch, seq, d = x.shape
    bb, tt = _row_blocks(batch, seq, ROW_TILE)
    rows = bb * tt
    n = w.shape[1]
    assert bb == 1 or not transposed
    tab = pl.BlockSpec((rows, LANES), (lambda b, t: (0, 0)) if table_per_tile else (lambda b, t: (t, 0)))
    vec = pl.BlockSpec((1, LANES), lambda b, t: (0, 0))

    def o_spec(width):
        return pl.BlockSpec((bb, tt, width), lambda b, t: (b, t, 0))

    def o_shape(width, dtype):
        return jax.ShapeDtypeStruct((batch, seq, width), dtype)

    def t_spec(width):
        return pl.BlockSpec((1, width, tt), lambda b, t: (b, 0, t))

    def t_shape(width, dtype):
        return jax.ShapeDtypeStruct((batch, width, seq), dtype)

    q_spec, q_shape = (t_spec, t_shape) if transposed else (o_spec, o_shape)
    extra_specs, extra_shapes = [], []
    if transposed:
        extra_specs = [t_spec(LANES), o_spec(B_KV), o_spec(IDX_DIM),
                       pl.BlockSpec((1, B_KV_HEADS, V_AUG_ROWS, tt), lambda b, t: (b, 0, 0, t)),
                       pl.BlockSpec((1, 8, LANES), lambda b, t: (b, 0, 0))]
        extra_shapes = [t_shape(LANES, F32), o_shape(B_KV, BF16), o_shape(IDX_DIM, BF16),
                        jax.ShapeDtypeStruct((batch, B_KV_HEADS, V_AUG_ROWS, seq), BF16),
                        jax.ShapeDtypeStruct((batch, 8, LANES), F32)]
    return pl.pallas_call(
        functools.partial(_b_project_kernel, transposed=transposed),
        grid=(batch // bb, seq // tt),
        in_specs=[
            pl.BlockSpec((bb, tt, d), lambda b, t: (b, t, 0)),
            _mod_spec(bb, 1, 2),
            _mod_spec(bb, 0, 2),
            pl.BlockSpec((d, n), lambda b, t: (0, 0)),
            tab,
            tab,
            vec,
            vec,
        ],
        out_specs=[q_spec(B_Q), q_spec(B_QI), o_spec(B_KV), o_spec(B_KV), o_spec(LANES)] + extra_specs,
        out_shape=[q_shape(B_Q, BF16), q_shape(B_QI, BF16), o_shape(B_KV, F32), o_shape(B_KV, F32),
                   o_shape(LANES, F32)] + extra_shapes,
        compiler_params=_params(2),
        name="b_project",
    )(x, mod, mod, w, cos, sin_signed, kn_g, kn_b)


def _rope_tables(pos):
    half = HEAD_DIM // 2
    inv = ROPE_THETA ** (-jnp.arange(half, dtype=F32) / half)
    ang = pos.astype(F32)[:, None] * inv[None, :]
    cos, sin = jnp.cos(ang), jnp.sin(ang)
    return (jnp.concatenate([cos, cos, cos, cos], axis=1),
            jnp.concatenate([-sin, sin, -sin, sin], axis=1))


def _key_norm_kernel(k_ref, grp_ref, o_ref):
    k = k_ref[0].astype(F32)
    n2 = jnp.dot((k * k).astype(BF16), grp_ref[...], preferred_element_type=F32)
    mx = jnp.broadcast_to(jnp.max(n2, axis=0, keepdims=True), o_ref.shape[1:])

    @pl.when(pl.program_id(1) == 0)
    def _():
        o_ref[0] = mx

    @pl.when(pl.program_id(1) > 0)
    def _():
        o_ref[0] = jnp.maximum(o_ref[0], mx)


def _key_norm_max(k):
    batch, lp, width = k.shape
    tile = DSA_KEY_TILE
    grp = (jnp.arange(width)[:, None] // HEAD_DIM == jnp.arange(LANES)[None, :]).astype(BF16)
    return pl.pallas_call(
        _key_norm_kernel,
        grid=(batch, lp // tile),
        in_specs=[pl.BlockSpec((1, tile, width), lambda b, t: (b, t, 0)),
                  pl.BlockSpec((width, LANES), lambda b, t: (0, 0))],
        out_specs=pl.BlockSpec((1, 8, LANES), lambda b, t: (b, 0, 0)),
        out_shape=jax.ShapeDtypeStruct((batch, 8, LANES), F32),
        compiler_params=_params(2),
        name="key_norm_max",
    )(k, grp)


def _dsa_kernel(qt_ref, qit_ref, kwt_ref, ki_ref, k_ref, vta_ref, kmax_ref, o_ref,
                s_ref, qpad_ref, bound_ref, m_ref, acc_ref, *, causal, n_keys, k_sel):
    qn = qt_ref.shape[2]
    lt = DSA_KEY_TILE
    i = pl.program_id(1)
    if causal:
        n_tiles = ((i + 1) * qn + lt - 1) // lt
        qpos = i * qn + lax.broadcasted_iota(jnp.int32, (1, qn), 1)
        limit = (qpos // CHUNK + 1) * CHUNK
    else:
        n_tiles = ki_ref.shape[1] // lt
        limit = jnp.full((1, qn), n_keys, jnp.int32)

    def key_slice(t):
        return pl.ds(pl.multiple_of(t * lt, lt), lt)

    def key_pos(t):
        return t * lt + lax.broadcasted_iota(jnp.int32, (lt, 1), 0)

    def fold_rows(x, op):
        return op(op(x.reshape(lt // 32, 4, 8, x.shape[-1]), axis=0), axis=0)

    def paired_tile_loop(body, init):
        def pair(t2, carry):
            return body(2 * t2 + 1, body(2 * t2, carry))
        carry = lax.fori_loop(0, n_tiles // 2, pair, init)
        return lax.cond(n_tiles % 2 == 1, lambda c: body(n_tiles - 1, c), lambda c: c, carry)

    qit = qit_ref[0]
    kwt = kwt_ref[0]
    qi_w = jnp.concatenate([qit[h * IDX_DIM:(h + 1) * IDX_DIM, :] for h in range(IDX_HEADS)], axis=1)
    w_row = jnp.concatenate([kwt[IDX_DIM + h:IDX_DIM + h + 1, :] for h in range(IDX_HEADS)], axis=1)
    w_row = w_row * (IDX_DIM ** -0.5)

    def score_body(t, carry):
        rmax, rmin = carry
        s = jnp.dot(ki_ref[0, key_slice(t), :], qi_w, preferred_element_type=F32)
        s = jnp.maximum(s, 0.0) * w_row
        sc = s[:, 0:qn]
        for h in range(1, IDX_HEADS):
            sc = sc + s[:, h * qn:(h + 1) * qn]
        adm = key_pos(t) < limit
        masked = jnp.where(adm, sc, -jnp.inf)
        s_ref[key_slice(t), :] = masked
        rmax = jnp.maximum(rmax, jnp.max(masked, axis=0, keepdims=True))
        rmin = jnp.minimum(rmin, jnp.min(jnp.where(adm, sc, jnp.inf), axis=0, keepdims=True))
        return rmax, rmin

    rmax, rmin = paired_tile_loop(score_body,
                                  (jnp.full((1, qn), -jnp.inf, F32), jnp.full((1, qn), jnp.inf, F32)))

    def count(pred):
        def body(t, acc):
            return acc + fold_rows(jnp.where(pred(s_ref[key_slice(t), :], key_pos(t)), 1.0, 0.0), jnp.sum)
        acc = paired_tile_loop(body, jnp.zeros((8, qn), F32))
        return jnp.sum(acc, axis=0, keepdims=True)

    def to_key(x):
        bits = lax.bitcast_convert_type(x, jnp.int32)
        return bits ^ ((bits >> 31) & 0x7FFFFFFF)

    def from_key(key):
        return lax.bitcast_convert_type(key ^ ((key >> 31) & 0x7FFFFFFF), F32)

    n_adm = limit.astype(F32)
    target = jnp.minimum(n_adm, float(k_sel))
    log_target = jnp.log(target)

    def any_left(done):
        return jnp.sum(done) < qn

    c_ge0 = count(lambda s, _: s >= 0.0)
    c_gt0 = count(lambda s, _: s > 0.0)
    is_pos = c_gt0 >= target
    is_neg = c_ge0 < target
    zero_key = jnp.zeros((1, qn), jnp.int32)
    lo0 = jnp.where(is_neg, to_key(rmin), zero_key)
    c_lo0 = jnp.where(is_neg, n_adm, c_ge0)
    c_hi0 = jnp.where(is_pos, 0.0, c_ge0)
    done0 = jnp.where((n_adm == target) | jnp.logical_not(is_pos | is_neg) | (c_lo0 == target), 1.0, 0.0)
    lo0 = jnp.where(n_adm == target, to_key(rmin), lo0)
    c_lo0 = jnp.where(n_adm == target, n_adm, c_lo0)

    ones = jnp.ones((1, qn), F32)

    def search(st0, key_to_value, value_to_key, count_ge):
        def cond(st):
            return any_left(st[7])

        def body(st):
            it, lo, hi, c_lo, c_hi, w_lo, w_hi, done, side = st
            t_lo, t_hi = key_to_value(lo), key_to_value(hi)
            f_lo = (jnp.log(c_lo) - log_target) * w_lo
            f_hi = (log_target - jnp.log(jnp.maximum(c_hi, 0.5))) * w_hi
            guess = value_to_key(t_lo + f_lo / (f_lo + f_hi) * (t_hi - t_lo))
            halve = (lo >> 1) + (hi >> 1) + (lo & hi & 1)
            mid = jnp.where(it >= SECANT_STEPS, halve, jnp.clip(guess, lo + 1, hi - 1))
            stuck = halve == lo
            c = count_ge(key_to_value(mid))
            live = (done < 0.5) & jnp.logical_not(stuck)
            up = live & (c >= target)
            dn = live & (c < target)
            w_hi = jnp.where(up, jnp.where(side > 0, 0.5 * w_hi, 1.0), jnp.where(dn, 1.0, w_hi))
            w_lo = jnp.where(dn, jnp.where(side < 0, 0.5 * w_lo, 1.0), jnp.where(up, 1.0, w_lo))
            side = jnp.where(up, 1.0, jnp.where(dn, -1.0, side))
            lo = jnp.where(up, mid, lo)
            c_lo = jnp.where(up, c, c_lo)
            hi = jnp.where(dn, mid, hi)
            c_hi = jnp.where(dn, c, c_hi)
            done = jnp.where(stuck | (c_lo == target) | (c_lo - c_hi <= 2.0), 1.0, done)
            return it + 1, lo, hi, c_lo, c_hi, w_lo, w_hi, done, side

        lo, hi, c_lo, c_hi, done = st0
        st = lax.while_loop(cond, body, (jnp.int32(0), lo, hi, c_lo, c_hi, ones, ones, done, 0.0 * ones))
        return st[1], st[2], st[3], st[4]

    hi0 = jnp.where(is_pos, to_key(rmax) + 1, zero_key)
    k_lo, k_hi, c_thr, c_above = search((lo0, hi0, c_lo0, c_hi0, done0),
                                        from_key, to_key, lambda v: count(lambda s, _: s >= v))
    thr = from_key(k_lo)

    two_left = (c_thr != target) & (c_thr - c_above == 2.0)

    def settle(_):
        t_lo, t_hi = from_key(k_lo), from_key(k_hi)

        def body(t, carry):
            top, bottom = carry
            s = s_ref[key_slice(t), :]
            top = jnp.maximum(top, fold_rows(jnp.where(s < t_hi, s, -jnp.inf), jnp.max))
            bottom = jnp.minimum(bottom, fold_rows(jnp.where(s >= t_lo, s, jnp.inf), jnp.min))
            return top, bottom

        top, bottom = lax.fori_loop(0, n_tiles, body, (jnp.full((8, qn), -jnp.inf, F32),
                                                       jnp.full((8, qn), jnp.inf, F32)))
        top = jnp.max(top, axis=0, keepdims=True)
        bottom = jnp.min(bottom, axis=0, keepdims=True)
        return (jnp.where(two_left, top, thr),
                jnp.where(two_left & (top != bottom), target, c_thr))

    thr, c_thr = lax.cond(jnp.sum(jnp.where(two_left, 1.0, 0.0)) > 0, settle, lambda _: (thr, c_thr), 0)

    tied = c_thr > target

    @pl.when(jnp.sum(jnp.where(tied, 1.0, 0.0)) > 0)
    def _():
        need = target - count(lambda s, _: s > thr)

        def idx_cond(st):
            return any_left(st[5])

        def idx_body(st):
            it, lo_j, hi_j, c_lo_j, c_hi_j, done = st
            span = (hi_j - lo_j).astype(F32)
            step = jnp.ceil((need - c_lo_j) / (c_hi_j - c_lo_j) * span).astype(jnp.int32)
            step = jnp.where(it % 2 == 0, step, (hi_j - lo_j) // 2)
            mid = lo_j + jnp.clip(step, 1, jnp.maximum(hi_j - lo_j - 1, 1))
            c = count(lambda s, kp: (s == thr) & (kp <= mid))
            live = (done < 0.5) & (hi_j - lo_j > 1)
            below = live & (c < need)
            above = live & (c >= need)
            lo_j = jnp.where(below, mid, lo_j)
            c_lo_j = jnp.where(below, c, c_lo_j)
            hi_j = jnp.where(above, mid, hi_j)
            c_hi_j = jnp.where(above, c, c_hi_j)
            done = jnp.where((c_hi_j == need) | (hi_j - lo_j <= 1), 1.0, done)
            return it + 1, lo_j, hi_j, c_lo_j, c_hi_j, done

        first = jnp.full((1, qn), -1, jnp.int32)
        final = jnp.full((1, qn), n_tiles * lt - 1, jnp.int32)
        n_ties = c_thr - (target - need)
        idx_done0 = jnp.where(tied & (n_ties > need), 0.0, 1.0)
        idx = lax.while_loop(idx_cond, idx_body,
                             (jnp.int32(0), first, final, 0.0 * ones, n_ties, idx_done0))
        last = idx[2]

        def drop_body(t, carry):
            s = s_ref[key_slice(t), :]
            s_ref[key_slice(t), :] = jnp.where(tied & (s == thr) & (key_pos(t) > last), -jnp.inf, s)
            return carry

        lax.fori_loop(0, n_tiles, drop_body, 0)

    qt = qt_ref[0]
    zeros = jnp.zeros((HEAD_DIM, B_GROUP * qn), BF16)
    for g in range(B_KV_HEADS):
        q_g = jnp.concatenate(
            [qt[(g * B_GROUP + j) * HEAD_DIM:(g * B_GROUP + j + 1) * HEAD_DIM, :] for j in range(B_GROUP)],
            axis=1)
        qpad_ref[g] = jnp.concatenate([zeros] * g + [q_g] + [zeros] * (B_KV_HEADS - 1 - g), axis=0)

    qf = qt.astype(F32)
    q_norm2 = jnp.sum((qf * qf).reshape(B_HEADS, HEAD_DIM, qn), axis=1)
    for g in range(B_KV_HEADS):
        b = jnp.sqrt(q_norm2[g * B_GROUP:(g + 1) * B_GROUP] * kmax_ref[0, 0:1, g:g + 1])
        b = b * BOUND_SLACK + BOUND_SLACK_ABS
        bound_ref[g] = jnp.concatenate([b[j:j + 1] for j in range(B_GROUP)], axis=1)
    acc_ref[...] = jnp.zeros(acc_ref.shape, F32)

    def fast_body(t, carry):
        keys = k_ref[0, key_slice(t), :]
        sel = jnp.where(s_ref[key_slice(t), :] >= thr, 1.0, 0.0).astype(BF16)
        sel = jnp.concatenate([sel] * B_GROUP, axis=1)
        def logits(g):
            return jnp.dot(keys, qpad_ref[g], preferred_element_type=F32)

        lg = logits(0)
        for g in range(B_KV_HEADS):
            lg_next = logits(g + 1) if g + 1 < B_KV_HEADS else None
            p = jnp.exp2(lg - bound_ref[g]).astype(BF16) * sel
            acc_ref[g] += jnp.dot(vta_ref[0, g, :, key_slice(t)], p, preferred_element_type=F32)
            lg = lg_next
        return carry

    paired_tile_loop(fast_body, 0)
    den_min = jnp.min(acc_ref[:, HEAD_DIM:HEAD_DIM + 1, :])

    @pl.when(jnp.logical_not(den_min >= MIN_TRUSTED_DENOMINATOR))
    def _():
        m_ref[...] = jnp.full(m_ref.shape, NEG_BIG, F32)
        acc_ref[...] = jnp.zeros(acc_ref.shape, F32)

        def exact_body(t, carry):
            keys = k_ref[0, key_slice(t), :]
            mb = jnp.where(s_ref[key_slice(t), :] >= thr, 0.0, -jnp.inf)
            mb = jnp.concatenate([mb] * B_GROUP, axis=1)
            for g in range(B_KV_HEADS):
                lg = jnp.dot(keys, qpad_ref[g], preferred_element_type=F32) + mb
                m_old = m_ref[g]
                m_new = jnp.maximum(m_old, jnp.max(fold_rows(lg, jnp.max), axis=0, keepdims=True))
                p = jnp.exp2(lg - m_new).astype(BF16)
                pv = jnp.dot(vta_ref[0, g, :, key_slice(t)], p, preferred_element_type=F32)
                acc_ref[g] = jnp.exp2(m_old - m_new) * acc_ref[g] + pv
                m_ref[g] = m_new
            return carry

        lax.fori_loop(0, n_tiles, exact_body, 0)

    for g in range(B_KV_HEADS):
        acc = acc_ref[g]
        o_g = acc[:HEAD_DIM] / acc[HEAD_DIM:HEAD_DIM + 1]
        for j in range(B_GROUP):
            hh = g * B_GROUP + j
            o_ref[0, hh * HEAD_DIM:(hh + 1) * HEAD_DIM, :] = o_g[:, j * qn:(j + 1) * qn].astype(o_ref.dtype)


V_AUG_ROWS = HEAD_DIM + 16


def _dsa(q, qi, kw, k, v, ki, *, q_tile, causal, k_sel, transposed=False, kmax=None):
    batch = q.shape[0]
    seq = q.shape[2] if transposed else q.shape[1]
    n_keys = k.shape[1]
    lp = -(-n_keys // DSA_KEY_TILE) * DSA_KEY_TILE
    assert seq % q_tile == 0 and q_tile % LANES == 0
    if v.ndim == 4:
        assert lp == n_keys and k.dtype == BF16 and ki.dtype == BF16
        vta = v
    else:
        pad = ((0, 0), (0, lp - n_keys), (0, 0))
        k, v, ki = (jnp.pad(a.astype(BF16), pad) for a in (k, v, ki))
        vt = jnp.swapaxes(v, 1, 2).reshape(batch, B_KV_HEADS, HEAD_DIM, lp)
        vta = jnp.concatenate([vt, jnp.ones((batch, B_KV_HEADS, 1, lp), BF16),
                               jnp.zeros((batch, B_KV_HEADS, V_AUG_ROWS - HEAD_DIM - 1, lp), BF16)], axis=2)
    qt, qit, kwt = (q, qi, kw) if transposed else (jnp.swapaxes(a, 1, 2) for a in (q, qi, kw))

    def qspec(rows):
        return pl.BlockSpec((1, rows, q_tile), lambda b, i: (b, 0, i))

    def resident(shape):
        zero = (0,) * len(shape)
        return pl.BlockSpec((1,) + shape, lambda b, i: (b,) + zero, pipeline_mode=pl.Buffered(1))

    ot = pl.pallas_call(
        functools.partial(_dsa_kernel, causal=causal, n_keys=n_keys, k_sel=k_sel),
        grid=(batch, seq // q_tile),
        in_specs=[qspec(B_Q), qspec(B_QI), qspec(LANES),
                  resident((lp, IDX_DIM)), resident((lp, B_KV)),
                  resident((B_KV_HEADS, V_AUG_ROWS, lp)),
                  pl.BlockSpec((1, 8, LANES), lambda b, i: (b, 0, 0))],
        out_specs=qspec(B_Q),
        out_shape=jax.ShapeDtypeStruct((batch, B_Q, seq), BF16),
        scratch_shapes=[pltpu.VMEM((lp, q_tile), F32),
                        pltpu.VMEM((B_KV_HEADS, B_KV, B_GROUP * q_tile), BF16),
                        pltpu.VMEM((B_KV_HEADS, 1, B_GROUP * q_tile), F32),
                        pltpu.VMEM((B_KV_HEADS, 1, B_GROUP * q_tile), F32),
                        pltpu.VMEM((B_KV_HEADS, V_AUG_ROWS, B_GROUP * q_tile), F32)],
        compiler_params=_params(2),
        name="dsa_prompt" if causal else "dsa_sample",
    )(qt, qit, kwt, ki, k, vta, _key_norm_max(k) if kmax is None else kmax)
    return ot if transposed else jnp.swapaxes(ot, 1, 2)


def kernel(x_prompt, x_sample, cache_k_a, cache_v_a, cache_k_b, cache_v_b, cache_kidx_b,
           c_prompt, c_sample, w_cond, b_cond, ln_g, ln_b, a_w_in, a_w_o, a_rel_bias,
           b_w_in, b_w_o, b_kidx_ln_g, b_kidx_ln_b, ffn_w_gu, ffn_w_down,
           moe_w_router, moe_b_router, moe_w_gu, moe_w_down):
    depth = w_cond.shape[0]
    alpha = (2 * depth) ** 0.25
    n_p, seq, d = x_prompt.shape
    n_s, dec_seq, _ = x_sample.shape
    past = cache_k_b.shape[2]
    a_hd = A_HEADS * HEAD_DIM

    rows = n_p + n_s
    rows_pad = -(-rows // 8) * 8
    c_all = jnp.pad(jnp.concatenate([c_prompt, c_sample], axis=0), ((0, rows_pad - rows), (0, 0)))
    mod_all = _modulation(c_all, w_cond, b_cond)

    xp, xs = x_prompt, x_sample
    outs = {k: [] for k in ("ka_p", "va_p", "kb_p", "vb_p", "ib_p", "ka_s", "va_s", "kb_s", "vb_s", "ib_s")}
    for i in range(depth):
        j = i // 2
        mod_p = mod_all[i, :n_p].reshape(n_p, 1, 6 * d)
        mod_s = mod_all[i, n_p:rows].reshape(n_s, 1, 6 * d)
        g1, b1, g2, b2 = ln_g[i, 0], ln_b[i, 0], ln_g[i, 1], ln_b[i, 1]
        if i % 2 == 0:
            w_in = a_w_in[j].astype(BF16)
            keep = min(BAND_ROWS, seq)
            qkv_p = _modmm(xp, mod_p, w_in, BF16)
            kv_p = _modmm(xp, mod_p, w_in[:, a_hd:], F32, t_start=seq - keep)
            qkv_s = _modmm(xs, mod_s, w_in, BF16)
            kv_s = _modmm(xs, mod_s, w_in[:, a_hd:], F32)
            outs["ka_p"].append(kv_p[..., :a_hd].reshape(n_p, keep, A_HEADS, HEAD_DIM))
            outs["va_p"].append(kv_p[..., a_hd:].reshape(n_p, keep, A_HEADS, HEAD_DIM))
            outs["ka_s"].append(kv_s[..., :a_hd].reshape(n_s, dec_seq, A_HEADS, HEAD_DIM))
            outs["va_s"].append(kv_s[..., a_hd:].reshape(n_s, dec_seq, A_HEADS, HEAD_DIM))
            o_p = _band_attn_prompt(qkv_p, a_rel_bias[j])
            ck = cache_k_a[j].reshape(n_s, -1, a_hd).astype(BF16)
            cv = cache_v_a[j].reshape(n_s, -1, a_hd).astype(BF16)
            o_s = _band_attn_sample(qkv_s, ck, cv, a_rel_bias[j])
            w_o = a_w_o[j].astype(BF16)
        else:
            w_in = jnp.pad(b_w_in[j], ((0, 0), (0, B_PROJ_PAD - B_PROJ))).astype(BF16)
            kn_g = jnp.pad(b_kidx_ln_g[j], (0, LANES - IDX_DIM)).reshape(1, LANES)
            kn_b = jnp.pad(b_kidx_ln_b[j], (0, LANES - IDX_DIM)).reshape(1, LANES)
            cos_p, sin_p = _rope_tables(jnp.arange(seq))
            cos_s, sin_s = _rope_tables(past + jnp.arange(dec_seq))
            bb_s, _ = _row_blocks(n_s, dec_seq, ROW_TILE)
            cos_s, sin_s = jnp.tile(cos_s, (bb_s, 1)), jnp.tile(sin_s, (bb_s, 1))
            q_p, qi_p, k_p, v_p, kw_p, kwt_p, k16_p, ki16_p, vta_p, kmax_p = _b_project(
                xp, mod_p, w_in, cos_p, sin_p, kn_g, kn_b, False, transposed=True)
            q_s, qi_s, k_s, v_s, kw_s = _b_project(xs, mod_s, w_in, cos_s, sin_s, kn_g, kn_b, True)
            outs["kb_p"].append(k_p.reshape(n_p, seq, B_KV_HEADS, HEAD_DIM))
            outs["vb_p"].append(v_p.reshape(n_p, seq, B_KV_HEADS, HEAD_DIM))
            outs["ib_p"].append(kw_p[..., :IDX_DIM])
            outs["kb_s"].append(k_s.reshape(n_s, dec_seq, B_KV_HEADS, HEAD_DIM))
            outs["vb_s"].append(v_s.reshape(n_s, dec_seq, B_KV_HEADS, HEAD_DIM))
            outs["ib_s"].append(kw_s[..., :IDX_DIM])
            o_p = _dsa(q_p, qi_p, kwt_p, k16_p, vta_p, ki16_p, q_tile=DSA_Q_TILE, causal=True,
                       k_sel=min(TOPK_MAX, seq // 4), transposed=True, kmax=kmax_p)
            n_keys = past + dec_seq
            kk = jnp.concatenate([cache_k_b[j].reshape(n_s, past, B_KV), k_s], axis=1)
            vc = jnp.concatenate([cache_v_b[j].reshape(n_s, past, B_KV), v_s], axis=1)
            kki = jnp.concatenate([cache_kidx_b[j], kw_s[..., :IDX_DIM]], axis=1)
            rep = LANES // dec_seq
            q_r, qi_r, kw_r = (jnp.concatenate([a] * rep, axis=1) for a in (q_s, qi_s, kw_s))
            o_s = _dsa(q_r, qi_r, kw_r, kk, vc, kki, q_tile=LANES, causal=False,
                       k_sel=min(TOPK_MAX, n_keys // 4))[:, :dec_seq]
            w_o = b_w_o[j].astype(BF16)
        xp = _mm_postnorm(o_p, w_o, xp, mod_p, 2, g1, b1, alpha, o_transposed=(i % 2 == 1))
        xs = _mm_postnorm(o_s, w_o, xs, mod_s, 2, g1, b1, alpha)
        if i % 2 == 0:
            w_gu = ffn_w_gu[j].astype(BF16)
            w_dn = ffn_w_down[j].astype(BF16)
            dense = dict(ff_chunk=D_FF // 2, alpha=alpha)
            xp = _dense_ffn(xp, mod_p, w_gu, w_dn, g2, b2, **dense)
            xs = _dense_ffn(xs, mod_s, w_gu, w_dn, g2, b2, **dense)
        else:
            w_gu = moe_w_gu[j].astype(BF16)
            w_dn = moe_w_down[j].astype(BF16)
            w_r = jnp.pad(moe_w_router[j], ((0, 0), (0, LANES - N_EXPERTS)))
            b_r = jnp.pad(moe_b_router[j], (0, LANES - N_EXPERTS)).reshape(1, LANES)
            moe = dict(ff_chunk=MOE_FF_CHUNK, alpha=alpha)
            xp = _moe(xp, mod_p, w_r, b_r, w_gu, w_dn, g2, b2, **moe)
            xs = _moe(xs, mod_s, w_r, b_r, w_gu, w_dn, g2, b2, **moe)

    st = lambda name: jnp.stack(outs[name])
    return (xp, xs, st("ka_p"), st("va_p"), st("kb_p"), st("vb_p"), st("ib_p"),
            st("ka_s"), st("va_s"), st("kb_s"), st("vb_s"), st("ib_s"))
```

```python
import functools

import jax
import jax.numpy as jnp
from jax import lax
from jax.experimental import pallas as pl
from jax.experimental.pallas import tpu as pltpu

F32 = jnp.float32
BF16 = jnp.bfloat16

D_MODEL = 1024
CHUNK = 64
N_PAST_CHUNKS = 8
BAND_ROWS = N_PAST_CHUNKS * CHUNK
REL_CLIP = 2 * CHUNK
HEAD_DIM = 64
A_HEADS = 16
B_HEADS = 16
B_KV_HEADS = 4
B_GROUP = B_HEADS // B_KV_HEADS
IDX_HEADS = 8
IDX_DIM = 64
TOPK_MAX = 256
D_FF = 2816
N_EXPERTS = 8
D_FF_EXPERT = 3584
ROPE_THETA = 10000.0
LN_EPS = 1e-5
B_Q = B_HEADS * HEAD_DIM
B_KV = B_KV_HEADS * HEAD_DIM
B_QI = IDX_HEADS * IDX_DIM
B_PROJ = B_Q + 2 * B_KV + B_QI + IDX_DIM + IDX_HEADS

LANES = 128
VMEM_LIMIT_BYTES = 58 * 1024 * 1024

A_Q_TILE = 4 * CHUNK
DSA_Q_TILE = 4 * CHUNK
DSA_KEY_TILE = 512
ROW_TILE = 512
FFN_ROW_TILE = 1024
MOE_CAPACITY = 288
MOE_FF_CHUNK = 1792
B_PROJ_PAD = B_Q + 2 * B_KV + B_QI + LANES
NEG_BIG = -1e30
QK_SCALE_LOG2 = HEAD_DIM ** -0.5 * 1.4426950408889634
BOUND_SLACK = 1.02
BOUND_SLACK_ABS = 0.01
MIN_TRUSTED_DENOMINATOR = 2.0 ** -100
SECANT_STEPS = 12


def _params(n_grid):
    return pltpu.CompilerParams(
        dimension_semantics=("arbitrary",) * n_grid,
        vmem_limit_bytes=VMEM_LIMIT_BYTES,
    )


def _row_blocks(batch, seq, target):
    if seq >= target:
        assert seq % target == 0
        return 1, target
    bb = max(1, min(batch, target // seq))
    while batch % bb:
        bb -= 1
    return bb, seq


def _mod_spec(bb, chunk, n_grid):
    if n_grid == 2:
        return pl.BlockSpec((bb, 1, D_MODEL), lambda b, t: (b, 0, chunk))
    return pl.BlockSpec((bb, 1, D_MODEL), lambda b, t, e, c: (b, 0, chunk))


def _silu(x):
    return x / (1.0 + jnp.exp(-x))


def _layer_norm_rows(z, g, b):
    mu = jnp.mean(z, axis=-1, keepdims=True)
    zc = z - mu
    var = jnp.mean(zc * zc, axis=-1, keepdims=True)
    return zc * lax.rsqrt(var + LN_EPS) * g + b


def _modulation_kernel(c_ref, w_ref, b_ref, o_ref):
    a = _silu(c_ref[...]).astype(BF16)
    w = w_ref[0].astype(BF16)
    o_ref[0] = jnp.dot(a, w, preferred_element_type=F32) + b_ref[0]


def _modulation(c_all, w_cond, b_cond):
    depth, d, n = w_cond.shape
    rows = c_all.shape[0]
    tn = 1536
    return pl.pallas_call(
        _modulation_kernel,
        grid=(depth, n // tn),
        in_specs=[
            pl.BlockSpec((rows, d), lambda i, j: (0, 0)),
            pl.BlockSpec((1, d, tn), lambda i, j: (i, 0, j)),
            pl.BlockSpec((1, 1, tn), lambda i, j: (i, 0, j)),
        ],
        out_specs=pl.BlockSpec((1, rows, tn), lambda i, j: (i, 0, j)),
        out_shape=jax.ShapeDtypeStruct((depth, rows, n), F32),
        compiler_params=_params(2),
        name="modulation",
    )(c_all, w_cond, b_cond.reshape(depth, 1, n))


def _modmm_kernel(x_ref, sc_ref, sh_ref, w_ref, o_ref):
    bb, tt, d = x_ref.shape
    h = x_ref[...] * (1.0 + sc_ref[...]) + sh_ref[...]
    h = h.reshape(bb * tt, d).astype(BF16)
    acc = jnp.dot(h, w_ref[...], preferred_element_type=F32)
    o_ref[...] = acc.reshape(bb, tt, acc.shape[-1]).astype(o_ref.dtype)


def _modmm(x, mod, w, out_dtype, t_start=0):
    batch, seq, d = x.shape
    n = w.shape[1]
    seq_out = seq - t_start
    bb, tt = _row_blocks(batch, seq_out, ROW_TILE)
    assert t_start % tt == 0
    off = t_start // tt
    return pl.pallas_call(
        _modmm_kernel,
        grid=(batch // bb, seq_out // tt),
        in_specs=[
            pl.BlockSpec((bb, tt, d), lambda b, t: (b, t + off, 0)),
            _mod_spec(bb, 1, 2),
            _mod_spec(bb, 0, 2),
            pl.BlockSpec((d, n), lambda b, t: (0, 0)),
        ],
        out_specs=pl.BlockSpec((bb, tt, n), lambda b, t: (b, t, 0)),
        out_shape=jax.ShapeDtypeStruct((batch, seq_out, n), out_dtype),
        compiler_params=_params(2),
        name="a_project",
    )(x, mod, mod, w)


def _band_attn_kernel(*refs, n_kb, n_maybe_invalid):
    q_ref = refs[0]
    k_refs = refs[1:1 + n_kb]
    v_refs = refs[1 + n_kb:1 + 2 * n_kb]
    f_ref = refs[1 + 2 * n_kb]
    o_ref = refs[2 + 2 * n_kb]
    bias_ref = refs[3 + 2 * n_kb]
    q_tile = q_ref.shape[1]
    kb_sizes = [r.shape[1] for r in k_refs]
    k_tot = sum(kb_sizes)
    width = f_ref.shape[-1]
    i = pl.program_id(1)

    @pl.when((pl.program_id(0) == 0) & (i == 0))
    def _():
        rq = lax.broadcasted_iota(jnp.int32, (q_tile, k_tot), 0) // CHUNK
        ck = lax.broadcasted_iota(jnp.int32, (q_tile, k_tot), 1) // CHUNK
        in_band = (ck >= rq) & (ck <= rq + N_PAST_CHUNKS)
        for h in range(A_HEADS):
            rows = jnp.broadcast_to(f_ref[h], (q_tile, width))
            toep = pltpu.roll(rows, k_tot + 1, 1, stride=1, stride_axis=0)
            bias_ref[h] = jnp.where(in_band, toep[:, :k_tot], -jnp.inf)

    q = q_ref[0] * (HEAD_DIM ** -0.5)
    for h in range(A_HEADS):
        cols = slice(h * HEAD_DIM, (h + 1) * HEAD_DIM)
        qh = q[:, cols]
        parts = []
        for kb in range(n_kb):
            lg = lax.dot_general(qh, k_refs[kb][0, :, cols], (((1,), (1,)), ((), ())),
                                 preferred_element_type=F32)
            if kb < n_maybe_invalid:
                lg = jnp.where(i - n_maybe_invalid + kb >= 0, lg, -jnp.inf)
            parts.append(lg)
        logits = jnp.concatenate(parts, axis=1) + bias_ref[h]
        m = jnp.max(logits, axis=1, keepdims=True)
        p = jnp.exp(logits - m)
        l = jnp.sum(p, axis=1, keepdims=True)
        pb = p.astype(BF16)
        acc = jnp.zeros((q_tile, HEAD_DIM), F32)
        start = 0
        for kb in range(n_kb):
            acc = acc + jnp.dot(pb[:, start:start + kb_sizes[kb]], v_refs[kb][0, :, cols],
                                preferred_element_type=F32)
            start += kb_sizes[kb]
        o_ref[0, :, cols] = (acc / l).astype(o_ref.dtype)


def _bias_vector(table, q_tile, k_tot):
    width = k_tot + q_tile
    lo = -(q_tile - 1) - BAND_ROWS + REL_CLIP
    left = max(0, -lo)
    start = max(0, lo)
    n_mid = min(2 * REL_CLIP + 1 - start, width - left)
    right = width - left - n_mid
    mid = table.T[:, start:start + n_mid]
    f = jnp.pad(mid, ((0, 0), (left, right)), mode="edge")
    return f.reshape(A_HEADS, 1, width)


def _band_attn_prompt(qkv, table):
    batch, seq, _ = qkv.shape
    hd = A_HEADS * HEAD_DIM
    qt = A_Q_TILE
    n_prev = BAND_ROWS // qt
    n_kb = n_prev + 1
    k_tot = n_kb * qt
    f = _bias_vector(table, qt, k_tot)

    def kv_spec(kb, col):
        return pl.BlockSpec((1, qt, hd), lambda b, i: (b, jnp.maximum(i - n_prev + kb, 0), col))

    return pl.pallas_call(
        functools.partial(_band_attn_kernel, n_kb=n_kb, n_maybe_invalid=n_prev),
        grid=(batch, seq // qt),
        in_specs=[pl.BlockSpec((1, qt, hd), lambda b, i: (b, i, 0))]
        + [kv_spec(kb, 1) for kb in range(n_kb)]
        + [kv_spec(kb, 2) for kb in range(n_kb)]
        + [pl.BlockSpec(f.shape, lambda b, i: (0, 0, 0))],
        out_specs=pl.BlockSpec((1, qt, hd), lambda b, i: (b, i, 0)),
        out_shape=jax.ShapeDtypeStruct((batch, seq, hd), BF16),
        scratch_shapes=[pltpu.VMEM((A_HEADS, qt, k_tot), F32)],
        compiler_params=_params(2),
        name="band_attn_prompt",
    )(*([qkv] * (1 + 2 * n_kb)), f)


def _band_attn_sample(qkv, cache_k, cache_v, table):
    batch, seq, _ = qkv.shape
    hd = A_HEADS * HEAD_DIM
    win = cache_k.shape[1]
    assert seq == CHUNK and win == BAND_ROWS
    k_tot = win + seq
    f = _bias_vector(table, seq, k_tot)
    new = lambda col: pl.BlockSpec((1, seq, hd), lambda b, i: (b, 0, col))
    old = pl.BlockSpec((1, win, hd), lambda b, i: (b, 0, 0))
    return pl.pallas_call(
        functools.partial(_band_attn_kernel, n_kb=2, n_maybe_invalid=0),
        grid=(batch, 1),
        in_specs=[new(0), old, new(1), old, new(2), pl.BlockSpec(f.shape, lambda b, i: (0, 0, 0))],
        out_specs=pl.BlockSpec((1, seq, hd), lambda b, i: (b, 0, 0)),
        out_shape=jax.ShapeDtypeStruct((batch, seq, hd), BF16),
        scratch_shapes=[pltpu.VMEM((A_HEADS, seq, k_tot), F32)],
        compiler_params=_params(2),
        name="band_attn_sample",
    )(qkv, cache_k, qkv, cache_v, qkv, f)


def _mm_postnorm_kernel(o_ref, w_ref, x_ref, gate_ref, g_ref, b_ref, y_ref, *, alpha, o_transposed):
    bb, tt, d = x_ref.shape
    if o_transposed:
        sub = lax.dot_general(o_ref[0], w_ref[...], (((0,), (0,)), ((), ())), preferred_element_type=F32)
    else:
        o = o_ref[...].reshape(bb * tt, o_ref.shape[-1])
        sub = jnp.dot(o, w_ref[...], preferred_element_type=F32)
    z = alpha * x_ref[...] + (1.0 + gate_ref[...]) * sub.reshape(bb, tt, d)
    y_ref[...] = _layer_norm_rows(z, g_ref[...], b_ref[...])


def _mm_postnorm(o, w, x, mod, gate_chunk, g, b, alpha, o_transposed=False):
    batch, seq, d = x.shape
    bb, tt = _row_blocks(batch, seq, ROW_TILE)
    k = w.shape[0]
    assert bb == 1 or not o_transposed
    o_spec = (pl.BlockSpec((1, k, tt), lambda b_, t: (b_, 0, t)) if o_transposed
              else pl.BlockSpec((bb, tt, k), lambda b_, t: (b_, t, 0)))
    return pl.pallas_call(
        functools.partial(_mm_postnorm_kernel, alpha=alpha, o_transposed=o_transposed),
        grid=(batch // bb, seq // tt),
        in_specs=[
            o_spec,
            pl.BlockSpec((k, d), lambda b_, t: (0, 0)),
            pl.BlockSpec((bb, tt, d), lambda b_, t: (b_, t, 0)),
            _mod_spec(bb, gate_chunk, 2),
            pl.BlockSpec((1, 1, d), lambda b_, t: (0, 0, 0)),
            pl.BlockSpec((1, 1, d), lambda b_, t: (0, 0, 0)),
        ],
        out_specs=pl.BlockSpec((bb, tt, d), lambda b_, t: (b_, t, 0)),
        out_shape=jax.ShapeDtypeStruct((batch, seq, d), F32),
        compiler_params=_params(2),
        name="out_proj_postnorm",
    )(o, w, x, mod, g.reshape(1, 1, d), b.reshape(1, 1, d))


def _dense_ffn_kernel(x_ref, sc_ref, sh_ref, gate_ref, wg_ref, wu_ref, wd_ref, g_ref, b_ref,
                      y_ref, h_ref, acc_ref, *, alpha):
    bb, tt, d = x_ref.shape
    rows = bb * tt
    c = pl.program_id(2)

    @pl.when(c == 0)
    def _():
        h = (x_ref[...] * (1.0 + sc_ref[...]) + sh_ref[...]).reshape(rows, d)
        h_ref[...] = h.astype(BF16)
        acc_ref[...] = jnp.zeros_like(acc_ref)

    h = h_ref[...]
    gt = jnp.dot(h, wg_ref[...], preferred_element_type=F32)
    up = jnp.dot(h, wu_ref[...], preferred_element_type=F32)
    act = (_silu(gt) * up).astype(BF16)
    acc_ref[...] += jnp.dot(act, wd_ref[...], preferred_element_type=F32)

    @pl.when(c == pl.num_programs(2) - 1)
    def _():
        z = alpha * x_ref[...] + (1.0 + gate_ref[...]) * acc_ref[...].reshape(bb, tt, d)
        y_ref[...] = _layer_norm_rows(z, g_ref[...], b_ref[...])


def _dense_ffn(x, mod, w_gu, w_down, g, b, *, ff_chunk, alpha):
    batch, seq, d = x.shape
    ff = w_down.shape[0]
    assert ff % ff_chunk == 0
    n_c = ff // ff_chunk
    bb, tt = _row_blocks(batch, seq, FFN_ROW_TILE)
    rows = bb * tt
    vec = pl.BlockSpec((1, 1, d), lambda b_, t, c: (0, 0, 0))

    def mod_spec(chunk):
        return pl.BlockSpec((bb, 1, d), lambda b_, t, c: (b_, 0, chunk))

    return pl.pallas_call(
        functools.partial(_dense_ffn_kernel, alpha=alpha),
        grid=(batch // bb, seq // tt, n_c),
        in_specs=[
            pl.BlockSpec((bb, tt, d), lambda b_, t, c: (b_, t, 0)),
            mod_spec(4),
            mod_spec(3),
            mod_spec(5),
            pl.BlockSpec((d, ff_chunk), lambda b_, t, c: (0, c)),
            pl.BlockSpec((d, ff_chunk), lambda b_, t, c: (0, n_c + c)),
            pl.BlockSpec((ff_chunk, d), lambda b_, t, c: (c, 0)),
            vec,
            vec,
        ],
        out_specs=pl.BlockSpec((bb, tt, d), lambda b_, t, c: (b_, t, 0)),
        out_shape=jax.ShapeDtypeStruct((batch, seq, d), F32),
        scratch_shapes=[pltpu.VMEM((rows, d), BF16), pltpu.VMEM((rows, d), F32)],
        compiler_params=_params(3),
        name="dense_ffn",
    )(x, mod, mod, mod, w_gu, w_gu, w_down, g.reshape(1, 1, d), b.reshape(1, 1, d))


def _moe_kernel(x_ref, sc_ref, sh_ref, gate_ref, wr_ref, br_ref, wg_ref, wu_ref, wd_ref, g_ref, b_ref,
                y_ref, h_ref, comb_ref, slot_ref, cnt_ref, tri_ref, hc_ref, acc_ref, out_ref, *, alpha):
    bb, tt, d = x_ref.shape
    rows = bb * tt
    cap = hc_ref.shape[1]
    e = pl.program_id(2)
    c = pl.program_id(3)
    last_c = pl.num_programs(3) - 1
    lane = lax.broadcasted_iota(jnp.int32, (rows, LANES), 1)

    @pl.when((pl.program_id(0) == 0) & (pl.program_id(1) == 0) & (e == 0) & (c == 0))
    def _():
        r = lax.broadcasted_iota(jnp.int32, (rows, rows), 0)
        col = lax.broadcasted_iota(jnp.int32, (rows, rows), 1)
        tri_ref[...] = jnp.where(r <= col, 1.0, 0.0).astype(BF16)

    @pl.when((e == 0) & (c == 0))
    def _():
        h = (x_ref[...] * (1.0 + sc_ref[...]) + sh_ref[...]).reshape(rows, d)
        h_ref[...] = h.astype(BF16)
        out_ref[...] = jnp.zeros_like(out_ref)
        logits = jnp.dot(h, wr_ref[...], preferred_element_type=F32,
                         precision=lax.Precision.HIGHEST) + br_ref[...]
        logits = jnp.where(lane < N_EXPERTS, logits, -jnp.inf)
        m1 = jnp.max(logits, axis=1, keepdims=True)
        i1 = jnp.min(jnp.where(logits == m1, lane, LANES), axis=1, keepdims=True)
        rest = jnp.where(lane == i1, -jnp.inf, logits)
        m2 = jnp.max(rest, axis=1, keepdims=True)
        i2 = jnp.min(jnp.where(rest == m2, lane, LANES), axis=1, keepdims=True)
        e2 = jnp.exp(m2 - m1)
        den = 1.0 + e2
        comb_ref[...] = jnp.where(lane == i1, 1.0 / den, 0.0) + jnp.where(lane == i2, e2 / den, 0.0)
        member = jnp.where((lane == i1) | (lane == i2), 1.0, 0.0)
        member_t = member.T[:2 * N_EXPERTS]
        upto = jnp.dot(member_t.astype(BF16), tri_ref[...], preferred_element_type=F32)
        slot_ref[...] = jnp.where(member_t > 0, upto - 1.0, -1.0)
        cnt_ref[...] = jnp.broadcast_to(upto[:, rows - 1:rows], cnt_ref.shape)

    n_sub = (jnp.max(cnt_ref[pl.ds(e, 1), :]).astype(jnp.int32) + cap - 1) // cap

    def selection(s):
        slot = slot_ref[pl.ds(e, 1), :] - (s * cap).astype(F32)
        r = lax.broadcasted_iota(jnp.int32, (cap, rows), 0).astype(F32)
        return jnp.where(slot == r, 1.0, 0.0).astype(BF16)

    @pl.when(c == 0)
    def _():
        def pack(s, carry):
            hc_ref[s] = jnp.dot(selection(s), h_ref[...], preferred_element_type=F32).astype(BF16)
            acc_ref[s] = jnp.zeros(acc_ref.shape[1:], F32)
            return carry
        lax.fori_loop(0, n_sub, pack, 0)

    def expert(s, carry):
        hc = hc_ref[s]
        gt = jnp.dot(hc, wg_ref[0], preferred_element_type=F32)
        up = jnp.dot(hc, wu_ref[0], preferred_element_type=F32)
        act = (_silu(gt) * up).astype(BF16)
        acc_ref[s] += jnp.dot(act, wd_ref[0], preferred_element_type=F32)
        return carry

    lax.fori_loop(0, n_sub, expert, 0)

    @pl.when(c == last_c)
    def _():
        ce = jnp.sum(jnp.where(lane == e, comb_ref[...], 0.0), axis=1, keepdims=True)

        def unpack(s, carry):
            back = lax.dot_general(selection(s), acc_ref[s].astype(BF16), (((0,), (0,)), ((), ())),
                                   preferred_element_type=F32)
            out_ref[...] += ce * back
            return carry
        lax.fori_loop(0, n_sub, unpack, 0)

    @pl.when((e == pl.num_programs(2) - 1) & (c == last_c))
    def _():
        z = alpha * x_ref[...] + (1.0 + gate_ref[...]) * out_ref[...].reshape(bb, tt, d)
        y_ref[...] = _layer_norm_rows(z, g_ref[...], b_ref[...])


def _moe(x, mod, w_r, b_r, w_gu, w_down, g, b, *, ff_chunk, alpha):
    batch, seq, d = x.shape
    n_e, ff, _ = w_down.shape
    assert ff % ff_chunk == 0
    n_c = ff // ff_chunk
    bb, tt = _row_blocks(batch, seq, FFN_ROW_TILE)
    rows = bb * tt
    cap = MOE_CAPACITY
    max_sub = -(-rows // cap)
    vec = pl.BlockSpec((1, 1, d), lambda b_, t, e, c: (0, 0, 0))
    return pl.pallas_call(
        functools.partial(_moe_kernel, alpha=alpha),
        grid=(batch // bb, seq // tt, n_e, n_c),
        in_specs=[
            pl.BlockSpec((bb, tt, d), lambda b_, t, e, c: (b_, t, 0), pipeline_mode=pl.Buffered(1)),
            _mod_spec(bb, 4, 4),
            _mod_spec(bb, 3, 4),
            _mod_spec(bb, 5, 4),
            pl.BlockSpec((d, LANES), lambda b_, t, e, c: (0, 0)),
            pl.BlockSpec((1, LANES), lambda b_, t, e, c: (0, 0)),
            pl.BlockSpec((1, d, ff_chunk), lambda b_, t, e, c: (e, 0, c)),
            pl.BlockSpec((1, d, ff_chunk), lambda b_, t, e, c: (e, 0, n_c + c)),
            pl.BlockSpec((1, ff_chunk, d), lambda b_, t, e, c: (e, c, 0)),
            vec,
            vec,
        ],
        out_specs=pl.BlockSpec((bb, tt, d), lambda b_, t, e, c: (b_, t, 0)),
        out_shape=jax.ShapeDtypeStruct((batch, seq, d), F32),
        scratch_shapes=[
            pltpu.VMEM((rows, d), BF16),
            pltpu.VMEM((rows, LANES), F32),
            pltpu.VMEM((2 * N_EXPERTS, rows), F32),
            pltpu.VMEM((2 * N_EXPERTS, LANES), F32),
            pltpu.VMEM((rows, rows), BF16),
            pltpu.VMEM((max_sub, cap, d), BF16),
            pltpu.VMEM((max_sub, cap, d), F32),
            pltpu.VMEM((rows, d), F32),
        ],
        compiler_params=_params(4),
        name="moe_ffn",
    )(x, mod, mod, mod, w_r, b_r, w_gu, w_gu, w_down, g.reshape(1, 1, d), b.reshape(1, 1, d))


def _rope_lanes(seg, cos, sin_signed, first_half):
    width = seg.shape[-1]
    reps = width // LANES
    if reps > 1:
        cos = jnp.concatenate([cos] * reps, axis=1)
        sin_signed = jnp.concatenate([sin_signed] * reps, axis=1)
        first_half = jnp.concatenate([first_half] * reps, axis=1)
    half = HEAD_DIM // 2
    swapped = jnp.where(first_half, pltpu.roll(seg, width - half, 1), pltpu.roll(seg, half, 1))
    return seg * cos + swapped * sin_signed


def _b_project_kernel(x_ref, sc_ref, sh_ref, w_ref, cos_ref, sin_ref, kg_ref, kb_ref,
                      q_ref, qi_ref, k_ref, v_ref, kw_ref, *extra_refs, transposed):
    bb, tt, d = x_ref.shape
    rows = bb * tt
    h = x_ref[...] * (1.0 + sc_ref[...]) + sh_ref[...]
    h = h.reshape(rows, d).astype(BF16)
    acc = jnp.dot(h, w_ref[...], preferred_element_type=F32)
    cos = cos_ref[...]
    sin_signed = sin_ref[...]
    lane = lax.broadcasted_iota(jnp.int32, (1, LANES), 1)
    first_half = (lane % HEAD_DIM) < (HEAD_DIM // 2)

    def out(ref, val):
        ref[...] = val.reshape(bb, tt, val.shape[-1]).astype(ref.dtype)

    def out_query_side(ref, val):
        if transposed:
            ref[0] = val.T.astype(ref.dtype)
        else:
            out(ref, val)

    o_k, o_v, o_qi, o_ki = B_Q, B_Q + B_KV, B_Q + 2 * B_KV, B_Q + 2 * B_KV + B_QI
    out_query_side(q_ref, _rope_lanes(acc[:, :o_k], cos, sin_signed, first_half) * QK_SCALE_LOG2)
    k_rot = _rope_lanes(acc[:, o_k:o_v], cos, sin_signed, first_half)
    out(k_ref, k_rot)
    out(v_ref, acc[:, o_v:o_qi])
    out_query_side(qi_ref, _rope_lanes(acc[:, o_qi:o_ki], cos, sin_signed, first_half))
    seg = acc[:, o_ki:]
    is_ki = lane < IDX_DIM
    mu = jnp.sum(jnp.where(is_ki, seg, 0.0), axis=1, keepdims=True) / IDX_DIM
    cen = jnp.where(is_ki, seg - mu, 0.0)
    var = jnp.sum(cen * cen, axis=1, keepdims=True) / IDX_DIM
    ki = cen * lax.rsqrt(var + LN_EPS) * kg_ref[...] + kb_ref[...]
    ki = _rope_lanes(ki, cos, sin_signed, first_half)
    kw = jnp.where(is_ki, ki, seg * (IDX_HEADS ** -0.5))
    out(kw_ref, kw)
    if transposed:
        kwt_ref, k16_ref, ki16_ref, vta_ref, kmax_ref = extra_refs
        kwt_ref[0] = kw.T
        out(k16_ref, k_rot)
        k_sq = k_rot.astype(BF16).astype(F32)
        k_sq = k_sq * k_sq
        norm_max = jnp.zeros((1, LANES), F32)
        for g in range(B_KV_HEADS):
            part = k_sq[:, (g // 2) * LANES:(g // 2 + 1) * LANES]
            n2 = jnp.sum(jnp.where(lane // HEAD_DIM == g % 2, part, 0.0), axis=1, keepdims=True)
            norm_max = jnp.where(lane == g, jnp.max(n2, axis=0, keepdims=True), norm_max)
        norm_max = jnp.broadcast_to(norm_max, kmax_ref.shape[1:])

        @pl.when(pl.program_id(1) == 0)
        def _():
            kmax_ref[0] = norm_max

        @pl.when(pl.program_id(1) > 0)
        def _():
            kmax_ref[0] = jnp.maximum(kmax_ref[0], norm_max)
        out(ki16_ref, kw[:, :IDX_DIM])
        v_t = acc[:, o_v:o_qi].T.astype(BF16).reshape(B_KV_HEADS, HEAD_DIM, tt)
        tail_rows = vta_ref.shape[2] - HEAD_DIM
        first_row = lax.broadcasted_iota(jnp.int32, (B_KV_HEADS, tail_rows, tt), 1) == 0
        vta_ref[0] = jnp.concatenate([v_t, jnp.where(first_row, 1.0, 0.0).astype(BF16)], axis=1)


def _b_project(x, mod, w, cos, sin_signed, kn_g, kn_b, table_per_tile, transposed=False):
    batch, seq, d = x.shape
    bb, tt = _row_blocks(batch, seq, ROW_TILE)
    rows = bb * tt
    n = w.shape[1]
    assert bb == 1 or not transposed
    tab = pl.BlockSpec((rows, LANES), (lambda b, t: (0, 0)) if table_per_tile else (lambda b, t: (t, 0)))
    vec = pl.BlockSpec((1, LANES), lambda b, t: (0, 0))

    def o_spec(width):
        return pl.BlockSpec((bb, tt, width), lambda b, t: (b, t, 0))

    def o_shape(width, dtype):
        return jax.ShapeDtypeStruct((batch, seq, width), dtype)

    def t_spec(width):
        return pl.BlockSpec((1, width, tt), lambda b, t: (b, 0, t))

    def t_shape(width, dtype):
        return jax.ShapeDtypeStruct((batch, width, seq), dtype)

    q_spec, q_shape = (t_spec, t_shape) if transposed else (o_spec, o_shape)
    extra_specs, extra_shapes = [], []
    if transposed:
        extra_specs = [t_spec(LANES), o_spec(B_KV), o_spec(IDX_DIM),
                       pl.BlockSpec((1, B_KV_HEADS, V_AUG_ROWS, tt), lambda b, t: (b, 0, 0, t)),
                       pl.BlockSpec((1, 8, LANES), lambda b, t: (b, 0, 0))]
        extra_shapes = [t_shape(LANES, F32), o_shape(B_KV, BF16), o_shape(IDX_DIM, BF16),
                        jax.ShapeDtypeStruct((batch, B_KV_HEADS, V_AUG_ROWS, seq), BF16),
                        jax.ShapeDtypeStruct((batch, 8, LANES), F32)]
    return pl.pallas_call(
        functools.partial(_b_project_kernel, transposed=transposed),
        grid=(batch // bb, seq // tt),
        in_specs=[
            pl.BlockSpec((bb, tt, d), lambda b, t: (b, t, 0)),
            _mod_spec(bb, 1, 2),
            _mod_spec(bb, 0, 2),
            pl.BlockSpec((d, n), lambda b, t: (0, 0)),
            tab,
            tab,
            vec,
            vec,
        ],
        out_specs=[q_spec(B_Q), q_spec(B_QI), o_spec(B_KV), o_spec(B_KV), o_spec(LANES)] + extra_specs,
        out_shape=[q_shape(B_Q, BF16), q_shape(B_QI, BF16), o_shape(B_KV, F32), o_shape(B_KV, F32),
                   o_shape(LANES, F32)] + extra_shapes,
        compiler_params=_params(2),
        name="b_project",
    )(x, mod, mod, w, cos, sin_signed, kn_g, kn_b)


def _rope_tables(pos):
    half = HEAD_DIM // 2
    inv = ROPE_THETA ** (-jnp.arange(half, dtype=F32) / half)
    ang = pos.astype(F32)[:, None] * inv[None, :]
    cos, sin = jnp.cos(ang), jnp.sin(ang)
    return (jnp.concatenate([cos, cos, cos, cos], axis=1),
            jnp.concatenate([-sin, sin, -sin, sin], axis=1))


def _key_norm_kernel(k_ref, grp_ref, o_ref):
    k = k_ref[0].astype(F32)
    n2 = jnp.dot((k * k).astype(BF16), grp_ref[...], preferred_element_type=F32)
    mx = jnp.broadcast_to(jnp.max(n2, axis=0, keepdims=True), o_ref.shape[1:])

    @pl.when(pl.program_id(1) == 0)
    def _():
        o_ref[0] = mx

    @pl.when(pl.program_id(1) > 0)
    def _():
        o_ref[0] = jnp.maximum(o_ref[0], mx)


def _key_norm_max(k):
    batch, lp, width = k.shape
    tile = lp if lp <= 8 * DSA_KEY_TILE else DSA_KEY_TILE
    grp = (jnp.arange(width)[:, None] // HEAD_DIM == jnp.arange(LANES)[None, :]).astype(BF16)
    return pl.pallas_call(
        _key_norm_kernel,
        grid=(batch, lp // tile),
        in_specs=[pl.BlockSpec((1, tile, width), lambda b, t: (b, t, 0)),
                  pl.BlockSpec((width, LANES), lambda b, t: (0, 0))],
        out_specs=pl.BlockSpec((1, 8, LANES), lambda b, t: (b, 0, 0)),
        out_shape=jax.ShapeDtypeStruct((batch, 8, LANES), F32),
        compiler_params=_params(2),
        name="key_norm_max",
    )(k, grp)


def _dsa_kernel(qt_ref, qit_ref, kwt_ref, ki_ref, k_ref, vta_ref, kmax_ref, o_ref,
                s_ref, qpad_ref, bound_ref, m_ref, acc_ref, *, causal, n_keys, k_sel):
    qn = qt_ref.shape[2]
    lt = DSA_KEY_TILE
    i = pl.program_id(1)
    if causal:
        n_tiles = ((i + 1) * qn + lt - 1) // lt
        qpos = i * qn + lax.broadcasted_iota(jnp.int32, (1, qn), 1)
        limit = (qpos // CHUNK + 1) * CHUNK
    else:
        n_tiles = ki_ref.shape[1] // lt
        limit = jnp.full((1, qn), n_keys, jnp.int32)

    def key_slice(t):
        return pl.ds(pl.multiple_of(t * lt, lt), lt)

    def key_pos(t):
        return t * lt + lax.broadcasted_iota(jnp.int32, (lt, 1), 0)

    def fold_rows(x, op):
        return op(op(x.reshape(lt // 32, 4, 8, x.shape[-1]), axis=0), axis=0)

    def paired_tile_loop(body, init):
        def pair(t2, carry):
            return body(2 * t2 + 1, body(2 * t2, carry))
        carry = lax.fori_loop(0, n_tiles // 2, pair, init)
        return lax.cond(n_tiles % 2 == 1, lambda c: body(n_tiles - 1, c), lambda c: c, carry)

    qit = qit_ref[0]
    kwt = kwt_ref[0]
    qi_w = jnp.concatenate([qit[h * IDX_DIM:(h + 1) * IDX_DIM, :] for h in range(IDX_HEADS)], axis=1)
    w_row = jnp.concatenate([kwt[IDX_DIM + h:IDX_DIM + h + 1, :] for h in range(IDX_HEADS)], axis=1)
    w_row = w_row * (IDX_DIM ** -0.5)

    def score_body(t, carry):
        rmax, rmin = carry
        s = jnp.dot(ki_ref[0, key_slice(t), :], qi_w, preferred_element_type=F32)
        s = jnp.maximum(s, 0.0) * w_row
        sc = s[:, 0:qn]
        for h in range(1, IDX_HEADS):
            sc = sc + s[:, h * qn:(h + 1) * qn]
        adm = key_pos(t) < limit
        masked = jnp.where(adm, sc, -jnp.inf)
        s_ref[key_slice(t), :] = masked
        rmax = jnp.maximum(rmax, jnp.max(masked, axis=0, keepdims=True))
        rmin = jnp.minimum(rmin, jnp.min(jnp.where(adm, sc, jnp.inf), axis=0, keepdims=True))
        return rmax, rmin

    rmax, rmin = paired_tile_loop(score_body,
                                  (jnp.full((1, qn), -jnp.inf, F32), jnp.full((1, qn), jnp.inf, F32)))

    def count(pred):
        def body(t, acc):
            return acc + fold_rows(jnp.where(pred(s_ref[key_slice(t), :], key_pos(t)), 1.0, 0.0), jnp.sum)
        acc = paired_tile_loop(body, jnp.zeros((8, qn), F32))
        return jnp.sum(acc, axis=0, keepdims=True)

    def to_key(x):
        bits = lax.bitcast_convert_type(x, jnp.int32)
        return bits ^ ((bits >> 31) & 0x7FFFFFFF)

    def from_key(key):
        return lax.bitcast_convert_type(key ^ ((key >> 31) & 0x7FFFFFFF), F32)

    n_adm = limit.astype(F32)
    target = jnp.minimum(n_adm, float(k_sel))
    log_target = jnp.log(target)

    def any_left(done):
        return jnp.sum(done) < qn

    c_ge0 = count(lambda s, _: s >= 0.0)
    c_gt0 = count(lambda s, _: s > 0.0)
    is_pos = c_gt0 >= target
    is_neg = c_ge0 < target
    zero_key = jnp.zeros((1, qn), jnp.int32)
    lo0 = jnp.where(is_neg, to_key(rmin), zero_key)
    c_lo0 = jnp.where(is_neg, n_adm, c_ge0)
    c_hi0 = jnp.where(is_pos, 0.0, c_ge0)
    done0 = jnp.where((n_adm == target) | jnp.logical_not(is_pos | is_neg) | (c_lo0 == target), 1.0, 0.0)
    lo0 = jnp.where(n_adm == target, to_key(rmin), lo0)
    c_lo0 = jnp.where(n_adm == target, n_adm, c_lo0)

    ones = jnp.ones((1, qn), F32)

    def search(st0, key_to_value, value_to_key, count_ge):
        def cond(st):
            return any_left(st[7])

        def body(st):
            it, lo, hi, c_lo, c_hi, w_lo, w_hi, done, side = st
            t_lo, t_hi = key_to_value(lo), key_to_value(hi)
            f_lo = (jnp.log(c_lo) - log_target) * w_lo
            f_hi = (log_target - jnp.log(jnp.maximum(c_hi, 0.5))) * w_hi
            guess = value_to_key(t_lo + f_lo / (f_lo + f_hi) * (t_hi - t_lo))
            halve = (lo >> 1) + (hi >> 1) + (lo & hi & 1)
            mid = jnp.where(it >= SECANT_STEPS, halve, jnp.clip(guess, lo + 1, hi - 1))
            stuck = halve == lo
            c = count_ge(key_to_value(mid))
            live = (done < 0.5) & jnp.logical_not(stuck)
            up = live & (c >= target)
            dn = live & (c < target)
            w_hi = jnp.where(up, jnp.where(side > 0, 0.5 * w_hi, 1.0), jnp.where(dn, 1.0, w_hi))
            w_lo = jnp.where(dn, jnp.where(side < 0, 0.5 * w_lo, 1.0), jnp.where(up, 1.0, w_lo))
            side = jnp.where(up, 1.0, jnp.where(dn, -1.0, side))
            lo = jnp.where(up, mid, lo)
            c_lo = jnp.where(up, c, c_lo)
            hi = jnp.where(dn, mid, hi)
            c_hi = jnp.where(dn, c, c_hi)
            done = jnp.where(stuck | (c_lo == target) | (c_lo - c_hi <= 2.0), 1.0, done)
            return it + 1, lo, hi, c_lo, c_hi, w_lo, w_hi, done, side

        lo, hi, c_lo, c_hi, done = st0
        st = lax.while_loop(cond, body, (jnp.int32(0), lo, hi, c_lo, c_hi, ones, ones, done, 0.0 * ones))
        return st[1], st[2], st[3], st[4]

    hi0 = jnp.where(is_pos, to_key(rmax) + 1, zero_key)
    k_lo, k_hi, c_thr, c_above = search((lo0, hi0, c_lo0, c_hi0, done0),
                                        from_key, to_key, lambda v: count(lambda s, _: s >= v))
    thr = from_key(k_lo)

    two_left = (c_thr != target) & (c_thr - c_above == 2.0)

    def settle(_):
        t_lo, t_hi = from_key(k_lo), from_key(k_hi)

        def body(t, carry):
            top, bottom = carry
            s = s_ref[key_slice(t), :]
            top = jnp.maximum(top, fold_rows(jnp.where(s < t_hi, s, -jnp.inf), jnp.max))
            bottom = jnp.minimum(bottom, fold_rows(jnp.where(s >= t_lo, s, jnp.inf), jnp.min))
            return top, bottom

        top, bottom = lax.fori_loop(0, n_tiles, body, (jnp.full((8, qn), -jnp.inf, F32),
                                                       jnp.full((8, qn), jnp.inf, F32)))
        top = jnp.max(top, axis=0, keepdims=True)
        bottom = jnp.min(bottom, axis=0, keepdims=True)
        return (jnp.where(two_left, top, thr),
                jnp.where(two_left & (top != bottom), target, c_thr))

    thr, c_thr = lax.cond(jnp.sum(jnp.where(two_left, 1.0, 0.0)) > 0, settle, lambda _: (thr, c_thr), 0)

    tied = c_thr > target

    @pl.when(jnp.sum(jnp.where(tied, 1.0, 0.0)) > 0)
    def _():
        need = target - count(lambda s, _: s > thr)

        def idx_cond(st):
            return any_left(st[5])

        def idx_body(st):
            it, lo_j, hi_j, c_lo_j, c_hi_j, done = st
            span = (hi_j - lo_j).astype(F32)
            step = jnp.ceil((need - c_lo_j) / (c_hi_j - c_lo_j) * span).astype(jnp.int32)
            step = jnp.where(it % 2 == 0, step, (hi_j - lo_j) // 2)
            mid = lo_j + jnp.clip(step, 1, jnp.maximum(hi_j - lo_j - 1, 1))
            c = count(lambda s, kp: (s == thr) & (kp <= mid))
            live = (done < 0.5) & (hi_j - lo_j > 1)
            below = live & (c < need)
            above = live & (c >= need)
            lo_j = jnp.where(below, mid, lo_j)
            c_lo_j = jnp.where(below, c, c_lo_j)
            hi_j = jnp.where(above, mid, hi_j)
            c_hi_j = jnp.where(above, c, c_hi_j)
            done = jnp.where((c_hi_j == need) | (hi_j - lo_j <= 1), 1.0, done)
            return it + 1, lo_j, hi_j, c_lo_j, c_hi_j, done

        first = jnp.full((1, qn), -1, jnp.int32)
        final = jnp.full((1, qn), n_tiles * lt - 1, jnp.int32)
        n_ties = c_thr - (target - need)
        idx_done0 = jnp.where(tied & (n_ties > need), 0.0, 1.0)
        idx = lax.while_loop(idx_cond, idx_body,
                             (jnp.int32(0), first, final, 0.0 * ones, n_ties, idx_done0))
        last = idx[2]

        def drop_body(t, carry):
            s = s_ref[key_slice(t), :]
            s_ref[key_slice(t), :] = jnp.where(tied & (s == thr) & (key_pos(t) > last), -jnp.inf, s)
            return carry

        lax.fori_loop(0, n_tiles, drop_body, 0)

    qt = qt_ref[0]
    zeros = jnp.zeros((HEAD_DIM, B_GROUP * qn), BF16)
    for g in range(B_KV_HEADS):
        q_g = jnp.concatenate(
            [qt[(g * B_GROUP + j) * HEAD_DIM:(g * B_GROUP + j + 1) * HEAD_DIM, :] for j in range(B_GROUP)],
            axis=1)
        qpad_ref[g] = jnp.concatenate([zeros] * g + [q_g] + [zeros] * (B_KV_HEADS - 1 - g), axis=0)

    qf = qt.astype(F32)
    q_norm2 = jnp.sum((qf * qf).reshape(B_HEADS, HEAD_DIM, qn), axis=1)
    for g in range(B_KV_HEADS):
        b = jnp.sqrt(q_norm2[g * B_GROUP:(g + 1) * B_GROUP] * kmax_ref[0, 0:1, g:g + 1])
        b = b * BOUND_SLACK + BOUND_SLACK_ABS
        bound_ref[g] = jnp.concatenate([b[j:j + 1] for j in range(B_GROUP)], axis=1)
    acc_ref[...] = jnp.zeros(acc_ref.shape, F32)

    def fast_body(t, carry):
        keys = k_ref[0, key_slice(t), :]
        sel = jnp.where(s_ref[key_slice(t), :] >= thr, 1.0, 0.0).astype(BF16)
        sel = jnp.concatenate([sel] * B_GROUP, axis=1)
        def logits(g):
            return jnp.dot(keys, qpad_ref[g], preferred_element_type=F32)

        lg = logits(0)
        for g in range(B_KV_HEADS):
            lg_next = logits(g + 1) if g + 1 < B_KV_HEADS else None
            p = jnp.exp2(lg - bound_ref[g]).astype(BF16) * sel
            acc_ref[g] += jnp.dot(vta_ref[0, g, :, key_slice(t)], p, preferred_element_type=F32)
            lg = lg_next
        return carry

    paired_tile_loop(fast_body, 0)
    den_min = jnp.min(acc_ref[:, HEAD_DIM:HEAD_DIM + 1, :])

    @pl.when(jnp.logical_not(den_min >= MIN_TRUSTED_DENOMINATOR))
    def _():
        m_ref[...] = jnp.full(m_ref.shape, NEG_BIG, F32)
        acc_ref[...] = jnp.zeros(acc_ref.shape, F32)

        def exact_body(t, carry):
            keys = k_ref[0, key_slice(t), :]
            mb = jnp.where(s_ref[key_slice(t), :] >= thr, 0.0, -jnp.inf)
            mb = jnp.concatenate([mb] * B_GROUP, axis=1)
            for g in range(B_KV_HEADS):
                lg = jnp.dot(keys, qpad_ref[g], preferred_element_type=F32) + mb
                m_old = m_ref[g]
                m_new = jnp.maximum(m_old, jnp.max(fold_rows(lg, jnp.max), axis=0, keepdims=True))
                p = jnp.exp2(lg - m_new).astype(BF16)
                pv = jnp.dot(vta_ref[0, g, :, key_slice(t)], p, preferred_element_type=F32)
                acc_ref[g] = jnp.exp2(m_old - m_new) * acc_ref[g] + pv
                m_ref[g] = m_new
            return carry

        lax.fori_loop(0, n_tiles, exact_body, 0)

    for g in range(B_KV_HEADS):
        acc = acc_ref[g]
        o_g = acc[:HEAD_DIM] / acc[HEAD_DIM:HEAD_DIM + 1]
        for j in range(B_GROUP):
            hh = g * B_GROUP + j
            o_ref[0, hh * HEAD_DIM:(hh + 1) * HEAD_DIM, :] = o_g[:, j * qn:(j + 1) * qn].astype(o_ref.dtype)


V_AUG_ROWS = HEAD_DIM + 16


def _dsa(q, qi, kw, k, v, ki, *, q_tile, causal, k_sel, transposed=False, kmax=None):
    batch = q.shape[0]
    seq = q.shape[2] if transposed else q.shape[1]
    n_keys = k.shape[1]
    lp = -(-n_keys // DSA_KEY_TILE) * DSA_KEY_TILE
    assert seq % q_tile == 0 and q_tile % LANES == 0
    if v.ndim == 4:
        assert lp == n_keys and k.dtype == BF16 and ki.dtype == BF16
        vta = v
    else:
        pad = ((0, 0), (0, lp - n_keys), (0, 0))
        k, v, ki = (jnp.pad(a.astype(BF16), pad) for a in (k, v, ki))
        vt = jnp.swapaxes(v, 1, 2).reshape(batch, B_KV_HEADS, HEAD_DIM, lp)
        vta = jnp.concatenate([vt, jnp.ones((batch, B_KV_HEADS, 1, lp), BF16),
                               jnp.zeros((batch, B_KV_HEADS, V_AUG_ROWS - HEAD_DIM - 1, lp), BF16)], axis=2)
    qt, qit, kwt = (q, qi, kw) if transposed else (jnp.swapaxes(a, 1, 2) for a in (q, qi, kw))

    def qspec(rows):
        return pl.BlockSpec((1, rows, q_tile), lambda b, i: (b, 0, i))

    def resident(shape):
        zero = (0,) * len(shape)
        return pl.BlockSpec((1,) + shape, lambda b, i: (b,) + zero, pipeline_mode=pl.Buffered(1))

    ot = pl.pallas_call(
        functools.partial(_dsa_kernel, causal=causal, n_keys=n_keys, k_sel=k_sel),
        grid=(batch, seq // q_tile),
        in_specs=[qspec(B_Q), qspec(B_QI), qspec(LANES),
                  resident((lp, IDX_DIM)), resident((lp, B_KV)),
                  resident((B_KV_HEADS, V_AUG_ROWS, lp)),
                  pl.BlockSpec((1, 8, LANES), lambda b, i: (b, 0, 0))],
        out_specs=qspec(B_Q),
        out_shape=jax.ShapeDtypeStruct((batch, B_Q, seq), BF16),
        scratch_shapes=[pltpu.VMEM((lp, q_tile), F32),
                        pltpu.VMEM((B_KV_HEADS, B_KV, B_GROUP * q_tile), BF16),
                        pltpu.VMEM((B_KV_HEADS, 1, B_GROUP * q_tile), F32),
                        pltpu.VMEM((B_KV_HEADS, 1, B_GROUP * q_tile), F32),
                        pltpu.VMEM((B_KV_HEADS, V_AUG_ROWS, B_GROUP * q_tile), F32)],
        compiler_params=_params(2),
        name="dsa_prompt" if causal else "dsa_sample",
    )(qt, qit, kwt, ki, k, vta, _key_norm_max(k) if kmax is None else kmax)
    return ot if transposed else jnp.swapaxes(ot, 1, 2)


def kernel(x_prompt, x_sample, cache_k_a, cache_v_a, cache_k_b, cache_v_b, cache_kidx_b,
           c_prompt, c_sample, w_cond, b_cond, ln_g, ln_b, a_w_in, a_w_o, a_rel_bias,
           b_w_in, b_w_o, b_kidx_ln_g, b_kidx_ln_b, ffn_w_gu, ffn_w_down,
           moe_w_router, moe_b_router, moe_w_gu, moe_w_down):
    depth = w_cond.shape[0]
    alpha = (2 * depth) ** 0.25
    n_p, seq, d = x_prompt.shape
    n_s, dec_seq, _ = x_sample.shape
    past = cache_k_b.shape[2]
    a_hd = A_HEADS * HEAD_DIM

    rows = n_p + n_s
    rows_pad = -(-rows // 8) * 8
    c_all = jnp.pad(jnp.concatenate([c_prompt, c_sample], axis=0), ((0, rows_pad - rows), (0, 0)))
    mod_all = _modulation(c_all, w_cond, b_cond)

    xp, xs = x_prompt, x_sample
    outs = {k: [] for k in ("ka_p", "va_p", "kb_p", "vb_p", "ib_p", "ka_s", "va_s", "kb_s", "vb_s", "ib_s")}
    for i in range(depth):
        j = i // 2
        mod_p = mod_all[i, :n_p].reshape(n_p, 1, 6 * d)
        mod_s = mod_all[i, n_p:rows].reshape(n_s, 1, 6 * d)
        g1, b1, g2, b2 = ln_g[i, 0], ln_b[i, 0], ln_g[i, 1], ln_b[i, 1]
        if i % 2 == 0:
            w_in = a_w_in[j].astype(BF16)
            keep = min(BAND_ROWS, seq)
            qkv_p = _modmm(xp, mod_p, w_in, BF16)
            kv_p = _modmm(xp, mod_p, w_in[:, a_hd:], F32, t_start=seq - keep)
            qkv_s = _modmm(xs, mod_s, w_in, BF16)
            kv_s = _modmm(xs, mod_s, w_in[:, a_hd:], F32)
            outs["ka_p"].append(kv_p[..., :a_hd].reshape(n_p, keep, A_HEADS, HEAD_DIM))
            outs["va_p"].append(kv_p[..., a_hd:].reshape(n_p, keep, A_HEADS, HEAD_DIM))
            outs["ka_s"].append(kv_s[..., :a_hd].reshape(n_s, dec_seq, A_HEADS, HEAD_DIM))
            outs["va_s"].append(kv_s[..., a_hd:].reshape(n_s, dec_seq, A_HEADS, HEAD_DIM))
            o_p = _band_attn_prompt(qkv_p, a_rel_bias[j])
            ck = cache_k_a[j].reshape(n_s, -1, a_hd).astype(BF16)
            cv = cache_v_a[j].reshape(n_s, -1, a_hd).astype(BF16)
            o_s = _band_attn_sample(qkv_s, ck, cv, a_rel_bias[j])
            w_o = a_w_o[j].astype(BF16)
        else:
            w_in = jnp.pad(b_w_in[j], ((0, 0), (0, B_PROJ_PAD - B_PROJ))).astype(BF16)
            kn_g = jnp.pad(b_kidx_ln_g[j], (0, LANES - IDX_DIM)).reshape(1, LANES)
            kn_b = jnp.pad(b_kidx_ln_b[j], (0, LANES - IDX_DIM)).reshape(1, LANES)
            cos_p, sin_p = _rope_tables(jnp.arange(seq))
            cos_s, sin_s = _rope_tables(past + jnp.arange(dec_seq))
            bb_s, _ = _row_blocks(n_s, dec_seq, ROW_TILE)
            cos_s, sin_s = jnp.tile(cos_s, (bb_s, 1)), jnp.tile(sin_s, (bb_s, 1))
            q_p, qi_p, k_p, v_p, kw_p, kwt_p, k16_p, ki16_p, vta_p, kmax_p = _b_project(
                xp, mod_p, w_in, cos_p, sin_p, kn_g, kn_b, False, transposed=True)
            q_s, qi_s, k_s, v_s, kw_s = _b_project(xs, mod_s, w_in, cos_s, sin_s, kn_g, kn_b, True)
            outs["kb_p"].append(k_p.reshape(n_p, seq, B_KV_HEADS, HEAD_DIM))
            outs["vb_p"].append(v_p.reshape(n_p, seq, B_KV_HEADS, HEAD_DIM))
            outs["ib_p"].append(kw_p[..., :IDX_DIM])
            outs["kb_s"].append(k_s.reshape(n_s, dec_seq, B_KV_HEADS, HEAD_DIM))
            outs["vb_s"].append(v_s.reshape(n_s, dec_seq, B_KV_HEADS, HEAD_DIM))
            outs["ib_s"].append(kw_s[..., :IDX_DIM])
            o_p = _dsa(q_p, qi_p, kwt_p, k16_p, vta_p, ki16_p, q_tile=DSA_Q_TILE, causal=True,
                       k_sel=min(TOPK_MAX, seq // 4), transposed=True, kmax=kmax_p)
            n_keys = past + dec_seq
            kk = jnp.concatenate([cache_k_b[j].reshape(n_s, past, B_KV), k_s], axis=1)
            vc = jnp.concatenate([cache_v_b[j].reshape(n_s, past, B_KV), v_s], axis=1)
            kki = jnp.concatenate([cache_kidx_b[j], kw_s[..., :IDX_DIM]], axis=1)
            rep = LANES // dec_seq
            q_r, qi_r, kw_r = (jnp.concatenate([a] * rep, axis=1) for a in (q_s, qi_s, kw_s))
            o_s = _dsa(q_r, qi_r, kw_r, kk, vc, kki, q_tile=LANES, causal=False,
                       k_sel=min(TOPK_MAX, n_keys // 4))[:, :dec_seq]
            w_o = b_w_o[j].astype(BF16)
        xp = _mm_postnorm(o_p, w_o, xp, mod_p, 2, g1, b1, alpha, o_transposed=(i % 2 == 1))
        xs = _mm_postnorm(o_s, w_o, xs, mod_s, 2, g1, b1, alpha)
        if i % 2 == 0:
            w_gu = ffn_w_gu[j].astype(BF16)
            w_dn = ffn_w_down[j].astype(BF16)
            dense = dict(ff_chunk=D_FF // 2, alpha=alpha)
            xp = _dense_ffn(xp, mod_p, w_gu, w_dn, g2, b2, **dense)
            xs = _dense_ffn(xs, mod_s, w_gu, w_dn, g2, b2, **dense)
        else:
            w_gu = moe_w_gu[j].astype(BF16)
            w_dn = moe_w_down[j].astype(BF16)
            w_r = jnp.pad(moe_w_router[j], ((0, 0), (0, LANES - N_EXPERTS)))
            b_r = jnp.pad(moe_b_router[j], (0, LANES - N_EXPERTS)).reshape(1, LANES)
            moe = dict(ff_chunk=MOE_FF_CHUNK, alpha=alpha)
            xp = _moe(xp, mod_p, w_r, b_r, w_gu, w_dn, g2, b2, **moe)
            xs = _moe(xs, mod_s, w_r, b_r, w_gu, w_dn, g2, b2, **moe)

    st = lambda name: jnp.stack(outs[name])
    return (xp, xs, st("ka_p"), st("va_p"), st("kb_p"), st("vb_p"), st("ib_p"),
            st("ka_s"), st("va_s"), st("kb_s"), st("vb_s"), st("ib_s"))
```
